```python
import jax, jax.numpy as jnp
from jax import lax
import numpy as np

D_MODEL = 2048
BATCH = 8
SEQ = 8192
DEPTH = 1

N_MLA_HEADS = D_MODEL // 256
MLA_NOPE_DIM = 128
ROPE_DIM = 64
MLA_QK_DIM = MLA_NOPE_DIM + ROPE_DIM
MLA_V_DIM = 128
Q_LORA = D_MODEL // 4
KV_LORA = D_MODEL // 8
ROPE_THETA = 10000.0
Q_BLOCK = 128
MLA_WIDTH = N_MLA_HEADS * MLA_V_DIM

N_MLSTM_HEADS = 4
MLSTM_HEAD_DIM = D_MODEL // 8
MLSTM_WIDTH = N_MLSTM_HEADS * MLSTM_HEAD_DIM
MLSTM_CHUNK = 128
CONV_WIDTH = 5
N_GATE_COLS = 4 * N_MLSTM_HEADS

D_MIX = MLA_WIDTH + MLSTM_WIDTH
D_FF = 4 * D_MODEL
IN_SIZES = (Q_LORA, KV_LORA, ROPE_DIM, MLSTM_WIDTH, MLSTM_WIDTH, MLSTM_WIDTH, MLSTM_WIDTH, N_GATE_COLS)
D_IN = sum(IN_SIZES)
IN_SPLIT_POINTS = tuple(int(v) for v in np.cumsum(IN_SIZES)[:-1])
EPS = 1e-6
M_INIT = -1e30

kernel_name = "hybrid_mla_mlstm_adaln_block"


def rmsnorm(x, g):
    xf = x.astype(jnp.float32)
    y = xf * lax.rsqrt(jnp.mean(xf * xf, axis=-1, keepdims=True) + EPS)
    return (y * g.astype(jnp.float32)).astype(x.dtype)


def modulate(h, shift, scale):
    return h * (1 + scale) + shift


def rope(xp, positions):
    half = ROPE_DIM // 2
    freqs = ROPE_THETA ** (-jnp.arange(half, dtype=jnp.float32) / half)
    ang = positions.astype(jnp.float32)[..., None] * freqs
    cos = jnp.cos(ang)[:, :, None, :]
    sin = jnp.sin(ang)[:, :, None, :]
    x1 = xp[..., :half].astype(jnp.float32)
    x2 = xp[..., half:].astype(jnp.float32)
    out = jnp.concatenate([x1 * cos - x2 * sin, x1 * sin + x2 * cos], axis=-1)
    return out.astype(xp.dtype)


def dense_attention_blocks(q, k, v):
    B, S, H, Dq = q.shape
    nb = S // Q_BLOCK
    qb = q.reshape(B, nb, Q_BLOCK, H, Dq).transpose(1, 0, 2, 3, 4)
    scale = Dq ** -0.5

    def one_block(qi):
        s = jnp.einsum('bqhd,bkhd->bhqk', qi, k).astype(jnp.float32) * scale
        p = jax.nn.softmax(s, axis=-1).astype(v.dtype)
        return jnp.einsum('bhqk,bkhd->bqhd', p, v)

    o = lax.map(one_block, qb)
    return o.transpose(1, 0, 2, 3, 4).reshape(B, S, H * v.shape[-1])


def mlstm_chunkwise(q, k, v, log_i, log_f):
    B, H, S, d = q.shape
    L = MLSTM_CHUNK
    nc = S // L
    chunk = lambda t: t.reshape(B, H, nc, L, *t.shape[3:]).swapaxes(0, 2).swapaxes(1, 2)
    qc, kc, vc = chunk(q), chunk(k), chunk(v)
    ic, fc = chunk(log_i), chunk(log_f)
    tril = jnp.tril(jnp.ones((L, L), dtype=bool))

    def step(carry, inp):
        C, n, m = carry
        qt, kt, vt, it, ft = inp
        b = jnp.cumsum(ft, axis=-1)
        log_inter = b + m[..., None]
        logD = jnp.where(tril, b[..., :, None] - b[..., None, :] + it[..., None, :], -jnp.inf)
        m_t = jnp.maximum(log_inter, jnp.max(logD, axis=-1))
        Dm = jnp.exp(logD - m_t[..., None])
        w_inter = jnp.exp(log_inter - m_t)
        scores = jnp.einsum('bhtd,bhsd->bhts', qt, kt) * Dm
        num = jnp.einsum('bhts,bhsd->bhtd', scores, vt) \
            + w_inter[..., None] * jnp.einsum('bhvk,bhtk->bhtv', C, qt)
        den = jnp.sum(scores, axis=-1) + w_inter * jnp.einsum('bhtk,bhk->bht', qt, n)
        h = num / jnp.maximum(jnp.abs(den), jnp.exp(-m_t))[..., None]
        bL = b[..., -1]
        log_w = bL[..., None] - b + it
        m_new = jnp.maximum(bL + m, jnp.max(log_w, axis=-1))
        decay = jnp.exp(bL + m - m_new)
        w = jnp.exp(log_w - m_new[..., None])
        C_new = decay[..., None, None] * C + jnp.einsum('bhs,bhsv,bhsk->bhvk', w, vt, kt)
        n_new = decay[..., None] * n + jnp.einsum('bhs,bhsk->bhk', w, kt)
        return (C_new, n_new, m_new), h

    init = (jnp.zeros((B, H, d, d), jnp.float32),
            jnp.zeros((B, H, d), jnp.float32),
            jnp.full((B, H), M_INIT, jnp.float32))
    _, hs = lax.scan(step, init, (qc, kc, vc, ic, fc))
    return hs.transpose(1, 2, 0, 3, 4).reshape(B, H, S, d)


def _fwd_setup_inputs(seed: int = 0) -> dict:
    key = jax.random.key(seed)
    ks = jax.random.split(key, 24)
    f32 = jnp.float32
    nrm = lambda k, shape, fan_in, s=1.0: jax.random.normal(k, shape, f32) * (s * fan_in ** -0.5)
    gain = lambda k, shape: 1.0 + 0.05 * jax.random.normal(k, shape, f32)
    gate_base = jnp.concatenate([jnp.zeros((N_MLSTM_HEADS,), f32), jnp.linspace(3.0, 6.0, N_MLSTM_HEADS, dtype=f32),
                                 jnp.zeros((N_MLSTM_HEADS,), f32), jnp.linspace(3.0, 6.0, N_MLSTM_HEADS, dtype=f32)])
    return {
        "x": jax.random.normal(ks[0], (BATCH, SEQ, D_MODEL), f32),
        "c": jax.random.normal(ks[1], (BATCH, D_MODEL), f32),
        "positions": jnp.broadcast_to(jnp.arange(SEQ, dtype=jnp.int32), (BATCH, SEQ)),
        "w_ada": nrm(ks[2], (DEPTH, D_MODEL, 6 * D_MODEL), D_MODEL, 0.5),
        "b_ada": 0.02 * jax.random.normal(ks[3], (DEPTH, 6 * D_MODEL), f32),
        "norm_mix_g": gain(ks[4], (DEPTH, D_MODEL)),
        "w_in": nrm(ks[5], (DEPTH, D_MODEL, D_IN), D_MODEL),
        "b_gates": gate_base + 0.1 * jax.random.normal(ks[6], (DEPTH, N_GATE_COLS), f32),
        "conv_w": nrm(ks[7], (DEPTH, CONV_WIDTH, 2 * MLSTM_WIDTH), CONV_WIDTH),
        "conv_b": 0.02 * jax.random.normal(ks[8], (DEPTH, 2 * MLSTM_WIDTH), f32),
        "q_lora_g": gain(ks[9], (DEPTH, Q_LORA)),
        "w_uq": nrm(ks[10], (DEPTH, Q_LORA, N_MLA_HEADS * MLA_QK_DIM), Q_LORA),
        "kv_lora_g": gain(ks[11], (DEPTH, KV_LORA)),
        "w_ukv": nrm(ks[12], (DEPTH, KV_LORA, N_MLA_HEADS * (MLA_NOPE_DIM + MLA_V_DIM)), KV_LORA),
        "q_norm_g": gain(ks[13], (DEPTH, MLA_QK_DIM)),
        "k_norm_g": gain(ks[14], (DEPTH, MLA_QK_DIM)),
        "mlstm_norm_g": gain(ks[15], (DEPTH, N_MLSTM_HEADS, MLSTM_HEAD_DIM)),
        "w_out": nrm(ks[16], (DEPTH, D_MIX, D_MODEL), D_MIX),
        "norm_mlp_g": gain(ks[17], (DEPTH, D_MODEL)),
        "w_ff1": nrm(ks[18], (DEPTH, D_MODEL, D_FF), D_MODEL),
        "w_ff2": nrm(ks[19], (DEPTH, D_FF, D_MODEL), D_FF),
    }


def _fwd_reference(x, c, positions, w_ada, b_ada, norm_mix_g, w_in, b_gates, conv_w, conv_b,
              q_lora_g, w_uq, kv_lora_g, w_ukv, q_norm_g, k_norm_g, mlstm_norm_g,
              w_out, norm_mlp_g, w_ff1, w_ff2):
    B, S, D = x.shape
    H, HM, DM = N_MLA_HEADS, N_MLSTM_HEADS, MLSTM_HEAD_DIM
    for l in range(DEPTH):
        mod = jax.nn.silu(c) @ w_ada[l] + b_ada[l]
        shift1, scale1, gate1, shift2, scale2, gate2 = jnp.split(mod[:, None, :], 6, axis=-1)

        h = modulate(rmsnorm(x, norm_mix_g[l]), shift1, scale1)
        proj = h @ w_in[l]
        c_q, c_kv, k_pe, q_m, k_m, v_m, o_m, g_m = jnp.split(proj, IN_SPLIT_POINTS, axis=-1)

        q = (rmsnorm(c_q, q_lora_g[l]) @ w_uq[l]).reshape(B, S, H, MLA_QK_DIM)
        kv = (rmsnorm(c_kv, kv_lora_g[l]) @ w_ukv[l]).reshape(B, S, H, MLA_NOPE_DIM + MLA_V_DIM)
        k_nope, v = kv[..., :MLA_NOPE_DIM], kv[..., MLA_NOPE_DIM:]
        k_full = jnp.concatenate([k_nope, jnp.broadcast_to(k_pe[:, :, None, :], (B, S, H, ROPE_DIM))], axis=-1)
        q = rmsnorm(q, q_norm_g[l])
        k_full = rmsnorm(k_full, k_norm_g[l])
        q = jnp.concatenate([q[..., :MLA_NOPE_DIM], rope(q[..., MLA_NOPE_DIM:], positions)], axis=-1)
        k_full = jnp.concatenate([k_full[..., :MLA_NOPE_DIM], rope(k_full[..., MLA_NOPE_DIM:], positions)], axis=-1)
        attn_out = dense_attention_blocks(q, k_full, v)

        qk = jnp.concatenate([q_m, k_m], axis=-1)
        qk = lax.conv_general_dilated(qk, conv_w[l][:, None, :].astype(qk.dtype), window_strides=(1,),
                                      padding='SAME', dimension_numbers=('NWC', 'WIO', 'NWC'),
                                      feature_group_count=2 * MLSTM_WIDTH)
        qk = jax.nn.silu(qk + conv_b[l])
        to_heads = lambda t: t.reshape(B, S, HM, DM).transpose(0, 2, 1, 3).astype(jnp.float32)
        qh = to_heads(qk[..., :MLSTM_WIDTH])
        kh = to_heads(qk[..., MLSTM_WIDTH:]) * (DM ** -0.5)
        vh = to_heads(v_m)
        gates = (g_m.astype(jnp.float32) + b_gates[l].astype(jnp.float32)).reshape(B, S, 4, HM)
        gates = gates.transpose(2, 0, 3, 1)
        i_fwd, f_fwd, i_bwd, f_bwd = gates[0], gates[1], gates[2], gates[3]
        h_fwd = mlstm_chunkwise(qh, kh, vh, i_fwd, jax.nn.log_sigmoid(f_fwd))
        flip = lambda t: jnp.flip(t, axis=2)
        h_bwd = flip(mlstm_chunkwise(flip(qh), flip(kh), flip(vh), flip(i_bwd), flip(jax.nn.log_sigmoid(f_bwd))))
        hm = (h_fwd + h_bwd).transpose(0, 2, 1, 3)
        hm = rmsnorm(hm, mlstm_norm_g[l]).astype(x.dtype)
        mlstm_out = (jax.nn.sigmoid(o_m).reshape(B, S, HM, DM) * hm).reshape(B, S, MLSTM_WIDTH)

        mixed = jnp.concatenate([attn_out.astype(x.dtype), mlstm_out], axis=-1) @ w_out[l]
        x = x + gate1 * mixed

        h2 = modulate(rmsnorm(x, norm_mlp_g[l]), shift2, scale2)
        y = jnp.square(jax.nn.relu(h2 @ w_ff1[l])) @ w_ff2[l]
        x = x + gate2 * y
    return x


import jax as _jax
import jax.numpy as _jnp

TWIN_FORMAT = 'train_step'
FWD_PARAMS = ['x', 'c', 'positions', 'w_ada', 'b_ada', 'norm_mix_g', 'w_in', 'b_gates', 'conv_w', 'conv_b', 'q_lora_g', 'w_uq', 'kv_lora_g', 'w_ukv', 'q_norm_g', 'k_norm_g', 'mlstm_norm_g', 'w_out', 'norm_mlp_g', 'w_ff1', 'w_ff2']
TWIN_WEIGHTS = ['w_ada', 'b_ada', 'norm_mix_g', 'w_in', 'b_gates', 'conv_w', 'conv_b', 'q_lora_g', 'w_uq', 'kv_lora_g', 'w_ukv', 'q_norm_g', 'k_norm_g', 'mlstm_norm_g', 'w_out', 'norm_mlp_g', 'w_ff1', 'w_ff2']
TWIN_DIFF_INPUT = 'x'
TWIN_INPUTS = ['x', 'c', 'positions', 'w_ada', 'b_ada', 'norm_mix_g', 'w_in', 'b_gates', 'conv_w', 'conv_b', 'q_lora_g', 'w_uq', 'kv_lora_g', 'w_ukv', 'q_norm_g', 'k_norm_g', 'mlstm_norm_g', 'w_out', 'norm_mlp_g', 'w_ff1', 'w_ff2', 'loss_target', 'm_w_ada', 'm_b_ada', 'm_norm_mix_g', 'm_w_in', 'm_b_gates', 'm_conv_w', 'm_conv_b', 'm_q_lora_g', 'm_w_uq', 'm_kv_lora_g', 'm_w_ukv', 'm_q_norm_g', 'm_k_norm_g', 'm_mlstm_norm_g', 'm_w_out', 'm_norm_mlp_g', 'm_w_ff1', 'm_w_ff2', 'v_w_ada', 'v_b_ada', 'v_norm_mix_g', 'v_w_in', 'v_b_gates', 'v_conv_w', 'v_conv_b', 'v_q_lora_g', 'v_w_uq', 'v_kv_lora_g', 'v_w_ukv', 'v_q_norm_g', 'v_k_norm_g', 'v_mlstm_norm_g', 'v_w_out', 'v_norm_mlp_g', 'v_w_ff1', 'v_w_ff2']
TWIN_OUTPUTS = ['loss', 'grad_x', 'grad_w_ada', 'grad_b_ada', 'grad_norm_mix_g', 'grad_w_in', 'grad_b_gates', 'grad_conv_w', 'grad_conv_b', 'grad_q_lora_g', 'grad_w_uq', 'grad_kv_lora_g', 'grad_w_ukv', 'grad_q_norm_g', 'grad_k_norm_g', 'grad_mlstm_norm_g', 'grad_w_out', 'grad_norm_mlp_g', 'grad_w_ff1', 'grad_w_ff2', 'delta_w_ada', 'delta_b_ada', 'delta_norm_mix_g', 'delta_w_in', 'delta_b_gates', 'delta_conv_w', 'delta_conv_b', 'delta_q_lora_g', 'delta_w_uq', 'delta_kv_lora_g', 'delta_w_ukv', 'delta_q_norm_g', 'delta_k_norm_g', 'delta_mlstm_norm_g', 'delta_w_out', 'delta_norm_mlp_g', 'delta_w_ff1', 'delta_w_ff2', 'new_m_w_ada', 'new_m_b_ada', 'new_m_norm_mix_g', 'new_m_w_in', 'new_m_b_gates', 'new_m_conv_w', 'new_m_conv_b', 'new_m_q_lora_g', 'new_m_w_uq', 'new_m_kv_lora_g', 'new_m_w_ukv', 'new_m_q_norm_g', 'new_m_k_norm_g', 'new_m_mlstm_norm_g', 'new_m_w_out', 'new_m_norm_mlp_g', 'new_m_w_ff1', 'new_m_w_ff2', 'new_v_w_ada', 'new_v_b_ada', 'new_v_norm_mix_g', 'new_v_w_in', 'new_v_b_gates', 'new_v_conv_w', 'new_v_conv_b', 'new_v_q_lora_g', 'new_v_w_uq', 'new_v_kv_lora_g', 'new_v_w_ukv', 'new_v_q_norm_g', 'new_v_k_norm_g', 'new_v_mlstm_norm_g', 'new_v_w_out', 'new_v_norm_mlp_g', 'new_v_w_ff1', 'new_v_w_ff2']
TWIN_LEAF_KINDS = {'loss': 'loss', 'grad_x': 'grad_x', 'grad_w_ada': 'grad_w', 'grad_b_ada': 'grad_w', 'grad_norm_mix_g': 'grad_w', 'grad_w_in': 'grad_w', 'grad_b_gates': 'grad_w', 'grad_conv_w': 'grad_w', 'grad_conv_b': 'grad_w', 'grad_q_lora_g': 'grad_w', 'grad_w_uq': 'grad_w', 'grad_kv_lora_g': 'grad_w', 'grad_w_ukv': 'grad_w', 'grad_q_norm_g': 'grad_w', 'grad_k_norm_g': 'grad_w', 'grad_mlstm_norm_g': 'grad_w', 'grad_w_out': 'grad_w', 'grad_norm_mlp_g': 'grad_w', 'grad_w_ff1': 'grad_w', 'grad_w_ff2': 'grad_w', 'delta_w_ada': 'delta_w', 'delta_b_ada': 'delta_w', 'delta_norm_mix_g': 'delta_w', 'delta_w_in': 'delta_w', 'delta_b_gates': 'delta_w', 'delta_conv_w': 'delta_w', 'delta_conv_b': 'delta_w', 'delta_q_lora_g': 'delta_w', 'delta_w_uq': 'delta_w', 'delta_kv_lora_g': 'delta_w', 'delta_w_ukv': 'delta_w', 'delta_q_norm_g': 'delta_w', 'delta_k_norm_g': 'delta_w', 'delta_mlstm_norm_g': 'delta_w', 'delta_w_out': 'delta_w', 'delta_norm_mlp_g': 'delta_w', 'delta_w_ff1': 'delta_w', 'delta_w_ff2': 'delta_w', 'new_m_w_ada': 'new_m', 'new_m_b_ada': 'new_m', 'new_m_norm_mix_g': 'new_m', 'new_m_w_in': 'new_m', 'new_m_b_gates': 'new_m', 'new_m_conv_w': 'new_m', 'new_m_conv_b': 'new_m', 'new_m_q_lora_g': 'new_m', 'new_m_w_uq': 'new_m', 'new_m_kv_lora_g': 'new_m', 'new_m_w_ukv': 'new_m', 'new_m_q_norm_g': 'new_m', 'new_m_k_norm_g': 'new_m', 'new_m_mlstm_norm_g': 'new_m', 'new_m_w_out': 'new_m', 'new_m_norm_mlp_g': 'new_m', 'new_m_w_ff1': 'new_m', 'new_m_w_ff2': 'new_m', 'new_v_w_ada': 'new_v', 'new_v_b_ada': 'new_v', 'new_v_norm_mix_g': 'new_v', 'new_v_w_in': 'new_v', 'new_v_b_gates': 'new_v', 'new_v_conv_w': 'new_v', 'new_v_conv_b': 'new_v', 'new_v_q_lora_g': 'new_v', 'new_v_w_uq': 'new_v', 'new_v_kv_lora_g': 'new_v', 'new_v_w_ukv': 'new_v', 'new_v_q_norm_g': 'new_v', 'new_v_k_norm_g': 'new_v', 'new_v_mlstm_norm_g': 'new_v', 'new_v_w_out': 'new_v', 'new_v_norm_mlp_g': 'new_v', 'new_v_w_ff1': 'new_v', 'new_v_w_ff2': 'new_v'}


def _forward(args):
    return _fwd_reference(*[args[k] for k in FWD_PARAMS])


def _output_shape():
    def fwd():
        inp = _fwd_setup_inputs(0)
        return _fwd_reference(*[inp[k] for k in FWD_PARAMS])
    out = _jax.eval_shape(fwd)
    return out.shape, out.dtype

N_MICROBATCH = 1
ADAM_LR = 0.001
ADAM_B1 = 0.9
ADAM_B2 = 0.999
ADAM_EPS = 1e-08
ADAM_WD = 0.01
ADAM_STEP = 10
PER_EXAMPLE_BATCH_AXIS = {'x': 0, 'c': 0, 'positions': 0, 'loss_target': 0}
SHARED_INPUTS = []
_WEIGHT_DTYPES = {'w_ada': _jnp.float32, 'b_ada': _jnp.float32, 'norm_mix_g': _jnp.float32, 'w_in': _jnp.float32, 'b_gates': _jnp.float32, 'conv_w': _jnp.float32, 'conv_b': _jnp.float32, 'q_lora_g': _jnp.float32, 'w_uq': _jnp.float32, 'kv_lora_g': _jnp.float32, 'w_ukv': _jnp.float32, 'q_norm_g': _jnp.float32, 'k_norm_g': _jnp.float32, 'mlstm_norm_g': _jnp.float32, 'w_out': _jnp.float32, 'norm_mlp_g': _jnp.float32, 'w_ff1': _jnp.float32, 'w_ff2': _jnp.float32}
MOMENT_SCALE = {'w_ada': 2.386689e+00, 'b_ada': 6.171003e+00, 'norm_mix_g': 6.197965e-02, 'w_in': 2.713153e-01, 'b_gates': 3.162231e-01, 'conv_w': 1.107445e-02, 'conv_b': 1.100331e-02, 'q_lora_g': 1.279612e-02, 'w_uq': 6.868003e-03, 'kv_lora_g': 1.143085e+00, 'w_ukv': 2.223830e-01, 'q_norm_g': 3.475634e-02, 'k_norm_g': 3.475941e-02, 'mlstm_norm_g': 1.051317e+00, 'w_out': 4.271250e-01, 'norm_mlp_g': 1.169840e+01, 'w_ff1': 2.715707e-01, 'w_ff2': 1.188704e+00}


def _to_microbatches(a, axis):
    t = _jnp.moveaxis(a, axis, 0)
    t = t.reshape((N_MICROBATCH, t.shape[0] // N_MICROBATCH) + t.shape[1:])
    return _jnp.moveaxis(t, 1, axis + 1)


def setup_inputs(seed: int = 0) -> dict:
    inp = _fwd_setup_inputs(seed)
    key = _jax.random.fold_in(_jax.random.key(seed), 7919)
    shape, _ = _output_shape()
    out = dict(inp)
    out["loss_target"] = _jax.random.normal(_jax.random.fold_in(key, 0), shape, _jnp.float32)
    for i, name in enumerate(TWIN_WEIGHTS):
        w = inp[name].astype(_jnp.float32)
        if MOMENT_SCALE is None:
            s = _jnp.sqrt(_jnp.mean(_jnp.square(w)) + 1e-30)
        else:
            s = MOMENT_SCALE[name]
        km, kv = _jax.random.split(_jax.random.fold_in(key, i + 1))
        out[name] = w
        out["m_" + name] = s * _jax.random.normal(km, w.shape, _jnp.float32)
        out["v_" + name] = (s * s) * _jax.random.uniform(kv, w.shape, _jnp.float32, 0.5, 1.5)
    if N_MICROBATCH > 1:
        for name, axis in PER_EXAMPLE_BATCH_AXIS.items():
            out[name] = _to_microbatches(out[name], axis)
    return {'x': out['x'], 'c': out['c'], 'positions': out['positions'], 'w_ada': out['w_ada'], 'b_ada': out['b_ada'], 'norm_mix_g': out['norm_mix_g'], 'w_in': out['w_in'], 'b_gates': out['b_gates'], 'conv_w': out['conv_w'], 'conv_b': out['conv_b'], 'q_lora_g': out['q_lora_g'], 'w_uq': out['w_uq'], 'kv_lora_g': out['kv_lora_g'], 'w_ukv': out['w_ukv'], 'q_norm_g': out['q_norm_g'], 'k_norm_g': out['k_norm_g'], 'mlstm_norm_g': out['mlstm_norm_g'], 'w_out': out['w_out'], 'norm_mlp_g': out['norm_mlp_g'], 'w_ff1': out['w_ff1'], 'w_ff2': out['w_ff2'], 'loss_target': out['loss_target'], 'm_w_ada': out['m_w_ada'], 'm_b_ada': out['m_b_ada'], 'm_norm_mix_g': out['m_norm_mix_g'], 'm_w_in': out['m_w_in'], 'm_b_gates': out['m_b_gates'], 'm_conv_w': out['m_conv_w'], 'm_conv_b': out['m_conv_b'], 'm_q_lora_g': out['m_q_lora_g'], 'm_w_uq': out['m_w_uq'], 'm_kv_lora_g': out['m_kv_lora_g'], 'm_w_ukv': out['m_w_ukv'], 'm_q_norm_g': out['m_q_norm_g'], 'm_k_norm_g': out['m_k_norm_g'], 'm_mlstm_norm_g': out['m_mlstm_norm_g'], 'm_w_out': out['m_w_out'], 'm_norm_mlp_g': out['m_norm_mlp_g'], 'm_w_ff1': out['m_w_ff1'], 'm_w_ff2': out['m_w_ff2'], 'v_w_ada': out['v_w_ada'], 'v_b_ada': out['v_b_ada'], 'v_norm_mix_g': out['v_norm_mix_g'], 'v_w_in': out['v_w_in'], 'v_b_gates': out['v_b_gates'], 'v_conv_w': out['v_conv_w'], 'v_conv_b': out['v_conv_b'], 'v_q_lora_g': out['v_q_lora_g'], 'v_w_uq': out['v_w_uq'], 'v_kv_lora_g': out['v_kv_lora_g'], 'v_w_ukv': out['v_w_ukv'], 'v_q_norm_g': out['v_q_norm_g'], 'v_k_norm_g': out['v_k_norm_g'], 'v_mlstm_norm_g': out['v_mlstm_norm_g'], 'v_w_out': out['v_w_out'], 'v_norm_mlp_g': out['v_norm_mlp_g'], 'v_w_ff1': out['v_w_ff1'], 'v_w_ff2': out['v_w_ff2']}


def _loss(weights, diff, rest, loss_target):
    with _jax.named_scope("forward"):
        args = {**rest, TWIN_DIFF_INPUT: diff, **{k: w.astype(_WEIGHT_DTYPES[k]) for k, w in weights.items()}}
        y = _forward(args)
    with _jax.named_scope("loss_head"):
        err = _jnp.square(y.astype(_jnp.float32) - loss_target)
        return 0.5 * _jnp.sum(_jnp.mean(err, axis=-1)) if err.ndim else 0.5 * err


def _adamw(w, g, m, v):
    m = ADAM_B1 * m + (1.0 - ADAM_B1) * g
    v = ADAM_B2 * v + (1.0 - ADAM_B2) * _jnp.square(g)
    m_hat = m / (1.0 - ADAM_B1 ** ADAM_STEP)
    v_hat = v / (1.0 - ADAM_B2 ** ADAM_STEP)
    delta = -ADAM_LR * (m_hat / (_jnp.sqrt(v_hat) + ADAM_EPS) + ADAM_WD * w)
    return delta, m, v


def reference(x, c, positions, w_ada, b_ada, norm_mix_g, w_in, b_gates, conv_w, conv_b, q_lora_g, w_uq, kv_lora_g, w_ukv, q_norm_g, k_norm_g, mlstm_norm_g, w_out, norm_mlp_g, w_ff1, w_ff2, loss_target, m_w_ada, m_b_ada, m_norm_mix_g, m_w_in, m_b_gates, m_conv_w, m_conv_b, m_q_lora_g, m_w_uq, m_kv_lora_g, m_w_ukv, m_q_norm_g, m_k_norm_g, m_mlstm_norm_g, m_w_out, m_norm_mlp_g, m_w_ff1, m_w_ff2, v_w_ada, v_b_ada, v_norm_mix_g, v_w_in, v_b_gates, v_conv_w, v_conv_b, v_q_lora_g, v_w_uq, v_kv_lora_g, v_w_ukv, v_q_norm_g, v_k_norm_g, v_mlstm_norm_g, v_w_out, v_norm_mlp_g, v_w_ff1, v_w_ff2):
    given = dict(x=x, c=c, positions=positions, w_ada=w_ada, b_ada=b_ada, norm_mix_g=norm_mix_g, w_in=w_in, b_gates=b_gates, conv_w=conv_w, conv_b=conv_b, q_lora_g=q_lora_g, w_uq=w_uq, kv_lora_g=kv_lora_g, w_ukv=w_ukv, q_norm_g=q_norm_g, k_norm_g=k_norm_g, mlstm_norm_g=mlstm_norm_g, w_out=w_out, norm_mlp_g=norm_mlp_g, w_ff1=w_ff1, w_ff2=w_ff2, loss_target=loss_target, m_w_ada=m_w_ada, m_b_ada=m_b_ada, m_norm_mix_g=m_norm_mix_g, m_w_in=m_w_in, m_b_gates=m_b_gates, m_conv_w=m_conv_w, m_conv_b=m_conv_b, m_q_lora_g=m_q_lora_g, m_w_uq=m_w_uq, m_kv_lora_g=m_kv_lora_g, m_w_ukv=m_w_ukv, m_q_norm_g=m_q_norm_g, m_k_norm_g=m_k_norm_g, m_mlstm_norm_g=m_mlstm_norm_g, m_w_out=m_w_out, m_norm_mlp_g=m_norm_mlp_g, m_w_ff1=m_w_ff1, m_w_ff2=m_w_ff2, v_w_ada=v_w_ada, v_b_ada=v_b_ada, v_norm_mix_g=v_norm_mix_g, v_w_in=v_w_in, v_b_gates=v_b_gates, v_conv_w=v_conv_w, v_conv_b=v_conv_b, v_q_lora_g=v_q_lora_g, v_w_uq=v_w_uq, v_kv_lora_g=v_kv_lora_g, v_w_ukv=v_w_ukv, v_q_norm_g=v_q_norm_g, v_k_norm_g=v_k_norm_g, v_mlstm_norm_g=v_mlstm_norm_g, v_w_out=v_w_out, v_norm_mlp_g=v_norm_mlp_g, v_w_ff1=v_w_ff1, v_w_ff2=v_w_ff2)
    weights = {n: given[n] for n in TWIN_WEIGHTS}
    shared = {n: given[n] for n in SHARED_INPUTS}
    per_example = {n: given[n] for n in ['x', 'c', 'positions']}
    grad_fn = _jax.value_and_grad(_loss, argnums=(0, 1))

    def one_microbatch(ex, loss_target):
        ex = dict(ex)
        diff = ex.pop(TWIN_DIFF_INPUT)
        return grad_fn(weights, diff, {**shared, **ex}, loss_target)

    if N_MICROBATCH == 1:
        loss, (grad_w, grad_x) = one_microbatch(per_example, given["loss_target"])
    else:
        def body(carry, xs):
            loss_sum, grad_sum = carry
            l_k, (gw_k, gx_k) = one_microbatch(xs[0], xs[1])
            with _jax.named_scope("update"):
                return (loss_sum + l_k, _jax.tree.map(_jnp.add, grad_sum, gw_k)), gx_k

        init = (_jnp.zeros((), _jnp.float32), _jax.tree.map(_jnp.zeros_like, weights))
        (loss, grad_w), grad_x = _jax.lax.scan(body, init, (per_example, given["loss_target"]))
    with _jax.named_scope("update"):
        delta_w, new_m, new_v = {}, {}, {}
        for n in TWIN_WEIGHTS:
            delta_w[n], new_m[n], new_v[n] = _adamw(weights[n], grad_w[n], given["m_" + n], given["v_" + n])
    return (loss, grad_x, *[grad_w[n] for n in TWIN_WEIGHTS], *[delta_w[n] for n in TWIN_WEIGHTS],
            *[new_m[n] for n in TWIN_WEIGHTS], *[new_v[n] for n in TWIN_WEIGHTS])
```

```python
import functools
import math

import numpy as np
import jax
import jax.numpy as jnp
from jax import lax
from jax.experimental import pallas as pl
from jax.experimental.pallas import tpu as pltpu

f32 = jnp.float32
bf16 = jnp.bfloat16

N_DEV = 8
AXES = ("x", "y", "c")
MESH = pl.DeviceIdType.MESH

NOPE = 128
ROPE = 64
HALF = ROPE // 2
QK_DIM = NOPE + ROPE
QK_PAD = 256
V_DIM = 128
ROPE_THETA = 10000.0
CHUNK = 128
CONV_W = 5
N_GATES = 16
EPS = 1e-6
M_INIT = -1e30

ADAM_LR, ADAM_B1, ADAM_B2, ADAM_EPS, ADAM_WD, ADAM_STEP = 0.001, 0.9, 0.999, 1e-08, 0.01, 10

LANE = 128
VMEM_LIMIT = 56 * 1024 * 1024


def _cp(sem=None, vmem=VMEM_LIMIT):
    return pltpu.CompilerParams(dimension_semantics=sem, vmem_limit_bytes=vmem)


def _pick(n, target):
    best = None
    t = LANE
    while t <= min(n, target):
        if n % t == 0:
            best = t
        t += LANE
    return best if best is not None else n


def _pick_rows(n, target):
    t = min(n, target)
    while n % t:
        t -= 8
    return t


def _make_dots(cast, precision):
    def dg(a, b, ca, cb):
        if cast is not None:
            a = a.astype(cast)
            b = b.astype(cast)
        return lax.dot_general(a, b, (((ca,), (cb,)), ((), ())), precision=precision, preferred_element_type=f32)

    @jax.custom_vjp
    def nn(a, b):
        return dg(a, b, 1, 0)

    def nn_f(a, b):
        return dg(a, b, 1, 0), (a, b)

    def nn_b(res, g):
        a, b = res
        return dg(g, b, 1, 1).astype(a.dtype), dg(a, g, 0, 0).astype(b.dtype)

    nn.defvjp(nn_f, nn_b)

    @jax.custom_vjp
    def nt(a, b):
        return dg(a, b, 1, 1)

    def nt_f(a, b):
        return dg(a, b, 1, 1), (a, b)

    def nt_b(res, g):
        a, b = res
        return dg(g, b, 1, 0).astype(a.dtype), dg(g, a, 0, 0).astype(b.dtype)

    nt.defvjp(nt_f, nt_b)

    @jax.custom_vjp
    def tn(a, b):
        return dg(a, b, 0, 0)

    def tn_f(a, b):
        return dg(a, b, 0, 0), (a, b)

    def tn_b(res, g):
        a, b = res
        return dg(b, g, 1, 1).astype(a.dtype), dg(a, g, 1, 0).astype(b.dtype)

    tn.defvjp(tn_f, tn_b)
    return nn, nt, tn


bdot, bdot_nt, bdot_tn = _make_dots(bf16, None)
hdot, hdot_nt, hdot_tn = _make_dots(None, lax.Precision.HIGHEST)


def _silu(x):
    return x * jax.nn.sigmoid(x)


def _rms(x, n):
    return x * lax.rsqrt(jnp.sum(x * x, axis=-1, keepdims=True) * (1.0 / n) + EPS)


def _place():
    return lax.axis_index("x"), lax.axis_index("y"), lax.axis_index("c")


def all_gather(ops, name):
    n = len(ops)

    def body(*refs):
        ins, outs = refs[:n], refs[n:2 * n]
        send_sems, recv_sems, local_sems = refs[2 * n:]
        x, y, c = _place()
        me, sibling = (x, y, c), (x, y, 1 - c)
        chips = [(1 - x, y), (x, 1 - y), (1 - x, 1 - y)]

        def slot(o, p):
            return outs[o].at[4 * p[0] + 2 * p[1] + p[2]]

        def copy(o, k, block, to, src=None):
            dst = slot(o, block)
            return pltpu.make_async_remote_copy(
                src_ref=dst if src is None else src, dst_ref=dst,
                send_sem=send_sems.at[o, k], recv_sem=recv_sems.at[o, k],
                device_id=to, device_id_type=MESH)

        started = []
        for o in range(n):
            mine = pltpu.make_async_copy(ins[o], slot(o, me), local_sems.at[o])
            mine.start()
            started.append(mine)
        sends = []
        for o in range(n):
            first = [copy(o, 0, me, sibling, src=ins[o])]
            first += [copy(o, 1 + j, me, (*chip, c), src=ins[o]) for j, chip in enumerate(chips)]
            for cp in first:
                cp.start()
            sends += first
        for o in range(n):
            for j, chip in enumerate(chips):
                copy(o, 1 + j, (*chip, c), me).wait_recv()
                passed = copy(o, 4 + j, (*chip, c), sibling)
                passed.start()
                sends.append(passed)
        for o in range(n):
            copy(o, 0, sibling, me).wait_recv()
            for j, chip in enumerate(chips):
                copy(o, 4 + j, (*chip, 1 - c), me).wait_recv()
        for cp in sends:
            cp.wait_send()
        for mine in started:
            mine.wait()

    anyspec = pl.BlockSpec(memory_space=pl.ANY)
    return pl.pallas_call(
        body, name=name,
        out_shape=[jax.ShapeDtypeStruct((N_DEV,) + o.shape, o.dtype) for o in ops],
        in_specs=[anyspec] * n, out_specs=[anyspec] * n,
        scratch_shapes=[pltpu.SemaphoreType.DMA((n, 7)), pltpu.SemaphoreType.DMA((n, 7)),
                        pltpu.SemaphoreType.DMA((n,))],
    )(*ops)


def pair_exchange(g, name):
    _, R, C = g.shape

    def body(g_ref, out_ref, send_sems, recv_sems):
        x, y, c = _place()
        sibling = (x, y, 1 - c)
        cps = []
        for q in range(4):
            cp = pltpu.make_async_remote_copy(
                src_ref=g_ref.at[2 * q + (1 - c)], dst_ref=out_ref.at[q],
                send_sem=send_sems.at[q], recv_sem=recv_sems.at[q],
                device_id=sibling, device_id_type=MESH)
            cp.start()
            cps.append(cp)
        for cp in cps:
            cp.wait_recv()
        for cp in cps:
            cp.wait_send()

    anyspec = pl.BlockSpec(memory_space=pl.ANY)
    return pl.pallas_call(
        body, name=name, out_shape=jax.ShapeDtypeStruct((4, R, C), g.dtype),
        in_specs=[anyspec], out_specs=anyspec,
        scratch_shapes=[pltpu.SemaphoreType.DMA((4,)), pltpu.SemaphoreType.DMA((4,))],
    )(g)


def chip_exchange(p, name):
    _, R, C = p.shape

    def body(p_ref, out_ref, send_sems, recv_sems):
        x, y, c = _place()
        chips = [(1 - x, y), (x, 1 - y), (1 - x, 1 - y)]
        cps = []
        for j, chip in enumerate(chips):
            cp = pltpu.make_async_remote_copy(
                src_ref=p_ref.at[j], dst_ref=out_ref.at[j],
                send_sem=send_sems.at[j], recv_sem=recv_sems.at[j],
                device_id=(*chip, c), device_id_type=MESH)
            cp.start()
            cps.append(cp)
        for cp in cps:
            cp.wait_recv()
        for cp in cps:
            cp.wait_send()

    anyspec = pl.BlockSpec(memory_space=pl.ANY)
    return pl.pallas_call(
        body, name=name, out_shape=jax.ShapeDtypeStruct((3, R, C), p.dtype),
        in_specs=[anyspec], out_specs=anyspec,
        scratch_shapes=[pltpu.SemaphoreType.DMA((3,)), pltpu.SemaphoreType.DMA((3,))],
    )(p)


def chip_partials(g, recv, slots, name):
    _, R, C = g.shape
    tr = _pick_rows(R, 512)

    def body(s_ref, a_ref, b_ref, o_ref):
        o_ref[...] = (a_ref[...].astype(f32) + b_ref[...].astype(f32)).astype(o_ref.dtype)

    grid_spec = pltpu.PrefetchScalarGridSpec(
        num_scalar_prefetch=1, grid=(3, R // tr),
        in_specs=[pl.BlockSpec((None, tr, C), lambda j, i, s: (s[j], i, 0)),
                  pl.BlockSpec((None, tr, C), lambda j, i, s: (s[j] // 2, i, 0))],
        out_specs=pl.BlockSpec((None, tr, C), lambda j, i, s: (j, i, 0)))
    return pl.pallas_call(body, name=name, grid_spec=grid_spec,
                          out_shape=jax.ShapeDtypeStruct((3, R, C), g.dtype),
                          compiler_params=_cp(("arbitrary", "arbitrary")))(slots, g, recv)


def adamw(parts, w, m, v, name, rows=256):
    R, C = w.shape
    tr = _pick_rows(R, rows)
    npart = len(parts)
    c1 = 1.0 - ADAM_B1 ** ADAM_STEP
    c2 = 1.0 - ADAM_B2 ** ADAM_STEP

    def body(*refs):
        p_refs = refs[:npart]
        w_ref, m_ref, v_ref, g_out, d_out, m_out, v_out = refs[npart:]
        g = p_refs[0][...].astype(f32)
        for p in p_refs[1:]:
            g = g + p[...].astype(f32)
        mn = ADAM_B1 * m_ref[...] + (1.0 - ADAM_B1) * g
        vn = ADAM_B2 * v_ref[...] + (1.0 - ADAM_B2) * (g * g)
        m_hat = mn / c1
        v_hat = vn / c2
        g_out[...] = g
        d_out[...] = -ADAM_LR * (m_hat / (jnp.sqrt(v_hat) + ADAM_EPS) + ADAM_WD * w_ref[...])
        m_out[...] = mn
        v_out[...] = vn

    spec = pl.BlockSpec((tr, C), lambda i: (i, 0))
    return pl.pallas_call(
        body, name=name, grid=(R // tr,),
        in_specs=[spec] * (npart + 3), out_specs=[spec] * 4,
        out_shape=[jax.ShapeDtypeStruct((R, C), f32)] * 4,
        compiler_params=_cp(("arbitrary",)))(*parts, w, m, v)


def mm(a, b, *, name, ta=False, tb=False, a_fn=None, epi=None, extras=(), out_dtype=f32,
       tm=512, tn=1024, tk=1024):
    K, M = a.shape if ta else a.shape[::-1]
    N, K2 = b.shape if tb else b.shape[::-1]
    assert K == K2, (a.shape, b.shape, ta, tb)
    tm, tn, tk = _pick(M, tm), _pick(N, tn), _pick(K, tk)
    nk = K // tk
    ne = len(extras)
    dims = (((0 if ta else 1,), (1 if tb else 0,)), ((), ()))

    def body(a_ref, b_ref, *rest):
        e_refs, o_ref, acc = rest[:ne], rest[ne], rest[ne + 1]
        k = pl.program_id(2)

        @pl.when(k == 0)
        def _():
            acc[...] = jnp.zeros_like(acc)

        av = a_ref[...]
        if a_fn is not None:
            av = a_fn(av.astype(f32))
        acc[...] += lax.dot_general(av.astype(bf16), b_ref[...].astype(bf16), dims, preferred_element_type=f32)

        @pl.when(k == nk - 1)
        def _():
            r = acc[...]
            if epi is not None:
                r = epi(r, *[e[...] for e in e_refs])
            o_ref[...] = r.astype(o_ref.dtype)

    a_spec = pl.BlockSpec((tk, tm), lambda i, j, k: (k, i)) if ta else pl.BlockSpec((tm, tk), lambda i, j, k: (i, k))
    b_spec = pl.BlockSpec((tn, tk), lambda i, j, k: (j, k)) if tb else pl.BlockSpec((tk, tn), lambda i, j, k: (k, j))
    o_spec = pl.BlockSpec((tm, tn), lambda i, j, k: (i, j))
    return pl.pallas_call(
        body, name=name, grid=(M // tm, N // tn, nk),
        in_specs=[a_spec, b_spec] + [o_spec] * ne, out_specs=o_spec,
        out_shape=jax.ShapeDtypeStruct((M, N), out_dtype),
        scratch_shapes=[pltpu.VMEM((tm, tn), f32)],
        compiler_params=_cp(("parallel", "parallel", "arbitrary")))(a, b, *extras)


class Row:
    def __init__(self, arr, width=None, col=0, lead=None, diff=True):
        self.arr, self.col, self.lead, self.diff = arr, col, lead, diff
        self.width = arr.shape[-1] if width is None else width

    def spec(self, t):
        col, lead = self.col, self.lead
        if lead is None:
            return pl.BlockSpec((t, self.width), lambda i: (i, col))
        return pl.BlockSpec((None, t, self.width), lambda i: (lead, i, col))


def _whole(p):
    return pl.BlockSpec(p.shape, lambda i: (0,) * p.ndim)


def rowwise(fn, rows, params, outs, *, n_rows, tile, name):
    t = _pick_rows(n_rows, tile)
    nr, npar, no = len(rows), len(params), len(outs)

    def body(*refs):
        r_refs, p_refs, o_refs = refs[:nr], refs[nr:nr + npar], refs[nr + npar:]
        res = fn(*[r[...].astype(f32) for r in r_refs], *[p[...] for p in p_refs])
        for o_ref, val in zip(o_refs, res):
            o_ref[...] = val.astype(o_ref.dtype)

    return pl.pallas_call(
        body, name=name, grid=(n_rows // t,),
        in_specs=[r.spec(t) for r in rows] + [_whole(p) for p in params],
        out_specs=[pl.BlockSpec((t, w), lambda i: (i, 0)) for w, _ in outs],
        out_shape=[jax.ShapeDtypeStruct((n_rows, w), dt) for w, dt in outs],
        compiler_params=_cp(("arbitrary",)))(*[r.arr for r in rows], *params)


def rowwise_vjp(fn, rows, params, cts, *, n_rows, tile, name, row_grad_dtypes=None, param_diff=None):
    t = _pick_rows(n_rows, tile)
    nr, npar, nc = len(rows), len(params), len(cts)
    param_diff = [True] * npar if param_diff is None else param_diff
    d_rows = [k for k, r in enumerate(rows) if r.diff]
    d_pars = [k for k in range(npar) if param_diff[k]]
    row_grad_dtypes = [f32] * len(d_rows) if row_grad_dtypes is None else row_grad_dtypes

    def body(*refs):
        r_refs, p_refs = refs[:nr], refs[nr:nr + npar]
        c_refs = refs[nr + npar:nr + npar + nc]
        dr_refs = refs[nr + npar + nc:nr + npar + nc + len(d_rows)]
        dp_refs = refs[nr + npar + nc + len(d_rows):]
        rv = [r[...].astype(f32) for r in r_refs]
        pv = [p[...] for p in p_refs]

        def g(*dvals):
            full_r, full_p = list(rv), list(pv)
            for k, val in zip(d_rows, dvals[:len(d_rows)]):
                full_r[k] = val
            for k, val in zip(d_pars, dvals[len(d_rows):]):
                full_p[k] = val
            return tuple(fn(*full_r, *full_p))

        prim = [rv[k] for k in d_rows] + [pv[k].astype(f32) for k in d_pars]
        _, pull = jax.vjp(g, *prim)
        grads = pull(tuple(c[...].astype(f32) for c in c_refs))
        for ref, val in zip(dr_refs, grads[:len(d_rows)]):
            ref[...] = val.astype(ref.dtype)

        @pl.when(pl.program_id(0) == 0)
        def _():
            for ref in dp_refs:
                ref[...] = jnp.zeros_like(ref)

        for ref, val in zip(dp_refs, grads[len(d_rows):]):
            ref[...] += val

    out_specs = [pl.BlockSpec((t, rows[k].width), lambda i: (i, 0)) for k in d_rows]
    out_specs += [_whole(params[k]) for k in d_pars]
    out_shape = [jax.ShapeDtypeStruct((n_rows, rows[k].width), dt) for k, dt in zip(d_rows, row_grad_dtypes)]
    out_shape += [jax.ShapeDtypeStruct(params[k].shape, f32) for k in d_pars]
    res = pl.pallas_call(
        body, name=name, grid=(n_rows // t,),
        in_specs=[r.spec(t) for r in rows] + [_whole(p) for p in params] + [c.spec(t) for c in cts],
        out_specs=out_specs, out_shape=out_shape,
        compiler_params=_cp(("arbitrary",)))(*[r.arr for r in rows], *params, *[c.arr for c in cts])
    return res[:len(d_rows)], res[len(d_rows):]


def f_norm_mod(x, g, shift, scale):
    return (_rms(x, x.shape[-1]) * g * (1.0 + scale) + shift,)


def f_norm_mod_thru(x, g, shift, scale):
    return f_norm_mod(x, g, shift, scale) + (x,)


def f_resid_norm_mod(x, mixed, gate1, g2, shift2, scale2):
    x1 = x + gate1 * mixed
    return (x1,) + f_norm_mod(x1, g2, shift2, scale2)


def _rope_rot():
    i = lax.broadcasted_iota(jnp.int32, (LANE, LANE), 0)
    j = lax.broadcasted_iota(jnp.int32, (LANE, LANE), 1)
    neg = jnp.where((i == j + HALF) & (j < HALF), -1.0, 0.0)
    pos = jnp.where((i == j - HALF) & (j >= HALF) & (j < ROPE), 1.0, 0.0)
    return (neg + pos).astype(f32)


def make_f_mla_prep(n_heads):
    def fn(cq, ckv, kpe, pos, gq, gkv, gqn, gkn, w_uq, w_ukv, freqs):
        rot = _rope_rot()
        ang = pos * freqs
        cos, sin = jnp.cos(ang), jnp.sin(ang)

        def rope(u):
            return u * cos + hdot(u, rot) * sin

        qraw = bdot(_rms(cq, cq.shape[-1]) * gq, w_uq)
        kv = bdot(_rms(ckv, ckv.shape[-1]) * gkv, w_ukv)
        kpe_ss = jnp.sum(kpe * kpe, axis=-1, keepdims=True)
        qs, ks = [], []
        for h in range(n_heads):
            qh = _rms(qraw[:, h * QK_PAD:(h + 1) * QK_PAD], QK_DIM) * gqn
            qs += [qh[:, :NOPE], rope(qh[:, NOPE:])]
            kn = kv[:, h * NOPE:(h + 1) * NOPE]
            r = lax.rsqrt((jnp.sum(kn * kn, axis=-1, keepdims=True) + kpe_ss) * (1.0 / QK_DIM) + EPS)
            ks += [kn * r * gkn[:, :NOPE], rope(kpe * r * gkn[:, NOPE:])]
        return jnp.concatenate(qs, axis=-1), jnp.concatenate(ks, axis=-1), kv[:, n_heads * NOPE:]
    return fn


def make_f_mlstm_post(n_heads, dm):
    def fn(hf, hb, o, g):
        hm = hf + hb
        outs = []
        for h in range(n_heads):
            sl = slice(h * dm, (h + 1) * dm)
            outs.append(jax.nn.sigmoid(o[:, sl]) * (_rms(hm[:, sl], dm) * g[:, sl]))
        return (jnp.concatenate(outs, axis=-1),)
    return fn


def f_add_pairs(a0, a1, b0, b1):
    return (jnp.concatenate([a0 + a1, b0 + b1], axis=-1),)


def f_add(a, b):
    return (a + b,)


def loss_head(x1, y, target, gate2, name, tile=256):
    S, D = x1.shape
    t = _pick_rows(S, tile)

    def body(x1_ref, y_ref, t_ref, g_ref, loss_ref, dout_ref, dy_ref, dgate_ref):
        @pl.when(pl.program_id(0) == 0)
        def _():
            loss_ref[...] = jnp.zeros_like(loss_ref)
            dgate_ref[...] = jnp.zeros_like(dgate_ref)

        yv, gv = y_ref[...], g_ref[...]
        e = x1_ref[...] + gv * yv - t_ref[...]
        loss_ref[...] += 0.5 * jnp.sum(jnp.sum(e * e, axis=-1, keepdims=True) * (1.0 / D), axis=0, keepdims=True)
        d_out = e * (1.0 / D)
        dout_ref[...] = d_out
        dy_ref[...] = (d_out * gv).astype(dy_ref.dtype)
        dgate_ref[...] += jnp.sum(d_out * yv, axis=0, keepdims=True)

    row = pl.BlockSpec((t, D), lambda i: (i, 0))
    return pl.pallas_call(
        body, name=name, grid=(S // t,),
        in_specs=[row, row, row, pl.BlockSpec((1, D), lambda i: (0, 0))],
        out_specs=[pl.BlockSpec((1, 1), lambda i: (0, 0)), row, row, pl.BlockSpec((1, D), lambda i: (0, 0))],
        out_shape=[jax.ShapeDtypeStruct((1, 1), f32), jax.ShapeDtypeStruct((S, D), f32),
                   jax.ShapeDtypeStruct((S, D), bf16), jax.ShapeDtypeStruct((1, D), f32)],
        compiler_params=_cp(("arbitrary",)))(x1, y, target, gate2)


def ada_fwd(c_all, w_blk, b_blk, name):
    B, D = c_all.shape
    N = w_blk.shape[1]
    tn = _pick(N, 512)

    def body(c_ref, w_ref, b_ref, o_ref):
        o_ref[...] = bdot(_silu(c_ref[...]), w_ref[...]) + b_ref[...]

    return pl.pallas_call(
        body, name=name, grid=(N // tn,),
        in_specs=[pl.BlockSpec((B, D), lambda j: (0, 0)), pl.BlockSpec((D, tn), lambda j: (0, j)),
                  pl.BlockSpec((1, tn), lambda j: (0, j))],
        out_specs=pl.BlockSpec((B, tn), lambda j: (0, j)),
        out_shape=jax.ShapeDtypeStruct((B, N), f32), compiler_params=_cp(("arbitrary",)))(c_all, w_blk, b_blk)


def ada_wgrad(c_all, dmod_blk, name):
    B, D = c_all.shape
    N = dmod_blk.shape[1]
    tn = _pick(N, 512)

    def body(c_ref, d_ref, o_ref):
        o_ref[...] = hdot_tn(_silu(c_ref[...]), d_ref[...])

    return pl.pallas_call(
        body, name=name, grid=(N // tn,),
        in_specs=[pl.BlockSpec((B, D), lambda j: (0, 0)), pl.BlockSpec((B, tn), lambda j: (0, j))],
        out_specs=pl.BlockSpec((D, tn), lambda j: (0, j)),
        out_shape=jax.ShapeDtypeStruct((D, N), f32), compiler_params=_cp(("arbitrary",)))(c_all, dmod_blk)


def _nt(a, b):
    return lax.dot_general(a, b, (((1,), (1,)), ((), ())), preferred_element_type=f32)


def _tn(a, b):
    return lax.dot_general(a, b, (((0,), (0,)), ((), ())), preferred_element_type=f32)


def flash_fwd(q, k, v, n_heads, scale, name, tq=512, tk=512):
    S = q.shape[0]
    tq, tk = _pick(S, tq), _pick(S, tk)
    nk = S // tk

    def body(q_ref, k_ref, v_ref, o_ref, lse_ref, m_sc, l_sc, acc_sc):
        j = pl.program_id(2)

        @pl.when(j == 0)
        def _():
            m_sc[...] = jnp.full_like(m_sc, -jnp.inf)
            l_sc[...] = jnp.zeros_like(l_sc)
            acc_sc[...] = jnp.zeros_like(acc_sc)

        s = _nt(q_ref[...], k_ref[...]) * scale
        m_new = jnp.maximum(m_sc[...], jnp.max(s, axis=-1, keepdims=True))
        alpha = jnp.exp(m_sc[...] - m_new)
        p = jnp.exp(s - m_new)
        l_sc[...] = alpha * l_sc[...] + jnp.sum(p, axis=-1, keepdims=True)
        acc_sc[...] = alpha * acc_sc[...] + jnp.dot(p.astype(bf16), v_ref[...], preferred_element_type=f32)
        m_sc[...] = m_new

        @pl.when(j == nk - 1)
        def _():
            o_ref[...] = (acc_sc[...] / l_sc[...]).astype(o_ref.dtype)
            lse_ref[...] = m_sc[...] + jnp.log(l_sc[...])

    return pl.pallas_call(
        body, name=name, grid=(n_heads, S // tq, nk),
        in_specs=[pl.BlockSpec((tq, QK_PAD), lambda h, i, j: (i, h)),
                  pl.BlockSpec((tk, QK_PAD), lambda h, i, j: (j, h)),
                  pl.BlockSpec((tk, V_DIM), lambda h, i, j: (j, h))],
        out_specs=[pl.BlockSpec((tq, V_DIM), lambda h, i, j: (i, h)),
                   pl.BlockSpec((None, tq, 1), lambda h, i, j: (h, i, 0))],
        out_shape=[jax.ShapeDtypeStruct((S, n_heads * V_DIM), bf16), jax.ShapeDtypeStruct((n_heads, S, 1), f32)],
        scratch_shapes=[pltpu.VMEM((tq, 1), f32), pltpu.VMEM((tq, 1), f32), pltpu.VMEM((tq, V_DIM), f32)],
        compiler_params=_cp(("parallel", "parallel", "arbitrary")))(q, k, v)


def _flash_p_ds(q, k, v, do, o, lse, scale):
    s = _nt(q, k) * scale
    p = jnp.exp(s - lse)
    dof = do.astype(f32)
    delta = jnp.sum(dof * o.astype(f32), axis=-1, keepdims=True)
    dp = _nt(do.astype(bf16), v)
    ds = p * (dp - delta) * scale
    return p, ds


def flash_bwd_dkv(q, k, v, o, lse, do, do_col0, n_heads, scale, name, tq=512, tk=512):
    S = q.shape[0]
    tq, tk = _pick(S, tq), _pick(S, tk)

    def body(q_ref, k_ref, v_ref, o_ref, lse_ref, do_ref, dk_ref, dv_ref):
        @pl.when(pl.program_id(2) == 0)
        def _():
            dk_ref[...] = jnp.zeros_like(dk_ref)
            dv_ref[...] = jnp.zeros_like(dv_ref)

        qv, dov = q_ref[...], do_ref[...]
        p, ds = _flash_p_ds(qv, k_ref[...], v_ref[...], dov, o_ref[...], lse_ref[...], scale)
        dv_ref[...] += _tn(p.astype(bf16), dov.astype(bf16))
        dk_ref[...] += _tn(ds.astype(bf16), qv)

    return pl.pallas_call(
        body, name=name, grid=(n_heads, S // tk, S // tq),
        in_specs=[pl.BlockSpec((tq, QK_PAD), lambda h, j, i: (i, h)),
                  pl.BlockSpec((tk, QK_PAD), lambda h, j, i: (j, h)),
                  pl.BlockSpec((tk, V_DIM), lambda h, j, i: (j, h)),
                  pl.BlockSpec((tq, V_DIM), lambda h, j, i: (i, h)),
                  pl.BlockSpec((None, tq, 1), lambda h, j, i: (h, i, 0)),
                  pl.BlockSpec((tq, V_DIM), lambda h, j, i: (i, do_col0 + h))],
        out_specs=[pl.BlockSpec((tk, QK_PAD), lambda h, j, i: (j, h)),
                   pl.BlockSpec((tk, V_DIM), lambda h, j, i: (j, h))],
        out_shape=[jax.ShapeDtypeStruct((S, n_heads * QK_PAD), f32), jax.ShapeDtypeStruct((S, n_heads * V_DIM), f32)],
        compiler_params=_cp(("parallel", "parallel", "arbitrary")))(q, k, v, o, lse, do)


def flash_bwd_dq(q, k, v, o, lse, do, do_col0, n_heads, scale, name, tq=512, tk=512):
    S = q.shape[0]
    tq, tk = _pick(S, tq), _pick(S, tk)

    def body(q_ref, k_ref, v_ref, o_ref, lse_ref, do_ref, dq_ref):
        @pl.when(pl.program_id(2) == 0)
        def _():
            dq_ref[...] = jnp.zeros_like(dq_ref)

        kv_ = k_ref[...]
        _, ds = _flash_p_ds(q_ref[...], kv_, v_ref[...], do_ref[...], o_ref[...], lse_ref[...], scale)
        dq_ref[...] += jnp.dot(ds.astype(bf16), kv_, preferred_element_type=f32)

    return pl.pallas_call(
        body, name=name, grid=(n_heads, S // tq, S // tk),
        in_specs=[pl.BlockSpec((tq, QK_PAD), lambda h, i, j: (i, h)),
                  pl.BlockSpec((tk, QK_PAD), lambda h, i, j: (j, h)),
                  pl.BlockSpec((tk, V_DIM), lambda h, i, j: (j, h)),
                  pl.BlockSpec((tq, V_DIM), lambda h, i, j: (i, h)),
                  pl.BlockSpec((None, tq, 1), lambda h, i, j: (h, i, 0)),
                  pl.BlockSpec((tq, V_DIM), lambda h, i, j: (i, do_col0 + h))],
        out_specs=pl.BlockSpec((tq, QK_PAD), lambda h, i, j: (i, h)),
        out_shape=jax.ShapeDtypeStruct((S, n_heads * QK_PAD), f32),
        compiler_params=_cp(("parallel", "parallel", "arbitrary")))(q, k, v, o, lse, do)


def _shifted(prev, cur, nxt, k, first, last):
    if k == 0:
        return cur
    t = cur.shape[0]
    r = lax.broadcasted_iota(jnp.int32, cur.shape, 0)
    if k < 0:
        body = pltpu.roll(cur, -k, 0)
        edge = jnp.where(first, 0.0, pltpu.roll(prev, -k, 0))
        return jnp.where(r < -k, edge, body)
    body = pltpu.roll(cur, t - k, 0)
    edge = jnp.where(last, 0.0, pltpu.roll(nxt, t - k, 0))
    return jnp.where(r >= t - k, edge, body)


def _halo_specs(t, width, n_tiles, col=0):
    return [pl.BlockSpec((t, width), lambda i: (jnp.maximum(i - 1, 0), col)),
            pl.BlockSpec((t, width), lambda i: (i, col)),
            pl.BlockSpec((t, width), lambda i: (jnp.minimum(i + 1, n_tiles - 1), col))]


def conv_fwd(proj, width, w, b, name, tile=256):
    S = proj.shape[0]
    t = _pick_rows(S, tile)
    n_tiles = S // t

    def body(p_ref, c_ref, n_ref, w_ref, b_ref, z_ref):
        i = pl.program_id(0)
        first, last = i == 0, i == n_tiles - 1
        prev, cur, nxt = p_ref[...], c_ref[...], n_ref[...]
        z = b_ref[...] + jnp.zeros_like(cur)
        for j in range(CONV_W):
            z = z + w_ref[j:j + 1, :] * _shifted(prev, cur, nxt, j - CONV_W // 2, first, last)
        z_ref[...] = z

    return pl.pallas_call(
        body, name=name, grid=(n_tiles,),
        in_specs=_halo_specs(t, width, n_tiles) + [_whole(w), _whole(b)],
        out_specs=pl.BlockSpec((t, width), lambda i: (i, 0)),
        out_shape=jax.ShapeDtypeStruct((S, width), f32),
        compiler_params=_cp(("arbitrary",)))(proj, proj, proj, w, b)


def conv_bwd(dz, proj, width, w, name, tile=256):
    S = proj.shape[0]
    t = _pick_rows(S, tile)
    n_tiles = S // t

    def body(dp_ref, dc_ref, dn_ref, up_ref, uc_ref, un_ref, w_ref, du_ref, dw_ref, db_ref):
        i = pl.program_id(0)
        first, last = i == 0, i == n_tiles - 1

        @pl.when(first)
        def _():
            dw_ref[...] = jnp.zeros_like(dw_ref)
            db_ref[...] = jnp.zeros_like(db_ref)

        dprev, dcur, dnxt = dp_ref[...], dc_ref[...], dn_ref[...]
        uprev, ucur, unxt = up_ref[...], uc_ref[...], un_ref[...]
        du = jnp.zeros_like(dcur)
        for j in range(CONV_W):
            k = j - CONV_W // 2
            du = du + w_ref[j:j + 1, :] * _shifted(dprev, dcur, dnxt, -k, first, last)
            dw_ref[j:j + 1, :] += jnp.sum(dcur * _shifted(uprev, ucur, unxt, k, first, last), axis=0, keepdims=True)
        du_ref[...] = du
        db_ref[...] += jnp.sum(dcur, axis=0, keepdims=True)

    return pl.pallas_call(
        body, name=name, grid=(n_tiles,),
        in_specs=_halo_specs(t, width, n_tiles) + _halo_specs(t, width, n_tiles) + [_whole(w)],
        out_specs=[pl.BlockSpec((t, width), lambda i: (i, 0)), pl.BlockSpec((8, width), lambda i: (0, 0)),
                   pl.BlockSpec((1, width), lambda i: (0, 0))],
        out_shape=[jax.ShapeDtypeStruct((S, width), f32), jax.ShapeDtypeStruct((8, width), f32),
                   jax.ShapeDtypeStruct((1, width), f32)],
        compiler_params=_cp(("arbitrary",)))(dz, dz, dz, proj, proj, proj, w)


def _mlstm_step(dm, d, C, n, m, zq, zk, v, ic, fc, ir, fr, bi, bf_):
    L = zq.shape[0]
    q = _silu(zq)
    k = _silu(zk) * (dm ** -0.5)
    i_c, f_c = ic + bi, jax.nn.log_sigmoid(fc + bf_)
    i_r, f_r = ir + bi, jax.nn.log_sigmoid(fr + bf_)
    r = lax.broadcasted_iota(jnp.int32, (L, L), 0)
    c = lax.broadcasted_iota(jnp.int32, (L, L), 1)
    sgn = jnp.where(d == 0, r - c, c - r)
    mask = sgn >= 0
    b_c = jnp.sum(jnp.where(mask, f_r, 0.0), axis=-1, keepdims=True)
    b_r = jnp.sum(jnp.where(sgn <= 0, f_c, 0.0), axis=0, keepdims=True)
    log_inter = b_c + m
    logD = jnp.where(mask, b_c - b_r + i_r, -jnp.inf)
    m_t = jnp.maximum(log_inter, jnp.max(logD, axis=-1, keepdims=True))
    Dm = jnp.exp(logD - m_t)
    w_inter = jnp.exp(log_inter - m_t)
    scores = bdot_nt(q, k) * Dm
    num = bdot(scores, v) + w_inter * bdot_nt(q, C)
    den = jnp.sum(scores, axis=-1, keepdims=True) + w_inter * jnp.sum(q * n, axis=-1, keepdims=True)
    h = num / jnp.maximum(jnp.abs(den), jnp.exp(-m_t))
    bL = jnp.sum(f_c, axis=0, keepdims=True)
    log_w = bL - b_c + i_c
    m_new = jnp.maximum(bL + m, jnp.max(log_w, axis=0, keepdims=True))
    decay = jnp.exp(bL + m - m_new)
    w = jnp.exp(log_w - m_new)
    C_new = decay * C + bdot_tn(w * v, k)
    n_new = decay * n + jnp.sum(w * k, axis=0, keepdims=True)
    return C_new, n_new, m_new, h


def _mlstm_in_specs(L, dm, hm, nc, step_of):
    def chunk(d, j):
        s = step_of(j)
        return s + d * (nc - 1 - 2 * s)
    return [
        pl.BlockSpec((L, dm), lambda d, h, j: (chunk(d, j), h)),
        pl.BlockSpec((L, dm), lambda d, h, j: (chunk(d, j), hm + h)),
        pl.BlockSpec((L, dm), lambda d, h, j: (chunk(d, j), 2 * hm + h)),
        pl.BlockSpec((None, None, L, 1), lambda d, h, j: (d, h, chunk(d, j), 0)),
        pl.BlockSpec((None, None, L, 1), lambda d, h, j: (d, h, chunk(d, j), 0)),
        pl.BlockSpec((None, None, 1, L), lambda d, h, j: (d, h, 0, chunk(d, j))),
        pl.BlockSpec((None, None, 1, L), lambda d, h, j: (d, h, 0, chunk(d, j))),
        pl.BlockSpec((None, None, 1, 1), lambda d, h, j: (d, h, 0, 0)),
        pl.BlockSpec((None, None, 1, 1), lambda d, h, j: (d, h, 0, 0)),
    ], chunk


def mlstm_fwd(z, proj, gates, hm, dm, name):
    S = z.shape[0]
    L = CHUNK
    nc = S // L
    in_specs, chunk = _mlstm_in_specs(L, dm, hm, nc, lambda j: j)

    def body(zq, zk, v, ic, fc, ir, fr, bi, bf_, h_ref, cs_ref, ns_ref, ms_ref, C_sc, n_sc, m_sc):
        d = pl.program_id(0)

        @pl.when(pl.program_id(2) == 0)
        def _():
            C_sc[...] = jnp.zeros_like(C_sc)
            n_sc[...] = jnp.zeros_like(n_sc)
            m_sc[...] = jnp.full_like(m_sc, M_INIT)

        C, n, m = C_sc[...], n_sc[...], m_sc[...]
        cs_ref[...], ns_ref[...], ms_ref[...] = C, n, m
        C2, n2, m2, h = _mlstm_step(dm, d, C, n, m, zq[...], zk[...], v[...], ic[...], fc[...], ir[...], fr[...],
                                    bi[...], bf_[...])
        C_sc[...], n_sc[...], m_sc[...] = C2, n2, m2
        h_ref[...] = h

    return pl.pallas_call(
        body, name=name, grid=(2, hm, nc), in_specs=in_specs,
        out_specs=[pl.BlockSpec((None, L, dm), lambda d, h, j: (d, chunk(d, j), h)),
                   pl.BlockSpec((None, None, None, dm, dm), lambda d, h, j: (d, h, j, 0, 0)),
                   pl.BlockSpec((None, None, None, 1, dm), lambda d, h, j: (d, h, j, 0, 0)),
                   pl.BlockSpec((None, None, None, 1, 1), lambda d, h, j: (d, h, j, 0, 0))],
        out_shape=[jax.ShapeDtypeStruct((2, S, hm * dm), f32), jax.ShapeDtypeStruct((2, hm, nc, dm, dm), f32),
                   jax.ShapeDtypeStruct((2, hm, nc, 1, dm), f32), jax.ShapeDtypeStruct((2, hm, nc, 1, 1), f32)],
        scratch_shapes=[pltpu.VMEM((dm, dm), f32), pltpu.VMEM((1, dm), f32), pltpu.VMEM((1, 1), f32)],
        compiler_params=_cp(("arbitrary", "arbitrary", "arbitrary")))(z, z, proj, *gates)


def mlstm_bwd(z, proj, gates, states, dh, hm, dm, name):
    S = z.shape[0]
    L = CHUNK
    nc = S // L
    in_specs, chunk = _mlstm_in_specs(L, dm, hm, nc, lambda j: nc - 1 - j)
    st = lambda j: nc - 1 - j
    in_specs = in_specs + [
        pl.BlockSpec((None, None, None, dm, dm), lambda d, h, j: (d, h, st(j), 0, 0)),
        pl.BlockSpec((None, None, None, 1, dm), lambda d, h, j: (d, h, st(j), 0, 0)),
        pl.BlockSpec((None, None, None, 1, 1), lambda d, h, j: (d, h, st(j), 0, 0)),
        pl.BlockSpec((None, L, dm), lambda d, h, j: (d, chunk(d, j), h)),
    ]

    def body(zq, zk, v, ic, fc, ir, fr, bi, bf_, cs, ns, ms, dh_ref,
             dzq, dzk, dv, dic, dfc, dir_, dfr, dbi, dbf, dC_sc, dn_sc, dm_sc):
        d = pl.program_id(0)

        @pl.when(pl.program_id(2) == 0)
        def _():
            dC_sc[...] = jnp.zeros_like(dC_sc)
            dn_sc[...] = jnp.zeros_like(dn_sc)
            dm_sc[...] = jnp.zeros_like(dm_sc)
            dbi[...] = jnp.zeros_like(dbi)
            dbf[...] = jnp.zeros_like(dbf)

        prim = (cs[...], ns[...], ms[...], zq[...], zk[...], v[...], ic[...], fc[...], ir[...], fr[...],
                bi[...], bf_[...])
        _, pull = jax.vjp(functools.partial(_mlstm_step, dm, d), *prim)
        g = pull((dC_sc[...], dn_sc[...], dm_sc[...], dh_ref[...]))
        dC_sc[...], dn_sc[...], dm_sc[...] = g[0], g[1], g[2]
        dzq[...], dzk[...], dv[...] = g[3], g[4], g[5]
        dic[...], dfc[...], dir_[...], dfr[...] = g[6], g[7], g[8], g[9]
        dbi[...] += g[10]
        dbf[...] += g[11]

    tile = pl.BlockSpec((None, L, dm), lambda d, h, j: (d, chunk(d, j), h))
    col = pl.BlockSpec((None, None, L, 1), lambda d, h, j: (d, h, chunk(d, j), 0))
    row = pl.BlockSpec((None, None, 1, L), lambda d, h, j: (d, h, 0, chunk(d, j)))
    one = pl.BlockSpec((None, None, 1, 1), lambda d, h, j: (d, h, 0, 0))
    big = jax.ShapeDtypeStruct((2, S, hm * dm), f32)
    cols = jax.ShapeDtypeStruct((2, hm, S, 1), f32)
    rows = jax.ShapeDtypeStruct((2, hm, 1, S), f32)
    ones = jax.ShapeDtypeStruct((2, hm, 1, 1), f32)
    return pl.pallas_call(
        body, name=name, grid=(2, hm, nc), in_specs=in_specs,
        out_specs=[tile, tile, tile, col, col, row, row, one, one],
        out_shape=[big, big, big, cols, cols, rows, rows, ones, ones],
        scratch_shapes=[pltpu.VMEM((dm, dm), f32), pltpu.VMEM((1, dm), f32), pltpu.VMEM((1, 1), f32)],
        compiler_params=_cp(("arbitrary", "arbitrary", "arbitrary")))(z, z, proj, *gates, *states, dh)


def _blocks_to_cols(g):
    return g.transpose(1, 0, 2).reshape(g.shape[1], N_DEV * g.shape[2])


def _cols_to_blocks(a):
    return a.reshape(a.shape[0], N_DEV, a.shape[1] // N_DEV).transpose(1, 0, 2)


def _pad_cols(a, n):
    return jnp.pad(a, ((0, 0), (0, n - a.shape[1])))


def _relu2(u):
    r = jnp.maximum(u, 0.0)
    return r * r


def kernel(x, c, positions, w_ada, b_ada, norm_mix_g, w_in, b_gates, conv_w, conv_b, q_lora_g, w_uq, kv_lora_g, w_ukv, q_norm_g, k_norm_g, mlstm_norm_g, w_out, norm_mlp_g, w_ff1, w_ff2, loss_target, m_w_ada, m_b_ada, m_norm_mix_g, m_w_in, m_b_gates, m_conv_w, m_conv_b, m_q_lora_g, m_w_uq, m_kv_lora_g, m_w_ukv, m_q_norm_g, m_k_norm_g, m_mlstm_norm_g, m_w_out, m_norm_mlp_g, m_w_ff1, m_w_ff2, v_w_ada, v_b_ada, v_norm_mix_g, v_w_in, v_b_gates, v_conv_w, v_conv_b, v_q_lora_g, v_w_uq, v_kv_lora_g, v_w_ukv, v_q_norm_g, v_k_norm_g, v_mlstm_norm_g, v_w_out, v_norm_mlp_g, v_w_ff1, v_w_ff2):
    S, D = x.shape[1], x.shape[2]
    QL, KVL = w_uq.shape[1], w_ukv.shape[1]
    H = w_uq.shape[2] * N_DEV // QK_DIM
    HM = mlstm_norm_g.shape[1]
    DM = mlstm_norm_g.shape[2] * N_DEV
    MW = HM * DM
    D_IN = w_in.shape[2] * N_DEV
    NADA = w_ada.shape[2]
    assert D_IN == QL + KVL + ROPE + 4 * MW + N_GATES and DM % LANE == 0 and S % CHUNK == 0
    assert (4 * MW) % QL == 0 and (4 * MW + QL) % KVL == 0 and KVL % LANE == 0
    idx = 4 * lax.axis_index("x") + 2 * lax.axis_index("y") + lax.axis_index("c")
    x2, tgt = x[0], loss_target[0]

    g_in, g_uq, g_ukv, g_out, g_ff1, g_ff2, g_conv, g_mn, c_all = all_gather(
        [w_in[0].astype(bf16), w_uq[0].astype(bf16), w_ukv[0].astype(bf16), w_out[0].astype(bf16),
         w_ff1[0].astype(bf16), w_ff2[0].astype(bf16), conv_w[0], mlstm_norm_g[0], c], "gather_weights")
    c_all = c_all.reshape(N_DEV, D)

    wi = _blocks_to_cols(g_in)
    o_cq, o_ckv, o_kpe, o_m, o_g = 0, QL, QL + KVL, QL + KVL + ROPE, QL + KVL + ROPE + 4 * MW
    w_in_p = jnp.concatenate([wi[:, o_m:o_g], wi[:, o_cq:o_kpe], _pad_cols(wi[:, o_kpe:o_m], LANE),
                              _pad_cols(wi[:, o_g:], LANE)], axis=1)
    NP = w_in_p.shape[1]
    cb_cq, cb_ckv, cb_kpe, cb_g = 4 * MW // QL, (4 * MW + QL) // KVL, (4 * MW + QL + KVL) // LANE, NP // LANE - 1
    w_uq_p = jnp.pad(_blocks_to_cols(g_uq).reshape(QL, H, QK_DIM), ((0, 0), (0, 0), (0, QK_PAD - QK_DIM))).reshape(QL, H * QK_PAD)
    w_ukv_p = _blocks_to_cols(g_ukv).reshape(KVL, H, 2, NOPE).transpose(0, 2, 1, 3).reshape(KVL, 2 * H * NOPE)
    w_out_f = g_out.reshape(N_DEV * g_out.shape[1], D)
    w_ff1_f = _blocks_to_cols(g_ff1)
    w_ff2_f = g_ff2.reshape(N_DEV * g_ff2.shape[1], D)
    D_FF = w_ff1_f.shape[1]
    conv_w_f = jnp.pad(_blocks_to_cols(g_conv), ((0, 8 - CONV_W), (0, 0)))
    mn_g = _blocks_to_cols(g_mn).reshape(1, MW)
    gqn = _pad_cols(q_norm_g, QK_PAD)
    gkn = _pad_cols(k_norm_g, QK_PAD)
    fr_np = np.zeros((1, LANE), np.float32)
    fr_np[0, :HALF] = fr_np[0, HALF:ROPE] = ROPE_THETA ** (-np.arange(HALF, dtype=np.float32) / HALF)
    freqs = jnp.asarray(fr_np)
    pos = positions.astype(f32).reshape(S, 1)

    b_blk = lax.dynamic_slice(b_ada, (0, idx * NADA), (1, NADA))
    mod_part = ada_fwd(c_all, w_ada[0], b_blk, "ada_fwd")
    (mod_all,) = all_gather([mod_part], "gather_mod")
    mod = lax.dynamic_index_in_dim(mod_all, idx, axis=1, keepdims=False).reshape(1, N_DEV * NADA)
    shift1, scale1, gate1, shift2, scale2, gate2 = [mod[:, k * D:(k + 1) * D] for k in range(6)]

    (h,) = rowwise(f_norm_mod, [Row(x2)], [norm_mix_g, shift1, scale1], [(D, bf16)], n_rows=S, tile=256, name="norm_mix")
    proj = mm(h, w_in_p, name="proj_in", out_dtype=f32)
    r_cq, r_ckv, r_kpe = Row(proj, QL, cb_cq), Row(proj, KVL, cb_ckv), Row(proj, LANE, cb_kpe)
    f_prep = make_f_mla_prep(H)
    prep_params = [q_lora_g, kv_lora_g, gqn, gkn, w_uq_p, w_ukv_p, freqs]
    Q, K, V = rowwise(f_prep, [r_cq, r_ckv, r_kpe, Row(pos, diff=False)], prep_params,
                      [(H * QK_PAD, bf16), (H * QK_PAD, bf16), (H * V_DIM, bf16)], n_rows=S, tile=256, name="mla_prep")
    att_scale = QK_DIM ** -0.5
    attn, lse = flash_fwd(Q, K, V, H, att_scale, "flash_fwd")

    conv_bias = conv_b
    z = conv_fwd(proj, 2 * MW, conv_w_f, conv_bias, "conv_fwd")
    graw = proj[:, cb_g * LANE:cb_g * LANE + N_GATES].reshape(S, 4, HM)
    gcol = graw.transpose(1, 2, 0).reshape(2, 2, HM, S)
    bg = b_gates.reshape(2, 2, HM)
    gates = (gcol[:, 0].reshape(2, HM, S, 1), gcol[:, 1].reshape(2, HM, S, 1),
             gcol[:, 0].reshape(2, HM, 1, S), gcol[:, 1].reshape(2, HM, 1, S),
             bg[:, 0].reshape(2, HM, 1, 1), bg[:, 1].reshape(2, HM, 1, 1))
    hdir, cs, ns, ms = mlstm_fwd(z, proj, gates, HM, DM, "mlstm_fwd")
    f_post = make_f_mlstm_post(HM, DM)
    post_rows = [Row(hdir, MW, 0, lead=0), Row(hdir, MW, 0, lead=1), Row(proj, MW, 3)]
    (ml_out,) = rowwise(f_post, post_rows, [mn_g], [(MW, bf16)], n_rows=S, tile=256, name="mlstm_post")

    cat = jnp.concatenate([attn, ml_out], axis=1)
    mixed = mm(cat, w_out_f, name="proj_out", out_dtype=f32)
    mlp_params = [gate1, norm_mlp_g, shift2, scale2]
    x1, h2 = rowwise(f_resid_norm_mod, [Row(x2), Row(mixed)], mlp_params, [(D, f32), (D, bf16)],
                     n_rows=S, tile=256, name="resid_norm_mlp")
    u = mm(h2, w_ff1_f, name="ff1", out_dtype=bf16)
    y = mm(u, w_ff2_f, name="ff2", a_fn=_relu2, out_dtype=f32)
    loss_l, d_out, d_y, d_gate2 = loss_head(x1, y, tgt, gate2, "loss_head")
    loss = lax.psum(loss_l[0, 0], AXES)

    dw_ff2 = mm(u, d_y, name="dw_ff2", ta=True, a_fn=_relu2, out_dtype=bf16)
    d_u = mm(d_y, w_ff2_f, name="d_u", tb=True, epi=lambda acc, uu: acc * (2.0 * jnp.maximum(uu.astype(f32), 0.0)),
             extras=(u,), out_dtype=bf16)
    dw_ff1 = mm(h2, d_u, name="dw_ff1", ta=True, out_dtype=bf16)
    d_h2 = mm(d_u, w_ff1_f, name="d_h2", tb=True, out_dtype=f32)
    (d_x1, d_mixed), (d_gate1, d_g_mlp, d_shift2, d_scale2) = rowwise_vjp(
        f_resid_norm_mod, [Row(x2), Row(mixed)], mlp_params, [Row(d_out), Row(d_h2)],
        n_rows=S, tile=256, name="resid_norm_mlp_bwd", row_grad_dtypes=[f32, bf16])
    dw_out = mm(cat, d_mixed, name="dw_out", ta=True, out_dtype=bf16)
    d_cat = mm(d_mixed, w_out_f, name="d_cat", tb=True, out_dtype=f32)

    (d_hf, d_hb, d_om), (d_mn_g,) = rowwise_vjp(
        f_post, post_rows, [mn_g], [Row(d_cat, MW, H * V_DIM // MW)], n_rows=S, tile=256, name="mlstm_post_bwd")
    dh = jnp.stack([d_hf, d_hb])
    dzq, dzk, dvm, dic, dfc, dir_, dfr, dbi, dbf = mlstm_bwd(z, proj, gates, (cs, ns, ms), dh, HM, DM, "mlstm_bwd")
    (dz,) = rowwise(f_add_pairs, [Row(dzq, MW, 0, lead=0), Row(dzq, MW, 0, lead=1), Row(dzk, MW, 0, lead=0),
                                  Row(dzk, MW, 0, lead=1)], [], [(2 * MW, f32)], n_rows=S, tile=256, name="dz_sum")
    (d_vm,) = rowwise(f_add, [Row(dvm, MW, 0, lead=0), Row(dvm, MW, 0, lead=1)], [], [(MW, bf16)], n_rows=S, tile=256,
                      name="dv_sum")
    d_qk, d_conv_w, d_conv_b = conv_bwd(dz, proj, 2 * MW, conv_w_f, "conv_bwd")
    dg = jnp.stack([dic.reshape(2, HM, S) + dir_.reshape(2, HM, S), dfc.reshape(2, HM, S) + dfr.reshape(2, HM, S)], axis=1)
    d_gates = dg.reshape(4 * HM, S).T
    d_b_gates = jnp.stack([dbi.reshape(2, HM), dbf.reshape(2, HM)], axis=1).reshape(1, N_GATES)

    dq = flash_bwd_dq(Q, K, V, attn, lse, d_cat, 0, H, att_scale, "flash_bwd_dq")
    dk, dv = flash_bwd_dkv(Q, K, V, attn, lse, d_cat, 0, H, att_scale, "flash_bwd_dkv")
    (d_cq, d_ckv, d_kpe), (d_gq, d_gkv, d_gqn, d_gkn, dw_uq_p, dw_ukv_p) = rowwise_vjp(
        f_prep, [r_cq, r_ckv, r_kpe, Row(pos, diff=False)], prep_params, [Row(dq), Row(dk), Row(dv)],
        n_rows=S, tile=256, name="mla_prep_bwd", row_grad_dtypes=[bf16, bf16, bf16],
        param_diff=[True, True, True, True, True, True, False])

    d_proj = jnp.concatenate([d_qk.astype(bf16), d_vm, d_om.astype(bf16), d_cq, d_ckv, d_kpe,
                              _pad_cols(d_gates.astype(bf16), LANE)], axis=1)
    dw_in_p = mm(h, d_proj, name="dw_in", ta=True, out_dtype=bf16)
    d_h = mm(d_proj, w_in_p, name="d_h", tb=True, out_dtype=f32)
    (grad_x,), (d_g_mix, d_shift1, d_scale1) = rowwise_vjp(
        f_norm_mod_thru, [Row(x2)], [norm_mix_g, shift1, scale1], [Row(d_h), Row(d_x1)],
        n_rows=S, tile=256, name="norm_mix_bwd")

    dmod = jnp.concatenate([d_shift1, d_scale1, d_gate1, d_shift2, d_scale2, d_gate2], axis=1)
    small = [(norm_mix_g, m_norm_mix_g, v_norm_mix_g, d_g_mix), (b_gates, m_b_gates, v_b_gates, d_b_gates),
             (conv_b, m_conv_b, v_conv_b, d_conv_b), (q_lora_g, m_q_lora_g, v_q_lora_g, d_gq),
             (kv_lora_g, m_kv_lora_g, v_kv_lora_g, d_gkv), (q_norm_g, m_q_norm_g, v_q_norm_g, d_gqn[:, :QK_DIM]),
             (k_norm_g, m_k_norm_g, v_k_norm_g, d_gkn[:, :QK_DIM]), (norm_mlp_g, m_norm_mlp_g, v_norm_mlp_g, d_g_mlp),
             (b_ada, m_b_ada, v_b_ada, dmod)]
    sizes = [s[0].shape[1] for s in small]
    P = sum(sizes)
    PP = -(-P // LANE) * LANE
    pack = lambda k: _pad_cols(jnp.concatenate([s[k] for s in small], axis=1), PP)
    (sg_all,) = all_gather([pack(3)], "gather_small_grads")
    s_out = adamw([sg_all[k] for k in range(N_DEV)], pack(0), pack(1), pack(2), "adamw_small")
    offs = np.concatenate([[0], np.cumsum(sizes)])
    small_out = [[o[:, offs[k]:offs[k + 1]] for o in s_out] for k in range(len(small))]

    dmod_all = sg_all[:, 0, offs[-2]:offs[-1]]
    dmod_blk = lax.dynamic_slice(dmod_all, (0, idx * NADA), (N_DEV, NADA))
    g_w_ada = ada_wgrad(c_all, dmod_blk, "ada_wgrad")
    ada_out = adamw([g_w_ada], w_ada[0], m_w_ada[0], v_w_ada[0], "adamw_ada")

    dwi = jnp.concatenate([dw_in_p[:, 4 * MW:4 * MW + QL + KVL + ROPE], dw_in_p[:, :4 * MW],
                           dw_in_p[:, cb_g * LANE:cb_g * LANE + N_GATES]], axis=1)
    dw_uq = dw_uq_p.reshape(QL, H, QK_PAD)[:, :, :QK_DIM].reshape(QL, H * QK_DIM)
    dw_ukv = dw_ukv_p.reshape(KVL, 2, H, NOPE).transpose(0, 2, 1, 3).reshape(KVL, 2 * H * NOPE)
    big = [(w_in, m_w_in, v_w_in, _cols_to_blocks(dwi)),
           (w_uq, m_w_uq, v_w_uq, _cols_to_blocks(dw_uq)),
           (w_ukv, m_w_ukv, v_w_ukv, _cols_to_blocks(dw_ukv)),
           (w_out, m_w_out, v_w_out, dw_out.reshape(N_DEV, -1, D)),
           (w_ff1, m_w_ff1, v_w_ff1, _cols_to_blocks(dw_ff1)),
           (w_ff2, m_w_ff2, v_w_ff2, dw_ff2.reshape(N_DEV, -1, D)),
           (conv_w, m_conv_w, v_conv_w, _cols_to_blocks(d_conv_w[:CONV_W])),
           (mlstm_norm_g, m_mlstm_norm_g, v_mlstm_norm_g, _cols_to_blocks(d_mn_g.reshape(HM, DM)))]
    bsizes = [int(np.prod(b[0].shape)) for b in big]
    T = sum(bsizes)
    PC = 512
    PR = -(-T // (PC * 256)) * 256
    gpack = jnp.concatenate([b[3].astype(bf16).reshape(N_DEV, -1) for b in big], axis=1)
    gpack = jnp.pad(gpack, ((0, 0), (0, PR * PC - T))).reshape(N_DEV, PR, PC)
    wpack = lambda k: jnp.pad(jnp.concatenate([b[k].reshape(1, -1) for b in big], axis=1),
                              ((0, 0), (0, PR * PC - T))).reshape(PR, PC)

    xi, yi, ci = lax.axis_index("x"), lax.axis_index("y"), lax.axis_index("c")
    from_sibling = pair_exchange(gpack, "grad_pair_exchange")
    slots = jnp.stack([4 * (1 - xi) + 2 * yi + ci, 4 * xi + 2 * (1 - yi) + ci, 4 * (1 - xi) + 2 * (1 - yi) + ci]).astype(jnp.int32)
    partials = chip_partials(gpack, from_sibling, slots, "grad_chip_partials")
    from_chips = chip_exchange(partials, "grad_chip_exchange")
    mine = lax.dynamic_index_in_dim(gpack, idx, axis=0, keepdims=False)
    sib = lax.dynamic_index_in_dim(from_sibling, 2 * xi + yi, axis=0, keepdims=False)
    b_out = adamw([mine, sib, from_chips[0], from_chips[1], from_chips[2]], wpack(0), wpack(1), wpack(2), "adamw_sharded")
    boffs = np.concatenate([[0], np.cumsum(bsizes)])
    big_out = [[o.reshape(-1)[boffs[k]:boffs[k + 1]].reshape(big[k][0].shape) for o in b_out] for k in range(len(big))]

    names = ["w_ada", "b_ada", "norm_mix_g", "w_in", "b_gates", "conv_w", "conv_b", "q_lora_g", "w_uq", "kv_lora_g",
             "w_ukv", "q_norm_g", "k_norm_g", "mlstm_norm_g", "w_out", "norm_mlp_g", "w_ff1", "w_ff2"]
    res = {"w_ada": [o[None] for o in ada_out]}
    for k, nm in enumerate(["norm_mix_g", "b_gates", "conv_b", "q_lora_g", "kv_lora_g", "q_norm_g", "k_norm_g",
                            "norm_mlp_g", "b_ada"]):
        res[nm] = small_out[k]
    for k, nm in enumerate(["w_in", "w_uq", "w_ukv", "w_out", "w_ff1", "w_ff2", "conv_w", "mlstm_norm_g"]):
        res[nm] = big_out[k]
    outs = [loss, grad_x[None]]
    for part in range(4):
        outs += [res[nm][part] for nm in names]
    return tuple(outs)
```

```python
import functools
import math

import numpy as np
import jax
import jax.numpy as jnp
from jax import lax
from jax.experimental import pallas as pl
from jax.experimental.pallas import tpu as pltpu

f32 = jnp.float32
bf16 = jnp.bfloat16

N_DEV = 8
AXES = ("x", "y", "c")
MESH = pl.DeviceIdType.MESH

NOPE = 128
ROPE = 64
HALF = ROPE // 2
QK_DIM = NOPE + ROPE
QK_PAD = 256
V_DIM = 128
ROPE_THETA = 10000.0
CHUNK = 128
CONV_W = 5
N_GATES = 16
EPS = 1e-6
M_INIT = -1e30

ADAM_LR, ADAM_B1, ADAM_B2, ADAM_EPS, ADAM_WD, ADAM_STEP = 0.001, 0.9, 0.999, 1e-08, 0.01, 10

LANE = 128
VMEM_LIMIT = 56 * 1024 * 1024


def _cp(sem=None, vmem=VMEM_LIMIT):
    return pltpu.CompilerParams(dimension_semantics=sem, vmem_limit_bytes=vmem)


def _pick(n, target):
    best = None
    t = LANE
    while t <= min(n, target):
        if n % t == 0:
            best = t
        t += LANE
    return best if best is not None else n


def _pick_rows(n, target):
    t = min(n, target)
    while n % t:
        t -= 8
    return t


def _make_dots(cast, precision):
    def dg(a, b, ca, cb):
        if cast is not None:
            a = a.astype(cast)
            b = b.astype(cast)
        return lax.dot_general(a, b, (((ca,), (cb,)), ((), ())), precision=precision, preferred_element_type=f32)

    @jax.custom_vjp
    def nn(a, b):
        return dg(a, b, 1, 0)

    def nn_f(a, b):
        return dg(a, b, 1, 0), (a, b)

    def nn_b(res, g):
        a, b = res
        return dg(g, b, 1, 1).astype(a.dtype), dg(a, g, 0, 0).astype(b.dtype)

    nn.defvjp(nn_f, nn_b)

    @jax.custom_vjp
    def nt(a, b):
        return dg(a, b, 1, 1)

    def nt_f(a, b):
        return dg(a, b, 1, 1), (a, b)

    def nt_b(res, g):
        a, b = res
        return dg(g, b, 1, 0).astype(a.dtype), dg(g, a, 0, 0).astype(b.dtype)

    nt.defvjp(nt_f, nt_b)

    @jax.custom_vjp
    def tn(a, b):
        return dg(a, b, 0, 0)

    def tn_f(a, b):
        return dg(a, b, 0, 0), (a, b)

    def tn_b(res, g):
        a, b = res
        return dg(b, g, 1, 1).astype(a.dtype), dg(a, g, 1, 0).astype(b.dtype)

    tn.defvjp(tn_f, tn_b)
    return nn, nt, tn


bdot, bdot_nt, bdot_tn = _make_dots(bf16, None)
hdot, hdot_nt, hdot_tn = _make_dots(None, lax.Precision.HIGHEST)


def _silu(x):
    return x * jax.nn.sigmoid(x)


def _rms(x, n):
    return x * lax.rsqrt(jnp.sum(x * x, axis=-1, keepdims=True) * (1.0 / n) + EPS)


def _place():
    return lax.axis_index("x"), lax.axis_index("y"), lax.axis_index("c")


def all_gather(ops, name):
    n = len(ops)

    def body(*refs):
        ins, outs = refs[:n], refs[n:2 * n]
        send_sems, recv_sems, local_sems = refs[2 * n:]
        x, y, c = _place()
        me, sibling = (x, y, c), (x, y, 1 - c)
        chips = [(1 - x, y), (x, 1 - y), (1 - x, 1 - y)]

        def slot(o, p):
            return outs[o].at[4 * p[0] + 2 * p[1] + p[2]]

        def copy(o, k, block, to, src=None):
            dst = slot(o, block)
            return pltpu.make_async_remote_copy(
                src_ref=dst if src is None else src, dst_ref=dst,
                send_sem=send_sems.at[o, k], recv_sem=recv_sems.at[o, k],
                device_id=to, device_id_type=MESH)

        started = []
        for o in range(n):
            mine = pltpu.make_async_copy(ins[o], slot(o, me), local_sems.at[o])
            mine.start()
            started.append(mine)
        sends = []
        for o in range(n):
            first = [copy(o, 0, me, sibling, src=ins[o])]
            first += [copy(o, 1 + j, me, (*chip, c), src=ins[o]) for j, chip in enumerate(chips)]
            for cp in first:
                cp.start()
            sends += first
        for o in range(n):
            for j, chip in enumerate(chips):
                copy(o, 1 + j, (*chip, c), me).wait_recv()
                passed = copy(o, 4 + j, (*chip, c), sibling)
                passed.start()
                sends.append(passed)
        for o in range(n):
            copy(o, 0, sibling, me).wait_recv()
            for j, chip in enumerate(chips):
                copy(o, 4 + j, (*chip, 1 - c), me).wait_recv()
        for cp in sends:
            cp.wait_send()
        for mine in started:
            mine.wait()

    anyspec = pl.BlockSpec(memory_space=pl.ANY)
    return pl.pallas_call(
        body, name=name,
        out_shape=[jax.ShapeDtypeStruct((N_DEV,) + o.shape, o.dtype) for o in ops],
        in_specs=[anyspec] * n, out_specs=[anyspec] * n,
        scratch_shapes=[pltpu.SemaphoreType.DMA((n, 7)), pltpu.SemaphoreType.DMA((n, 7)),
                        pltpu.SemaphoreType.DMA((n,))],
    )(*ops)


def pair_exchange(gs, name):
    n = len(gs)

    def body(*refs):
        g_refs, out_refs = refs[:n], refs[n:2 * n]
        send_sems, recv_sems = refs[2 * n:]
        x, y, c = _place()
        sibling = (x, y, 1 - c)
        cps = []
        for o in range(n):
            for q in range(4):
                cp = pltpu.make_async_remote_copy(
                    src_ref=g_refs[o].at[2 * q + (1 - c)], dst_ref=out_refs[o].at[q],
                    send_sem=send_sems.at[o, q], recv_sem=recv_sems.at[o, q],
                    device_id=sibling, device_id_type=MESH)
                cp.start()
                cps.append(cp)
        for cp in cps:
            cp.wait_recv()
        for cp in cps:
            cp.wait_send()

    anyspec = pl.BlockSpec(memory_space=pl.ANY)
    return pl.pallas_call(
        body, name=name, out_shape=[jax.ShapeDtypeStruct((4,) + g.shape[1:], g.dtype) for g in gs],
        in_specs=[anyspec] * n, out_specs=[anyspec] * n,
        scratch_shapes=[pltpu.SemaphoreType.DMA((n, 4)), pltpu.SemaphoreType.DMA((n, 4))],
    )(*gs)


def chip_exchange(ps, name):
    n = len(ps)

    def body(*refs):
        p_refs, out_refs = refs[:n], refs[n:2 * n]
        send_sems, recv_sems = refs[2 * n:]
        x, y, c = _place()
        chips = [(1 - x, y), (x, 1 - y), (1 - x, 1 - y)]
        cps = []
        for o in range(n):
            for j, chip in enumerate(chips):
                cp = pltpu.make_async_remote_copy(
                    src_ref=p_refs[o].at[j], dst_ref=out_refs[o].at[j],
                    send_sem=send_sems.at[o, j], recv_sem=recv_sems.at[o, j],
                    device_id=(*chip, c), device_id_type=MESH)
                cp.start()
                cps.append(cp)
        for cp in cps:
            cp.wait_recv()
        for cp in cps:
            cp.wait_send()

    anyspec = pl.BlockSpec(memory_space=pl.ANY)
    return pl.pallas_call(
        body, name=name, out_shape=[jax.ShapeDtypeStruct(p.shape, p.dtype) for p in ps],
        in_specs=[anyspec] * n, out_specs=[anyspec] * n,
        scratch_shapes=[pltpu.SemaphoreType.DMA((n, 3)), pltpu.SemaphoreType.DMA((n, 3))],
    )(*ps)


def chip_partials(g, recv, slots, name):
    _, R, C = g.shape
    tr = _pick_rows(R, 512)

    def body(s_ref, a_ref, b_ref, o_ref):
        o_ref[...] = (a_ref[...].astype(f32) + b_ref[...].astype(f32)).astype(o_ref.dtype)

    grid_spec = pltpu.PrefetchScalarGridSpec(
        num_scalar_prefetch=1, grid=(3, R // tr),
        in_specs=[pl.BlockSpec((None, tr, C), lambda j, i, s: (s[j], i, 0)),
                  pl.BlockSpec((None, tr, C), lambda j, i, s: (s[j] // 2, i, 0))],
        out_specs=pl.BlockSpec((None, tr, C), lambda j, i, s: (j, i, 0)))
    return pl.pallas_call(body, name=name, grid_spec=grid_spec,
                          out_shape=jax.ShapeDtypeStruct((3, R, C), g.dtype),
                          compiler_params=_cp(("arbitrary", "arbitrary")))(slots, g, recv)


def adamw(parts, w, m, v, name, rows=256):
    R, C = w.shape
    tr = _pick_rows(R, rows)
    npart = len(parts)
    c1 = 1.0 - ADAM_B1 ** ADAM_STEP
    c2 = 1.0 - ADAM_B2 ** ADAM_STEP

    def body(*refs):
        p_refs = refs[:npart]
        w_ref, m_ref, v_ref, g_out, d_out, m_out, v_out = refs[npart:]
        g = p_refs[0][...].astype(f32)
        for p in p_refs[1:]:
            g = g + p[...].astype(f32)
        mn = ADAM_B1 * m_ref[...] + (1.0 - ADAM_B1) * g
        vn = ADAM_B2 * v_ref[...] + (1.0 - ADAM_B2) * (g * g)
        m_hat = mn / c1
        v_hat = vn / c2
        g_out[...] = g
        d_out[...] = -ADAM_LR * (m_hat / (jnp.sqrt(v_hat) + ADAM_EPS) + ADAM_WD * w_ref[...])
        m_out[...] = mn
        v_out[...] = vn

    spec = pl.BlockSpec((tr, C), lambda i: (i, 0))
    return pl.pallas_call(
        body, name=name, grid=(R // tr,),
        in_specs=[spec] * (npart + 3), out_specs=[spec] * 4,
        out_shape=[jax.ShapeDtypeStruct((R, C), f32)] * 4,
        compiler_params=_cp(("arbitrary",)))(*parts, w, m, v)


def mm(a, b, *, name, ta=False, tb=False, a_fn=None, epi=None, extras=(), out_dtype=f32, out_blocks=False,
       tm=512, tn=1024, tk=2048):
    K, M = a.shape if ta else a.shape[::-1]
    b3 = b.ndim == 3
    if b3:
        N, K2 = (b.shape[1], N_DEV * b.shape[2]) if tb else (N_DEV * b.shape[2], b.shape[1])
    else:
        N, K2 = b.shape if tb else b.shape[::-1]
    assert K == K2, (a.shape, b.shape, ta, tb)
    n_split = N // N_DEV if (out_blocks or (b3 and not tb)) else N
    k_split = K // N_DEV if (b3 and tb) else K
    tm, tn, tk = _pick(M, tm), _pick(n_split, tn), _pick(k_split, tk)
    nb, kb = n_split // tn, k_split // tk
    nk = K // tk
    ne = len(extras)
    assert not (out_blocks and ne)
    dims = (((0 if ta else 1,), (1 if tb else 0,)), ((), ()))

    def body(a_ref, b_ref, *rest):
        e_refs, o_ref, acc = rest[:ne], rest[ne], rest[ne + 1]
        k = pl.program_id(2)

        @pl.when(k == 0)
        def _():
            acc[...] = jnp.zeros_like(acc)

        av = a_ref[...]
        if a_fn is not None:
            av = a_fn(av.astype(f32))
        acc[...] += lax.dot_general(av.astype(bf16), b_ref[...].astype(bf16), dims, preferred_element_type=f32)

        @pl.when(k == nk - 1)
        def _():
            r = acc[...]
            if epi is not None:
                r = epi(r, *[e[...] for e in e_refs])
            o_ref[...] = r.astype(o_ref.dtype)

    a_spec = pl.BlockSpec((tk, tm), lambda i, j, k: (k, i)) if ta else pl.BlockSpec((tm, tk), lambda i, j, k: (i, k))
    if b3 and tb:
        b_spec = pl.BlockSpec((None, tn, tk), lambda i, j, k: (k // kb, j, k % kb))
    elif b3:
        b_spec = pl.BlockSpec((None, tk, tn), lambda i, j, k: (j // nb, k, j % nb))
    else:
        b_spec = pl.BlockSpec((tn, tk), lambda i, j, k: (j, k)) if tb else pl.BlockSpec((tk, tn), lambda i, j, k: (k, j))
    if out_blocks:
        o_spec = pl.BlockSpec((None, tm, tn), lambda i, j, k: (j // nb, i, j % nb))
        o_shape = jax.ShapeDtypeStruct((N_DEV, M, N // N_DEV), out_dtype)
    else:
        o_spec = pl.BlockSpec((tm, tn), lambda i, j, k: (i, j))
        o_shape = jax.ShapeDtypeStruct((M, N), out_dtype)
    return pl.pallas_call(
        body, name=name, grid=(M // tm, N // tn, nk),
        in_specs=[a_spec, b_spec] + [o_spec] * ne, out_specs=o_spec,
        out_shape=o_shape,
        scratch_shapes=[pltpu.VMEM((tm, tn), f32)],
        compiler_params=_cp(("parallel", "parallel", "arbitrary")))(a, b, *extras)


class Row:
    def __init__(self, arr, width=None, col=0, lead=None, diff=True):
        self.arr, self.col, self.lead, self.diff = arr, col, lead, diff
        self.width = arr.shape[-1] if width is None else width

    def spec(self, t):
        col, lead = self.col, self.lead
        if lead is None:
            return pl.BlockSpec((t, self.width), lambda i: (i, col))
        return pl.BlockSpec((None, t, self.width), lambda i: (lead, i, col))


def _whole(p):
    return pl.BlockSpec(p.shape, lambda i: (0,) * p.ndim)


def rowwise(fn, rows, params, outs, *, n_rows, tile, name):
    t = _pick_rows(n_rows, tile)
    nr, npar, no = len(rows), len(params), len(outs)

    def body(*refs):
        r_refs, p_refs, o_refs = refs[:nr], refs[nr:nr + npar], refs[nr + npar:]
        res = fn(*[r[...].astype(f32) for r in r_refs], *[p[...] for p in p_refs])
        for o_ref, val in zip(o_refs, res):
            o_ref[...] = val.astype(o_ref.dtype)

    return pl.pallas_call(
        body, name=name, grid=(n_rows // t,),
        in_specs=[r.spec(t) for r in rows] + [_whole(p) for p in params],
        out_specs=[pl.BlockSpec((t, w), lambda i: (i, 0)) for w, _ in outs],
        out_shape=[jax.ShapeDtypeStruct((n_rows, w), dt) for w, dt in outs],
        compiler_params=_cp(("arbitrary",)))(*[r.arr for r in rows], *params)


def rowwise_vjp(fn, rows, params, cts, *, n_rows, tile, name, row_grad_dtypes=None, param_diff=None):
    t = _pick_rows(n_rows, tile)
    nr, npar, nc = len(rows), len(params), len(cts)
    param_diff = [True] * npar if param_diff is None else param_diff
    d_rows = [k for k, r in enumerate(rows) if r.diff]
    d_pars = [k for k in range(npar) if param_diff[k]]
    row_grad_dtypes = [f32] * len(d_rows) if row_grad_dtypes is None else row_grad_dtypes

    def body(*refs):
        r_refs, p_refs = refs[:nr], refs[nr:nr + npar]
        c_refs = refs[nr + npar:nr + npar + nc]
        dr_refs = refs[nr + npar + nc:nr + npar + nc + len(d_rows)]
        dp_refs = refs[nr + npar + nc + len(d_rows):]
        rv = [r[...].astype(f32) for r in r_refs]
        pv = [p[...] for p in p_refs]

        def g(*dvals):
            full_r, full_p = list(rv), list(pv)
            for k, val in zip(d_rows, dvals[:len(d_rows)]):
                full_r[k] = val
            for k, val in zip(d_pars, dvals[len(d_rows):]):
                full_p[k] = val
            return tuple(fn(*full_r, *full_p))

        prim = [rv[k] for k in d_rows] + [pv[k].astype(f32) for k in d_pars]
        _, pull = jax.vjp(g, *prim)
        grads = pull(tuple(c[...].astype(f32) for c in c_refs))
        for ref, val in zip(dr_refs, grads[:len(d_rows)]):
            ref[...] = val.astype(ref.dtype)

        @pl.when(pl.program_id(0) == 0)
        def _():
            for ref in dp_refs:
                ref[...] = jnp.zeros_like(ref)

        for ref, val in zip(dp_refs, grads[len(d_rows):]):
            ref[...] += val

    out_specs = [pl.BlockSpec((t, rows[k].width), lambda i: (i, 0)) for k in d_rows]
    out_specs += [_whole(params[k]) for k in d_pars]
    out_shape = [jax.ShapeDtypeStruct((n_rows, rows[k].width), dt) for k, dt in zip(d_rows, row_grad_dtypes)]
    out_shape += [jax.ShapeDtypeStruct(params[k].shape, f32) for k in d_pars]
    res = pl.pallas_call(
        body, name=name, grid=(n_rows // t,),
        in_specs=[r.spec(t) for r in rows] + [_whole(p) for p in params] + [c.spec(t) for c in cts],
        out_specs=out_specs, out_shape=out_shape,
        compiler_params=_cp(("arbitrary",)))(*[r.arr for r in rows], *params, *[c.arr for c in cts])
    return res[:len(d_rows)], res[len(d_rows):]


def f_norm_mod(x, g, shift, scale):
    return (_rms(x, x.shape[-1]) * g * (1.0 + scale) + shift,)


def f_norm_mod_thru(x, g, shift, scale):
    return f_norm_mod(x, g, shift, scale) + (x,)


def f_resid_norm_mod(x, mixed, gate1, g2, shift2, scale2):
    x1 = x + gate1 * mixed
    return (x1,) + f_norm_mod(x1, g2, shift2, scale2)


def _rope_rot():
    i = lax.broadcasted_iota(jnp.int32, (LANE, LANE), 0)
    j = lax.broadcasted_iota(jnp.int32, (LANE, LANE), 1)
    neg = jnp.where((i == j + HALF) & (j < HALF), -1.0, 0.0)
    pos = jnp.where((i == j - HALF) & (j >= HALF) & (j < ROPE), 1.0, 0.0)
    return (neg + pos).astype(f32)


def make_f_mla_prep(n_heads, q_scale):
    def fn(cq, ckv, kpe, pos, gq, gkv, gqn, gkn, w_uq, w_ukv, freqs):
        rot = _rope_rot()
        ang = pos * freqs
        cos, sin = jnp.cos(ang), jnp.sin(ang)

        def rope(u):
            return u * cos + hdot(u, rot) * sin

        qraw = bdot(_rms(cq, cq.shape[-1]) * gq, w_uq)
        kv = bdot(_rms(ckv, ckv.shape[-1]) * gkv, w_ukv)
        kpe_ss = jnp.sum(kpe * kpe, axis=-1, keepdims=True)
        qs, ks = [], []
        for h in range(n_heads):
            qh = _rms(qraw[:, h * QK_PAD:(h + 1) * QK_PAD], QK_DIM) * gqn
            qs += [qh[:, :NOPE], rope(qh[:, NOPE:])]
            kn = kv[:, h * NOPE:(h + 1) * NOPE]
            r = lax.rsqrt((jnp.sum(kn * kn, axis=-1, keepdims=True) + kpe_ss) * (1.0 / QK_DIM) + EPS)
            ks += [kn * r * gkn[:, :NOPE], rope(kpe * r * gkn[:, NOPE:])]
        return jnp.concatenate(qs, axis=-1) * q_scale, jnp.concatenate(ks, axis=-1), kv[:, n_heads * NOPE:]
    return fn


def make_f_mlstm_post(n_heads, dm):
    def fn(hf, hb, o, g):
        hm = hf + hb
        outs = []
        for h in range(n_heads):
            sl = slice(h * dm, (h + 1) * dm)
            outs.append(jax.nn.sigmoid(o[:, sl]) * (_rms(hm[:, sl], dm) * g[:, sl]))
        return (jnp.concatenate(outs, axis=-1),)
    return fn


def f_add_pairs(a0, a1, b0, b1):
    return (jnp.concatenate([a0 + a1, b0 + b1], axis=-1),)


def f_add(a, b):
    return (a + b,)


def loss_head(x1, y, target, gate2, name, tile=256):
    S, D = x1.shape
    t = _pick_rows(S, tile)

    def body(x1_ref, y_ref, t_ref, g_ref, loss_ref, dout_ref, dy_ref, dgate_ref):
        @pl.when(pl.program_id(0) == 0)
        def _():
            loss_ref[...] = jnp.zeros_like(loss_ref)
            dgate_ref[...] = jnp.zeros_like(dgate_ref)

        yv, gv = y_ref[...], g_ref[...]
        e = x1_ref[...] + gv * yv - t_ref[...]
        loss_ref[...] += 0.5 * jnp.sum(jnp.sum(e * e, axis=-1, keepdims=True) * (1.0 / D), axis=0, keepdims=True)
        d_out = e * (1.0 / D)
        dout_ref[...] = d_out
        dy_ref[...] = (d_out * gv).astype(dy_ref.dtype)
        dgate_ref[...] += jnp.sum(d_out * yv, axis=0, keepdims=True)

    row = pl.BlockSpec((t, D), lambda i: (i, 0))
    return pl.pallas_call(
        body, name=name, grid=(S // t,),
        in_specs=[row, row, row, pl.BlockSpec((1, D), lambda i: (0, 0))],
        out_specs=[pl.BlockSpec((1, 1), lambda i: (0, 0)), row, row, pl.BlockSpec((1, D), lambda i: (0, 0))],
        out_shape=[jax.ShapeDtypeStruct((1, 1), f32), jax.ShapeDtypeStruct((S, D), f32),
                   jax.ShapeDtypeStruct((S, D), bf16), jax.ShapeDtypeStruct((1, D), f32)],
        compiler_params=_cp(("arbitrary",)))(x1, y, target, gate2)


def ada_fwd(c_all, w_blk, b_blk, name):
    B, D = c_all.shape
    N = w_blk.shape[1]
    tn = _pick(N, 512)

    def body(c_ref, w_ref, b_ref, o_ref):
        o_ref[...] = bdot(_silu(c_ref[...]), w_ref[...]) + b_ref[...]

    return pl.pallas_call(
        body, name=name, grid=(N // tn,),
        in_specs=[pl.BlockSpec((B, D), lambda j: (0, 0)), pl.BlockSpec((D, tn), lambda j: (0, j)),
                  pl.BlockSpec((1, tn), lambda j: (0, j))],
        out_specs=pl.BlockSpec((B, tn), lambda j: (0, j)),
        out_shape=jax.ShapeDtypeStruct((B, N), f32), compiler_params=_cp(("arbitrary",)))(c_all, w_blk, b_blk)


def ada_wgrad(c_all, dmod_blk, name):
    B, D = c_all.shape
    N = dmod_blk.shape[1]
    tn = _pick(N, 512)

    def body(c_ref, d_ref, o_ref):
        o_ref[...] = hdot_tn(_silu(c_ref[...]), d_ref[...])

    return pl.pallas_call(
        body, name=name, grid=(N // tn,),
        in_specs=[pl.BlockSpec((B, D), lambda j: (0, 0)), pl.BlockSpec((B, tn), lambda j: (0, j))],
        out_specs=pl.BlockSpec((D, tn), lambda j: (0, j)),
        out_shape=jax.ShapeDtypeStruct((D, N), f32), compiler_params=_cp(("arbitrary",)))(c_all, dmod_blk)


def _nt(a, b):
    return lax.dot_general(a, b, (((1,), (1,)), ((), ())), preferred_element_type=f32)


def _tn(a, b):
    return lax.dot_general(a, b, (((0,), (0,)), ((), ())), preferred_element_type=f32)


def flash_fwd(q, k, v, n_heads, name, tq=512, tk=8192, sub=1024):
    S = q.shape[0]
    tq, tk = _pick(S, tq), _pick(S, tk)
    sub = _pick(tk, sub)
    nk, nsub = S // tk, tk // sub

    def body(q_ref, k_ref, v_ref, o_ref, lse_ref, m_sc, l_sc, acc_sc):
        j = pl.program_id(2)

        @pl.when(j == 0)
        def _():
            m_sc[...] = jnp.full_like(m_sc, -jnp.inf)
            l_sc[...] = jnp.zeros_like(l_sc)
            acc_sc[...] = jnp.zeros_like(acc_sc)

        qv = q_ref[...]
        m = m_sc[...]
        ss = [_nt(qv, k_ref[b * sub:(b + 1) * sub, :]) for b in range(nsub)]
        mx = ss[0]
        for s in ss[1:]:
            mx = jnp.maximum(mx, s)
        m_new = jnp.maximum(m, jnp.max(mx, axis=-1, keepdims=True))
        alpha = jnp.exp2(m - m_new)
        psum, pv = None, None
        for b in range(nsub):
            p = jnp.exp2(ss[b] - m_new)
            d = jnp.dot(p.astype(bf16), v_ref[b * sub:(b + 1) * sub, :], preferred_element_type=f32)
            psum = p if psum is None else psum + p
            pv = d if pv is None else pv + d
        m, l, acc = m_new, alpha * l_sc[...] + jnp.sum(psum, axis=-1, keepdims=True), alpha * acc_sc[...] + pv
        m_sc[...], l_sc[...], acc_sc[...] = m, l, acc

        @pl.when(j == nk - 1)
        def _():
            o_ref[...] = (acc / l).astype(o_ref.dtype)
            lse_ref[...] = m + jnp.log2(l)

    return pl.pallas_call(
        body, name=name, grid=(n_heads, S // tq, nk),
        in_specs=[pl.BlockSpec((tq, QK_PAD), lambda h, i, j: (i, h)),
                  pl.BlockSpec((tk, QK_PAD), lambda h, i, j: (j, h)),
                  pl.BlockSpec((tk, V_DIM), lambda h, i, j: (j, h))],
        out_specs=[pl.BlockSpec((tq, V_DIM), lambda h, i, j: (i, h)),
                   pl.BlockSpec((None, tq, 1), lambda h, i, j: (h, i, 0))],
        out_shape=[jax.ShapeDtypeStruct((S, n_heads * V_DIM), bf16), jax.ShapeDtypeStruct((n_heads, S, 1), f32)],
        scratch_shapes=[pltpu.VMEM((tq, 1), f32), pltpu.VMEM((tq, 1), f32), pltpu.VMEM((tq, V_DIM), f32)],
        compiler_params=_cp(("parallel", "parallel", "arbitrary")))(q, k, v)


def flash_bwd(q, k, v, o, lse_row, do, do_col0, n_heads, name, tq=512, tk=8192, sub=512):
    S = q.shape[0]
    tq, tk = _pick(S, tq), _pick(S, tk)
    sub = _pick(tk, sub)
    nsub = tk // sub
    ln2 = math.log(2.0)

    def body(q_ref, k_ref, v_ref, o_ref, lse_ref, do_ref, dq_ref, dk_ref, dv_ref):
        i, j = pl.program_id(1), pl.program_id(2)

        @pl.when(j == 0)
        def _():
            dq_ref[...] = jnp.zeros_like(dq_ref)

        @pl.when((i == 0) & (j == 0))
        def _():
            dk_ref[...] = jnp.zeros_like(dk_ref)
            dv_ref[...] = jnp.zeros_like(dv_ref)

        qv = q_ref[...]
        dof = do_ref[...].astype(f32)
        do_b = dof.astype(bf16)
        do_s = (dof * ln2).astype(bf16)
        delta = hdot_nt(jnp.ones((8, V_DIM), f32), dof * ln2 * o_ref[...].astype(f32))[0:1, :]
        lse = lse_ref[...]
        dq = jnp.zeros((tq, QK_PAD), f32)
        for b in range(nsub):
            kb = k_ref[b * sub:(b + 1) * sub, :]
            rows = pl.ds(pl.multiple_of(j * tk + b * sub, sub), sub)
            pt = jnp.exp2(_nt(kb, qv) - lse)
            dpt = _nt(v_ref[b * sub:(b + 1) * sub, :], do_s)
            dst = (pt * (dpt - delta)).astype(bf16)
            dv_ref[rows, :] += jnp.dot(pt.astype(bf16), do_b, preferred_element_type=f32)
            dk_ref[rows, :] += jnp.dot(dst, qv, preferred_element_type=f32)
            dq = dq + _tn(dst, kb)
        dq_ref[...] += dq

    return pl.pallas_call(
        body, name=name, grid=(n_heads, S // tq, S // tk),
        in_specs=[pl.BlockSpec((tq, QK_PAD), lambda h, i, j: (i, h)),
                  pl.BlockSpec((tk, QK_PAD), lambda h, i, j: (j, h)),
                  pl.BlockSpec((tk, V_DIM), lambda h, i, j: (j, h)),
                  pl.BlockSpec((tq, V_DIM), lambda h, i, j: (i, h)),
                  pl.BlockSpec((None, 1, tq), lambda h, i, j: (h, 0, i)),
                  pl.BlockSpec((tq, V_DIM), lambda h, i, j: (i, do_col0 + h))],
        out_specs=[pl.BlockSpec((tq, QK_PAD), lambda h, i, j: (i, h)),
                   pl.BlockSpec((S, QK_PAD), lambda h, i, j: (0, h)),
                   pl.BlockSpec((S, V_DIM), lambda h, i, j: (0, h))],
        out_shape=[jax.ShapeDtypeStruct((S, n_heads * QK_PAD), f32), jax.ShapeDtypeStruct((S, n_heads * QK_PAD), f32),
                   jax.ShapeDtypeStruct((S, n_heads * V_DIM), f32)],
        compiler_params=_cp(("arbitrary", "arbitrary", "arbitrary")))(q, k, v, o, lse_row, do)


def _shifted(prev, cur, nxt, k, first, last):
    if k == 0:
        return cur
    t = cur.shape[0]
    r = lax.broadcasted_iota(jnp.int32, cur.shape, 0)
    if k < 0:
        body = pltpu.roll(cur, -k, 0)
        edge = jnp.where(first, 0.0, pltpu.roll(prev, -k, 0))
        return jnp.where(r < -k, edge, body)
    body = pltpu.roll(cur, t - k, 0)
    edge = jnp.where(last, 0.0, pltpu.roll(nxt, t - k, 0))
    return jnp.where(r >= t - k, edge, body)


def _halo_specs(t, width, n_tiles, col=0):
    return [pl.BlockSpec((t, width), lambda i: (jnp.maximum(i - 1, 0), col)),
            pl.BlockSpec((t, width), lambda i: (i, col)),
            pl.BlockSpec((t, width), lambda i: (jnp.minimum(i + 1, n_tiles - 1), col))]


def conv_fwd(proj, width, w, b, name, tile=256):
    S = proj.shape[0]
    t = _pick_rows(S, tile)
    n_tiles = S // t

    def body(p_ref, c_ref, n_ref, w_ref, b_ref, z_ref):
        i = pl.program_id(0)
        first, last = i == 0, i == n_tiles - 1
        prev, cur, nxt = p_ref[...], c_ref[...], n_ref[...]
        z = b_ref[...] + jnp.zeros_like(cur)
        for j in range(CONV_W):
            z = z + w_ref[j:j + 1, :] * _shifted(prev, cur, nxt, j - CONV_W // 2, first, last)
        z_ref[...] = z

    return pl.pallas_call(
        body, name=name, grid=(n_tiles,),
        in_specs=_halo_specs(t, width, n_tiles) + [_whole(w), _whole(b)],
        out_specs=pl.BlockSpec((t, width), lambda i: (i, 0)),
        out_shape=jax.ShapeDtypeStruct((S, width), f32),
        compiler_params=_cp(("arbitrary",)))(proj, proj, proj, w, b)


def conv_bwd(dz, proj, width, w, name, tile=256):
    S = proj.shape[0]
    t = _pick_rows(S, tile)
    n_tiles = S // t

    def body(dp_ref, dc_ref, dn_ref, up_ref, uc_ref, un_ref, w_ref, du_ref, dw_ref, db_ref):
        i = pl.program_id(0)
        first, last = i == 0, i == n_tiles - 1

        @pl.when(first)
        def _():
            dw_ref[...] = jnp.zeros_like(dw_ref)
            db_ref[...] = jnp.zeros_like(db_ref)

        dprev, dcur, dnxt = dp_ref[...], dc_ref[...], dn_ref[...]
        uprev, ucur, unxt = up_ref[...], uc_ref[...], un_ref[...]
        du = jnp.zeros_like(dcur)
        for j in range(CONV_W):
            k = j - CONV_W // 2
            du = du + w_ref[j:j + 1, :] * _shifted(dprev, dcur, dnxt, -k, first, last)
            dw_ref[j:j + 1, :] += jnp.sum(dcur * _shifted(uprev, ucur, unxt, k, first, last), axis=0, keepdims=True)
        du_ref[...] = du
        db_ref[...] += jnp.sum(dcur, axis=0, keepdims=True)

    return pl.pallas_call(
        body, name=name, grid=(n_tiles,),
        in_specs=_halo_specs(t, width, n_tiles) + _halo_specs(t, width, n_tiles) + [_whole(w)],
        out_specs=[pl.BlockSpec((t, width), lambda i: (i, 0)), pl.BlockSpec((8, width), lambda i: (0, 0)),
                   pl.BlockSpec((1, width), lambda i: (0, 0))],
        out_shape=[jax.ShapeDtypeStruct((S, width), f32), jax.ShapeDtypeStruct((8, width), f32),
                   jax.ShapeDtypeStruct((1, width), f32)],
        compiler_params=_cp(("arbitrary",)))(dz, dz, dz, proj, proj, proj, w)


def _mlstm_step(dm, d, C, n, m, zq, zk, v, ic, fc, ir, fr, bi, bf_):
    L = zq.shape[0]
    q = _silu(zq)
    k = _silu(zk) * (dm ** -0.5)
    i_c, f_c = ic + bi, jax.nn.log_sigmoid(fc + bf_)
    i_r, f_r = ir + bi, jax.nn.log_sigmoid(fr + bf_)
    r = lax.broadcasted_iota(jnp.int32, (L, L), 0)
    c = lax.broadcasted_iota(jnp.int32, (L, L), 1)
    sgn = jnp.where(d == 0, r - c, c - r)
    mask = sgn >= 0
    b_c = jnp.sum(jnp.where(mask, f_r, 0.0), axis=-1, keepdims=True)
    b_r = jnp.sum(jnp.where(sgn <= 0, f_c, 0.0), axis=0, keepdims=True)
    log_inter = b_c + m
    logD = jnp.where(mask, b_c - b_r + i_r, -jnp.inf)
    m_t = jnp.maximum(log_inter, jnp.max(logD, axis=-1, keepdims=True))
    Dm = jnp.exp(logD - m_t)
    w_inter = jnp.exp(log_inter - m_t)
    scores = bdot_nt(q, k) * Dm
    num = bdot(scores, v) + w_inter * bdot_nt(q, C)
    den = jnp.sum(scores, axis=-1, keepdims=True) + w_inter * jnp.sum(q * n, axis=-1, keepdims=True)
    h = num / jnp.maximum(jnp.abs(den), jnp.exp(-m_t))
    bL = jnp.sum(f_c, axis=0, keepdims=True)
    log_w = bL - b_c + i_c
    m_new = jnp.maximum(bL + m, jnp.max(log_w, axis=0, keepdims=True))
    decay = jnp.exp(bL + m - m_new)
    w = jnp.exp(log_w - m_new)
    C_new = decay * C + bdot_tn(w * v, k)
    n_new = decay * n + jnp.sum(w * k, axis=0, keepdims=True)
    return C_new, n_new, m_new, h


def _mlstm_in_specs(L, dm, hm, nc, step_of):
    def chunk(d, j):
        s = step_of(j)
        return s + d * (nc - 1 - 2 * s)
    return [
        pl.BlockSpec((L, dm), lambda d, h, j: (chunk(d, j), h)),
        pl.BlockSpec((L, dm), lambda d, h, j: (chunk(d, j), hm + h)),
        pl.BlockSpec((L, dm), lambda d, h, j: (chunk(d, j), 2 * hm + h)),
        pl.BlockSpec((None, None, L, 1), lambda d, h, j: (d, h, chunk(d, j), 0)),
        pl.BlockSpec((None, None, L, 1), lambda d, h, j: (d, h, chunk(d, j), 0)),
        pl.BlockSpec((None, None, 1, L), lambda d, h, j: (d, h, 0, chunk(d, j))),
        pl.BlockSpec((None, None, 1, L), lambda d, h, j: (d, h, 0, chunk(d, j))),
        pl.BlockSpec((None, None, 1, 1), lambda d, h, j: (d, h, 0, 0)),
        pl.BlockSpec((None, None, 1, 1), lambda d, h, j: (d, h, 0, 0)),
    ], chunk


def mlstm_fwd(z, proj, gates, hm, dm, name):
    S = z.shape[0]
    L = CHUNK
    nc = S // L
    in_specs, chunk = _mlstm_in_specs(L, dm, hm, nc, lambda j: j)

    def body(zq, zk, v, ic, fc, ir, fr, bi, bf_, h_ref, cs_ref, ns_ref, ms_ref, C_sc, n_sc, m_sc):
        d = pl.program_id(0)

        @pl.when(pl.program_id(2) == 0)
        def _():
            C_sc[...] = jnp.zeros_like(C_sc)
            n_sc[...] = jnp.zeros_like(n_sc)
            m_sc[...] = jnp.full_like(m_sc, M_INIT)

        C, n, m = C_sc[...], n_sc[...], m_sc[...]
        cs_ref[...], ns_ref[...], ms_ref[...] = C, n, m
        C2, n2, m2, h = _mlstm_step(dm, d, C, n, m, zq[...], zk[...], v[...], ic[...], fc[...], ir[...], fr[...],
                                    bi[...], bf_[...])
        C_sc[...], n_sc[...], m_sc[...] = C2, n2, m2
        h_ref[...] = h

    return pl.pallas_call(
        body, name=name, grid=(2, hm, nc), in_specs=in_specs,
        out_specs=[pl.BlockSpec((None, L, dm), lambda d, h, j: (d, chunk(d, j), h)),
                   pl.BlockSpec((None, None, None, dm, dm), lambda d, h, j: (d, h, j, 0, 0)),
                   pl.BlockSpec((None, None, None, 1, dm), lambda d, h, j: (d, h, j, 0, 0)),
                   pl.BlockSpec((None, None, None, 1, 1), lambda d, h, j: (d, h, j, 0, 0))],
        out_shape=[jax.ShapeDtypeStruct((2, S, hm * dm), f32), jax.ShapeDtypeStruct((2, hm, nc, dm, dm), f32),
                   jax.ShapeDtypeStruct((2, hm, nc, 1, dm), f32), jax.ShapeDtypeStruct((2, hm, nc, 1, 1), f32)],
        scratch_shapes=[pltpu.VMEM((dm, dm), f32), pltpu.VMEM((1, dm), f32), pltpu.VMEM((1, 1), f32)],
        compiler_params=_cp(("arbitrary", "arbitrary", "arbitrary")))(z, z, proj, *gates)


def mlstm_bwd(z, proj, gates, states, dh, hm, dm, name):
    S = z.shape[0]
    L = CHUNK
    nc = S // L
    in_specs, chunk = _mlstm_in_specs(L, dm, hm, nc, lambda j: nc - 1 - j)
    st = lambda j: nc - 1 - j
    in_specs = in_specs + [
        pl.BlockSpec((None, None, None, dm, dm), lambda d, h, j: (d, h, st(j), 0, 0)),
        pl.BlockSpec((None, None, None, 1, dm), lambda d, h, j: (d, h, st(j), 0, 0)),
        pl.BlockSpec((None, None, None, 1, 1), lambda d, h, j: (d, h, st(j), 0, 0)),
        pl.BlockSpec((None, L, dm), lambda d, h, j: (d, chunk(d, j), h)),
    ]

    def body(zq, zk, v, ic, fc, ir, fr, bi, bf_, cs, ns, ms, dh_ref,
             dzq, dzk, dv, dic, dfc, dir_, dfr, dbi, dbf, dC_sc, dn_sc, dm_sc):
        d = pl.program_id(0)

        @pl.when(pl.program_id(2) == 0)
        def _():
            dC_sc[...] = jnp.zeros_like(dC_sc)
            dn_sc[...] = jnp.zeros_like(dn_sc)
            dm_sc[...] = jnp.zeros_like(dm_sc)
            dbi[...] = jnp.zeros_like(dbi)
            dbf[...] = jnp.zeros_like(dbf)

        prim = (cs[...], ns[...], ms[...], zq[...], zk[...], v[...], ic[...], fc[...], ir[...], fr[...],
                bi[...], bf_[...])
        _, pull = jax.vjp(functools.partial(_mlstm_step, dm, d), *prim)
        g = pull((dC_sc[...], dn_sc[...], dm_sc[...], dh_ref[...]))
        dC_sc[...], dn_sc[...], dm_sc[...] = g[0], g[1], g[2]
        dzq[...], dzk[...], dv[...] = g[3], g[4], g[5]
        dic[...], dfc[...], dir_[...], dfr[...] = g[6], g[7], g[8], g[9]
        dbi[...] += g[10]
        dbf[...] += g[11]

    tile = pl.BlockSpec((None, L, dm), lambda d, h, j: (d, chunk(d, j), h))
    col = pl.BlockSpec((None, None, L, 1), lambda d, h, j: (d, h, chunk(d, j), 0))
    row = pl.BlockSpec((None, None, 1, L), lambda d, h, j: (d, h, 0, chunk(d, j)))
    one = pl.BlockSpec((None, None, 1, 1), lambda d, h, j: (d, h, 0, 0))
    big = jax.ShapeDtypeStruct((2, S, hm * dm), f32)
    cols = jax.ShapeDtypeStruct((2, hm, S, 1), f32)
    rows = jax.ShapeDtypeStruct((2, hm, 1, S), f32)
    ones = jax.ShapeDtypeStruct((2, hm, 1, 1), f32)
    return pl.pallas_call(
        body, name=name, grid=(2, hm, nc), in_specs=in_specs,
        out_specs=[tile, tile, tile, col, col, row, row, one, one],
        out_shape=[big, big, big, cols, cols, rows, rows, ones, ones],
        scratch_shapes=[pltpu.VMEM((dm, dm), f32), pltpu.VMEM((1, dm), f32), pltpu.VMEM((1, 1), f32)],
        compiler_params=_cp(("arbitrary", "arbitrary", "arbitrary")))(z, z, proj, *gates, *states, dh)


def _blocks_to_cols(g):
    return g.transpose(1, 0, 2).reshape(g.shape[1], N_DEV * g.shape[2])


def _cols_to_blocks(a):
    return a.reshape(a.shape[0], N_DEV, a.shape[1] // N_DEV).transpose(1, 0, 2)


def _pad_cols(a, n):
    return jnp.pad(a, ((0, 0), (0, n - a.shape[1])))


def _relu2(u):
    r = jnp.maximum(u, 0.0)
    return r * r


def kernel(x, c, positions, w_ada, b_ada, norm_mix_g, w_in, b_gates, conv_w, conv_b, q_lora_g, w_uq, kv_lora_g, w_ukv, q_norm_g, k_norm_g, mlstm_norm_g, w_out, norm_mlp_g, w_ff1, w_ff2, loss_target, m_w_ada, m_b_ada, m_norm_mix_g, m_w_in, m_b_gates, m_conv_w, m_conv_b, m_q_lora_g, m_w_uq, m_kv_lora_g, m_w_ukv, m_q_norm_g, m_k_norm_g, m_mlstm_norm_g, m_w_out, m_norm_mlp_g, m_w_ff1, m_w_ff2, v_w_ada, v_b_ada, v_norm_mix_g, v_w_in, v_b_gates, v_conv_w, v_conv_b, v_q_lora_g, v_w_uq, v_kv_lora_g, v_w_ukv, v_q_norm_g, v_k_norm_g, v_mlstm_norm_g, v_w_out, v_norm_mlp_g, v_w_ff1, v_w_ff2):
    S, D = x.shape[1], x.shape[2]
    QL, KVL = w_uq.shape[1], w_ukv.shape[1]
    H = w_uq.shape[2] * N_DEV // QK_DIM
    HM = mlstm_norm_g.shape[1]
    DM = mlstm_norm_g.shape[2] * N_DEV
    MW = HM * DM
    D_IN = w_in.shape[2] * N_DEV
    NADA = w_ada.shape[2]
    assert D_IN == QL + KVL + ROPE + 4 * MW + N_GATES and DM % LANE == 0 and S % CHUNK == 0
    assert (4 * MW) % QL == 0 and (4 * MW + QL) % KVL == 0 and KVL % LANE == 0
    idx = 4 * lax.axis_index("x") + 2 * lax.axis_index("y") + lax.axis_index("c")
    x2, tgt = x[0], loss_target[0]

    g_in, g_uq, g_ukv, g_out, g_ff1, g_ff2, g_conv, g_mn, c_all = all_gather(
        [w_in[0].astype(bf16), w_uq[0].astype(bf16), w_ukv[0].astype(bf16), w_out[0].astype(bf16),
         w_ff1[0].astype(bf16), w_ff2[0].astype(bf16), conv_w[0], mlstm_norm_g[0], c], "gather_weights")
    c_all = c_all.reshape(N_DEV, D)

    wi = _blocks_to_cols(g_in)
    o_cq, o_ckv, o_kpe, o_m, o_g = 0, QL, QL + KVL, QL + KVL + ROPE, QL + KVL + ROPE + 4 * MW
    w_in_p = jnp.concatenate([wi[:, o_m:o_g], wi[:, o_cq:o_kpe], _pad_cols(wi[:, o_kpe:o_m], LANE),
                              _pad_cols(wi[:, o_g:], LANE)], axis=1)
    NP = w_in_p.shape[1]
    cb_cq, cb_ckv, cb_kpe, cb_g = 4 * MW // QL, (4 * MW + QL) // KVL, (4 * MW + QL + KVL) // LANE, NP // LANE - 1
    w_uq_p = jnp.pad(_blocks_to_cols(g_uq).reshape(QL, H, QK_DIM), ((0, 0), (0, 0), (0, QK_PAD - QK_DIM))).reshape(QL, H * QK_PAD)
    w_ukv_p = _blocks_to_cols(g_ukv).reshape(KVL, H, 2, NOPE).transpose(0, 2, 1, 3).reshape(KVL, 2 * H * NOPE)
    w_out_f = g_out.reshape(N_DEV * g_out.shape[1], D)
    w_ff2_f = g_ff2.reshape(N_DEV * g_ff2.shape[1], D)
    conv_w_f = jnp.pad(_blocks_to_cols(g_conv), ((0, 8 - CONV_W), (0, 0)))
    mn_g = _blocks_to_cols(g_mn).reshape(1, MW)
    gqn = _pad_cols(q_norm_g, QK_PAD)
    gkn = _pad_cols(k_norm_g, QK_PAD)
    fr_np = np.zeros((1, LANE), np.float32)
    fr_np[0, :HALF] = fr_np[0, HALF:ROPE] = ROPE_THETA ** (-np.arange(HALF, dtype=np.float32) / HALF)
    freqs = jnp.asarray(fr_np)
    pos = positions.astype(f32).reshape(S, 1)

    b_blk = lax.dynamic_slice(b_ada, (0, idx * NADA), (1, NADA))
    mod_part = ada_fwd(c_all, w_ada[0], b_blk, "ada_fwd")
    (mod_all,) = all_gather([mod_part], "gather_mod")
    mod = lax.dynamic_index_in_dim(mod_all, idx, axis=1, keepdims=False).reshape(1, N_DEV * NADA)
    shift1, scale1, gate1, shift2, scale2, gate2 = [mod[:, k * D:(k + 1) * D] for k in range(6)]

    (h,) = rowwise(f_norm_mod, [Row(x2)], [norm_mix_g, shift1, scale1], [(D, bf16)], n_rows=S, tile=256, name="norm_mix")
    proj = mm(h, w_in_p, name="proj_in", out_dtype=f32)
    r_cq, r_ckv, r_kpe = Row(proj, QL, cb_cq), Row(proj, KVL, cb_ckv), Row(proj, LANE, cb_kpe)
    f_prep = make_f_mla_prep(H, QK_DIM ** -0.5 * math.log2(math.e))
    prep_params = [q_lora_g, kv_lora_g, gqn, gkn, w_uq_p, w_ukv_p, freqs]
    Q, K, V = rowwise(f_prep, [r_cq, r_ckv, r_kpe, Row(pos, diff=False)], prep_params,
                      [(H * QK_PAD, bf16), (H * QK_PAD, bf16), (H * V_DIM, bf16)], n_rows=S, tile=256, name="mla_prep")
    attn, lse = flash_fwd(Q, K, V, H, "flash_fwd")

    conv_bias = conv_b
    z = conv_fwd(proj, 2 * MW, conv_w_f, conv_bias, "conv_fwd")
    graw = proj[:, cb_g * LANE:cb_g * LANE + N_GATES].reshape(S, 4, HM)
    gcol = graw.transpose(1, 2, 0).reshape(2, 2, HM, S)
    bg = b_gates.reshape(2, 2, HM)
    gates = (gcol[:, 0].reshape(2, HM, S, 1), gcol[:, 1].reshape(2, HM, S, 1),
             gcol[:, 0].reshape(2, HM, 1, S), gcol[:, 1].reshape(2, HM, 1, S),
             bg[:, 0].reshape(2, HM, 1, 1), bg[:, 1].reshape(2, HM, 1, 1))
    hdir, cs, ns, ms = mlstm_fwd(z, proj, gates, HM, DM, "mlstm_fwd")
    f_post = make_f_mlstm_post(HM, DM)
    post_rows = [Row(hdir, MW, 0, lead=0), Row(hdir, MW, 0, lead=1), Row(proj, MW, 3)]
    (ml_out,) = rowwise(f_post, post_rows, [mn_g], [(MW, bf16)], n_rows=S, tile=256, name="mlstm_post")

    cat = jnp.concatenate([attn, ml_out], axis=1)
    mixed = mm(cat, w_out_f, name="proj_out", out_dtype=f32)
    mlp_params = [gate1, norm_mlp_g, shift2, scale2]
    x1, h2 = rowwise(f_resid_norm_mod, [Row(x2), Row(mixed)], mlp_params, [(D, f32), (D, bf16)],
                     n_rows=S, tile=256, name="resid_norm_mlp")
    u = mm(h2, g_ff1, name="ff1", out_dtype=bf16)
    y = mm(u, w_ff2_f, name="ff2", a_fn=_relu2, out_dtype=f32)
    loss_l, d_out, d_y, d_gate2 = loss_head(x1, y, tgt, gate2, "loss_head")
    loss = lax.psum(loss_l[0, 0], AXES)

    dw_ff2 = mm(u, d_y, name="dw_ff2", ta=True, a_fn=_relu2, out_dtype=bf16)
    d_u = mm(d_y, w_ff2_f, name="d_u", tb=True, epi=lambda acc, uu: acc * (2.0 * jnp.maximum(uu.astype(f32), 0.0)),
             extras=(u,), out_dtype=bf16)
    dw_ff1 = mm(h2, d_u, name="dw_ff1", ta=True, out_dtype=bf16, out_blocks=True)
    d_h2 = mm(d_u, g_ff1, name="d_h2", tb=True, out_dtype=f32, tn=2048)
    (d_x1, d_mixed), (d_gate1, d_g_mlp, d_shift2, d_scale2) = rowwise_vjp(
        f_resid_norm_mod, [Row(x2), Row(mixed)], mlp_params, [Row(d_out), Row(d_h2)],
        n_rows=S, tile=256, name="resid_norm_mlp_bwd", row_grad_dtypes=[f32, bf16])
    dw_out = mm(cat, d_mixed, name="dw_out", ta=True, out_dtype=bf16)
    d_cat = mm(d_mixed, w_out_f, name="d_cat", tb=True, out_dtype=f32, tn=2048)

    (d_hf, d_hb, d_om), (d_mn_g,) = rowwise_vjp(
        f_post, post_rows, [mn_g], [Row(d_cat, MW, H * V_DIM // MW)], n_rows=S, tile=256, name="mlstm_post_bwd")
    dh = jnp.stack([d_hf, d_hb])
    dzq, dzk, dvm, dic, dfc, dir_, dfr, dbi, dbf = mlstm_bwd(z, proj, gates, (cs, ns, ms), dh, HM, DM, "mlstm_bwd")
    (dz,) = rowwise(f_add_pairs, [Row(dzq, MW, 0, lead=0), Row(dzq, MW, 0, lead=1), Row(dzk, MW, 0, lead=0),
                                  Row(dzk, MW, 0, lead=1)], [], [(2 * MW, f32)], n_rows=S, tile=256, name="dz_sum")
    (d_vm,) = rowwise(f_add, [Row(dvm, MW, 0, lead=0), Row(dvm, MW, 0, lead=1)], [], [(MW, bf16)], n_rows=S, tile=256,
                      name="dv_sum")
    d_qk, d_conv_w, d_conv_b = conv_bwd(dz, proj, 2 * MW, conv_w_f, "conv_bwd")
    dg = jnp.stack([dic.reshape(2, HM, S) + dir_.reshape(2, HM, S), dfc.reshape(2, HM, S) + dfr.reshape(2, HM, S)], axis=1)
    d_gates = dg.reshape(4 * HM, S).T
    d_b_gates = jnp.stack([dbi.reshape(2, HM), dbf.reshape(2, HM)], axis=1).reshape(1, N_GATES)

    dq, dk, dv = flash_bwd(Q, K, V, attn, lse.reshape(H, 1, S), d_cat, 0, H, "flash_bwd")
    (d_cq, d_ckv, d_kpe), (d_gq, d_gkv, d_gqn, d_gkn, dw_uq_p, dw_ukv_p) = rowwise_vjp(
        f_prep, [r_cq, r_ckv, r_kpe, Row(pos, diff=False)], prep_params, [Row(dq), Row(dk), Row(dv)],
        n_rows=S, tile=256, name="mla_prep_bwd", row_grad_dtypes=[bf16, bf16, bf16],
        param_diff=[True, True, True, True, True, True, False])

    d_proj = jnp.concatenate([d_qk.astype(bf16), d_vm, d_om.astype(bf16), d_cq, d_ckv, d_kpe,
                              _pad_cols(d_gates.astype(bf16), LANE)], axis=1)
    dw_in_p = mm(h, d_proj, name="dw_in", ta=True, out_dtype=bf16)
    d_h = mm(d_proj, w_in_p, name="d_h", tb=True, out_dtype=f32, tn=2048)
    (grad_x,), (d_g_mix, d_shift1, d_scale1) = rowwise_vjp(
        f_norm_mod_thru, [Row(x2)], [norm_mix_g, shift1, scale1], [Row(d_h), Row(d_x1)],
        n_rows=S, tile=256, name="norm_mix_bwd")

    dmod = jnp.concatenate([d_shift1, d_scale1, d_gate1, d_shift2, d_scale2, d_gate2], axis=1)
    small = [(norm_mix_g, m_norm_mix_g, v_norm_mix_g, d_g_mix), (b_gates, m_b_gates, v_b_gates, d_b_gates),
             (conv_b, m_conv_b, v_conv_b, d_conv_b), (q_lora_g, m_q_lora_g, v_q_lora_g, d_gq),
             (kv_lora_g, m_kv_lora_g, v_kv_lora_g, d_gkv), (q_norm_g, m_q_norm_g, v_q_norm_g, d_gqn[:, :QK_DIM]),
             (k_norm_g, m_k_norm_g, v_k_norm_g, d_gkn[:, :QK_DIM]), (norm_mlp_g, m_norm_mlp_g, v_norm_mlp_g, d_g_mlp),
             (b_ada, m_b_ada, v_b_ada, dmod)]
    sizes = [s[0].shape[1] for s in small]
    P = sum(sizes)
    PP = -(-P // LANE) * LANE
    pack = lambda k: _pad_cols(jnp.concatenate([s[k] for s in small], axis=1), PP)
    (sg_all,) = all_gather([pack(3)], "gather_small_grads")
    s_out = adamw([sg_all[k] for k in range(N_DEV)], pack(0), pack(1), pack(2), "adamw_small")
    offs = np.concatenate([[0], np.cumsum(sizes)])
    small_out = [[o[:, offs[k]:offs[k + 1]] for o in s_out] for k in range(len(small))]

    dmod_all = sg_all[:, 0, offs[-2]:offs[-1]]
    dmod_blk = lax.dynamic_slice(dmod_all, (0, idx * NADA), (N_DEV, NADA))
    g_w_ada = ada_wgrad(c_all, dmod_blk, "ada_wgrad")
    ada_out = adamw([g_w_ada], w_ada[0], m_w_ada[0], v_w_ada[0], "adamw_ada")

    dwi = jnp.concatenate([dw_in_p[:, 4 * MW:4 * MW + QL + KVL + ROPE], dw_in_p[:, :4 * MW],
                           dw_in_p[:, cb_g * LANE:cb_g * LANE + N_GATES]], axis=1)
    dw_uq = dw_uq_p.reshape(QL, H, QK_PAD)[:, :, :QK_DIM].reshape(QL, H * QK_DIM)
    dw_ukv = dw_ukv_p.reshape(KVL, 2, H, NOPE).transpose(0, 2, 1, 3).reshape(KVL, 2 * H * NOPE)
    tiny = [(w_uq, m_w_uq, v_w_uq, _cols_to_blocks(dw_uq)),
            (w_ukv, m_w_ukv, v_w_ukv, _cols_to_blocks(dw_ukv)),
            (conv_w, m_conv_w, v_conv_w, _cols_to_blocks(d_conv_w[:CONV_W])),
            (mlstm_norm_g, m_mlstm_norm_g, v_mlstm_norm_g, _cols_to_blocks(d_mn_g.reshape(HM, DM)))]
    tsizes = [int(np.prod(b[0].shape)) for b in tiny]
    T = sum(tsizes)
    PC = 512
    PR = -(-T // (PC * 64)) * 64
    gpack = jnp.concatenate([b[3].astype(bf16).reshape(N_DEV, -1) for b in tiny], axis=1)
    gpack = jnp.pad(gpack, ((0, 0), (0, PR * PC - T))).reshape(N_DEV, PR, PC)
    wpack = lambda k: jnp.pad(jnp.concatenate([b[k].reshape(1, -1) for b in tiny], axis=1),
                              ((0, 0), (0, PR * PC - T))).reshape(PR, PC)
    large = [(w_in[0], m_w_in[0], v_w_in[0], _cols_to_blocks(dwi)),
             (w_out[0], m_w_out[0], v_w_out[0], dw_out.reshape(N_DEV, -1, D)),
             (w_ff1[0], m_w_ff1[0], v_w_ff1[0], dw_ff1),
             (w_ff2[0], m_w_ff2[0], v_w_ff2[0], dw_ff2.reshape(N_DEV, -1, D)),
             (wpack(0), wpack(1), wpack(2), gpack)]
    gblocks = [b[3] for b in large]

    xi, yi, ci = lax.axis_index("x"), lax.axis_index("y"), lax.axis_index("c")
    from_sibling = pair_exchange(gblocks, "grad_pair_exchange")
    slots = jnp.stack([4 * (1 - xi) + 2 * yi + ci, 4 * xi + 2 * (1 - yi) + ci, 4 * (1 - xi) + 2 * (1 - yi) + ci]).astype(jnp.int32)
    tags = ["w_in", "w_out", "w_ff1", "w_ff2", "tiny"]
    partials = [chip_partials(g, r, slots, "grad_chip_partials_" + t) for g, r, t in zip(gblocks, from_sibling, tags)]
    from_chips = chip_exchange(partials, "grad_chip_exchange")
    l_out = []
    for (w_, m_, v_, g), r, fc, t in zip(large, from_sibling, from_chips, tags):
        mine = lax.dynamic_index_in_dim(g, idx, axis=0, keepdims=False)
        sib = lax.dynamic_index_in_dim(r, 2 * xi + yi, axis=0, keepdims=False)
        l_out.append(adamw([mine, sib, fc[0], fc[1], fc[2]], w_, m_, v_, "adamw_" + t))
    toffs = np.concatenate([[0], np.cumsum(tsizes)])
    tiny_out = [[o.reshape(-1)[toffs[k]:toffs[k + 1]].reshape(tiny[k][0].shape) for o in l_out[4]] for k in range(len(tiny))]
    big_out = [[o[None] for o in l_out[0]], tiny_out[0], tiny_out[1], [o[None] for o in l_out[1]],
               [o[None] for o in l_out[2]], [o[None] for o in l_out[3]], tiny_out[2], tiny_out[3]]

    names = ["w_ada", "b_ada", "norm_mix_g", "w_in", "b_gates", "conv_w", "conv_b", "q_lora_g", "w_uq", "kv_lora_g",
             "w_ukv", "q_norm_g", "k_norm_g", "mlstm_norm_g", "w_out", "norm_mlp_g", "w_ff1", "w_ff2"]
    res = {"w_ada": [o[None] for o in ada_out]}
    for k, nm in enumerate(["norm_mix_g", "b_gates", "conv_b", "q_lora_g", "kv_lora_g", "q_norm_g", "k_norm_g",
                            "norm_mlp_g", "b_ada"]):
        res[nm] = small_out[k]
    for k, nm in enumerate(["w_in", "w_uq", "w_ukv", "w_out", "w_ff1", "w_ff2", "conv_w", "mlstm_norm_g"]):
        res[nm] = big_out[k]
    outs = [loss, grad_x[None]]
    for part in range(4):
        outs += [res[nm][part] for nm in names]
    return tuple(outs)
```

```python
import functools
import math

import numpy as np
import jax
import jax.numpy as jnp
from jax import lax
from jax.experimental import pallas as pl
from jax.experimental.pallas import tpu as pltpu

f32 = jnp.float32
bf16 = jnp.bfloat16

N_DEV = 8
AXES = ("x", "y", "c")
MESH = pl.DeviceIdType.MESH

NOPE = 128
ROPE = 64
HALF = ROPE // 2
QK_DIM = NOPE + ROPE
QK_PAD = 256
V_DIM = 128
ROPE_THETA = 10000.0
CHUNK = 128
CONV_W = 5
N_GATES = 16
EPS = 1e-6
M_INIT = -1e30

ADAM_LR, ADAM_B1, ADAM_B2, ADAM_EPS, ADAM_WD, ADAM_STEP = 0.001, 0.9, 0.999, 1e-08, 0.01, 10

LANE = 128
VMEM_LIMIT = 56 * 1024 * 1024


def _cp(sem=None, vmem=VMEM_LIMIT):
    return pltpu.CompilerParams(dimension_semantics=sem, vmem_limit_bytes=vmem)


def _pick(n, target):
    best = None
    t = LANE
    while t <= min(n, target):
        if n % t == 0:
            best = t
        t += LANE
    return best if best is not None else n


def _pick_rows(n, target):
    t = min(n, target)
    while n % t:
        t -= 8
    return t


def _make_dots(cast, precision):
    def dg(a, b, ca, cb):
        if cast is not None:
            a = a.astype(cast)
            b = b.astype(cast)
        return lax.dot_general(a, b, (((ca,), (cb,)), ((), ())), precision=precision, preferred_element_type=f32)

    @jax.custom_vjp
    def nn(a, b):
        return dg(a, b, 1, 0)

    def nn_f(a, b):
        return dg(a, b, 1, 0), (a, b)

    def nn_b(res, g):
        a, b = res
        return dg(g, b, 1, 1).astype(a.dtype), dg(a, g, 0, 0).astype(b.dtype)

    nn.defvjp(nn_f, nn_b)

    @jax.custom_vjp
    def nt(a, b):
        return dg(a, b, 1, 1)

    def nt_f(a, b):
        return dg(a, b, 1, 1), (a, b)

    def nt_b(res, g):
        a, b = res
        return dg(g, b, 1, 0).astype(a.dtype), dg(g, a, 0, 0).astype(b.dtype)

    nt.defvjp(nt_f, nt_b)

    @jax.custom_vjp
    def tn(a, b):
        return dg(a, b, 0, 0)

    def tn_f(a, b):
        return dg(a, b, 0, 0), (a, b)

    def tn_b(res, g):
        a, b = res
        return dg(b, g, 1, 1).astype(a.dtype), dg(a, g, 1, 0).astype(b.dtype)

    tn.defvjp(tn_f, tn_b)
    return nn, nt, tn


bdot, bdot_nt, bdot_tn = _make_dots(bf16, None)
hdot, hdot_nt, hdot_tn = _make_dots(None, lax.Precision.HIGHEST)


def _silu(x):
    return x * jax.nn.sigmoid(x)


def _rms(x, n):
    return x * lax.rsqrt(jnp.sum(x * x, axis=-1, keepdims=True) * (1.0 / n) + EPS)


def _place():
    return lax.axis_index("x"), lax.axis_index("y"), lax.axis_index("c")


def _gather_phases(ins, outs, send_sems, recv_sems, local_sems):
    n = len(ins)
    x, y, c = _place()
    me, sibling = (x, y, c), (x, y, 1 - c)
    chips = [(1 - x, y), (x, 1 - y), (1 - x, 1 - y)]

    def slot(o, p):
        return outs[o].at[4 * p[0] + 2 * p[1] + p[2]]

    def copy(o, k, block, to, src=None):
        dst = slot(o, block)
        return pltpu.make_async_remote_copy(
            src_ref=dst if src is None else src, dst_ref=dst,
            send_sem=send_sems.at[o, k], recv_sem=recv_sems.at[o, k],
            device_id=to, device_id_type=MESH)

    def local(o):
        return pltpu.make_async_copy(ins[o], slot(o, me), local_sems.at[o])

    def first(o):
        return [copy(o, 0, me, sibling, src=ins[o])] + [copy(o, 1 + j, me, (*chip, c), src=ins[o])
                                                        for j, chip in enumerate(chips)]

    def start():
        for o in range(n):
            local(o).start()
        for o in range(n):
            for cp in first(o):
                cp.start()

    def mid():
        for o in range(n):
            for j, chip in enumerate(chips):
                copy(o, 1 + j, (*chip, c), me).wait_recv()
                copy(o, 4 + j, (*chip, c), sibling).start()

    def finish():
        for o in range(n):
            copy(o, 0, sibling, me).wait_recv()
            for j, chip in enumerate(chips):
                copy(o, 4 + j, (*chip, 1 - c), me).wait_recv()
        for o in range(n):
            for cp in first(o):
                cp.wait_send()
            for j, chip in enumerate(chips):
                copy(o, 4 + j, (*chip, c), sibling).wait_send()
        for o in range(n):
            local(o).wait()

    return start, mid, finish


def _gather_scratch(n):
    return [pltpu.SemaphoreType.DMA((n, 7)), pltpu.SemaphoreType.DMA((n, 7)), pltpu.SemaphoreType.DMA((n,))]


def all_gather(ops, name):
    n = len(ops)

    def body(*refs):
        start, mid, finish = _gather_phases(refs[:n], refs[n:2 * n], *refs[2 * n:])
        start()
        mid()
        finish()

    anyspec = pl.BlockSpec(memory_space=pl.ANY)
    return pl.pallas_call(
        body, name=name,
        out_shape=[jax.ShapeDtypeStruct((N_DEV,) + o.shape, o.dtype) for o in ops],
        in_specs=[anyspec] * n, out_specs=[anyspec] * n,
        scratch_shapes=_gather_scratch(n),
    )(*ops)


def pair_exchange(gs, name):
    n = len(gs)

    def body(*refs):
        g_refs, out_refs = refs[:n], refs[n:2 * n]
        send_sems, recv_sems = refs[2 * n:]
        x, y, c = _place()
        sibling = (x, y, 1 - c)
        cps = []
        for o in range(n):
            for q in range(4):
                cp = pltpu.make_async_remote_copy(
                    src_ref=g_refs[o].at[2 * q + (1 - c)], dst_ref=out_refs[o].at[q],
                    send_sem=send_sems.at[o, q], recv_sem=recv_sems.at[o, q],
                    device_id=sibling, device_id_type=MESH)
                cp.start()
                cps.append(cp)
        for cp in cps:
            cp.wait_recv()
        for cp in cps:
            cp.wait_send()

    anyspec = pl.BlockSpec(memory_space=pl.ANY)
    return pl.pallas_call(
        body, name=name, out_shape=[jax.ShapeDtypeStruct((4,) + g.shape[1:], g.dtype) for g in gs],
        in_specs=[anyspec] * n, out_specs=[anyspec] * n,
        scratch_shapes=[pltpu.SemaphoreType.DMA((n, 4)), pltpu.SemaphoreType.DMA((n, 4))],
    )(*gs)


def chip_exchange(ps, name):
    n = len(ps)

    def body(*refs):
        start, finish = _chip_exchange_phases(refs[:n], refs[n:2 * n], *refs[2 * n:])
        start()
        finish()

    anyspec = pl.BlockSpec(memory_space=pl.ANY)
    return pl.pallas_call(
        body, name=name, out_shape=[jax.ShapeDtypeStruct(p.shape, p.dtype) for p in ps],
        in_specs=[anyspec] * n, out_specs=[anyspec] * n,
        scratch_shapes=_chip_exchange_scratch(n),
    )(*ps)


def _chip_exchange_phases(p_refs, out_refs, send_sems, recv_sems):
    n = len(p_refs)
    x, y, c = _place()
    chips = [(1 - x, y), (x, 1 - y), (1 - x, 1 - y)]

    def copies():
        return [pltpu.make_async_remote_copy(
            src_ref=p_refs[o].at[j], dst_ref=out_refs[o].at[j],
            send_sem=send_sems.at[o, j], recv_sem=recv_sems.at[o, j],
            device_id=(*chip, c), device_id_type=MESH) for o in range(n) for j, chip in enumerate(chips)]

    def start():
        for cp in copies():
            cp.start()

    def finish():
        for cp in copies():
            cp.wait_recv()
        for cp in copies():
            cp.wait_send()

    return start, finish


def _chip_exchange_scratch(n):
    return [pltpu.SemaphoreType.DMA((n, 3)), pltpu.SemaphoreType.DMA((n, 3))]


def chip_partials(g, recv, slots, name):
    _, R, C = g.shape
    tr = _pick_rows(R, 512)

    def body(s_ref, a_ref, b_ref, o_ref):
        o_ref[...] = (a_ref[...].astype(f32) + b_ref[...].astype(f32)).astype(o_ref.dtype)

    grid_spec = pltpu.PrefetchScalarGridSpec(
        num_scalar_prefetch=1, grid=(3, R // tr),
        in_specs=[pl.BlockSpec((None, tr, C), lambda j, i, s: (s[j], i, 0)),
                  pl.BlockSpec((None, tr, C), lambda j, i, s: (s[j] // 2, i, 0))],
        out_specs=pl.BlockSpec((None, tr, C), lambda j, i, s: (j, i, 0)))
    return pl.pallas_call(body, name=name, grid_spec=grid_spec,
                          out_shape=jax.ShapeDtypeStruct((3, R, C), g.dtype),
                          compiler_params=_cp(("arbitrary", "arbitrary")))(slots, g, recv)


def adamw(parts, w, m, v, name, rows=256):
    R, C = w.shape
    tr = _pick_rows(R, rows)
    npart = len(parts)
    c1 = 1.0 - ADAM_B1 ** ADAM_STEP
    c2 = 1.0 - ADAM_B2 ** ADAM_STEP

    def body(*refs):
        p_refs = refs[:npart]
        w_ref, m_ref, v_ref, g_out, d_out, m_out, v_out = refs[npart:]
        g = p_refs[0][...].astype(f32)
        for p in p_refs[1:]:
            g = g + p[...].astype(f32)
        mn = ADAM_B1 * m_ref[...] + (1.0 - ADAM_B1) * g
        vn = ADAM_B2 * v_ref[...] + (1.0 - ADAM_B2) * (g * g)
        m_hat = mn / c1
        v_hat = vn / c2
        g_out[...] = g
        d_out[...] = -ADAM_LR * (m_hat / (jnp.sqrt(v_hat) + ADAM_EPS) + ADAM_WD * w_ref[...])
        m_out[...] = mn
        v_out[...] = vn

    spec = pl.BlockSpec((tr, C), lambda i: (i, 0))
    return pl.pallas_call(
        body, name=name, grid=(R // tr,),
        in_specs=[spec] * (npart + 3), out_specs=[spec] * 4,
        out_shape=[jax.ShapeDtypeStruct((R, C), f32)] * 4,
        compiler_params=_cp(("arbitrary",)))(*parts, w, m, v)


def mm(a, b, *, name, ta=False, tb=False, a_fn=None, epi=None, extras=(), out_dtype=f32, out_blocks=False,
       tm=512, tn=1024, tk=2048):
    K, M = a.shape if ta else a.shape[::-1]
    b3 = b.ndim == 3
    if b3:
        N, K2 = (b.shape[1], N_DEV * b.shape[2]) if tb else (N_DEV * b.shape[2], b.shape[1])
    else:
        N, K2 = b.shape if tb else b.shape[::-1]
    assert K == K2, (a.shape, b.shape, ta, tb)
    n_split = N // N_DEV if (out_blocks or (b3 and not tb)) else N
    k_split = K // N_DEV if (b3 and tb) else K
    tm, tn, tk = _pick(M, tm), _pick(n_split, tn), _pick(k_split, tk)
    nb, kb = n_split // tn, k_split // tk
    nk = K // tk
    ne = len(extras)
    assert not (out_blocks and ne)
    dims = (((0 if ta else 1,), (1 if tb else 0,)), ((), ()))

    def body(a_ref, b_ref, *rest):
        e_refs, o_ref, acc = rest[:ne], rest[ne], rest[ne + 1]
        k = pl.program_id(2)

        @pl.when(k == 0)
        def _():
            acc[...] = jnp.zeros_like(acc)

        av = a_ref[...]
        if a_fn is not None:
            av = a_fn(av.astype(f32))
        acc[...] += lax.dot_general(av.astype(bf16), b_ref[...].astype(bf16), dims, preferred_element_type=f32)

        @pl.when(k == nk - 1)
        def _():
            r = acc[...]
            if epi is not None:
                r = epi(r, *[e[...] for e in e_refs])
            o_ref[...] = r.astype(o_ref.dtype)

    a_spec = pl.BlockSpec((tk, tm), lambda i, j, k: (k, i)) if ta else pl.BlockSpec((tm, tk), lambda i, j, k: (i, k))
    if b3 and tb:
        b_spec = pl.BlockSpec((None, tn, tk), lambda i, j, k: (k // kb, j, k % kb))
    elif b3:
        b_spec = pl.BlockSpec((None, tk, tn), lambda i, j, k: (j // nb, k, j % nb))
    else:
        b_spec = pl.BlockSpec((tn, tk), lambda i, j, k: (j, k)) if tb else pl.BlockSpec((tk, tn), lambda i, j, k: (k, j))
    if out_blocks:
        o_spec = pl.BlockSpec((None, tm, tn), lambda i, j, k: (j // nb, i, j % nb))
        o_shape = jax.ShapeDtypeStruct((N_DEV, M, N // N_DEV), out_dtype)
    else:
        o_spec = pl.BlockSpec((tm, tn), lambda i, j, k: (i, j))
        o_shape = jax.ShapeDtypeStruct((M, N), out_dtype)
    return pl.pallas_call(
        body, name=name, grid=(M // tm, N // tn, nk),
        in_specs=[a_spec, b_spec] + [o_spec] * ne, out_specs=o_spec,
        out_shape=o_shape,
        scratch_shapes=[pltpu.VMEM((tm, tn), f32)],
        compiler_params=_cp(("parallel", "parallel", "arbitrary")))(a, b, *extras)


class Row:
    def __init__(self, arr, width=None, col=0, lead=None, diff=True):
        self.arr, self.col, self.lead, self.diff = arr, col, lead, diff
        self.width = arr.shape[-1] if width is None else width

    def spec(self, t):
        col, lead = self.col, self.lead
        if lead is None:
            return pl.BlockSpec((t, self.width), lambda i: (i, col))
        return pl.BlockSpec((None, t, self.width), lambda i: (lead, i, col))


def _whole(p):
    return pl.BlockSpec(p.shape, lambda i: (0,) * p.ndim)


def rowwise(fn, rows, params, outs, *, n_rows, tile, name):
    t = _pick_rows(n_rows, tile)
    nr, npar, no = len(rows), len(params), len(outs)

    def body(*refs):
        r_refs, p_refs, o_refs = refs[:nr], refs[nr:nr + npar], refs[nr + npar:]
        res = fn(*[r[...].astype(f32) for r in r_refs], *[p[...] for p in p_refs])
        for o_ref, val in zip(o_refs, res):
            o_ref[...] = val.astype(o_ref.dtype)

    return pl.pallas_call(
        body, name=name, grid=(n_rows // t,),
        in_specs=[r.spec(t) for r in rows] + [_whole(p) for p in params],
        out_specs=[pl.BlockSpec((t, w), lambda i: (i, 0)) for w, _ in outs],
        out_shape=[jax.ShapeDtypeStruct((n_rows, w), dt) for w, dt in outs],
        compiler_params=_cp(("arbitrary",)))(*[r.arr for r in rows], *params)


def rowwise_vjp(fn, rows, params, cts, *, n_rows, tile, name, row_grad_dtypes=None, param_diff=None):
    t = _pick_rows(n_rows, tile)
    nr, npar, nc = len(rows), len(params), len(cts)
    param_diff = [True] * npar if param_diff is None else param_diff
    d_rows = [k for k, r in enumerate(rows) if r.diff]
    d_pars = [k for k in range(npar) if param_diff[k]]
    row_grad_dtypes = [f32] * len(d_rows) if row_grad_dtypes is None else row_grad_dtypes

    def body(*refs):
        r_refs, p_refs = refs[:nr], refs[nr:nr + npar]
        c_refs = refs[nr + npar:nr + npar + nc]
        dr_refs = refs[nr + npar + nc:nr + npar + nc + len(d_rows)]
        dp_refs = refs[nr + npar + nc + len(d_rows):]
        rv = [r[...].astype(f32) for r in r_refs]
        pv = [p[...] for p in p_refs]

        def g(*dvals):
            full_r, full_p = list(rv), list(pv)
            for k, val in zip(d_rows, dvals[:len(d_rows)]):
                full_r[k] = val
            for k, val in zip(d_pars, dvals[len(d_rows):]):
                full_p[k] = val
            return tuple(fn(*full_r, *full_p))

        prim = [rv[k] for k in d_rows] + [pv[k].astype(f32) for k in d_pars]
        _, pull = jax.vjp(g, *prim)
        grads = pull(tuple(c[...].astype(f32) for c in c_refs))
        for ref, val in zip(dr_refs, grads[:len(d_rows)]):
            ref[...] = val.astype(ref.dtype)

        @pl.when(pl.program_id(0) == 0)
        def _():
            for ref in dp_refs:
                ref[...] = jnp.zeros_like(ref)

        for ref, val in zip(dp_refs, grads[len(d_rows):]):
            ref[...] += val

    out_specs = [pl.BlockSpec((t, rows[k].width), lambda i: (i, 0)) for k in d_rows]
    out_specs += [_whole(params[k]) for k in d_pars]
    out_shape = [jax.ShapeDtypeStruct((n_rows, rows[k].width), dt) for k, dt in zip(d_rows, row_grad_dtypes)]
    out_shape += [jax.ShapeDtypeStruct(params[k].shape, f32) for k in d_pars]
    res = pl.pallas_call(
        body, name=name, grid=(n_rows // t,),
        in_specs=[r.spec(t) for r in rows] + [_whole(p) for p in params] + [c.spec(t) for c in cts],
        out_specs=out_specs, out_shape=out_shape,
        compiler_params=_cp(("arbitrary",)))(*[r.arr for r in rows], *params, *[c.arr for c in cts])
    return res[:len(d_rows)], res[len(d_rows):]


def f_norm_mod(x, g, shift, scale):
    return (_rms(x, x.shape[-1]) * g * (1.0 + scale) + shift,)


def f_norm_mod_thru(x, g, shift, scale):
    return f_norm_mod(x, g, shift, scale) + (x,)


def f_resid_norm_mod(x, mixed, gate1, g2, shift2, scale2):
    x1 = x + gate1 * mixed
    return (x1,) + f_norm_mod(x1, g2, shift2, scale2)


def _rope_rot():
    i = lax.broadcasted_iota(jnp.int32, (LANE, LANE), 0)
    j = lax.broadcasted_iota(jnp.int32, (LANE, LANE), 1)
    neg = jnp.where((i == j + HALF) & (j < HALF), -1.0, 0.0)
    pos = jnp.where((i == j - HALF) & (j >= HALF) & (j < ROPE), 1.0, 0.0)
    return (neg + pos).astype(f32)


def make_f_mla_prep(n_heads, q_scale):
    def fn(cq, ckv, kpe, pos, gq, gkv, gqn, gkn, w_uq, w_ukv, freqs):
        rot = _rope_rot()
        ang = pos * freqs
        cos, sin = jnp.cos(ang), jnp.sin(ang)

        def rope(u):
            return u * cos + hdot(u, rot) * sin

        qraw = bdot(_rms(cq, cq.shape[-1]) * gq, w_uq)
        kv = bdot(_rms(ckv, ckv.shape[-1]) * gkv, w_ukv)
        kpe_ss = jnp.sum(kpe * kpe, axis=-1, keepdims=True)
        qs, ks = [], []
        for h in range(n_heads):
            qh = _rms(qraw[:, h * QK_PAD:(h + 1) * QK_PAD], QK_DIM) * gqn
            qs += [qh[:, :NOPE], rope(qh[:, NOPE:])]
            kn = kv[:, h * NOPE:(h + 1) * NOPE]
            r = lax.rsqrt((jnp.sum(kn * kn, axis=-1, keepdims=True) + kpe_ss) * (1.0 / QK_DIM) + EPS)
            ks += [kn * r * gkn[:, :NOPE], rope(kpe * r * gkn[:, NOPE:])]
        return jnp.concatenate(qs, axis=-1) * q_scale, jnp.concatenate(ks, axis=-1), kv[:, n_heads * NOPE:]
    return fn


def make_f_mlstm_post(n_heads, dm):
    def fn(hf, hb, o, g):
        hm = hf + hb
        outs = []
        for h in range(n_heads):
            sl = slice(h * dm, (h + 1) * dm)
            outs.append(jax.nn.sigmoid(o[:, sl]) * (_rms(hm[:, sl], dm) * g[:, sl]))
        return (jnp.concatenate(outs, axis=-1),)
    return fn


def f_add_pairs(a0, a1, b0, b1):
    return (jnp.concatenate([a0 + a1, b0 + b1], axis=-1),)


def f_add(a, b):
    return (a + b,)


def loss_head(x1, y, target, gate2, name, tile=256):
    S, D = x1.shape
    t = _pick_rows(S, tile)

    def body(x1_ref, y_ref, t_ref, g_ref, loss_ref, dout_ref, dy_ref, dgate_ref):
        @pl.when(pl.program_id(0) == 0)
        def _():
            loss_ref[...] = jnp.zeros_like(loss_ref)
            dgate_ref[...] = jnp.zeros_like(dgate_ref)

        yv, gv = y_ref[...], g_ref[...]
        e = x1_ref[...] + gv * yv - t_ref[...]
        loss_ref[...] += 0.5 * jnp.sum(jnp.sum(e * e, axis=-1, keepdims=True) * (1.0 / D), axis=0, keepdims=True)
        d_out = e * (1.0 / D)
        dout_ref[...] = d_out
        dy_ref[...] = (d_out * gv).astype(dy_ref.dtype)
        dgate_ref[...] += jnp.sum(d_out * yv, axis=0, keepdims=True)

    row = pl.BlockSpec((t, D), lambda i: (i, 0))
    return pl.pallas_call(
        body, name=name, grid=(S // t,),
        in_specs=[row, row, row, pl.BlockSpec((1, D), lambda i: (0, 0))],
        out_specs=[pl.BlockSpec((1, 1), lambda i: (0, 0)), row, row, pl.BlockSpec((1, D), lambda i: (0, 0))],
        out_shape=[jax.ShapeDtypeStruct((1, 1), f32), jax.ShapeDtypeStruct((S, D), f32),
                   jax.ShapeDtypeStruct((S, D), bf16), jax.ShapeDtypeStruct((1, D), f32)],
        compiler_params=_cp(("arbitrary",)))(x1, y, target, gate2)


def ada_fwd(c_all, w_blk, b_blk, name):
    B, D = c_all.shape
    N = w_blk.shape[1]
    tn = _pick(N, 512)

    def body(c_ref, w_ref, b_ref, o_ref):
        o_ref[...] = bdot(_silu(c_ref[...]), w_ref[...]) + b_ref[...]

    return pl.pallas_call(
        body, name=name, grid=(N // tn,),
        in_specs=[pl.BlockSpec((B, D), lambda j: (0, 0)), pl.BlockSpec((D, tn), lambda j: (0, j)),
                  pl.BlockSpec((1, tn), lambda j: (0, j))],
        out_specs=pl.BlockSpec((B, tn), lambda j: (0, j)),
        out_shape=jax.ShapeDtypeStruct((B, N), f32), compiler_params=_cp(("arbitrary",)))(c_all, w_blk, b_blk)


def ada_wgrad(c_all, dmod_blk, name):
    B, D = c_all.shape
    N = dmod_blk.shape[1]
    tn = _pick(N, 512)

    def body(c_ref, d_ref, o_ref):
        o_ref[...] = hdot_tn(_silu(c_ref[...]), d_ref[...])

    return pl.pallas_call(
        body, name=name, grid=(N // tn,),
        in_specs=[pl.BlockSpec((B, D), lambda j: (0, 0)), pl.BlockSpec((B, tn), lambda j: (0, j))],
        out_specs=pl.BlockSpec((D, tn), lambda j: (0, j)),
        out_shape=jax.ShapeDtypeStruct((D, N), f32), compiler_params=_cp(("arbitrary",)))(c_all, dmod_blk)


def _nt(a, b):
    return lax.dot_general(a, b, (((1,), (1,)), ((), ())), preferred_element_type=f32)


def _tn(a, b):
    return lax.dot_general(a, b, (((0,), (0,)), ((), ())), preferred_element_type=f32)


def flash_fwd(q, k, v, n_heads, name, side=(), tq=512, tk=8192, sub=1024):
    S = q.shape[0]
    tq, tk = _pick(S, tq), _pick(S, tk)
    sub = _pick(tk, sub)
    nk, nsub = S // tk, tk // sub
    ns = len(side)
    n_steps = n_heads * (S // tq) * nk
    assert ns == 0 or n_steps >= 3

    def body(*refs):
        q_ref, k_ref, v_ref = refs[:3]
        o_ref, lse_ref = refs[3 + ns:5 + ns]
        m_sc, l_sc, acc_sc = refs[5 + 2 * ns:8 + 2 * ns]
        j = pl.program_id(2)
        step = (pl.program_id(0) * (S // tq) + pl.program_id(1)) * nk + j
        if ns:
            g_start, g_mid, g_finish = _gather_phases(refs[3:3 + ns], refs[5 + ns:5 + 2 * ns], *refs[8 + 2 * ns:])
            pl.when(step == 0)(g_start)
            pl.when(step == n_steps // 2)(g_mid)

        @pl.when(j == 0)
        def _():
            m_sc[...] = jnp.full_like(m_sc, -jnp.inf)
            l_sc[...] = jnp.zeros_like(l_sc)
            acc_sc[...] = jnp.zeros_like(acc_sc)

        qv = q_ref[...]
        m = m_sc[...]
        ss = [_nt(qv, k_ref[b * sub:(b + 1) * sub, :]) for b in range(nsub)]
        mx = ss[0]
        for s in ss[1:]:
            mx = jnp.maximum(mx, s)
        m_new = jnp.maximum(m, jnp.max(mx, axis=-1, keepdims=True))
        alpha = jnp.exp2(m - m_new)
        psum, pv = None, None
        for b in range(nsub):
            p = jnp.exp2(ss[b] - m_new)
            d = jnp.dot(p.astype(bf16), v_ref[b * sub:(b + 1) * sub, :], preferred_element_type=f32)
            psum = p if psum is None else psum + p
            pv = d if pv is None else pv + d
        m, l, acc = m_new, alpha * l_sc[...] + jnp.sum(psum, axis=-1, keepdims=True), alpha * acc_sc[...] + pv
        m_sc[...], l_sc[...], acc_sc[...] = m, l, acc

        @pl.when(j == nk - 1)
        def _():
            o_ref[...] = (acc / l).astype(o_ref.dtype)
            lse_ref[...] = m + jnp.log2(l)

        if ns:
            pl.when(step == n_steps - 1)(g_finish)

    anyspec = pl.BlockSpec(memory_space=pl.ANY)
    res = pl.pallas_call(
        body, name=name, grid=(n_heads, S // tq, nk),
        in_specs=[pl.BlockSpec((tq, QK_PAD), lambda h, i, j: (i, h)),
                  pl.BlockSpec((tk, QK_PAD), lambda h, i, j: (j, h)),
                  pl.BlockSpec((tk, V_DIM), lambda h, i, j: (j, h))] + [anyspec] * ns,
        out_specs=[pl.BlockSpec((tq, V_DIM), lambda h, i, j: (i, h)),
                   pl.BlockSpec((None, tq, 1), lambda h, i, j: (h, i, 0))] + [anyspec] * ns,
        out_shape=[jax.ShapeDtypeStruct((S, n_heads * V_DIM), bf16), jax.ShapeDtypeStruct((n_heads, S, 1), f32)]
        + [jax.ShapeDtypeStruct((N_DEV,) + a.shape, a.dtype) for a in side],
        scratch_shapes=[pltpu.VMEM((tq, 1), f32), pltpu.VMEM((tq, 1), f32), pltpu.VMEM((tq, V_DIM), f32)]
        + (_gather_scratch(ns) if ns else []),
        compiler_params=_cp(("arbitrary", "arbitrary", "arbitrary")))(q, k, v, *side)
    return res[0], res[1], list(res[2:])


def flash_bwd(q, k, v, o, lse_row, do, do_col0, n_heads, name, side=(), tq=512, tk=8192, sub=512):
    S = q.shape[0]
    tq, tk = _pick(S, tq), _pick(S, tk)
    sub = _pick(tk, sub)
    nsub = tk // sub
    ln2 = math.log(2.0)
    ns = len(side)
    n_steps = n_heads * (S // tq) * (S // tk)
    assert ns == 0 or n_steps >= 2

    def body(*refs):
        q_ref, k_ref, v_ref, o_ref, lse_ref, do_ref = refs[:6]
        dq_ref, dk_ref, dv_ref = refs[6 + ns:9 + ns]
        i, j = pl.program_id(1), pl.program_id(2)
        step = (pl.program_id(0) * (S // tq) + i) * (S // tk) + j
        if ns:
            x_start, x_finish = _chip_exchange_phases(refs[6:6 + ns], refs[9 + ns:9 + 2 * ns], *refs[9 + 2 * ns:])
            pl.when(step == 0)(x_start)

        @pl.when(j == 0)
        def _():
            dq_ref[...] = jnp.zeros_like(dq_ref)

        @pl.when((i == 0) & (j == 0))
        def _():
            dk_ref[...] = jnp.zeros_like(dk_ref)
            dv_ref[...] = jnp.zeros_like(dv_ref)

        qv = q_ref[...]
        dof = do_ref[...].astype(f32)
        do_b = dof.astype(bf16)
        do_s = (dof * ln2).astype(bf16)
        delta = hdot_nt(jnp.ones((8, V_DIM), f32), dof * ln2 * o_ref[...].astype(f32))[0:1, :]
        lse = lse_ref[...]
        dq = jnp.zeros((tq, QK_PAD), f32)
        for b in range(nsub):
            kb = k_ref[b * sub:(b + 1) * sub, :]
            rows = pl.ds(pl.multiple_of(j * tk + b * sub, sub), sub)
            pt = jnp.exp2(_nt(kb, qv) - lse)
            dpt = _nt(v_ref[b * sub:(b + 1) * sub, :], do_s)
            dst = (pt * (dpt - delta)).astype(bf16)
            dv_ref[rows, :] += jnp.dot(pt.astype(bf16), do_b, preferred_element_type=f32)
            dk_ref[rows, :] += jnp.dot(dst, qv, preferred_element_type=f32)
            dq = dq + _tn(dst, kb)
        dq_ref[...] += dq
        if ns:
            pl.when(step == n_steps - 1)(x_finish)

    anyspec = pl.BlockSpec(memory_space=pl.ANY)
    res = pl.pallas_call(
        body, name=name, grid=(n_heads, S // tq, S // tk),
        in_specs=[pl.BlockSpec((tq, QK_PAD), lambda h, i, j: (i, h)),
                  pl.BlockSpec((tk, QK_PAD), lambda h, i, j: (j, h)),
                  pl.BlockSpec((tk, V_DIM), lambda h, i, j: (j, h)),
                  pl.BlockSpec((tq, V_DIM), lambda h, i, j: (i, h)),
                  pl.BlockSpec((None, 1, tq), lambda h, i, j: (h, 0, i)),
                  pl.BlockSpec((tq, V_DIM), lambda h, i, j: (i, do_col0 + h))] + [anyspec] * ns,
        out_specs=[pl.BlockSpec((tq, QK_PAD), lambda h, i, j: (i, h)),
                   pl.BlockSpec((S, QK_PAD), lambda h, i, j: (0, h)),
                   pl.BlockSpec((S, V_DIM), lambda h, i, j: (0, h))] + [anyspec] * ns,
        out_shape=[jax.ShapeDtypeStruct((S, n_heads * QK_PAD), f32), jax.ShapeDtypeStruct((S, n_heads * QK_PAD), f32),
                   jax.ShapeDtypeStruct((S, n_heads * V_DIM), f32)] + [jax.ShapeDtypeStruct(a.shape, a.dtype) for a in side],
        scratch_shapes=_chip_exchange_scratch(ns) if ns else [],
        compiler_params=_cp(("arbitrary", "arbitrary", "arbitrary")))(q, k, v, o, lse_row, do, *side)
    return res[0], res[1], res[2], list(res[3:])


def _shifted(prev, cur, nxt, k, first, last):
    if k == 0:
        return cur
    t = cur.shape[0]
    r = lax.broadcasted_iota(jnp.int32, cur.shape, 0)
    if k < 0:
        body = pltpu.roll(cur, -k, 0)
        edge = jnp.where(first, 0.0, pltpu.roll(prev, -k, 0))
        return jnp.where(r < -k, edge, body)
    body = pltpu.roll(cur, t - k, 0)
    edge = jnp.where(last, 0.0, pltpu.roll(nxt, t - k, 0))
    return jnp.where(r >= t - k, edge, body)


def _halo_specs(t, width, n_tiles, col=0):
    return [pl.BlockSpec((t, width), lambda i: (jnp.maximum(i - 1, 0), col)),
            pl.BlockSpec((t, width), lambda i: (i, col)),
            pl.BlockSpec((t, width), lambda i: (jnp.minimum(i + 1, n_tiles - 1), col))]


def conv_fwd(proj, width, w, b, name, tile=256):
    S = proj.shape[0]
    t = _pick_rows(S, tile)
    n_tiles = S // t

    def body(p_ref, c_ref, n_ref, w_ref, b_ref, z_ref):
        i = pl.program_id(0)
        first, last = i == 0, i == n_tiles - 1
        prev, cur, nxt = p_ref[...], c_ref[...], n_ref[...]
        z = b_ref[...] + jnp.zeros_like(cur)
        for j in range(CONV_W):
            z = z + w_ref[j:j + 1, :] * _shifted(prev, cur, nxt, j - CONV_W // 2, first, last)
        z_ref[...] = z

    return pl.pallas_call(
        body, name=name, grid=(n_tiles,),
        in_specs=_halo_specs(t, width, n_tiles) + [_whole(w), _whole(b)],
        out_specs=pl.BlockSpec((t, width), lambda i: (i, 0)),
        out_shape=jax.ShapeDtypeStruct((S, width), f32),
        compiler_params=_cp(("arbitrary",)))(proj, proj, proj, w, b)


def conv_bwd(dz, proj, width, w, name, tile=256):
    S = proj.shape[0]
    t = _pick_rows(S, tile)
    n_tiles = S // t

    def body(dp_ref, dc_ref, dn_ref, up_ref, uc_ref, un_ref, w_ref, du_ref, dw_ref, db_ref):
        i = pl.program_id(0)
        first, last = i == 0, i == n_tiles - 1

        @pl.when(first)
        def _():
            dw_ref[...] = jnp.zeros_like(dw_ref)
            db_ref[...] = jnp.zeros_like(db_ref)

        dprev, dcur, dnxt = dp_ref[...], dc_ref[...], dn_ref[...]
        uprev, ucur, unxt = up_ref[...], uc_ref[...], un_ref[...]
        du = jnp.zeros_like(dcur)
        for j in range(CONV_W):
            k = j - CONV_W // 2
            du = du + w_ref[j:j + 1, :] * _shifted(dprev, dcur, dnxt, -k, first, last)
            dw_ref[j:j + 1, :] += jnp.sum(dcur * _shifted(uprev, ucur, unxt, k, first, last), axis=0, keepdims=True)
        du_ref[...] = du
        db_ref[...] += jnp.sum(dcur, axis=0, keepdims=True)

    return pl.pallas_call(
        body, name=name, grid=(n_tiles,),
        in_specs=_halo_specs(t, width, n_tiles) + _halo_specs(t, width, n_tiles) + [_whole(w)],
        out_specs=[pl.BlockSpec((t, width), lambda i: (i, 0)), pl.BlockSpec((8, width), lambda i: (0, 0)),
                   pl.BlockSpec((1, width), lambda i: (0, 0))],
        out_shape=[jax.ShapeDtypeStruct((S, width), f32), jax.ShapeDtypeStruct((8, width), f32),
                   jax.ShapeDtypeStruct((1, width), f32)],
        compiler_params=_cp(("arbitrary",)))(dz, dz, dz, proj, proj, proj, w)


def _mlstm_step(dm, d, C, n, m, zq, zk, v, ic, fc, ir, fr, bi, bf_):
    L = zq.shape[0]
    q = _silu(zq)
    k = _silu(zk) * (dm ** -0.5)
    i_c, f_c = ic + bi, jax.nn.log_sigmoid(fc + bf_)
    i_r, f_r = ir + bi, jax.nn.log_sigmoid(fr + bf_)
    r = lax.broadcasted_iota(jnp.int32, (L, L), 0)
    c = lax.broadcasted_iota(jnp.int32, (L, L), 1)
    sgn = jnp.where(d == 0, r - c, c - r)
    mask = sgn >= 0
    b_c = jnp.sum(jnp.where(mask, f_r, 0.0), axis=-1, keepdims=True)
    b_r = jnp.sum(jnp.where(sgn <= 0, f_c, 0.0), axis=0, keepdims=True)
    log_inter = b_c + m
    logD = jnp.where(mask, b_c - b_r + i_r, -jnp.inf)
    m_t = jnp.maximum(log_inter, jnp.max(logD, axis=-1, keepdims=True))
    Dm = jnp.exp(logD - m_t)
    w_inter = jnp.exp(log_inter - m_t)
    scores = bdot_nt(q, k) * Dm
    num = bdot(scores, v) + w_inter * bdot_nt(q, C)
    den = jnp.sum(scores, axis=-1, keepdims=True) + w_inter * jnp.sum(q * n, axis=-1, keepdims=True)
    h = num / jnp.maximum(jnp.abs(den), jnp.exp(-m_t))
    bL = jnp.sum(f_c, axis=0, keepdims=True)
    log_w = bL - b_c + i_c
    m_new = jnp.maximum(bL + m, jnp.max(log_w, axis=0, keepdims=True))
    decay = jnp.exp(bL + m - m_new)
    w = jnp.exp(log_w - m_new)
    C_new = decay * C + bdot_tn(w * v, k)
    n_new = decay * n + jnp.sum(w * k, axis=0, keepdims=True)
    return C_new, n_new, m_new, h


def _mlstm_in_specs(L, dm, hm, nc, step_of):
    def chunk(d, j):
        s = step_of(j)
        return s + d * (nc - 1 - 2 * s)
    return [
        pl.BlockSpec((L, dm), lambda d, h, j: (chunk(d, j), h)),
        pl.BlockSpec((L, dm), lambda d, h, j: (chunk(d, j), hm + h)),
        pl.BlockSpec((L, dm), lambda d, h, j: (chunk(d, j), 2 * hm + h)),
        pl.BlockSpec((None, None, L, 1), lambda d, h, j: (d, h, chunk(d, j), 0)),
        pl.BlockSpec((None, None, L, 1), lambda d, h, j: (d, h, chunk(d, j), 0)),
        pl.BlockSpec((None, None, 1, L), lambda d, h, j: (d, h, 0, chunk(d, j))),
        pl.BlockSpec((None, None, 1, L), lambda d, h, j: (d, h, 0, chunk(d, j))),
        pl.BlockSpec((None, None, 1, 1), lambda d, h, j: (d, h, 0, 0)),
        pl.BlockSpec((None, None, 1, 1), lambda d, h, j: (d, h, 0, 0)),
    ], chunk


def mlstm_fwd(z, proj, gates, hm, dm, name):
    S = z.shape[0]
    L = CHUNK
    nc = S // L
    in_specs, chunk = _mlstm_in_specs(L, dm, hm, nc, lambda j: j)

    def body(zq, zk, v, ic, fc, ir, fr, bi, bf_, h_ref, cs_ref, ns_ref, ms_ref, C_sc, n_sc, m_sc):
        d = pl.program_id(0)

        @pl.when(pl.program_id(2) == 0)
        def _():
            C_sc[...] = jnp.zeros_like(C_sc)
            n_sc[...] = jnp.zeros_like(n_sc)
            m_sc[...] = jnp.full_like(m_sc, M_INIT)

        C, n, m = C_sc[...], n_sc[...], m_sc[...]
        cs_ref[...], ns_ref[...], ms_ref[...] = C, n, m
        C2, n2, m2, h = _mlstm_step(dm, d, C, n, m, zq[...], zk[...], v[...], ic[...], fc[...], ir[...], fr[...],
                                    bi[...], bf_[...])
        C_sc[...], n_sc[...], m_sc[...] = C2, n2, m2
        h_ref[...] = h

    return pl.pallas_call(
        body, name=name, grid=(2, hm, nc), in_specs=in_specs,
        out_specs=[pl.BlockSpec((None, L, dm), lambda d, h, j: (d, chunk(d, j), h)),
                   pl.BlockSpec((None, None, None, dm, dm), lambda d, h, j: (d, h, j, 0, 0)),
                   pl.BlockSpec((None, None, None, 1, dm), lambda d, h, j: (d, h, j, 0, 0)),
                   pl.BlockSpec((None, None, None, 1, 1), lambda d, h, j: (d, h, j, 0, 0))],
        out_shape=[jax.ShapeDtypeStruct((2, S, hm * dm), f32), jax.ShapeDtypeStruct((2, hm, nc, dm, dm), f32),
                   jax.ShapeDtypeStruct((2, hm, nc, 1, dm), f32), jax.ShapeDtypeStruct((2, hm, nc, 1, 1), f32)],
        scratch_shapes=[pltpu.VMEM((dm, dm), f32), pltpu.VMEM((1, dm), f32), pltpu.VMEM((1, 1), f32)],
        compiler_params=_cp(("arbitrary", "arbitrary", "arbitrary")))(z, z, proj, *gates)


def mlstm_bwd(z, proj, gates, states, dh, hm, dm, name):
    S = z.shape[0]
    L = CHUNK
    nc = S // L
    in_specs, chunk = _mlstm_in_specs(L, dm, hm, nc, lambda j: nc - 1 - j)
    st = lambda j: nc - 1 - j
    in_specs = in_specs + [
        pl.BlockSpec((None, None, None, dm, dm), lambda d, h, j: (d, h, st(j), 0, 0)),
        pl.BlockSpec((None, None, None, 1, dm), lambda d, h, j: (d, h, st(j), 0, 0)),
        pl.BlockSpec((None, None, None, 1, 1), lambda d, h, j: (d, h, st(j), 0, 0)),
        pl.BlockSpec((None, L, dm), lambda d, h, j: (d, chunk(d, j), h)),
    ]

    def body(zq, zk, v, ic, fc, ir, fr, bi, bf_, cs, ns, ms, dh_ref,
             dzq, dzk, dv, dic, dfc, dir_, dfr, dbi, dbf, dC_sc, dn_sc, dm_sc):
        d = pl.program_id(0)

        @pl.when(pl.program_id(2) == 0)
        def _():
            dC_sc[...] = jnp.zeros_like(dC_sc)
            dn_sc[...] = jnp.zeros_like(dn_sc)
            dm_sc[...] = jnp.zeros_like(dm_sc)
            dbi[...] = jnp.zeros_like(dbi)
            dbf[...] = jnp.zeros_like(dbf)

        prim = (cs[...], ns[...], ms[...], zq[...], zk[...], v[...], ic[...], fc[...], ir[...], fr[...],
                bi[...], bf_[...])
        _, pull = jax.vjp(functools.partial(_mlstm_step, dm, d), *prim)
        g = pull((dC_sc[...], dn_sc[...], dm_sc[...], dh_ref[...]))
        dC_sc[...], dn_sc[...], dm_sc[...] = g[0], g[1], g[2]
        dzq[...], dzk[...], dv[...] = g[3], g[4], g[5]
        dic[...], dfc[...], dir_[...], dfr[...] = g[6], g[7], g[8], g[9]
        dbi[...] += g[10]
        dbf[...] += g[11]

    tile = pl.BlockSpec((None, L, dm), lambda d, h, j: (d, chunk(d, j), h))
    col = pl.BlockSpec((None, None, L, 1), lambda d, h, j: (d, h, chunk(d, j), 0))
    row = pl.BlockSpec((None, None, 1, L), lambda d, h, j: (d, h, 0, chunk(d, j)))
    one = pl.BlockSpec((None, None, 1, 1), lambda d, h, j: (d, h, 0, 0))
    big = jax.ShapeDtypeStruct((2, S, hm * dm), f32)
    cols = jax.ShapeDtypeStruct((2, hm, S, 1), f32)
    rows = jax.ShapeDtypeStruct((2, hm, 1, S), f32)
    ones = jax.ShapeDtypeStruct((2, hm, 1, 1), f32)
    return pl.pallas_call(
        body, name=name, grid=(2, hm, nc), in_specs=in_specs,
        out_specs=[tile, tile, tile, col, col, row, row, one, one],
        out_shape=[big, big, big, cols, cols, rows, rows, ones, ones],
        scratch_shapes=[pltpu.VMEM((dm, dm), f32), pltpu.VMEM((1, dm), f32), pltpu.VMEM((1, 1), f32)],
        compiler_params=_cp(("arbitrary", "arbitrary", "arbitrary")))(z, z, proj, *gates, *states, dh)


def _blocks_to_cols(g):
    return g.transpose(1, 0, 2).reshape(g.shape[1], N_DEV * g.shape[2])


def _cols_to_blocks(a):
    return a.reshape(a.shape[0], N_DEV, a.shape[1] // N_DEV).transpose(1, 0, 2)


def _pad_cols(a, n):
    return jnp.pad(a, ((0, 0), (0, n - a.shape[1])))


def _relu2(u):
    r = jnp.maximum(u, 0.0)
    return r * r


def kernel(x, c, positions, w_ada, b_ada, norm_mix_g, w_in, b_gates, conv_w, conv_b, q_lora_g, w_uq, kv_lora_g, w_ukv, q_norm_g, k_norm_g, mlstm_norm_g, w_out, norm_mlp_g, w_ff1, w_ff2, loss_target, m_w_ada, m_b_ada, m_norm_mix_g, m_w_in, m_b_gates, m_conv_w, m_conv_b, m_q_lora_g, m_w_uq, m_kv_lora_g, m_w_ukv, m_q_norm_g, m_k_norm_g, m_mlstm_norm_g, m_w_out, m_norm_mlp_g, m_w_ff1, m_w_ff2, v_w_ada, v_b_ada, v_norm_mix_g, v_w_in, v_b_gates, v_conv_w, v_conv_b, v_q_lora_g, v_w_uq, v_kv_lora_g, v_w_ukv, v_q_norm_g, v_k_norm_g, v_mlstm_norm_g, v_w_out, v_norm_mlp_g, v_w_ff1, v_w_ff2):
    S, D = x.shape[1], x.shape[2]
    QL, KVL = w_uq.shape[1], w_ukv.shape[1]
    H = w_uq.shape[2] * N_DEV // QK_DIM
    HM = mlstm_norm_g.shape[1]
    DM = mlstm_norm_g.shape[2] * N_DEV
    MW = HM * DM
    D_IN = w_in.shape[2] * N_DEV
    NADA = w_ada.shape[2]
    assert D_IN == QL + KVL + ROPE + 4 * MW + N_GATES and DM % LANE == 0 and S % CHUNK == 0
    assert (4 * MW) % QL == 0 and (4 * MW + QL) % KVL == 0 and KVL % LANE == 0
    idx = 4 * lax.axis_index("x") + 2 * lax.axis_index("y") + lax.axis_index("c")
    x2, tgt = x[0], loss_target[0]

    g_in, g_uq, g_ukv, g_conv, g_mn, c_all = all_gather(
        [w_in[0].astype(bf16), w_uq[0].astype(bf16), w_ukv[0].astype(bf16), conv_w[0], mlstm_norm_g[0], c],
        "gather_weights")
    c_all = c_all.reshape(N_DEV, D)
    xi, yi, ci = lax.axis_index("x"), lax.axis_index("y"), lax.axis_index("c")
    slots = jnp.stack([4 * (1 - xi) + 2 * yi + ci, 4 * xi + 2 * (1 - yi) + ci, 4 * (1 - xi) + 2 * (1 - yi) + ci]).astype(jnp.int32)

    wi = _blocks_to_cols(g_in)
    o_cq, o_ckv, o_kpe, o_m, o_g = 0, QL, QL + KVL, QL + KVL + ROPE, QL + KVL + ROPE + 4 * MW
    w_in_p = jnp.concatenate([wi[:, o_m:o_g], wi[:, o_cq:o_kpe], _pad_cols(wi[:, o_kpe:o_m], LANE),
                              _pad_cols(wi[:, o_g:], LANE)], axis=1)
    NP = w_in_p.shape[1]
    cb_cq, cb_ckv, cb_kpe, cb_g = 4 * MW // QL, (4 * MW + QL) // KVL, (4 * MW + QL + KVL) // LANE, NP // LANE - 1
    w_uq_p = jnp.pad(_blocks_to_cols(g_uq).reshape(QL, H, QK_DIM), ((0, 0), (0, 0), (0, QK_PAD - QK_DIM))).reshape(QL, H * QK_PAD)
    w_ukv_p = _blocks_to_cols(g_ukv).reshape(KVL, H, 2, NOPE).transpose(0, 2, 1, 3).reshape(KVL, 2 * H * NOPE)
    conv_w_f = jnp.pad(_blocks_to_cols(g_conv), ((0, 8 - CONV_W), (0, 0)))
    mn_g = _blocks_to_cols(g_mn).reshape(1, MW)
    gqn = _pad_cols(q_norm_g, QK_PAD)
    gkn = _pad_cols(k_norm_g, QK_PAD)
    fr_np = np.zeros((1, LANE), np.float32)
    fr_np[0, :HALF] = fr_np[0, HALF:ROPE] = ROPE_THETA ** (-np.arange(HALF, dtype=np.float32) / HALF)
    freqs = jnp.asarray(fr_np)
    pos = positions.astype(f32).reshape(S, 1)

    b_blk = lax.dynamic_slice(b_ada, (0, idx * NADA), (1, NADA))
    mod_part = ada_fwd(c_all, w_ada[0], b_blk, "ada_fwd")
    (mod_all,) = all_gather([mod_part], "gather_mod")
    mod = lax.dynamic_index_in_dim(mod_all, idx, axis=1, keepdims=False).reshape(1, N_DEV * NADA)
    shift1, scale1, gate1, shift2, scale2, gate2 = [mod[:, k * D:(k + 1) * D] for k in range(6)]

    (h,) = rowwise(f_norm_mod, [Row(x2)], [norm_mix_g, shift1, scale1], [(D, bf16)], n_rows=S, tile=256, name="norm_mix")
    proj = mm(h, w_in_p, name="proj_in", out_dtype=f32)
    r_cq, r_ckv, r_kpe = Row(proj, QL, cb_cq), Row(proj, KVL, cb_ckv), Row(proj, LANE, cb_kpe)
    f_prep = make_f_mla_prep(H, QK_DIM ** -0.5 * math.log2(math.e))
    prep_params = [q_lora_g, kv_lora_g, gqn, gkn, w_uq_p, w_ukv_p, freqs]
    Q, K, V = rowwise(f_prep, [r_cq, r_ckv, r_kpe, Row(pos, diff=False)], prep_params,
                      [(H * QK_PAD, bf16), (H * QK_PAD, bf16), (H * V_DIM, bf16)], n_rows=S, tile=256, name="mla_prep")
    attn, lse, (g_out, g_ff1, g_ff2) = flash_fwd(
        Q, K, V, H, "flash_fwd", side=[w_out[0].astype(bf16), w_ff1[0].astype(bf16), w_ff2[0].astype(bf16)])
    w_out_f = g_out.reshape(N_DEV * g_out.shape[1], D)
    w_ff2_f = g_ff2.reshape(N_DEV * g_ff2.shape[1], D)

    conv_bias = conv_b
    z = conv_fwd(proj, 2 * MW, conv_w_f, conv_bias, "conv_fwd")
    graw = proj[:, cb_g * LANE:cb_g * LANE + N_GATES].reshape(S, 4, HM)
    gcol = graw.transpose(1, 2, 0).reshape(2, 2, HM, S)
    bg = b_gates.reshape(2, 2, HM)
    gates = (gcol[:, 0].reshape(2, HM, S, 1), gcol[:, 1].reshape(2, HM, S, 1),
             gcol[:, 0].reshape(2, HM, 1, S), gcol[:, 1].reshape(2, HM, 1, S),
             bg[:, 0].reshape(2, HM, 1, 1), bg[:, 1].reshape(2, HM, 1, 1))
    hdir, cs, ns, ms = mlstm_fwd(z, proj, gates, HM, DM, "mlstm_fwd")
    f_post = make_f_mlstm_post(HM, DM)
    post_rows = [Row(hdir, MW, 0, lead=0), Row(hdir, MW, 0, lead=1), Row(proj, MW, 3)]
    (ml_out,) = rowwise(f_post, post_rows, [mn_g], [(MW, bf16)], n_rows=S, tile=256, name="mlstm_post")

    cat = jnp.concatenate([attn, ml_out], axis=1)
    mixed = mm(cat, w_out_f, name="proj_out", out_dtype=f32)
    mlp_params = [gate1, norm_mlp_g, shift2, scale2]
    x1, h2 = rowwise(f_resid_norm_mod, [Row(x2), Row(mixed)], mlp_params, [(D, f32), (D, bf16)],
                     n_rows=S, tile=256, name="resid_norm_mlp")
    u = mm(h2, g_ff1, name="ff1", out_dtype=bf16)
    y = mm(u, w_ff2_f, name="ff2", a_fn=_relu2, out_dtype=f32)
    loss_l, d_out, d_y, d_gate2 = loss_head(x1, y, tgt, gate2, "loss_head")
    loss = lax.psum(loss_l[0, 0], AXES)

    dw_ff2 = mm(u, d_y, name="dw_ff2", ta=True, a_fn=_relu2, out_dtype=bf16)
    d_u = mm(d_y, w_ff2_f, name="d_u", tb=True, epi=lambda acc, uu: acc * (2.0 * jnp.maximum(uu.astype(f32), 0.0)),
             extras=(u,), out_dtype=bf16)
    dw_ff1 = mm(h2, d_u, name="dw_ff1", ta=True, out_dtype=bf16, out_blocks=True)
    d_h2 = mm(d_u, g_ff1, name="d_h2", tb=True, out_dtype=f32, tn=2048)
    (d_x1, d_mixed), (d_gate1, d_g_mlp, d_shift2, d_scale2) = rowwise_vjp(
        f_resid_norm_mod, [Row(x2), Row(mixed)], mlp_params, [Row(d_out), Row(d_h2)],
        n_rows=S, tile=256, name="resid_norm_mlp_bwd", row_grad_dtypes=[f32, bf16])
    dw_out = mm(cat, d_mixed, name="dw_out", ta=True, out_dtype=bf16)
    d_cat = mm(d_mixed, w_out_f, name="d_cat", tb=True, out_dtype=f32, tn=2048)

    (d_hf, d_hb, d_om), (d_mn_g,) = rowwise_vjp(
        f_post, post_rows, [mn_g], [Row(d_cat, MW, H * V_DIM // MW)], n_rows=S, tile=256, name="mlstm_post_bwd")
    dh = jnp.stack([d_hf, d_hb])
    dzq, dzk, dvm, dic, dfc, dir_, dfr, dbi, dbf = mlstm_bwd(z, proj, gates, (cs, ns, ms), dh, HM, DM, "mlstm_bwd")
    (dz,) = rowwise(f_add_pairs, [Row(dzq, MW, 0, lead=0), Row(dzq, MW, 0, lead=1), Row(dzk, MW, 0, lead=0),
                                  Row(dzk, MW, 0, lead=1)], [], [(2 * MW, f32)], n_rows=S, tile=256, name="dz_sum")
    (d_vm,) = rowwise(f_add, [Row(dvm, MW, 0, lead=0), Row(dvm, MW, 0, lead=1)], [], [(MW, bf16)], n_rows=S, tile=256,
                      name="dv_sum")
    d_qk, d_conv_w, d_conv_b = conv_bwd(dz, proj, 2 * MW, conv_w_f, "conv_bwd")
    dg = jnp.stack([dic.reshape(2, HM, S) + dir_.reshape(2, HM, S), dfc.reshape(2, HM, S) + dfr.reshape(2, HM, S)], axis=1)
    d_gates = dg.reshape(4 * HM, S).T
    d_b_gates = jnp.stack([dbi.reshape(2, HM), dbf.reshape(2, HM)], axis=1).reshape(1, N_GATES)

    mlp_g = [dw_out.reshape(N_DEV, -1, D), dw_ff1, dw_ff2.reshape(N_DEV, -1, D)]
    mlp_tags = ["w_out", "w_ff1", "w_ff2"]
    mlp_sib = pair_exchange(mlp_g, "grad_pair_exchange_mlp")
    mlp_part = [chip_partials(g, r, slots, "grad_chip_partials_" + t) for g, r, t in zip(mlp_g, mlp_sib, mlp_tags)]
    dq, dk, dv, mlp_chips = flash_bwd(Q, K, V, attn, lse.reshape(H, 1, S), d_cat, 0, H, "flash_bwd", side=mlp_part)
    (d_cq, d_ckv, d_kpe), (d_gq, d_gkv, d_gqn, d_gkn, dw_uq_p, dw_ukv_p) = rowwise_vjp(
        f_prep, [r_cq, r_ckv, r_kpe, Row(pos, diff=False)], prep_params, [Row(dq), Row(dk), Row(dv)],
        n_rows=S, tile=256, name="mla_prep_bwd", row_grad_dtypes=[bf16, bf16, bf16],
        param_diff=[True, True, True, True, True, True, False])

    d_proj = jnp.concatenate([d_qk.astype(bf16), d_vm, d_om.astype(bf16), d_cq, d_ckv, d_kpe,
                              _pad_cols(d_gates.astype(bf16), LANE)], axis=1)
    dw_in_p = mm(h, d_proj, name="dw_in", ta=True, out_dtype=bf16)
    d_h = mm(d_proj, w_in_p, name="d_h", tb=True, out_dtype=f32, tn=2048)
    (grad_x,), (d_g_mix, d_shift1, d_scale1) = rowwise_vjp(
        f_norm_mod_thru, [Row(x2)], [norm_mix_g, shift1, scale1], [Row(d_h), Row(d_x1)],
        n_rows=S, tile=256, name="norm_mix_bwd")

    dmod = jnp.concatenate([d_shift1, d_scale1, d_gate1, d_shift2, d_scale2, d_gate2], axis=1)
    small = [(norm_mix_g, m_norm_mix_g, v_norm_mix_g, d_g_mix), (b_gates, m_b_gates, v_b_gates, d_b_gates),
             (conv_b, m_conv_b, v_conv_b, d_conv_b), (q_lora_g, m_q_lora_g, v_q_lora_g, d_gq),
             (kv_lora_g, m_kv_lora_g, v_kv_lora_g, d_gkv), (q_norm_g, m_q_norm_g, v_q_norm_g, d_gqn[:, :QK_DIM]),
             (k_norm_g, m_k_norm_g, v_k_norm_g, d_gkn[:, :QK_DIM]), (norm_mlp_g, m_norm_mlp_g, v_norm_mlp_g, d_g_mlp),
             (b_ada, m_b_ada, v_b_ada, dmod)]
    sizes = [s[0].shape[1] for s in small]
    P = sum(sizes)
    PP = -(-P // LANE) * LANE
    pack = lambda k: _pad_cols(jnp.concatenate([s[k] for s in small], axis=1), PP)
    (sg_all,) = all_gather([pack(3)], "gather_small_grads")
    s_out = adamw([sg_all[k] for k in range(N_DEV)], pack(0), pack(1), pack(2), "adamw_small")
    offs = np.concatenate([[0], np.cumsum(sizes)])
    small_out = [[o[:, offs[k]:offs[k + 1]] for o in s_out] for k in range(len(small))]

    dmod_all = sg_all[:, 0, offs[-2]:offs[-1]]
    dmod_blk = lax.dynamic_slice(dmod_all, (0, idx * NADA), (N_DEV, NADA))
    g_w_ada = ada_wgrad(c_all, dmod_blk, "ada_wgrad")
    ada_out = adamw([g_w_ada], w_ada[0], m_w_ada[0], v_w_ada[0], "adamw_ada")

    dwi = jnp.concatenate([dw_in_p[:, 4 * MW:4 * MW + QL + KVL + ROPE], dw_in_p[:, :4 * MW],
                           dw_in_p[:, cb_g * LANE:cb_g * LANE + N_GATES]], axis=1)
    dw_uq = dw_uq_p.reshape(QL, H, QK_PAD)[:, :, :QK_DIM].reshape(QL, H * QK_DIM)
    dw_ukv = dw_ukv_p.reshape(KVL, 2, H, NOPE).transpose(0, 2, 1, 3).reshape(KVL, 2 * H * NOPE)
    tiny = [(w_uq, m_w_uq, v_w_uq, _cols_to_blocks(dw_uq)),
            (w_ukv, m_w_ukv, v_w_ukv, _cols_to_blocks(dw_ukv)),
            (conv_w, m_conv_w, v_conv_w, _cols_to_blocks(d_conv_w[:CONV_W])),
            (mlstm_norm_g, m_mlstm_norm_g, v_mlstm_norm_g, _cols_to_blocks(d_mn_g.reshape(HM, DM)))]
    tsizes = [int(np.prod(b[0].shape)) for b in tiny]
    T = sum(tsizes)
    PC = 512
    PR = -(-T // (PC * 64)) * 64
    gpack = jnp.concatenate([b[3].astype(bf16).reshape(N_DEV, -1) for b in tiny], axis=1)
    gpack = jnp.pad(gpack, ((0, 0), (0, PR * PC - T))).reshape(N_DEV, PR, PC)
    wpack = lambda k: jnp.pad(jnp.concatenate([b[k].reshape(1, -1) for b in tiny], axis=1),
                              ((0, 0), (0, PR * PC - T))).reshape(PR, PC)
    large = [(w_in[0], m_w_in[0], v_w_in[0], _cols_to_blocks(dwi)),
             (w_out[0], m_w_out[0], v_w_out[0], mlp_g[0]),
             (w_ff1[0], m_w_ff1[0], v_w_ff1[0], mlp_g[1]),
             (w_ff2[0], m_w_ff2[0], v_w_ff2[0], mlp_g[2]),
             (wpack(0), wpack(1), wpack(2), gpack)]
    tags = ["w_in", "w_out", "w_ff1", "w_ff2", "tiny"]
    late_g = [large[0][3], large[4][3]]
    late_sib = pair_exchange(late_g, "grad_pair_exchange")
    late_part = [chip_partials(g, r, slots, "grad_chip_partials_" + t) for g, r, t in zip(late_g, late_sib, ["w_in", "tiny"])]
    late_chips = chip_exchange(late_part, "grad_chip_exchange")
    from_sibling = [late_sib[0]] + list(mlp_sib) + [late_sib[1]]
    from_chips = [late_chips[0]] + list(mlp_chips) + [late_chips[1]]
    l_out = []
    for (w_, m_, v_, g), r, fc, t in zip(large, from_sibling, from_chips, tags):
        mine = lax.dynamic_index_in_dim(g, idx, axis=0, keepdims=False)
        sib = lax.dynamic_index_in_dim(r, 2 * xi + yi, axis=0, keepdims=False)
        l_out.append(adamw([mine, sib, fc[0], fc[1], fc[2]], w_, m_, v_, "adamw_" + t))
    toffs = np.concatenate([[0], np.cumsum(tsizes)])
    tiny_out = [[o.reshape(-1)[toffs[k]:toffs[k + 1]].reshape(tiny[k][0].shape) for o in l_out[4]] for k in range(len(tiny))]
    big_out = [[o[None] for o in l_out[0]], tiny_out[0], tiny_out[1], [o[None] for o in l_out[1]],
               [o[None] for o in l_out[2]], [o[None] for o in l_out[3]], tiny_out[2], tiny_out[3]]

    names = ["w_ada", "b_ada", "norm_mix_g", "w_in", "b_gates", "conv_w", "conv_b", "q_lora_g", "w_uq", "kv_lora_g",
             "w_ukv", "q_norm_g", "k_norm_g", "mlstm_norm_g", "w_out", "norm_mlp_g", "w_ff1", "w_ff2"]
    res = {"w_ada": [o[None] for o in ada_out]}
    for k, nm in enumerate(["norm_mix_g", "b_gates", "conv_b", "q_lora_g", "kv_lora_g", "q_norm_g", "k_norm_g",
                            "norm_mlp_g", "b_ada"]):
        res[nm] = small_out[k]
    for k, nm in enumerate(["w_in", "w_uq", "w_ukv", "w_out", "w_ff1", "w_ff2", "conv_w", "mlstm_norm_g"]):
        res[nm] = big_out[k]
    outs = [loss, grad_x[None]]
    for part in range(4):
        outs += [res[nm][part] for nm in names]
    return tuple(outs)
```

```python
import functools
import math

import numpy as np
import jax
import jax.numpy as jnp
from jax import lax
from jax.experimental import pallas as pl
from jax.experimental.pallas import tpu as pltpu

f32 = jnp.float32
bf16 = jnp.bfloat16

N_DEV = 8
AXES = ("x", "y", "c")
MESH = pl.DeviceIdType.MESH

NOPE = 128
ROPE = 64
HALF = ROPE // 2
QK_DIM = NOPE + ROPE
QK_PAD = 256
V_DIM = 128
ROPE_THETA = 10000.0
CHUNK = 128
CONV_W = 5
N_GATES = 16
EPS = 1e-6
M_INIT = -1e30

ADAM_LR, ADAM_B1, ADAM_B2, ADAM_EPS, ADAM_WD, ADAM_STEP = 0.001, 0.9, 0.999, 1e-08, 0.01, 10

LANE = 128
VMEM_LIMIT = 56 * 1024 * 1024


def _cp(sem=None, vmem=VMEM_LIMIT):
    return pltpu.CompilerParams(dimension_semantics=sem, vmem_limit_bytes=vmem)


def _pick(n, target):
    best = None
    t = LANE
    while t <= min(n, target):
        if n % t == 0:
            best = t
        t += LANE
    return best if best is not None else n


def _pick_rows(n, target):
    t = min(n, target)
    while n % t:
        t -= 8
    return t


def _make_dots(cast, precision):
    def dg(a, b, ca, cb):
        if cast is not None:
            a = a.astype(cast)
            b = b.astype(cast)
        return lax.dot_general(a, b, (((ca,), (cb,)), ((), ())), precision=precision, preferred_element_type=f32)

    @jax.custom_vjp
    def nn(a, b):
        return dg(a, b, 1, 0)

    def nn_f(a, b):
        return dg(a, b, 1, 0), (a, b)

    def nn_b(res, g):
        a, b = res
        return dg(g, b, 1, 1).astype(a.dtype), dg(a, g, 0, 0).astype(b.dtype)

    nn.defvjp(nn_f, nn_b)

    @jax.custom_vjp
    def nt(a, b):
        return dg(a, b, 1, 1)

    def nt_f(a, b):
        return dg(a, b, 1, 1), (a, b)

    def nt_b(res, g):
        a, b = res
        return dg(g, b, 1, 0).astype(a.dtype), dg(g, a, 0, 0).astype(b.dtype)

    nt.defvjp(nt_f, nt_b)

    @jax.custom_vjp
    def tn(a, b):
        return dg(a, b, 0, 0)

    def tn_f(a, b):
        return dg(a, b, 0, 0), (a, b)

    def tn_b(res, g):
        a, b = res
        return dg(b, g, 1, 1).astype(a.dtype), dg(a, g, 1, 0).astype(b.dtype)

    tn.defvjp(tn_f, tn_b)
    return nn, nt, tn


bdot, bdot_nt, bdot_tn = _make_dots(bf16, None)
hdot, hdot_nt, hdot_tn = _make_dots(None, lax.Precision.HIGHEST)


def _silu(x):
    return x * jax.nn.sigmoid(x)


def _rms(x, n):
    return x * lax.rsqrt(jnp.sum(x * x, axis=-1, keepdims=True) * (1.0 / n) + EPS)


def _place():
    return lax.axis_index("x"), lax.axis_index("y"), lax.axis_index("c")


def _gather_phases(ins, outs, send_sems, recv_sems, local_sems):
    n = len(ins)
    x, y, c = _place()
    me, sibling = (x, y, c), (x, y, 1 - c)
    chips = [(1 - x, y), (x, 1 - y), (1 - x, 1 - y)]

    def slot(o, p):
        return outs[o].at[4 * p[0] + 2 * p[1] + p[2]]

    def copy(o, k, block, to, src=None):
        dst = slot(o, block)
        return pltpu.make_async_remote_copy(
            src_ref=dst if src is None else src, dst_ref=dst,
            send_sem=send_sems.at[o, k], recv_sem=recv_sems.at[o, k],
            device_id=to, device_id_type=MESH)

    def local(o):
        return pltpu.make_async_copy(ins[o], slot(o, me), local_sems.at[o])

    def first(o):
        return [copy(o, 0, me, sibling, src=ins[o])] + [copy(o, 1 + j, me, (*chip, c), src=ins[o])
                                                        for j, chip in enumerate(chips)]

    def start():
        for o in range(n):
            local(o).start()
        for o in range(n):
            for cp in first(o):
                cp.start()

    def mid():
        for o in range(n):
            for j, chip in enumerate(chips):
                copy(o, 1 + j, (*chip, c), me).wait_recv()
                copy(o, 4 + j, (*chip, c), sibling).start()

    def finish():
        for o in range(n):
            copy(o, 0, sibling, me).wait_recv()
            for j, chip in enumerate(chips):
                copy(o, 4 + j, (*chip, 1 - c), me).wait_recv()
        for o in range(n):
            for cp in first(o):
                cp.wait_send()
            for j, chip in enumerate(chips):
                copy(o, 4 + j, (*chip, c), sibling).wait_send()
        for o in range(n):
            local(o).wait()

    return start, mid, finish


def _gather_scratch(n):
    return [pltpu.SemaphoreType.DMA((n, 7)), pltpu.SemaphoreType.DMA((n, 7)), pltpu.SemaphoreType.DMA((n,))]


def all_gather(ops, name):
    n = len(ops)

    def body(*refs):
        start, mid, finish = _gather_phases(refs[:n], refs[n:2 * n], *refs[2 * n:])
        start()
        mid()
        finish()

    anyspec = pl.BlockSpec(memory_space=pl.ANY)
    return pl.pallas_call(
        body, name=name,
        out_shape=[jax.ShapeDtypeStruct((N_DEV,) + o.shape, o.dtype) for o in ops],
        in_specs=[anyspec] * n, out_specs=[anyspec] * n,
        scratch_shapes=_gather_scratch(n),
    )(*ops)


def pair_exchange(gs, name):
    n = len(gs)

    def body(*refs):
        g_refs, out_refs = refs[:n], refs[n:2 * n]
        send_sems, recv_sems = refs[2 * n:]
        x, y, c = _place()
        sibling = (x, y, 1 - c)
        cps = []
        for o in range(n):
            for q in range(4):
                cp = pltpu.make_async_remote_copy(
                    src_ref=g_refs[o].at[2 * q + (1 - c)], dst_ref=out_refs[o].at[q],
                    send_sem=send_sems.at[o, q], recv_sem=recv_sems.at[o, q],
                    device_id=sibling, device_id_type=MESH)
                cp.start()
                cps.append(cp)
        for cp in cps:
            cp.wait_recv()
        for cp in cps:
            cp.wait_send()

    anyspec = pl.BlockSpec(memory_space=pl.ANY)
    return pl.pallas_call(
        body, name=name, out_shape=[jax.ShapeDtypeStruct((4,) + g.shape[1:], g.dtype) for g in gs],
        in_specs=[anyspec] * n, out_specs=[anyspec] * n,
        scratch_shapes=[pltpu.SemaphoreType.DMA((n, 4)), pltpu.SemaphoreType.DMA((n, 4))],
    )(*gs)


def _chip_exchange_phases(p_refs, out_refs, send_sems, recv_sems):
    n = len(p_refs)
    x, y, c = _place()
    chips = [(1 - x, y), (x, 1 - y), (1 - x, 1 - y)]

    def copies():
        return [pltpu.make_async_remote_copy(
            src_ref=p_refs[o].at[j], dst_ref=out_refs[o].at[j],
            send_sem=send_sems.at[o, j], recv_sem=recv_sems.at[o, j],
            device_id=(*chip, c), device_id_type=MESH) for o in range(n) for j, chip in enumerate(chips)]

    def start():
        for cp in copies():
            cp.start()

    def finish():
        for cp in copies():
            cp.wait_recv()
        for cp in copies():
            cp.wait_send()

    return start, finish


def _chip_exchange_scratch(n):
    return [pltpu.SemaphoreType.DMA((n, 3)), pltpu.SemaphoreType.DMA((n, 3))]


def chip_partials(g, recv, slots, name):
    _, R, C = g.shape
    tr = _pick_rows(R, 512)

    def body(s_ref, a_ref, b_ref, o_ref):
        o_ref[...] = (a_ref[...].astype(f32) + b_ref[...].astype(f32)).astype(o_ref.dtype)

    grid_spec = pltpu.PrefetchScalarGridSpec(
        num_scalar_prefetch=1, grid=(3, R // tr),
        in_specs=[pl.BlockSpec((None, tr, C), lambda j, i, s: (s[j], i, 0)),
                  pl.BlockSpec((None, tr, C), lambda j, i, s: (s[j] // 2, i, 0))],
        out_specs=pl.BlockSpec((None, tr, C), lambda j, i, s: (j, i, 0)))
    return pl.pallas_call(body, name=name, grid_spec=grid_spec,
                          out_shape=jax.ShapeDtypeStruct((3, R, C), g.dtype),
                          compiler_params=_cp(("arbitrary", "arbitrary")))(slots, g, recv)


def adamw(parts, w, m, v, name, rows=256):
    R, C = w.shape
    tr = _pick_rows(R, rows)
    npart = len(parts)
    c1 = 1.0 - ADAM_B1 ** ADAM_STEP
    c2 = 1.0 - ADAM_B2 ** ADAM_STEP

    def body(*refs):
        p_refs = refs[:npart]
        w_ref, m_ref, v_ref, g_out, d_out, m_out, v_out = refs[npart:]
        g = p_refs[0][...].astype(f32)
        for p in p_refs[1:]:
            g = g + p[...].astype(f32)
        mn = ADAM_B1 * m_ref[...] + (1.0 - ADAM_B1) * g
        vn = ADAM_B2 * v_ref[...] + (1.0 - ADAM_B2) * (g * g)
        m_hat = mn / c1
        v_hat = vn / c2
        g_out[...] = g
        d_out[...] = -ADAM_LR * (m_hat / (jnp.sqrt(v_hat) + ADAM_EPS) + ADAM_WD * w_ref[...])
        m_out[...] = mn
        v_out[...] = vn

    spec = pl.BlockSpec((tr, C), lambda i: (i, 0))
    return pl.pallas_call(
        body, name=name, grid=(R // tr,),
        in_specs=[spec] * (npart + 3), out_specs=[spec] * 4,
        out_shape=[jax.ShapeDtypeStruct((R, C), f32)] * 4,
        compiler_params=_cp(("arbitrary",)))(*parts, w, m, v)


def mm(a, b, *, name, ta=False, tb=False, a_fn=None, epi=None, extras=(), out_dtype=f32, out_blocks=False, side=(),
       tm=512, tn=1024, tk=2048):
    K, M = a.shape if ta else a.shape[::-1]
    b3 = b.ndim == 3
    if b3:
        N, K2 = (b.shape[1], N_DEV * b.shape[2]) if tb else (N_DEV * b.shape[2], b.shape[1])
    else:
        N, K2 = b.shape if tb else b.shape[::-1]
    assert K == K2, (a.shape, b.shape, ta, tb)
    n_split = N // N_DEV if (out_blocks or (b3 and not tb)) else N
    k_split = K // N_DEV if (b3 and tb) else K
    tm, tn, tk = _pick(M, tm), _pick(n_split, tn), _pick(k_split, tk)
    nb, kb = n_split // tn, k_split // tk
    nk = K // tk
    ne = len(extras)
    assert not (out_blocks and ne)
    dims = (((0 if ta else 1,), (1 if tb else 0,)), ((), ()))

    ns = len(side)
    n_steps = (M // tm) * (N // tn) * nk
    assert ns == 0 or n_steps >= 2

    def body(a_ref, b_ref, *rest):
        e_refs, o_ref, acc = rest[:ne], rest[ne + ns], rest[ne + 2 * ns + 1]
        k = pl.program_id(2)
        if ns:
            step = (pl.program_id(0) * (N // tn) + pl.program_id(1)) * nk + k
            x_start, x_finish = _chip_exchange_phases(rest[ne:ne + ns], rest[ne + ns + 1:ne + 2 * ns + 1],
                                                      *rest[ne + 2 * ns + 2:])
            pl.when(step == 0)(x_start)

        @pl.when(k == 0)
        def _():
            acc[...] = jnp.zeros_like(acc)

        av = a_ref[...]
        if a_fn is not None:
            av = a_fn(av.astype(f32))
        acc[...] += lax.dot_general(av.astype(bf16), b_ref[...].astype(bf16), dims, preferred_element_type=f32)

        @pl.when(k == nk - 1)
        def _():
            r = acc[...]
            if epi is not None:
                r = epi(r, *[e[...] for e in e_refs])
            o_ref[...] = r.astype(o_ref.dtype)

        if ns:
            pl.when(step == n_steps - 1)(x_finish)

    a_spec = pl.BlockSpec((tk, tm), lambda i, j, k: (k, i)) if ta else pl.BlockSpec((tm, tk), lambda i, j, k: (i, k))
    if b3 and tb:
        b_spec = pl.BlockSpec((None, tn, tk), lambda i, j, k: (k // kb, j, k % kb))
    elif b3:
        b_spec = pl.BlockSpec((None, tk, tn), lambda i, j, k: (j // nb, k, j % nb))
    else:
        b_spec = pl.BlockSpec((tn, tk), lambda i, j, k: (j, k)) if tb else pl.BlockSpec((tk, tn), lambda i, j, k: (k, j))
    if out_blocks:
        o_spec = pl.BlockSpec((None, tm, tn), lambda i, j, k: (j // nb, i, j % nb))
        o_shape = jax.ShapeDtypeStruct((N_DEV, M, N // N_DEV), out_dtype)
    else:
        o_spec = pl.BlockSpec((tm, tn), lambda i, j, k: (i, j))
        o_shape = jax.ShapeDtypeStruct((M, N), out_dtype)
    if not ns:
        return pl.pallas_call(
            body, name=name, grid=(M // tm, N // tn, nk),
            in_specs=[a_spec, b_spec] + [o_spec] * ne, out_specs=o_spec,
            out_shape=o_shape,
            scratch_shapes=[pltpu.VMEM((tm, tn), f32)],
            compiler_params=_cp(("parallel", "parallel", "arbitrary")))(a, b, *extras)
    anyspec = pl.BlockSpec(memory_space=pl.ANY)
    res = pl.pallas_call(
        body, name=name, grid=(M // tm, N // tn, nk),
        in_specs=[a_spec, b_spec] + [o_spec] * ne + [anyspec] * ns, out_specs=[o_spec] + [anyspec] * ns,
        out_shape=[o_shape] + [jax.ShapeDtypeStruct(p.shape, p.dtype) for p in side],
        scratch_shapes=[pltpu.VMEM((tm, tn), f32)] + _chip_exchange_scratch(ns),
        compiler_params=_cp(("arbitrary", "arbitrary", "arbitrary")))(a, b, *extras, *side)
    return res[0], list(res[1:])


class Row:
    def __init__(self, arr, width=None, col=0, lead=None, diff=True):
        self.arr, self.col, self.lead, self.diff = arr, col, lead, diff
        self.width = arr.shape[-1] if width is None else width

    def spec(self, t):
        col, lead = self.col, self.lead
        if lead is None:
            return pl.BlockSpec((t, self.width), lambda i: (i, col))
        return pl.BlockSpec((None, t, self.width), lambda i: (lead, i, col))


def _whole(p):
    return pl.BlockSpec(p.shape, lambda i: (0,) * p.ndim)


def rowwise(fn, rows, params, outs, *, n_rows, tile, name):
    t = _pick_rows(n_rows, tile)
    nr, npar, no = len(rows), len(params), len(outs)

    def body(*refs):
        r_refs, p_refs, o_refs = refs[:nr], refs[nr:nr + npar], refs[nr + npar:]
        res = fn(*[r[...].astype(f32) for r in r_refs], *[p[...] for p in p_refs])
        for o_ref, val in zip(o_refs, res):
            o_ref[...] = val.astype(o_ref.dtype)

    return pl.pallas_call(
        body, name=name, grid=(n_rows // t,),
        in_specs=[r.spec(t) for r in rows] + [_whole(p) for p in params],
        out_specs=[pl.BlockSpec((t, w), lambda i: (i, 0)) for w, _ in outs],
        out_shape=[jax.ShapeDtypeStruct((n_rows, w), dt) for w, dt in outs],
        compiler_params=_cp(("arbitrary",)))(*[r.arr for r in rows], *params)


def rowwise_vjp(fn, rows, params, cts, *, n_rows, tile, name, row_grad_dtypes=None, param_diff=None):
    t = _pick_rows(n_rows, tile)
    nr, npar, nc = len(rows), len(params), len(cts)
    param_diff = [True] * npar if param_diff is None else param_diff
    d_rows = [k for k, r in enumerate(rows) if r.diff]
    d_pars = [k for k in range(npar) if param_diff[k]]
    row_grad_dtypes = [f32] * len(d_rows) if row_grad_dtypes is None else row_grad_dtypes

    def body(*refs):
        r_refs, p_refs = refs[:nr], refs[nr:nr + npar]
        c_refs = refs[nr + npar:nr + npar + nc]
        dr_refs = refs[nr + npar + nc:nr + npar + nc + len(d_rows)]
        dp_refs = refs[nr + npar + nc + len(d_rows):]
        rv = [r[...].astype(f32) for r in r_refs]
        pv = [p[...] for p in p_refs]

        def g(*dvals):
            full_r, full_p = list(rv), list(pv)
            for k, val in zip(d_rows, dvals[:len(d_rows)]):
                full_r[k] = val
            for k, val in zip(d_pars, dvals[len(d_rows):]):
                full_p[k] = val
            return tuple(fn(*full_r, *full_p))

        prim = [rv[k] for k in d_rows] + [pv[k].astype(f32) for k in d_pars]
        _, pull = jax.vjp(g, *prim)
        grads = pull(tuple(c[...].astype(f32) for c in c_refs))
        for ref, val in zip(dr_refs, grads[:len(d_rows)]):
            ref[...] = val.astype(ref.dtype)

        @pl.when(pl.program_id(0) == 0)
        def _():
            for ref in dp_refs:
                ref[...] = jnp.zeros_like(ref)

        for ref, val in zip(dp_refs, grads[len(d_rows):]):
            ref[...] += val

    out_specs = [pl.BlockSpec((t, rows[k].width), lambda i: (i, 0)) for k in d_rows]
    out_specs += [_whole(params[k]) for k in d_pars]
    out_shape = [jax.ShapeDtypeStruct((n_rows, rows[k].width), dt) for k, dt in zip(d_rows, row_grad_dtypes)]
    out_shape += [jax.ShapeDtypeStruct(params[k].shape, f32) for k in d_pars]
    res = pl.pallas_call(
        body, name=name, grid=(n_rows // t,),
        in_specs=[r.spec(t) for r in rows] + [_whole(p) for p in params] + [c.spec(t) for c in cts],
        out_specs=out_specs, out_shape=out_shape,
        compiler_params=_cp(("arbitrary",)))(*[r.arr for r in rows], *params, *[c.arr for c in cts])
    return res[:len(d_rows)], res[len(d_rows):]


def f_norm_mod(x, g, shift, scale):
    return (_rms(x, x.shape[-1]) * g * (1.0 + scale) + shift,)


def f_norm_mod_thru(x, g, shift, scale):
    return f_norm_mod(x, g, shift, scale) + (x,)


def f_resid_norm_mod(x, mixed, gate1, g2, shift2, scale2):
    x1 = x + gate1 * mixed
    return (x1,) + f_norm_mod(x1, g2, shift2, scale2)


def _rope_rot():
    i = lax.broadcasted_iota(jnp.int32, (LANE, LANE), 0)
    j = lax.broadcasted_iota(jnp.int32, (LANE, LANE), 1)
    neg = jnp.where((i == j + HALF) & (j < HALF), -1.0, 0.0)
    pos = jnp.where((i == j - HALF) & (j >= HALF) & (j < ROPE), 1.0, 0.0)
    return (neg + pos).astype(f32)


def make_f_mla_prep(n_heads, q_scale):
    def fn(cq, ckv, kpe, pos, gq, gkv, gqn, gkn, w_uq, w_ukv, freqs):
        rot = _rope_rot()
        ang = pos * freqs
        cos, sin = jnp.cos(ang), jnp.sin(ang)

        def rope(u):
            return u * cos + hdot(u, rot) * sin

        qraw = bdot(_rms(cq, cq.shape[-1]) * gq, w_uq)
        kv = bdot(_rms(ckv, ckv.shape[-1]) * gkv, w_ukv)
        kpe_ss = jnp.sum(kpe * kpe, axis=-1, keepdims=True)
        qs, ks = [], []
        for h in range(n_heads):
            qh = _rms(qraw[:, h * QK_PAD:(h + 1) * QK_PAD], QK_DIM) * gqn
            qs += [qh[:, :NOPE], rope(qh[:, NOPE:])]
            kn = kv[:, h * NOPE:(h + 1) * NOPE]
            r = lax.rsqrt((jnp.sum(kn * kn, axis=-1, keepdims=True) + kpe_ss) * (1.0 / QK_DIM) + EPS)
            ks += [kn * r * gkn[:, :NOPE], rope(kpe * r * gkn[:, NOPE:])]
        return jnp.concatenate(qs, axis=-1) * q_scale, jnp.concatenate(ks, axis=-1), kv[:, n_heads * NOPE:]
    return fn


def make_f_mlstm_post(n_heads, dm):
    def fn(hf, hb, o, g):
        hm = hf + hb
        outs = []
        for h in range(n_heads):
            sl = slice(h * dm, (h + 1) * dm)
            outs.append(jax.nn.sigmoid(o[:, sl]) * (_rms(hm[:, sl], dm) * g[:, sl]))
        return (jnp.concatenate(outs, axis=-1),)
    return fn


def f_add_pairs(a0, a1, b0, b1):
    return (jnp.concatenate([a0 + a1, b0 + b1], axis=-1),)


def f_add(a, b):
    return (a + b,)


def loss_head(x1, y, target, gate2, name, tile=256):
    S, D = x1.shape
    t = _pick_rows(S, tile)

    def body(x1_ref, y_ref, t_ref, g_ref, loss_ref, dout_ref, dy_ref, dgate_ref):
        @pl.when(pl.program_id(0) == 0)
        def _():
            loss_ref[...] = jnp.zeros_like(loss_ref)
            dgate_ref[...] = jnp.zeros_like(dgate_ref)

        yv, gv = y_ref[...], g_ref[...]
        e = x1_ref[...] + gv * yv - t_ref[...]
        loss_ref[...] += 0.5 * jnp.sum(jnp.sum(e * e, axis=-1, keepdims=True) * (1.0 / D), axis=0, keepdims=True)
        d_out = e * (1.0 / D)
        dout_ref[...] = d_out
        dy_ref[...] = (d_out * gv).astype(dy_ref.dtype)
        dgate_ref[...] += jnp.sum(d_out * yv, axis=0, keepdims=True)

    row = pl.BlockSpec((t, D), lambda i: (i, 0))
    return pl.pallas_call(
        body, name=name, grid=(S // t,),
        in_specs=[row, row, row, pl.BlockSpec((1, D), lambda i: (0, 0))],
        out_specs=[pl.BlockSpec((1, 1), lambda i: (0, 0)), row, row, pl.BlockSpec((1, D), lambda i: (0, 0))],
        out_shape=[jax.ShapeDtypeStruct((1, 1), f32), jax.ShapeDtypeStruct((S, D), f32),
                   jax.ShapeDtypeStruct((S, D), bf16), jax.ShapeDtypeStruct((1, D), f32)],
        compiler_params=_cp(("arbitrary",)))(x1, y, target, gate2)


def ada_fwd(c_all, w_blk, b_blk, name):
    B, D = c_all.shape
    N = w_blk.shape[1]
    tn = _pick(N, 512)

    def body(c_ref, w_ref, b_ref, o_ref):
        o_ref[...] = bdot(_silu(c_ref[...]), w_ref[...]) + b_ref[...]

    return pl.pallas_call(
        body, name=name, grid=(N // tn,),
        in_specs=[pl.BlockSpec((B, D), lambda j: (0, 0)), pl.BlockSpec((D, tn), lambda j: (0, j)),
                  pl.BlockSpec((1, tn), lambda j: (0, j))],
        out_specs=pl.BlockSpec((B, tn), lambda j: (0, j)),
        out_shape=jax.ShapeDtypeStruct((B, N), f32), compiler_params=_cp(("arbitrary",)))(c_all, w_blk, b_blk)


def ada_wgrad(c_all, dmod_blk, name):
    B, D = c_all.shape
    N = dmod_blk.shape[1]
    tn = _pick(N, 512)

    def body(c_ref, d_ref, o_ref):
        o_ref[...] = hdot_tn(_silu(c_ref[...]), d_ref[...])

    return pl.pallas_call(
        body, name=name, grid=(N // tn,),
        in_specs=[pl.BlockSpec((B, D), lambda j: (0, 0)), pl.BlockSpec((B, tn), lambda j: (0, j))],
        out_specs=pl.BlockSpec((D, tn), lambda j: (0, j)),
        out_shape=jax.ShapeDtypeStruct((D, N), f32), compiler_params=_cp(("arbitrary",)))(c_all, dmod_blk)


def _nt(a, b):
    return lax.dot_general(a, b, (((1,), (1,)), ((), ())), preferred_element_type=f32)


def _tn(a, b):
    return lax.dot_general(a, b, (((0,), (0,)), ((), ())), preferred_element_type=f32)


def flash_fwd(q, k, v, n_heads, name, side=(), tq=512, tk=8192, sub=1024):
    S = q.shape[0]
    tq, tk = _pick(S, tq), _pick(S, tk)
    sub = _pick(tk, sub)
    nk, nsub = S // tk, tk // sub
    ns = len(side)
    n_steps = n_heads * (S // tq) * nk
    assert ns == 0 or n_steps >= 3

    def body(*refs):
        q_ref, k_ref, v_ref = refs[:3]
        o_ref, lse_ref = refs[3 + ns:5 + ns]
        m_sc, l_sc, acc_sc = refs[5 + 2 * ns:8 + 2 * ns]
        j = pl.program_id(2)
        step = (pl.program_id(0) * (S // tq) + pl.program_id(1)) * nk + j
        if ns:
            g_start, g_mid, g_finish = _gather_phases(refs[3:3 + ns], refs[5 + ns:5 + 2 * ns], *refs[8 + 2 * ns:])
            pl.when(step == 0)(g_start)
            pl.when(step == n_steps // 2)(g_mid)

        @pl.when(j == 0)
        def _():
            m_sc[...] = jnp.full_like(m_sc, -jnp.inf)
            l_sc[...] = jnp.zeros_like(l_sc)
            acc_sc[...] = jnp.zeros_like(acc_sc)

        qv = q_ref[...]
        m = m_sc[...]
        ss = [_nt(qv, k_ref[b * sub:(b + 1) * sub, :]) for b in range(nsub)]
        mx = ss[0]
        for s in ss[1:]:
            mx = jnp.maximum(mx, s)
        m_new = jnp.maximum(m, jnp.max(mx, axis=-1, keepdims=True))
        alpha = jnp.exp2(m - m_new)
        psum, pv = None, None
        for b in range(nsub):
            p = jnp.exp2(ss[b] - m_new)
            d = jnp.dot(p.astype(bf16), v_ref[b * sub:(b + 1) * sub, :], preferred_element_type=f32)
            psum = p if psum is None else psum + p
            pv = d if pv is None else pv + d
        m, l, acc = m_new, alpha * l_sc[...] + jnp.sum(psum, axis=-1, keepdims=True), alpha * acc_sc[...] + pv
        m_sc[...], l_sc[...], acc_sc[...] = m, l, acc

        @pl.when(j == nk - 1)
        def _():
            o_ref[...] = (acc / l).astype(o_ref.dtype)
            lse_ref[...] = m + jnp.log2(l)

        if ns:
            pl.when(step == n_steps - 1)(g_finish)

    anyspec = pl.BlockSpec(memory_space=pl.ANY)
    res = pl.pallas_call(
        body, name=name, grid=(n_heads, S // tq, nk),
        in_specs=[pl.BlockSpec((tq, QK_PAD), lambda h, i, j: (i, h)),
                  pl.BlockSpec((tk, QK_PAD), lambda h, i, j: (j, h)),
                  pl.BlockSpec((tk, V_DIM), lambda h, i, j: (j, h))] + [anyspec] * ns,
        out_specs=[pl.BlockSpec((tq, V_DIM), lambda h, i, j: (i, h)),
                   pl.BlockSpec((None, tq, 1), lambda h, i, j: (h, i, 0))] + [anyspec] * ns,
        out_shape=[jax.ShapeDtypeStruct((S, n_heads * V_DIM), bf16), jax.ShapeDtypeStruct((n_heads, S, 1), f32)]
        + [jax.ShapeDtypeStruct((N_DEV,) + a.shape, a.dtype) for a in side],
        scratch_shapes=[pltpu.VMEM((tq, 1), f32), pltpu.VMEM((tq, 1), f32), pltpu.VMEM((tq, V_DIM), f32)]
        + (_gather_scratch(ns) if ns else []),
        compiler_params=_cp(("arbitrary", "arbitrary", "arbitrary")))(q, k, v, *side)
    return res[0], res[1], list(res[2:])


def flash_bwd(q, k, v, o, lse_row, do, do_col0, n_heads, name, side=(), tq=512, tk=8192, sub=512):
    S = q.shape[0]
    tq, tk = _pick(S, tq), _pick(S, tk)
    sub = _pick(tk, sub)
    nsub = tk // sub
    ln2 = math.log(2.0)
    ns = len(side)
    n_steps = n_heads * (S // tq) * (S // tk)
    assert ns == 0 or n_steps >= 2

    def body(*refs):
        q_ref, k_ref, v_ref, o_ref, lse_ref, do_ref = refs[:6]
        dq_ref, dk_ref, dv_ref = refs[6 + ns:9 + ns]
        i, j = pl.program_id(1), pl.program_id(2)
        step = (pl.program_id(0) * (S // tq) + i) * (S // tk) + j
        if ns:
            x_start, x_finish = _chip_exchange_phases(refs[6:6 + ns], refs[9 + ns:9 + 2 * ns], *refs[9 + 2 * ns:])
            pl.when(step == 0)(x_start)

        @pl.when(j == 0)
        def _():
            dq_ref[...] = jnp.zeros_like(dq_ref)

        @pl.when((i == 0) & (j == 0))
        def _():
            dk_ref[...] = jnp.zeros_like(dk_ref)
            dv_ref[...] = jnp.zeros_like(dv_ref)

        qv = q_ref[...]
        dof = do_ref[...].astype(f32)
        do_b = dof.astype(bf16)
        do_s = (dof * ln2).astype(bf16)
        delta = hdot_nt(jnp.ones((8, V_DIM), f32), dof * ln2 * o_ref[...].astype(f32))[0:1, :]
        lse = lse_ref[...]
        dq = jnp.zeros((tq, QK_PAD), f32)
        for b in range(nsub):
            kb = k_ref[b * sub:(b + 1) * sub, :]
            rows = pl.ds(pl.multiple_of(j * tk + b * sub, sub), sub)
            pt = jnp.exp2(_nt(kb, qv) - lse)
            dpt = _nt(v_ref[b * sub:(b + 1) * sub, :], do_s)
            dst = (pt * (dpt - delta)).astype(bf16)
            dv_ref[rows, :] += jnp.dot(pt.astype(bf16), do_b, preferred_element_type=f32)
            dk_ref[rows, :] += jnp.dot(dst, qv, preferred_element_type=f32)
            dq = dq + _tn(dst, kb)
        dq_ref[...] += dq
        if ns:
            pl.when(step == n_steps - 1)(x_finish)

    anyspec = pl.BlockSpec(memory_space=pl.ANY)
    res = pl.pallas_call(
        body, name=name, grid=(n_heads, S // tq, S // tk),
        in_specs=[pl.BlockSpec((tq, QK_PAD), lambda h, i, j: (i, h)),
                  pl.BlockSpec((tk, QK_PAD), lambda h, i, j: (j, h)),
                  pl.BlockSpec((tk, V_DIM), lambda h, i, j: (j, h)),
                  pl.BlockSpec((tq, V_DIM), lambda h, i, j: (i, h)),
                  pl.BlockSpec((None, 1, tq), lambda h, i, j: (h, 0, i)),
                  pl.BlockSpec((tq, V_DIM), lambda h, i, j: (i, do_col0 + h))] + [anyspec] * ns,
        out_specs=[pl.BlockSpec((tq, QK_PAD), lambda h, i, j: (i, h)),
                   pl.BlockSpec((S, QK_PAD), lambda h, i, j: (0, h)),
                   pl.BlockSpec((S, V_DIM), lambda h, i, j: (0, h))] + [anyspec] * ns,
        out_shape=[jax.ShapeDtypeStruct((S, n_heads * QK_PAD), f32), jax.ShapeDtypeStruct((S, n_heads * QK_PAD), f32),
                   jax.ShapeDtypeStruct((S, n_heads * V_DIM), f32)] + [jax.ShapeDtypeStruct(a.shape, a.dtype) for a in side],
        scratch_shapes=_chip_exchange_scratch(ns) if ns else [],
        compiler_params=_cp(("arbitrary", "arbitrary", "arbitrary")))(q, k, v, o, lse_row, do, *side)
    return res[0], res[1], res[2], list(res[3:])


def _shifted(prev, cur, nxt, k, first, last):
    if k == 0:
        return cur
    t = cur.shape[0]
    r = lax.broadcasted_iota(jnp.int32, (HALO,) + cur.shape[1:], 0)
    if k < 0:
        body = pltpu.roll(cur, -k, 0)
        edge = jnp.where(first, 0.0, pltpu.roll(prev, -k, 0))
        return jnp.concatenate([jnp.where(r < -k, edge, body[:HALO]), body[HALO:]], axis=0)
    body = pltpu.roll(cur, t - k, 0)
    edge = jnp.where(last, 0.0, pltpu.roll(nxt, HALO - k, 0))
    return jnp.concatenate([body[:t - HALO], jnp.where(r >= HALO - k, edge, body[t - HALO:])], axis=0)


HALO = 8


def _halo_specs(t, width, n_tiles, col=0):
    per = t // HALO
    return [pl.BlockSpec((HALO, width), lambda i: (jnp.maximum(i * per - 1, 0), col)),
            pl.BlockSpec((t, width), lambda i: (i, col)),
            pl.BlockSpec((HALO, width), lambda i: (jnp.minimum((i + 1) * per, n_tiles * per - 1), col))]


def conv_fwd(proj, width, w, b, name, tile=256):
    S = proj.shape[0]
    t = _pick_rows(S, tile)
    n_tiles = S // t

    def body(p_ref, c_ref, n_ref, w_ref, b_ref, z_ref):
        i = pl.program_id(0)
        first, last = i == 0, i == n_tiles - 1
        prev, cur, nxt = p_ref[...], c_ref[...], n_ref[...]
        z = b_ref[...] + jnp.zeros_like(cur)
        for j in range(CONV_W):
            z = z + w_ref[j:j + 1, :] * _shifted(prev, cur, nxt, j - CONV_W // 2, first, last)
        z_ref[...] = z

    return pl.pallas_call(
        body, name=name, grid=(n_tiles,),
        in_specs=_halo_specs(t, width, n_tiles) + [_whole(w), _whole(b)],
        out_specs=pl.BlockSpec((t, width), lambda i: (i, 0)),
        out_shape=jax.ShapeDtypeStruct((S, width), f32),
        compiler_params=_cp(("arbitrary",)))(proj, proj, proj, w, b)


def conv_bwd(dz, proj, width, w, name, tile=256):
    S = proj.shape[0]
    t = _pick_rows(S, tile)
    n_tiles = S // t

    def body(dp_ref, dc_ref, dn_ref, up_ref, uc_ref, un_ref, w_ref, du_ref, dw_ref, db_ref):
        i = pl.program_id(0)
        first, last = i == 0, i == n_tiles - 1

        @pl.when(first)
        def _():
            dw_ref[...] = jnp.zeros_like(dw_ref)
            db_ref[...] = jnp.zeros_like(db_ref)

        dprev, dcur, dnxt = dp_ref[...], dc_ref[...], dn_ref[...]
        uprev, ucur, unxt = up_ref[...], uc_ref[...], un_ref[...]
        du = jnp.zeros_like(dcur)
        for j in range(CONV_W):
            k = j - CONV_W // 2
            du = du + w_ref[j:j + 1, :] * _shifted(dprev, dcur, dnxt, -k, first, last)
            dw_ref[j:j + 1, :] += jnp.sum(dcur * _shifted(uprev, ucur, unxt, k, first, last), axis=0, keepdims=True)
        du_ref[...] = du
        db_ref[...] += jnp.sum(dcur, axis=0, keepdims=True)

    return pl.pallas_call(
        body, name=name, grid=(n_tiles,),
        in_specs=_halo_specs(t, width, n_tiles) + _halo_specs(t, width, n_tiles) + [_whole(w)],
        out_specs=[pl.BlockSpec((t, width), lambda i: (i, 0)), pl.BlockSpec((8, width), lambda i: (0, 0)),
                   pl.BlockSpec((1, width), lambda i: (0, 0))],
        out_shape=[jax.ShapeDtypeStruct((S, width), f32), jax.ShapeDtypeStruct((8, width), f32),
                   jax.ShapeDtypeStruct((1, width), f32)],
        compiler_params=_cp(("arbitrary",)))(dz, dz, dz, proj, proj, proj, w)


def _mlstm_step(dm, d, C, n, m, zq, zk, v, ic, fc, ir, fr, bi, bf_):
    L = zq.shape[0]
    q = _silu(zq)
    k = _silu(zk) * (dm ** -0.5)
    i_c, f_c = ic + bi, jax.nn.log_sigmoid(fc + bf_)
    i_r, f_r = ir + bi, jax.nn.log_sigmoid(fr + bf_)
    r = lax.broadcasted_iota(jnp.int32, (L, L), 0)
    c = lax.broadcasted_iota(jnp.int32, (L, L), 1)
    sgn = jnp.where(d == 0, r - c, c - r)
    mask = sgn >= 0
    b_c = jnp.sum(jnp.where(mask, f_r, 0.0), axis=-1, keepdims=True)
    b_r = jnp.sum(jnp.where(sgn <= 0, f_c, 0.0), axis=0, keepdims=True)
    log_inter = b_c + m
    logD = jnp.where(mask, b_c - b_r + i_r, -jnp.inf)
    m_t = jnp.maximum(log_inter, jnp.max(logD, axis=-1, keepdims=True))
    Dm = jnp.exp(logD - m_t)
    w_inter = jnp.exp(log_inter - m_t)
    scores = bdot_nt(q, k) * Dm
    num = bdot(scores, v) + w_inter * bdot_nt(q, C)
    den = jnp.sum(scores, axis=-1, keepdims=True) + w_inter * jnp.sum(q * n, axis=-1, keepdims=True)
    h = num / jnp.maximum(jnp.abs(den), jnp.exp(-m_t))
    bL = jnp.sum(f_c, axis=0, keepdims=True)
    log_w = bL - b_c + i_c
    m_new = jnp.maximum(bL + m, jnp.max(log_w, axis=0, keepdims=True))
    decay = jnp.exp(bL + m - m_new)
    w = jnp.exp(log_w - m_new)
    C_new = decay * C + bdot_tn(w * v, k)
    n_new = decay * n + jnp.sum(w * k, axis=0, keepdims=True)
    return C_new, n_new, m_new, h


def _mlstm_in_specs(L, dm, hm, hb, nc, step_of):
    ng = hm // hb

    def chunk(d, j):
        s = step_of(j)
        return s + d * (nc - 1 - 2 * s)
    return [
        pl.BlockSpec((L, hb * dm), lambda d, g, j: (chunk(d, j), g)),
        pl.BlockSpec((L, hb * dm), lambda d, g, j: (chunk(d, j), ng + g)),
        pl.BlockSpec((L, hb * dm), lambda d, g, j: (chunk(d, j), 2 * ng + g)),
        pl.BlockSpec((None, hb, L, 1), lambda d, g, j: (d, g, chunk(d, j), 0)),
        pl.BlockSpec((None, hb, L, 1), lambda d, g, j: (d, g, chunk(d, j), 0)),
        pl.BlockSpec((None, hb, 1, L), lambda d, g, j: (d, g, 0, chunk(d, j))),
        pl.BlockSpec((None, hb, 1, L), lambda d, g, j: (d, g, 0, chunk(d, j))),
        pl.BlockSpec((None, hb, 1, 1), lambda d, g, j: (d, g, 0, 0)),
        pl.BlockSpec((None, hb, 1, 1), lambda d, g, j: (d, g, 0, 0)),
    ], chunk


def mlstm_fwd(z, proj, gates, hm, dm, name, hb=None):
    S = z.shape[0]
    L = CHUNK
    nc = S // L
    hb = hm if hb is None else hb
    in_specs, chunk = _mlstm_in_specs(L, dm, hm, hb, nc, lambda j: j)

    def body(zq, zk, v, ic, fc, ir, fr, bi, bf_, h_ref, cs_ref, ns_ref, ms_ref, C_sc, n_sc, m_sc):
        d = pl.program_id(0)

        @pl.when(pl.program_id(2) == 0)
        def _():
            C_sc[...] = jnp.zeros_like(C_sc)
            n_sc[...] = jnp.zeros_like(n_sc)
            m_sc[...] = jnp.full_like(m_sc, M_INIT)

        for hh in range(hb):
            cols = slice(hh * dm, (hh + 1) * dm)
            C, n, m = C_sc[hh], n_sc[hh], m_sc[hh]
            cs_ref[hh], ns_ref[hh], ms_ref[hh] = C, n, m
            C2, n2, m2, h = _mlstm_step(dm, d, C, n, m, zq[:, cols], zk[:, cols], v[:, cols], ic[hh], fc[hh],
                                        ir[hh], fr[hh], bi[hh], bf_[hh])
            C_sc[hh], n_sc[hh], m_sc[hh] = C2, n2, m2
            h_ref[:, cols] = h

    return pl.pallas_call(
        body, name=name, grid=(2, hm // hb, nc), in_specs=in_specs,
        out_specs=[pl.BlockSpec((None, L, hb * dm), lambda d, g, j: (d, chunk(d, j), g)),
                   pl.BlockSpec((None, hb, None, dm, dm), lambda d, g, j: (d, g, j, 0, 0)),
                   pl.BlockSpec((None, hb, None, 1, dm), lambda d, g, j: (d, g, j, 0, 0)),
                   pl.BlockSpec((None, hb, None, 1, 1), lambda d, g, j: (d, g, j, 0, 0))],
        out_shape=[jax.ShapeDtypeStruct((2, S, hm * dm), f32), jax.ShapeDtypeStruct((2, hm, nc, dm, dm), f32),
                   jax.ShapeDtypeStruct((2, hm, nc, 1, dm), f32), jax.ShapeDtypeStruct((2, hm, nc, 1, 1), f32)],
        scratch_shapes=[pltpu.VMEM((hb, dm, dm), f32), pltpu.VMEM((hb, 1, dm), f32), pltpu.VMEM((hb, 1, 1), f32)],
        compiler_params=_cp(("arbitrary", "arbitrary", "arbitrary")))(z, z, proj, *gates)


def mlstm_bwd(z, proj, gates, states, dh, hm, dm, name, hb=None):
    S = z.shape[0]
    L = CHUNK
    nc = S // L
    hb = hm if hb is None else hb
    in_specs, chunk = _mlstm_in_specs(L, dm, hm, hb, nc, lambda j: nc - 1 - j)
    st = lambda j: nc - 1 - j
    in_specs = in_specs + [
        pl.BlockSpec((None, hb, None, dm, dm), lambda d, g, j: (d, g, st(j), 0, 0)),
        pl.BlockSpec((None, hb, None, 1, dm), lambda d, g, j: (d, g, st(j), 0, 0)),
        pl.BlockSpec((None, hb, None, 1, 1), lambda d, g, j: (d, g, st(j), 0, 0)),
        pl.BlockSpec((None, L, hb * dm), lambda d, g, j: (d, chunk(d, j), g)),
    ]

    def body(zq, zk, v, ic, fc, ir, fr, bi, bf_, cs, ns, ms, dh_ref,
             dzq, dzk, dv, dic, dfc, dir_, dfr, dbi, dbf, dC_sc, dn_sc, dm_sc):
        d = pl.program_id(0)

        @pl.when(pl.program_id(2) == 0)
        def _():
            dC_sc[...] = jnp.zeros_like(dC_sc)
            dn_sc[...] = jnp.zeros_like(dn_sc)
            dm_sc[...] = jnp.zeros_like(dm_sc)
            dbi[...] = jnp.zeros_like(dbi)
            dbf[...] = jnp.zeros_like(dbf)

        for hh in range(hb):
            cols = slice(hh * dm, (hh + 1) * dm)
            prim = (cs[hh], ns[hh], ms[hh], zq[:, cols], zk[:, cols], v[:, cols], ic[hh], fc[hh], ir[hh], fr[hh],
                    bi[hh], bf_[hh])
            _, pull = jax.vjp(functools.partial(_mlstm_step, dm, d), *prim)
            g = pull((dC_sc[hh], dn_sc[hh], dm_sc[hh], dh_ref[:, cols]))
            dC_sc[hh], dn_sc[hh], dm_sc[hh] = g[0], g[1], g[2]
            dzq[:, cols], dzk[:, cols], dv[:, cols] = g[3], g[4], g[5]
            dic[hh], dfc[hh], dir_[hh], dfr[hh] = g[6], g[7], g[8], g[9]
            dbi[hh] += g[10]
            dbf[hh] += g[11]

    tile = pl.BlockSpec((None, L, hb * dm), lambda d, g, j: (d, chunk(d, j), g))
    col = pl.BlockSpec((None, hb, L, 1), lambda d, g, j: (d, g, chunk(d, j), 0))
    row = pl.BlockSpec((None, hb, 1, L), lambda d, g, j: (d, g, 0, chunk(d, j)))
    one = pl.BlockSpec((None, hb, 1, 1), lambda d, g, j: (d, g, 0, 0))
    big = jax.ShapeDtypeStruct((2, S, hm * dm), f32)
    cols_ = jax.ShapeDtypeStruct((2, hm, S, 1), f32)
    rows_ = jax.ShapeDtypeStruct((2, hm, 1, S), f32)
    ones_ = jax.ShapeDtypeStruct((2, hm, 1, 1), f32)
    return pl.pallas_call(
        body, name=name, grid=(2, hm // hb, nc), in_specs=in_specs,
        out_specs=[tile, tile, tile, col, col, row, row, one, one],
        out_shape=[big, big, big, cols_, cols_, rows_, rows_, ones_, ones_],
        scratch_shapes=[pltpu.VMEM((hb, dm, dm), f32), pltpu.VMEM((hb, 1, dm), f32), pltpu.VMEM((hb, 1, 1), f32)],
        compiler_params=_cp(("arbitrary", "arbitrary", "arbitrary")))(z, z, proj, *gates, *states, dh)


def _blocks_to_cols(g):
    return g.transpose(1, 0, 2).reshape(g.shape[1], N_DEV * g.shape[2])


def _cols_to_blocks(a):
    return a.reshape(a.shape[0], N_DEV, a.shape[1] // N_DEV).transpose(1, 0, 2)


def _pad_cols(a, n):
    return jnp.pad(a, ((0, 0), (0, n - a.shape[1])))


def _relu2(u):
    r = jnp.maximum(u, 0.0)
    return r * r


def kernel(x, c, positions, w_ada, b_ada, norm_mix_g, w_in, b_gates, conv_w, conv_b, q_lora_g, w_uq, kv_lora_g, w_ukv, q_norm_g, k_norm_g, mlstm_norm_g, w_out, norm_mlp_g, w_ff1, w_ff2, loss_target, m_w_ada, m_b_ada, m_norm_mix_g, m_w_in, m_b_gates, m_conv_w, m_conv_b, m_q_lora_g, m_w_uq, m_kv_lora_g, m_w_ukv, m_q_norm_g, m_k_norm_g, m_mlstm_norm_g, m_w_out, m_norm_mlp_g, m_w_ff1, m_w_ff2, v_w_ada, v_b_ada, v_norm_mix_g, v_w_in, v_b_gates, v_conv_w, v_conv_b, v_q_lora_g, v_w_uq, v_kv_lora_g, v_w_ukv, v_q_norm_g, v_k_norm_g, v_mlstm_norm_g, v_w_out, v_norm_mlp_g, v_w_ff1, v_w_ff2):
    S, D = x.shape[1], x.shape[2]
    QL, KVL = w_uq.shape[1], w_ukv.shape[1]
    H = w_uq.shape[2] * N_DEV // QK_DIM
    HM = mlstm_norm_g.shape[1]
    DM = mlstm_norm_g.shape[2] * N_DEV
    MW = HM * DM
    D_IN = w_in.shape[2] * N_DEV
    NADA = w_ada.shape[2]
    assert D_IN == QL + KVL + ROPE + 4 * MW + N_GATES and DM % LANE == 0 and S % CHUNK == 0
    assert (4 * MW) % QL == 0 and (4 * MW + QL) % KVL == 0 and KVL % LANE == 0
    idx = 4 * lax.axis_index("x") + 2 * lax.axis_index("y") + lax.axis_index("c")
    x2, tgt = x[0], loss_target[0]

    g_in, g_uq, g_ukv, g_conv, g_mn, c_all = all_gather(
        [w_in[0].astype(bf16), w_uq[0].astype(bf16), w_ukv[0].astype(bf16), conv_w[0], mlstm_norm_g[0], c],
        "gather_weights")
    c_all = c_all.reshape(N_DEV, D)
    xi, yi, ci = lax.axis_index("x"), lax.axis_index("y"), lax.axis_index("c")
    slots = jnp.stack([4 * (1 - xi) + 2 * yi + ci, 4 * xi + 2 * (1 - yi) + ci, 4 * (1 - xi) + 2 * (1 - yi) + ci]).astype(jnp.int32)

    wi = _blocks_to_cols(g_in)
    o_cq, o_ckv, o_kpe, o_m, o_g = 0, QL, QL + KVL, QL + KVL + ROPE, QL + KVL + ROPE + 4 * MW
    w_in_p = jnp.concatenate([wi[:, o_m:o_g], wi[:, o_cq:o_kpe], _pad_cols(wi[:, o_kpe:o_m], LANE),
                              _pad_cols(wi[:, o_g:], LANE)], axis=1)
    NP = w_in_p.shape[1]
    cb_cq, cb_ckv, cb_kpe, cb_g = 4 * MW // QL, (4 * MW + QL) // KVL, (4 * MW + QL + KVL) // LANE, NP // LANE - 1
    w_uq_p = jnp.pad(_blocks_to_cols(g_uq).reshape(QL, H, QK_DIM), ((0, 0), (0, 0), (0, QK_PAD - QK_DIM))).reshape(QL, H * QK_PAD)
    w_ukv_p = _blocks_to_cols(g_ukv).reshape(KVL, H, 2, NOPE).transpose(0, 2, 1, 3).reshape(KVL, 2 * H * NOPE)
    conv_w_f = jnp.pad(_blocks_to_cols(g_conv), ((0, 8 - CONV_W), (0, 0)))
    mn_g = _blocks_to_cols(g_mn).reshape(1, MW)
    gqn = _pad_cols(q_norm_g, QK_PAD)
    gkn = _pad_cols(k_norm_g, QK_PAD)
    fr_np = np.zeros((1, LANE), np.float32)
    fr_np[0, :HALF] = fr_np[0, HALF:ROPE] = ROPE_THETA ** (-np.arange(HALF, dtype=np.float32) / HALF)
    freqs = jnp.asarray(fr_np)
    pos = positions.astype(f32).reshape(S, 1)

    b_blk = lax.dynamic_slice(b_ada, (0, idx * NADA), (1, NADA))
    mod_part = ada_fwd(c_all, w_ada[0], b_blk, "ada_fwd")
    (mod_all,) = all_gather([mod_part], "gather_mod")
    mod = lax.dynamic_index_in_dim(mod_all, idx, axis=1, keepdims=False).reshape(1, N_DEV * NADA)
    shift1, scale1, gate1, shift2, scale2, gate2 = [mod[:, k * D:(k + 1) * D] for k in range(6)]

    (h,) = rowwise(f_norm_mod, [Row(x2)], [norm_mix_g, shift1, scale1], [(D, bf16)], n_rows=S, tile=256, name="norm_mix")
    proj = mm(h, w_in_p, name="proj_in", out_dtype=f32)
    r_cq, r_ckv, r_kpe = Row(proj, QL, cb_cq), Row(proj, KVL, cb_ckv), Row(proj, LANE, cb_kpe)
    f_prep = make_f_mla_prep(H, QK_DIM ** -0.5 * math.log2(math.e))
    prep_params = [q_lora_g, kv_lora_g, gqn, gkn, w_uq_p, w_ukv_p, freqs]
    Q, K, V = rowwise(f_prep, [r_cq, r_ckv, r_kpe, Row(pos, diff=False)], prep_params,
                      [(H * QK_PAD, bf16), (H * QK_PAD, bf16), (H * V_DIM, bf16)], n_rows=S, tile=256, name="mla_prep")
    attn, lse, (g_out, g_ff1, g_ff2) = flash_fwd(
        Q, K, V, H, "flash_fwd", side=[w_out[0].astype(bf16), w_ff1[0].astype(bf16), w_ff2[0].astype(bf16)])
    w_out_f = g_out.reshape(N_DEV * g_out.shape[1], D)
    w_ff2_f = g_ff2.reshape(N_DEV * g_ff2.shape[1], D)

    conv_bias = conv_b
    z = conv_fwd(proj, 2 * MW, conv_w_f, conv_bias, "conv_fwd")
    graw = proj[:, cb_g * LANE:cb_g * LANE + N_GATES].reshape(S, 4, HM)
    gcol = graw.transpose(1, 2, 0).reshape(2, 2, HM, S)
    bg = b_gates.reshape(2, 2, HM)
    gates = (gcol[:, 0].reshape(2, HM, S, 1), gcol[:, 1].reshape(2, HM, S, 1),
             gcol[:, 0].reshape(2, HM, 1, S), gcol[:, 1].reshape(2, HM, 1, S),
             bg[:, 0].reshape(2, HM, 1, 1), bg[:, 1].reshape(2, HM, 1, 1))
    hdir, cs, ns, ms = mlstm_fwd(z, proj, gates, HM, DM, "mlstm_fwd")
    f_post = make_f_mlstm_post(HM, DM)
    post_rows = [Row(hdir, MW, 0, lead=0), Row(hdir, MW, 0, lead=1), Row(proj, MW, 3)]
    (ml_out,) = rowwise(f_post, post_rows, [mn_g], [(MW, bf16)], n_rows=S, tile=256, name="mlstm_post")

    cat = jnp.concatenate([attn, ml_out], axis=1)
    mixed = mm(cat, w_out_f, name="proj_out", out_dtype=f32)
    mlp_params = [gate1, norm_mlp_g, shift2, scale2]
    x1, h2 = rowwise(f_resid_norm_mod, [Row(x2), Row(mixed)], mlp_params, [(D, f32), (D, bf16)],
                     n_rows=S, tile=256, name="resid_norm_mlp")
    u = mm(h2, g_ff1, name="ff1", out_dtype=bf16)
    y = mm(u, w_ff2_f, name="ff2", a_fn=_relu2, out_dtype=f32)
    loss_l, d_out, d_y, d_gate2 = loss_head(x1, y, tgt, gate2, "loss_head")
    loss = lax.psum(loss_l[0, 0], AXES)

    dw_ff2 = mm(u, d_y, name="dw_ff2", ta=True, a_fn=_relu2, out_dtype=bf16)
    d_u = mm(d_y, w_ff2_f, name="d_u", tb=True, epi=lambda acc, uu: acc * (2.0 * jnp.maximum(uu.astype(f32), 0.0)),
             extras=(u,), out_dtype=bf16)
    dw_ff1 = mm(h2, d_u, name="dw_ff1", ta=True, out_dtype=bf16, out_blocks=True)
    d_h2 = mm(d_u, g_ff1, name="d_h2", tb=True, out_dtype=f32, tn=2048)
    (d_x1, d_mixed), (d_gate1, d_g_mlp, d_shift2, d_scale2) = rowwise_vjp(
        f_resid_norm_mod, [Row(x2), Row(mixed)], mlp_params, [Row(d_out), Row(d_h2)],
        n_rows=S, tile=256, name="resid_norm_mlp_bwd", row_grad_dtypes=[f32, bf16])
    dw_out = mm(cat, d_mixed, name="dw_out", ta=True, out_dtype=bf16)
    d_cat = mm(d_mixed, w_out_f, name="d_cat", tb=True, out_dtype=f32, tn=2048)

    (d_hf, d_hb, d_om), (d_mn_g,) = rowwise_vjp(
        f_post, post_rows, [mn_g], [Row(d_cat, MW, H * V_DIM // MW)], n_rows=S, tile=256, name="mlstm_post_bwd")
    dh = jnp.stack([d_hf, d_hb])
    dzq, dzk, dvm, dic, dfc, dir_, dfr, dbi, dbf = mlstm_bwd(z, proj, gates, (cs, ns, ms), dh, HM, DM, "mlstm_bwd")
    (dz,) = rowwise(f_add_pairs, [Row(dzq, MW, 0, lead=0), Row(dzq, MW, 0, lead=1), Row(dzk, MW, 0, lead=0),
                                  Row(dzk, MW, 0, lead=1)], [], [(2 * MW, f32)], n_rows=S, tile=256, name="dz_sum")
    (d_vm,) = rowwise(f_add, [Row(dvm, MW, 0, lead=0), Row(dvm, MW, 0, lead=1)], [], [(MW, bf16)], n_rows=S, tile=256,
                      name="dv_sum")
    d_qk, d_conv_w, d_conv_b = conv_bwd(dz, proj, 2 * MW, conv_w_f, "conv_bwd")
    dg = jnp.stack([dic.reshape(2, HM, S) + dir_.reshape(2, HM, S), dfc.reshape(2, HM, S) + dfr.reshape(2, HM, S)], axis=1)
    d_gates = dg.reshape(4 * HM, S).T
    d_b_gates = jnp.stack([dbi.reshape(2, HM), dbf.reshape(2, HM)], axis=1).reshape(1, N_GATES)

    mlp_g = [dw_out.reshape(N_DEV, -1, D), dw_ff1, dw_ff2.reshape(N_DEV, -1, D)]
    mlp_tags = ["w_out", "w_ff1", "w_ff2"]
    mlp_sib = pair_exchange(mlp_g, "grad_pair_exchange_mlp")
    mlp_part = [chip_partials(g, r, slots, "grad_chip_partials_" + t) for g, r, t in zip(mlp_g, mlp_sib, mlp_tags)]
    dq, dk, dv, mlp_chips = flash_bwd(Q, K, V, attn, lse.reshape(H, 1, S), d_cat, 0, H, "flash_bwd", side=mlp_part)
    (d_cq, d_ckv, d_kpe), (d_gq, d_gkv, d_gqn, d_gkn, dw_uq_p, dw_ukv_p) = rowwise_vjp(
        f_prep, [r_cq, r_ckv, r_kpe, Row(pos, diff=False)], prep_params, [Row(dq), Row(dk), Row(dv)],
        n_rows=S, tile=256, name="mla_prep_bwd", row_grad_dtypes=[bf16, bf16, bf16],
        param_diff=[True, True, True, True, True, True, False])

    d_proj = jnp.concatenate([d_qk.astype(bf16), d_vm, d_om.astype(bf16), d_cq, d_ckv, d_kpe,
                              _pad_cols(d_gates.astype(bf16), LANE)], axis=1)
    dw_in_p = mm(h, d_proj, name="dw_in", ta=True, out_dtype=bf16)

    dwi = jnp.concatenate([dw_in_p[:, 4 * MW:4 * MW + QL + KVL + ROPE], dw_in_p[:, :4 * MW],
                           dw_in_p[:, cb_g * LANE:cb_g * LANE + N_GATES]], axis=1)
    dw_uq = dw_uq_p.reshape(QL, H, QK_PAD)[:, :, :QK_DIM].reshape(QL, H * QK_DIM)
    dw_ukv = dw_ukv_p.reshape(KVL, 2, H, NOPE).transpose(0, 2, 1, 3).reshape(KVL, 2 * H * NOPE)
    tiny = [(w_uq, m_w_uq, v_w_uq, _cols_to_blocks(dw_uq)),
            (w_ukv, m_w_ukv, v_w_ukv, _cols_to_blocks(dw_ukv)),
            (conv_w, m_conv_w, v_conv_w, _cols_to_blocks(d_conv_w[:CONV_W])),
            (mlstm_norm_g, m_mlstm_norm_g, v_mlstm_norm_g, _cols_to_blocks(d_mn_g.reshape(HM, DM)))]
    tsizes = [int(np.prod(b[0].shape)) for b in tiny]
    T = sum(tsizes)
    PC = 512
    PR = -(-T // (PC * 64)) * 64
    gpack = jnp.concatenate([b[3].astype(bf16).reshape(N_DEV, -1) for b in tiny], axis=1)
    gpack = jnp.pad(gpack, ((0, 0), (0, PR * PC - T))).reshape(N_DEV, PR, PC)
    wpack = lambda k: jnp.pad(jnp.concatenate([b[k].reshape(1, -1) for b in tiny], axis=1),
                              ((0, 0), (0, PR * PC - T))).reshape(PR, PC)
    late_g = [_cols_to_blocks(dwi), gpack]
    late_sib = pair_exchange(late_g, "grad_pair_exchange")
    late_part = [chip_partials(g, r, slots, "grad_chip_partials_" + t) for g, r, t in zip(late_g, late_sib, ["w_in", "tiny"])]
    d_h, late_chips = mm(d_proj, w_in_p, name="d_h", tb=True, out_dtype=f32, tn=2048, side=late_part)
    (grad_x,), (d_g_mix, d_shift1, d_scale1) = rowwise_vjp(
        f_norm_mod_thru, [Row(x2)], [norm_mix_g, shift1, scale1], [Row(d_h), Row(d_x1)],
        n_rows=S, tile=256, name="norm_mix_bwd")

    dmod = jnp.concatenate([d_shift1, d_scale1, d_gate1, d_shift2, d_scale2, d_gate2], axis=1)
    small = [(norm_mix_g, m_norm_mix_g, v_norm_mix_g, d_g_mix), (b_gates, m_b_gates, v_b_gates, d_b_gates),
             (conv_b, m_conv_b, v_conv_b, d_conv_b), (q_lora_g, m_q_lora_g, v_q_lora_g, d_gq),
             (kv_lora_g, m_kv_lora_g, v_kv_lora_g, d_gkv), (q_norm_g, m_q_norm_g, v_q_norm_g, d_gqn[:, :QK_DIM]),
             (k_norm_g, m_k_norm_g, v_k_norm_g, d_gkn[:, :QK_DIM]), (norm_mlp_g, m_norm_mlp_g, v_norm_mlp_g, d_g_mlp),
             (b_ada, m_b_ada, v_b_ada, dmod)]
    sizes = [s[0].shape[1] for s in small]
    P = sum(sizes)
    PP = -(-P // LANE) * LANE
    pack = lambda k: _pad_cols(jnp.concatenate([s[k] for s in small], axis=1), PP)
    (sg_all,) = all_gather([pack(3)], "gather_small_grads")
    s_out = adamw([sg_all[k] for k in range(N_DEV)], pack(0), pack(1), pack(2), "adamw_small")
    offs = np.concatenate([[0], np.cumsum(sizes)])
    small_out = [[o[:, offs[k]:offs[k + 1]] for o in s_out] for k in range(len(small))]

    dmod_all = sg_all[:, 0, offs[-2]:offs[-1]]
    dmod_blk = lax.dynamic_slice(dmod_all, (0, idx * NADA), (N_DEV, NADA))
    g_w_ada = ada_wgrad(c_all, dmod_blk, "ada_wgrad")
    ada_out = adamw([g_w_ada], w_ada[0], m_w_ada[0], v_w_ada[0], "adamw_ada")

    large = [(w_in[0], m_w_in[0], v_w_in[0], late_g[0]),
             (w_out[0], m_w_out[0], v_w_out[0], mlp_g[0]),
             (w_ff1[0], m_w_ff1[0], v_w_ff1[0], mlp_g[1]),
             (w_ff2[0], m_w_ff2[0], v_w_ff2[0], mlp_g[2]),
             (wpack(0), wpack(1), wpack(2), gpack)]
    tags = ["w_in", "w_out", "w_ff1", "w_ff2", "tiny"]
    from_sibling = [late_sib[0]] + list(mlp_sib) + [late_sib[1]]
    from_chips = [late_chips[0]] + list(mlp_chips) + [late_chips[1]]
    l_out = []
    for (w_, m_, v_, g), r, fc, t in zip(large, from_sibling, from_chips, tags):
        mine = lax.dynamic_index_in_dim(g, idx, axis=0, keepdims=False)
        sib = lax.dynamic_index_in_dim(r, 2 * xi + yi, axis=0, keepdims=False)
        l_out.append(adamw([mine, sib, fc[0], fc[1], fc[2]], w_, m_, v_, "adamw_" + t))
    toffs = np.concatenate([[0], np.cumsum(tsizes)])
    tiny_out = [[o.reshape(-1)[toffs[k]:toffs[k + 1]].reshape(tiny[k][0].shape) for o in l_out[4]] for k in range(len(tiny))]
    big_out = [[o[None] for o in l_out[0]], tiny_out[0], tiny_out[1], [o[None] for o in l_out[1]],
               [o[None] for o in l_out[2]], [o[None] for o in l_out[3]], tiny_out[2], tiny_out[3]]

    names = ["w_ada", "b_ada", "norm_mix_g", "w_in", "b_gates", "conv_w", "conv_b", "q_lora_g", "w_uq", "kv_lora_g",
             "w_ukv", "q_norm_g", "k_norm_g", "mlstm_norm_g", "w_out", "norm_mlp_g", "w_ff1", "w_ff2"]
    res = {"w_ada": [o[None] for o in ada_out]}
    for k, nm in enumerate(["norm_mix_g", "b_gates", "conv_b", "q_lora_g", "kv_lora_g", "q_norm_g", "k_norm_g",
                            "norm_mlp_g", "b_ada"]):
        res[nm] = small_out[k]
    for k, nm in enumerate(["w_in", "w_uq", "w_ukv", "w_out", "w_ff1", "w_ff2", "conv_w", "mlstm_norm_g"]):
        res[nm] = big_out[k]
    outs = [loss, grad_x[None]]
    for part in range(4):
        outs += [res[nm][part] for nm in names]
    return tuple(outs)
```

```python
import functools
import math

import numpy as np
import jax
import jax.numpy as jnp
from jax import lax
from jax.experimental import pallas as pl
from jax.experimental.pallas import tpu as pltpu

f32 = jnp.float32
bf16 = jnp.bfloat16

N_DEV = 8
AXES = ("x", "y", "c")
MESH = pl.DeviceIdType.MESH

NOPE = 128
ROPE = 64
HALF = ROPE // 2
QK_DIM = NOPE + ROPE
QK_PAD = 256
V_DIM = 128
ROPE_THETA = 10000.0
CHUNK = 128
CONV_W = 5
N_GATES = 16
EPS = 1e-6
M_INIT = -1e30

ADAM_LR, ADAM_B1, ADAM_B2, ADAM_EPS, ADAM_WD, ADAM_STEP = 0.001, 0.9, 0.999, 1e-08, 0.01, 10

LANE = 128
VMEM_LIMIT = 56 * 1024 * 1024


def _cp(sem=None, vmem=VMEM_LIMIT):
    return pltpu.CompilerParams(dimension_semantics=sem, vmem_limit_bytes=vmem)


def _pick(n, target):
    best = None
    t = LANE
    while t <= min(n, target):
        if n % t == 0:
            best = t
        t += LANE
    return best if best is not None else n


def _pick_rows(n, target):
    t = min(n, target)
    while n % t:
        t -= 8
    return t


def _make_dots(cast, precision):
    def dg(a, b, ca, cb):
        if cast is not None:
            a = a.astype(cast)
            b = b.astype(cast)
        return lax.dot_general(a, b, (((ca,), (cb,)), ((), ())), precision=precision, preferred_element_type=f32)

    @jax.custom_vjp
    def nn(a, b):
        return dg(a, b, 1, 0)

    def nn_f(a, b):
        return dg(a, b, 1, 0), (a, b)

    def nn_b(res, g):
        a, b = res
        return dg(g, b, 1, 1).astype(a.dtype), dg(a, g, 0, 0).astype(b.dtype)

    nn.defvjp(nn_f, nn_b)

    @jax.custom_vjp
    def nt(a, b):
        return dg(a, b, 1, 1)

    def nt_f(a, b):
        return dg(a, b, 1, 1), (a, b)

    def nt_b(res, g):
        a, b = res
        return dg(g, b, 1, 0).astype(a.dtype), dg(g, a, 0, 0).astype(b.dtype)

    nt.defvjp(nt_f, nt_b)

    @jax.custom_vjp
    def tn(a, b):
        return dg(a, b, 0, 0)

    def tn_f(a, b):
        return dg(a, b, 0, 0), (a, b)

    def tn_b(res, g):
        a, b = res
        return dg(b, g, 1, 1).astype(a.dtype), dg(a, g, 1, 0).astype(b.dtype)

    tn.defvjp(tn_f, tn_b)
    return nn, nt, tn


bdot, bdot_nt, bdot_tn = _make_dots(bf16, None)
hdot, hdot_nt, hdot_tn = _make_dots(None, lax.Precision.HIGHEST)


def _silu(x):
    return x * jax.nn.sigmoid(x)


def _rms(x, n):
    return x * lax.rsqrt(jnp.sum(x * x, axis=-1, keepdims=True) * (1.0 / n) + EPS)


def _place():
    return lax.axis_index("x"), lax.axis_index("y"), lax.axis_index("c")


def _gather_phases(ins, outs, send_sems, recv_sems, local_sems):
    n = len(ins)
    x, y, c = _place()
    me, sibling = (x, y, c), (x, y, 1 - c)
    chips = [(1 - x, y), (x, 1 - y), (1 - x, 1 - y)]

    def slot(o, p):
        return outs[o].at[4 * p[0] + 2 * p[1] + p[2]]

    def copy(o, k, block, to, src=None):
        dst = slot(o, block)
        return pltpu.make_async_remote_copy(
            src_ref=dst if src is None else src, dst_ref=dst,
            send_sem=send_sems.at[o, k], recv_sem=recv_sems.at[o, k],
            device_id=to, device_id_type=MESH)

    def local(o):
        return pltpu.make_async_copy(ins[o], slot(o, me), local_sems.at[o])

    def first(o):
        return [copy(o, 0, me, sibling, src=ins[o])] + [copy(o, 1 + j, me, (*chip, c), src=ins[o])
                                                        for j, chip in enumerate(chips)]

    def start():
        for o in range(n):
            local(o).start()
        for o in range(n):
            for cp in first(o):
                cp.start()

    def mid():
        for o in range(n):
            for j, chip in enumerate(chips):
                copy(o, 1 + j, (*chip, c), me).wait_recv()
                copy(o, 4 + j, (*chip, c), sibling).start()

    def finish():
        for o in range(n):
            copy(o, 0, sibling, me).wait_recv()
            for j, chip in enumerate(chips):
                copy(o, 4 + j, (*chip, 1 - c), me).wait_recv()
        for o in range(n):
            for cp in first(o):
                cp.wait_send()
            for j, chip in enumerate(chips):
                copy(o, 4 + j, (*chip, c), sibling).wait_send()
        for o in range(n):
            local(o).wait()

    return start, mid, finish


def _gather_scratch(n):
    return [pltpu.SemaphoreType.DMA((n, 7)), pltpu.SemaphoreType.DMA((n, 7)), pltpu.SemaphoreType.DMA((n,))]


def all_gather(ops, name):
    n = len(ops)

    def body(*refs):
        start, mid, finish = _gather_phases(refs[:n], refs[n:2 * n], *refs[2 * n:])
        start()
        mid()
        finish()

    anyspec = pl.BlockSpec(memory_space=pl.ANY)
    return pl.pallas_call(
        body, name=name,
        out_shape=[jax.ShapeDtypeStruct((N_DEV,) + o.shape, o.dtype) for o in ops],
        in_specs=[anyspec] * n, out_specs=[anyspec] * n,
        scratch_shapes=_gather_scratch(n),
    )(*ops)


def pair_exchange(gs, name):
    n = len(gs)

    def body(*refs):
        g_refs, out_refs = refs[:n], refs[n:2 * n]
        send_sems, recv_sems = refs[2 * n:]
        x, y, c = _place()
        sibling = (x, y, 1 - c)
        cps = []
        for o in range(n):
            for q in range(4):
                cp = pltpu.make_async_remote_copy(
                    src_ref=g_refs[o].at[2 * q + (1 - c)], dst_ref=out_refs[o].at[q],
                    send_sem=send_sems.at[o, q], recv_sem=recv_sems.at[o, q],
                    device_id=sibling, device_id_type=MESH)
                cp.start()
                cps.append(cp)
        for cp in cps:
            cp.wait_recv()
        for cp in cps:
            cp.wait_send()

    anyspec = pl.BlockSpec(memory_space=pl.ANY)
    return pl.pallas_call(
        body, name=name, out_shape=[jax.ShapeDtypeStruct((4,) + g.shape[1:], g.dtype) for g in gs],
        in_specs=[anyspec] * n, out_specs=[anyspec] * n,
        scratch_shapes=[pltpu.SemaphoreType.DMA((n, 4)), pltpu.SemaphoreType.DMA((n, 4))],
    )(*gs)


def _chip_exchange_phases(p_refs, out_refs, send_sems, recv_sems):
    n = len(p_refs)
    x, y, c = _place()
    chips = [(1 - x, y), (x, 1 - y), (1 - x, 1 - y)]

    def copies():
        return [pltpu.make_async_remote_copy(
            src_ref=p_refs[o].at[j], dst_ref=out_refs[o].at[j],
            send_sem=send_sems.at[o, j], recv_sem=recv_sems.at[o, j],
            device_id=(*chip, c), device_id_type=MESH) for o in range(n) for j, chip in enumerate(chips)]

    def start():
        for cp in copies():
            cp.start()

    def finish():
        for cp in copies():
            cp.wait_recv()
        for cp in copies():
            cp.wait_send()

    return start, finish


def _chip_exchange_scratch(n):
    return [pltpu.SemaphoreType.DMA((n, 3)), pltpu.SemaphoreType.DMA((n, 3))]


def chip_partials(g, recv, slots, name):
    _, R, C = g.shape
    tr = _pick_rows(R, 512)

    def body(s_ref, a_ref, b_ref, o_ref):
        o_ref[...] = (a_ref[...].astype(f32) + b_ref[...].astype(f32)).astype(o_ref.dtype)

    grid_spec = pltpu.PrefetchScalarGridSpec(
        num_scalar_prefetch=1, grid=(3, R // tr),
        in_specs=[pl.BlockSpec((None, tr, C), lambda j, i, s: (s[j], i, 0)),
                  pl.BlockSpec((None, tr, C), lambda j, i, s: (s[j] // 2, i, 0))],
        out_specs=pl.BlockSpec((None, tr, C), lambda j, i, s: (j, i, 0)))
    return pl.pallas_call(body, name=name, grid_spec=grid_spec,
                          out_shape=jax.ShapeDtypeStruct((3, R, C), g.dtype),
                          compiler_params=_cp(("arbitrary", "arbitrary")))(slots, g, recv)


def adamw(parts, w, m, v, name, rows=256):
    R, C = w.shape
    tr = _pick_rows(R, rows)
    npart = len(parts)
    c1 = 1.0 - ADAM_B1 ** ADAM_STEP
    c2 = 1.0 - ADAM_B2 ** ADAM_STEP

    def body(*refs):
        p_refs = refs[:npart]
        w_ref, m_ref, v_ref, g_out, d_out, m_out, v_out = refs[npart:]
        g = p_refs[0][...].astype(f32)
        for p in p_refs[1:]:
            g = g + p[...].astype(f32)
        mn = ADAM_B1 * m_ref[...] + (1.0 - ADAM_B1) * g
        vn = ADAM_B2 * v_ref[...] + (1.0 - ADAM_B2) * (g * g)
        m_hat = mn / c1
        v_hat = vn / c2
        g_out[...] = g
        d_out[...] = -ADAM_LR * (m_hat / (jnp.sqrt(v_hat) + ADAM_EPS) + ADAM_WD * w_ref[...])
        m_out[...] = mn
        v_out[...] = vn

    spec = pl.BlockSpec((tr, C), lambda i: (i, 0))
    return pl.pallas_call(
        body, name=name, grid=(R // tr,),
        in_specs=[spec] * (npart + 3), out_specs=[spec] * 4,
        out_shape=[jax.ShapeDtypeStruct((R, C), f32)] * 4,
        compiler_params=_cp(("arbitrary",)))(*parts, w, m, v)


def mm(a, b, *, name, ta=False, tb=False, a_fn=None, epi=None, extras=(), out_dtype=f32, out_blocks=False, side=(),
       tm=1024, tn=1024, tk=2048):
    K, M = a.shape if ta else a.shape[::-1]
    b3 = b.ndim == 3
    if b3:
        N, K2 = (b.shape[1], N_DEV * b.shape[2]) if tb else (N_DEV * b.shape[2], b.shape[1])
    else:
        N, K2 = b.shape if tb else b.shape[::-1]
    assert K == K2, (a.shape, b.shape, ta, tb)
    n_split = N // N_DEV if (out_blocks or (b3 and not tb)) else N
    k_split = K // N_DEV if (b3 and tb) else K
    tm, tn, tk = _pick(M, tm), _pick(n_split, tn), _pick(k_split, tk)
    nb, kb = n_split // tn, k_split // tk
    nk = K // tk
    ne = len(extras)
    assert not (out_blocks and ne)
    dims = (((0 if ta else 1,), (1 if tb else 0,)), ((), ()))

    ns = len(side)
    n_steps = (M // tm) * (N // tn) * nk
    assert ns == 0 or n_steps >= 2

    def body(a_ref, b_ref, *rest):
        e_refs, o_ref, acc = rest[:ne], rest[ne + ns], rest[ne + 2 * ns + 1]
        k = pl.program_id(2)
        if ns:
            step = (pl.program_id(0) * (N // tn) + pl.program_id(1)) * nk + k
            x_start, x_finish = _chip_exchange_phases(rest[ne:ne + ns], rest[ne + ns + 1:ne + 2 * ns + 1],
                                                      *rest[ne + 2 * ns + 2:])
            pl.when(step == 0)(x_start)

        @pl.when(k == 0)
        def _():
            acc[...] = jnp.zeros_like(acc)

        av = a_ref[...]
        if a_fn is not None:
            av = a_fn(av.astype(f32))
        acc[...] += lax.dot_general(av.astype(bf16), b_ref[...].astype(bf16), dims, preferred_element_type=f32)

        @pl.when(k == nk - 1)
        def _():
            r = acc[...]
            if epi is not None:
                r = epi(r, *[e[...] for e in e_refs])
            o_ref[...] = r.astype(o_ref.dtype)

        if ns:
            pl.when(step == n_steps - 1)(x_finish)

    a_spec = pl.BlockSpec((tk, tm), lambda i, j, k: (k, i)) if ta else pl.BlockSpec((tm, tk), lambda i, j, k: (i, k))
    if b3 and tb:
        b_spec = pl.BlockSpec((None, tn, tk), lambda i, j, k: (k // kb, j, k % kb))
    elif b3:
        b_spec = pl.BlockSpec((None, tk, tn), lambda i, j, k: (j // nb, k, j % nb))
    else:
        b_spec = pl.BlockSpec((tn, tk), lambda i, j, k: (j, k)) if tb else pl.BlockSpec((tk, tn), lambda i, j, k: (k, j))
    if out_blocks:
        o_spec = pl.BlockSpec((None, tm, tn), lambda i, j, k: (j // nb, i, j % nb))
        o_shape = jax.ShapeDtypeStruct((N_DEV, M, N // N_DEV), out_dtype)
    else:
        o_spec = pl.BlockSpec((tm, tn), lambda i, j, k: (i, j))
        o_shape = jax.ShapeDtypeStruct((M, N), out_dtype)
    if not ns:
        return pl.pallas_call(
            body, name=name, grid=(M // tm, N // tn, nk),
            in_specs=[a_spec, b_spec] + [o_spec] * ne, out_specs=o_spec,
            out_shape=o_shape,
            scratch_shapes=[pltpu.VMEM((tm, tn), f32)],
            compiler_params=_cp(("parallel", "parallel", "arbitrary")))(a, b, *extras)
    anyspec = pl.BlockSpec(memory_space=pl.ANY)
    res = pl.pallas_call(
        body, name=name, grid=(M // tm, N // tn, nk),
        in_specs=[a_spec, b_spec] + [o_spec] * ne + [anyspec] * ns, out_specs=[o_spec] + [anyspec] * ns,
        out_shape=[o_shape] + [jax.ShapeDtypeStruct(p.shape, p.dtype) for p in side],
        scratch_shapes=[pltpu.VMEM((tm, tn), f32)] + _chip_exchange_scratch(ns),
        compiler_params=_cp(("arbitrary", "arbitrary", "arbitrary")))(a, b, *extras, *side)
    return res[0], list(res[1:])


class Row:
    def __init__(self, arr, width=None, col=0, lead=None, diff=True):
        self.arr, self.col, self.lead, self.diff = arr, col, lead, diff
        self.width = arr.shape[-1] if width is None else width

    def spec(self, t):
        col, lead = self.col, self.lead
        if lead is None:
            return pl.BlockSpec((t, self.width), lambda i: (i, col))
        return pl.BlockSpec((None, t, self.width), lambda i: (lead, i, col))


def _whole(p):
    return pl.BlockSpec(p.shape, lambda i: (0,) * p.ndim)


def rowwise(fn, rows, params, outs, *, n_rows, tile, name):
    t = _pick_rows(n_rows, tile)
    nr, npar, no = len(rows), len(params), len(outs)

    def body(*refs):
        r_refs, p_refs, o_refs = refs[:nr], refs[nr:nr + npar], refs[nr + npar:]
        res = fn(*[r[...].astype(f32) for r in r_refs], *[p[...] for p in p_refs])
        for o_ref, val in zip(o_refs, res):
            o_ref[...] = val.astype(o_ref.dtype)

    return pl.pallas_call(
        body, name=name, grid=(n_rows // t,),
        in_specs=[r.spec(t) for r in rows] + [_whole(p) for p in params],
        out_specs=[pl.BlockSpec((t, w), lambda i: (i, 0)) for w, _ in outs],
        out_shape=[jax.ShapeDtypeStruct((n_rows, w), dt) for w, dt in outs],
        compiler_params=_cp(("arbitrary",)))(*[r.arr for r in rows], *params)


def rowwise_vjp(fn, rows, params, cts, *, n_rows, tile, name, row_grad_dtypes=None, param_diff=None):
    t = _pick_rows(n_rows, tile)
    nr, npar, nc = len(rows), len(params), len(cts)
    param_diff = [True] * npar if param_diff is None else param_diff
    d_rows = [k for k, r in enumerate(rows) if r.diff]
    d_pars = [k for k in range(npar) if param_diff[k]]
    row_grad_dtypes = [f32] * len(d_rows) if row_grad_dtypes is None else row_grad_dtypes

    def body(*refs):
        r_refs, p_refs = refs[:nr], refs[nr:nr + npar]
        c_refs = refs[nr + npar:nr + npar + nc]
        dr_refs = refs[nr + npar + nc:nr + npar + nc + len(d_rows)]
        dp_refs = refs[nr + npar + nc + len(d_rows):]
        rv = [r[...].astype(f32) for r in r_refs]
        pv = [p[...] for p in p_refs]

        def g(*dvals):
            full_r, full_p = list(rv), list(pv)
            for k, val in zip(d_rows, dvals[:len(d_rows)]):
                full_r[k] = val
            for k, val in zip(d_pars, dvals[len(d_rows):]):
                full_p[k] = val
            return tuple(fn(*full_r, *full_p))

        prim = [rv[k] for k in d_rows] + [pv[k].astype(f32) for k in d_pars]
        _, pull = jax.vjp(g, *prim)
        grads = pull(tuple(c[...].astype(f32) for c in c_refs))
        for ref, val in zip(dr_refs, grads[:len(d_rows)]):
            ref[...] = val.astype(ref.dtype)

        @pl.when(pl.program_id(0) == 0)
        def _():
            for ref in dp_refs:
                ref[...] = jnp.zeros_like(ref)

        for ref, val in zip(dp_refs, grads[len(d_rows):]):
            ref[...] += val

    out_specs = [pl.BlockSpec((t, rows[k].width), lambda i: (i, 0)) for k in d_rows]
    out_specs += [_whole(params[k]) for k in d_pars]
    out_shape = [jax.ShapeDtypeStruct((n_rows, rows[k].width), dt) for k, dt in zip(d_rows, row_grad_dtypes)]
    out_shape += [jax.ShapeDtypeStruct(params[k].shape, f32) for k in d_pars]
    res = pl.pallas_call(
        body, name=name, grid=(n_rows // t,),
        in_specs=[r.spec(t) for r in rows] + [_whole(p) for p in params] + [c.spec(t) for c in cts],
        out_specs=out_specs, out_shape=out_shape,
        compiler_params=_cp(("arbitrary",)))(*[r.arr for r in rows], *params, *[c.arr for c in cts])
    return res[:len(d_rows)], res[len(d_rows):]


def f_norm_mod(x, g, shift, scale):
    return (_rms(x, x.shape[-1]) * g * (1.0 + scale) + shift,)


def f_norm_mod_thru(x, g, shift, scale):
    return f_norm_mod(x, g, shift, scale) + (x,)


def f_resid_norm_mod(x, mixed, gate1, g2, shift2, scale2):
    x1 = x + gate1 * mixed
    return (x1,) + f_norm_mod(x1, g2, shift2, scale2)


def _rope_rot():
    i = lax.broadcasted_iota(jnp.int32, (LANE, LANE), 0)
    j = lax.broadcasted_iota(jnp.int32, (LANE, LANE), 1)
    neg = jnp.where((i == j + HALF) & (j < HALF), -1.0, 0.0)
    pos = jnp.where((i == j - HALF) & (j >= HALF) & (j < ROPE), 1.0, 0.0)
    return (neg + pos).astype(f32)


def make_f_mla_prep(n_heads, q_scale):
    def fn(cq, ckv, kpe, pos, gq, gkv, gqn, gkn, w_uq, w_ukv, freqs):
        rot = _rope_rot()
        ang = pos * freqs
        cos, sin = jnp.cos(ang), jnp.sin(ang)

        def rope(u):
            return u * cos + hdot(u, rot) * sin

        qraw = bdot(_rms(cq, cq.shape[-1]) * gq, w_uq)
        kv = bdot(_rms(ckv, ckv.shape[-1]) * gkv, w_ukv)
        kpe_ss = jnp.sum(kpe * kpe, axis=-1, keepdims=True)
        qs, ks = [], []
        for h in range(n_heads):
            qh = _rms(qraw[:, h * QK_PAD:(h + 1) * QK_PAD], QK_DIM) * gqn
            qs += [qh[:, :NOPE], rope(qh[:, NOPE:])]
            kn = kv[:, h * NOPE:(h + 1) * NOPE]
            r = lax.rsqrt((jnp.sum(kn * kn, axis=-1, keepdims=True) + kpe_ss) * (1.0 / QK_DIM) + EPS)
            ks += [kn * r * gkn[:, :NOPE], rope(kpe * r * gkn[:, NOPE:])]
        return jnp.concatenate(qs, axis=-1) * q_scale, jnp.concatenate(ks, axis=-1), kv[:, n_heads * NOPE:]
    return fn


def make_f_mlstm_post(n_heads, dm):
    def fn(hf, hb, o, g):
        hm = hf + hb
        outs = []
        for h in range(n_heads):
            sl = slice(h * dm, (h + 1) * dm)
            outs.append(jax.nn.sigmoid(o[:, sl]) * (_rms(hm[:, sl], dm) * g[:, sl]))
        return (jnp.concatenate(outs, axis=-1),)
    return fn


def f_add_pairs(a0, a1, b0, b1):
    return (jnp.concatenate([a0 + a1, b0 + b1], axis=-1),)


def f_add(a, b):
    return (a + b,)


def loss_head(x1, y, target, gate2, name, tile=256):
    S, D = x1.shape
    t = _pick_rows(S, tile)

    def body(x1_ref, y_ref, t_ref, g_ref, loss_ref, dout_ref, dy_ref, dgate_ref):
        @pl.when(pl.program_id(0) == 0)
        def _():
            loss_ref[...] = jnp.zeros_like(loss_ref)
            dgate_ref[...] = jnp.zeros_like(dgate_ref)

        yv, gv = y_ref[...], g_ref[...]
        e = x1_ref[...] + gv * yv - t_ref[...]
        loss_ref[...] += 0.5 * jnp.sum(jnp.sum(e * e, axis=-1, keepdims=True) * (1.0 / D), axis=0, keepdims=True)
        d_out = e * (1.0 / D)
        dout_ref[...] = d_out
        dy_ref[...] = (d_out * gv).astype(dy_ref.dtype)
        dgate_ref[...] += jnp.sum(d_out * yv, axis=0, keepdims=True)

    row = pl.BlockSpec((t, D), lambda i: (i, 0))
    return pl.pallas_call(
        body, name=name, grid=(S // t,),
        in_specs=[row, row, row, pl.BlockSpec((1, D), lambda i: (0, 0))],
        out_specs=[pl.BlockSpec((1, 1), lambda i: (0, 0)), row, row, pl.BlockSpec((1, D), lambda i: (0, 0))],
        out_shape=[jax.ShapeDtypeStruct((1, 1), f32), jax.ShapeDtypeStruct((S, D), f32),
                   jax.ShapeDtypeStruct((S, D), bf16), jax.ShapeDtypeStruct((1, D), f32)],
        compiler_params=_cp(("arbitrary",)))(x1, y, target, gate2)


def ada_fwd(c_all, w_blk, b_blk, name):
    B, D = c_all.shape
    N = w_blk.shape[1]
    tn = _pick(N, 512)

    def body(c_ref, w_ref, b_ref, o_ref):
        o_ref[...] = bdot(_silu(c_ref[...]), w_ref[...]) + b_ref[...]

    return pl.pallas_call(
        body, name=name, grid=(N // tn,),
        in_specs=[pl.BlockSpec((B, D), lambda j: (0, 0)), pl.BlockSpec((D, tn), lambda j: (0, j)),
                  pl.BlockSpec((1, tn), lambda j: (0, j))],
        out_specs=pl.BlockSpec((B, tn), lambda j: (0, j)),
        out_shape=jax.ShapeDtypeStruct((B, N), f32), compiler_params=_cp(("arbitrary",)))(c_all, w_blk, b_blk)


def ada_wgrad(c_all, dmod_blk, name):
    B, D = c_all.shape
    N = dmod_blk.shape[1]
    tn = _pick(N, 512)

    def body(c_ref, d_ref, o_ref):
        o_ref[...] = hdot_tn(_silu(c_ref[...]), d_ref[...])

    return pl.pallas_call(
        body, name=name, grid=(N // tn,),
        in_specs=[pl.BlockSpec((B, D), lambda j: (0, 0)), pl.BlockSpec((B, tn), lambda j: (0, j))],
        out_specs=pl.BlockSpec((D, tn), lambda j: (0, j)),
        out_shape=jax.ShapeDtypeStruct((D, N), f32), compiler_params=_cp(("arbitrary",)))(c_all, dmod_blk)


def _nt(a, b):
    return lax.dot_general(a, b, (((1,), (1,)), ((), ())), preferred_element_type=f32)


def _tn(a, b):
    return lax.dot_general(a, b, (((0,), (0,)), ((), ())), preferred_element_type=f32)


def flash_fwd(q, k, v, n_heads, name, side=(), tq=512, tk=8192, sub=1024):
    S = q.shape[0]
    tq, tk = _pick(S, tq), _pick(S, tk)
    sub = _pick(tk, sub)
    nk, nsub = S // tk, tk // sub
    ns = len(side)
    n_steps = n_heads * (S // tq) * nk
    assert ns == 0 or n_steps >= 3

    def body(*refs):
        q_ref, k_ref, v_ref = refs[:3]
        o_ref, lse_ref = refs[3 + ns:5 + ns]
        m_sc, l_sc, acc_sc = refs[5 + 2 * ns:8 + 2 * ns]
        j = pl.program_id(2)
        step = (pl.program_id(0) * (S // tq) + pl.program_id(1)) * nk + j
        if ns:
            g_start, g_mid, g_finish = _gather_phases(refs[3:3 + ns], refs[5 + ns:5 + 2 * ns], *refs[8 + 2 * ns:])
            pl.when(step == 0)(g_start)
            pl.when(step == n_steps // 2)(g_mid)

        @pl.when(j == 0)
        def _():
            m_sc[...] = jnp.full_like(m_sc, -jnp.inf)
            l_sc[...] = jnp.zeros_like(l_sc)
            acc_sc[...] = jnp.zeros_like(acc_sc)

        qv = q_ref[...]
        m = m_sc[...]
        ss = [_nt(qv, k_ref[b * sub:(b + 1) * sub, :]) for b in range(nsub)]
        mx = ss[0]
        for s in ss[1:]:
            mx = jnp.maximum(mx, s)
        m_new = jnp.maximum(m, jnp.max(mx, axis=-1, keepdims=True))
        alpha = jnp.exp2(m - m_new)
        psum, pv = None, None
        for b in range(nsub):
            p = jnp.exp2(ss[b] - m_new)
            d = jnp.dot(p.astype(bf16), v_ref[b * sub:(b + 1) * sub, :], preferred_element_type=f32)
            psum = p if psum is None else psum + p
            pv = d if pv is None else pv + d
        m, l, acc = m_new, alpha * l_sc[...] + jnp.sum(psum, axis=-1, keepdims=True), alpha * acc_sc[...] + pv
        m_sc[...], l_sc[...], acc_sc[...] = m, l, acc

        @pl.when(j == nk - 1)
        def _():
            o_ref[...] = (acc / l).astype(o_ref.dtype)
            lse_ref[...] = m + jnp.log2(l)

        if ns:
            pl.when(step == n_steps - 1)(g_finish)

    anyspec = pl.BlockSpec(memory_space=pl.ANY)
    res = pl.pallas_call(
        body, name=name, grid=(n_heads, S // tq, nk),
        in_specs=[pl.BlockSpec((tq, QK_PAD), lambda h, i, j: (i, h)),
                  pl.BlockSpec((tk, QK_PAD), lambda h, i, j: (j, h)),
                  pl.BlockSpec((tk, V_DIM), lambda h, i, j: (j, h))] + [anyspec] * ns,
        out_specs=[pl.BlockSpec((tq, V_DIM), lambda h, i, j: (i, h)),
                   pl.BlockSpec((None, tq, 1), lambda h, i, j: (h, i, 0))] + [anyspec] * ns,
        out_shape=[jax.ShapeDtypeStruct((S, n_heads * V_DIM), bf16), jax.ShapeDtypeStruct((n_heads, S, 1), f32)]
        + [jax.ShapeDtypeStruct((N_DEV,) + a.shape, a.dtype) for a in side],
        scratch_shapes=[pltpu.VMEM((tq, 1), f32), pltpu.VMEM((tq, 1), f32), pltpu.VMEM((tq, V_DIM), f32)]
        + (_gather_scratch(ns) if ns else []),
        compiler_params=_cp(("arbitrary", "arbitrary", "arbitrary")))(q, k, v, *side)
    return res[0], res[1], list(res[2:])


def flash_bwd(q, k, v, o, lse_row, do, do_col0, n_heads, name, side=(), tq=512, tk=8192, sub=512):
    S = q.shape[0]
    tq, tk = _pick(S, tq), _pick(S, tk)
    sub = _pick(tk, sub)
    nsub = tk // sub
    ln2 = math.log(2.0)
    ns = len(side)
    n_steps = n_heads * (S // tq) * (S // tk)
    assert ns == 0 or n_steps >= 2

    def body(*refs):
        q_ref, k_ref, v_ref, o_ref, lse_ref, do_ref = refs[:6]
        dq_ref, dk_ref, dv_ref = refs[6 + ns:9 + ns]
        i, j = pl.program_id(1), pl.program_id(2)
        step = (pl.program_id(0) * (S // tq) + i) * (S // tk) + j
        if ns:
            x_start, x_finish = _chip_exchange_phases(refs[6:6 + ns], refs[9 + ns:9 + 2 * ns], *refs[9 + 2 * ns:])
            pl.when(step == 0)(x_start)

        @pl.when(j == 0)
        def _():
            dq_ref[...] = jnp.zeros_like(dq_ref)

        @pl.when((i == 0) & (j == 0))
        def _():
            dk_ref[...] = jnp.zeros_like(dk_ref)
            dv_ref[...] = jnp.zeros_like(dv_ref)

        qv = q_ref[...]
        dof = do_ref[...].astype(f32)
        do_b = dof.astype(bf16)
        do_s = (dof * ln2).astype(bf16)
        delta = hdot_nt(jnp.ones((8, V_DIM), f32), dof * ln2 * o_ref[...].astype(f32))[0:1, :]
        lse = lse_ref[...]
        dq = jnp.zeros((tq, QK_PAD), f32)
        for b in range(nsub):
            kb = k_ref[b * sub:(b + 1) * sub, :]
            rows = pl.ds(pl.multiple_of(j * tk + b * sub, sub), sub)
            pt = jnp.exp2(_nt(kb, qv) - lse)
            dpt = _nt(v_ref[b * sub:(b + 1) * sub, :], do_s)
            dst = (pt * (dpt - delta)).astype(bf16)
            dv_ref[rows, :] += jnp.dot(pt.astype(bf16), do_b, preferred_element_type=f32)
            dk_ref[rows, :] += jnp.dot(dst, qv, preferred_element_type=f32)
            dq = dq + _tn(dst, kb)
        dq_ref[...] += dq
        if ns:
            pl.when(step == n_steps - 1)(x_finish)

    anyspec = pl.BlockSpec(memory_space=pl.ANY)
    res = pl.pallas_call(
        body, name=name, grid=(n_heads, S // tq, S // tk),
        in_specs=[pl.BlockSpec((tq, QK_PAD), lambda h, i, j: (i, h)),
                  pl.BlockSpec((tk, QK_PAD), lambda h, i, j: (j, h)),
                  pl.BlockSpec((tk, V_DIM), lambda h, i, j: (j, h)),
                  pl.BlockSpec((tq, V_DIM), lambda h, i, j: (i, h)),
                  pl.BlockSpec((None, 1, tq), lambda h, i, j: (h, 0, i)),
                  pl.BlockSpec((tq, V_DIM), lambda h, i, j: (i, do_col0 + h))] + [anyspec] * ns,
        out_specs=[pl.BlockSpec((tq, QK_PAD), lambda h, i, j: (i, h)),
                   pl.BlockSpec((S, QK_PAD), lambda h, i, j: (0, h)),
                   pl.BlockSpec((S, V_DIM), lambda h, i, j: (0, h))] + [anyspec] * ns,
        out_shape=[jax.ShapeDtypeStruct((S, n_heads * QK_PAD), f32), jax.ShapeDtypeStruct((S, n_heads * QK_PAD), f32),
                   jax.ShapeDtypeStruct((S, n_heads * V_DIM), f32)] + [jax.ShapeDtypeStruct(a.shape, a.dtype) for a in side],
        scratch_shapes=_chip_exchange_scratch(ns) if ns else [],
        compiler_params=_cp(("arbitrary", "arbitrary", "arbitrary")))(q, k, v, o, lse_row, do, *side)
    return res[0], res[1], res[2], list(res[3:])


def _shifted(prev, cur, nxt, k, first, last):
    if k == 0:
        return cur
    t = cur.shape[0]
    r = lax.broadcasted_iota(jnp.int32, (HALO,) + cur.shape[1:], 0)
    if k < 0:
        body = pltpu.roll(cur, -k, 0)
        edge = jnp.where(first, 0.0, pltpu.roll(prev, -k, 0))
        return jnp.concatenate([jnp.where(r < -k, edge, body[:HALO]), body[HALO:]], axis=0)
    body = pltpu.roll(cur, t - k, 0)
    edge = jnp.where(last, 0.0, pltpu.roll(nxt, HALO - k, 0))
    return jnp.concatenate([body[:t - HALO], jnp.where(r >= HALO - k, edge, body[t - HALO:])], axis=0)


HALO = 8


def _halo_specs(t, width, n_tiles, col=0):
    per = t // HALO
    return [pl.BlockSpec((HALO, width), lambda i: (jnp.maximum(i * per - 1, 0), col)),
            pl.BlockSpec((t, width), lambda i: (i, col)),
            pl.BlockSpec((HALO, width), lambda i: (jnp.minimum((i + 1) * per, n_tiles * per - 1), col))]


def conv_fwd(proj, width, w, b, name, tile=256):
    S = proj.shape[0]
    t = _pick_rows(S, tile)
    n_tiles = S // t

    def body(p_ref, c_ref, n_ref, w_ref, b_ref, z_ref):
        i = pl.program_id(0)
        first, last = i == 0, i == n_tiles - 1
        prev, cur, nxt = p_ref[...], c_ref[...], n_ref[...]
        z = b_ref[...] + jnp.zeros_like(cur)
        for j in range(CONV_W):
            z = z + w_ref[j:j + 1, :] * _shifted(prev, cur, nxt, j - CONV_W // 2, first, last)
        z_ref[...] = z

    return pl.pallas_call(
        body, name=name, grid=(n_tiles,),
        in_specs=_halo_specs(t, width, n_tiles) + [_whole(w), _whole(b)],
        out_specs=pl.BlockSpec((t, width), lambda i: (i, 0)),
        out_shape=jax.ShapeDtypeStruct((S, width), f32),
        compiler_params=_cp(("arbitrary",)))(proj, proj, proj, w, b)


def conv_bwd(dz, proj, width, w, name, tile=256):
    S = proj.shape[0]
    t = _pick_rows(S, tile)
    n_tiles = S // t

    def body(dp_ref, dc_ref, dn_ref, up_ref, uc_ref, un_ref, w_ref, du_ref, dw_ref, db_ref):
        i = pl.program_id(0)
        first, last = i == 0, i == n_tiles - 1

        @pl.when(first)
        def _():
            dw_ref[...] = jnp.zeros_like(dw_ref)
            db_ref[...] = jnp.zeros_like(db_ref)

        dprev, dcur, dnxt = dp_ref[...], dc_ref[...], dn_ref[...]
        uprev, ucur, unxt = up_ref[...], uc_ref[...], un_ref[...]
        du = jnp.zeros_like(dcur)
        for j in range(CONV_W):
            k = j - CONV_W // 2
            du = du + w_ref[j:j + 1, :] * _shifted(dprev, dcur, dnxt, -k, first, last)
            dw_ref[j:j + 1, :] += jnp.sum(dcur * _shifted(uprev, ucur, unxt, k, first, last), axis=0, keepdims=True)
        du_ref[...] = du
        db_ref[...] += jnp.sum(dcur, axis=0, keepdims=True)

    return pl.pallas_call(
        body, name=name, grid=(n_tiles,),
        in_specs=_halo_specs(t, width, n_tiles) + _halo_specs(t, width, n_tiles) + [_whole(w)],
        out_specs=[pl.BlockSpec((t, width), lambda i: (i, 0)), pl.BlockSpec((8, width), lambda i: (0, 0)),
                   pl.BlockSpec((1, width), lambda i: (0, 0))],
        out_shape=[jax.ShapeDtypeStruct((S, width), f32), jax.ShapeDtypeStruct((8, width), f32),
                   jax.ShapeDtypeStruct((1, width), f32)],
        compiler_params=_cp(("arbitrary",)))(dz, dz, dz, proj, proj, proj, w)


def _mlstm_step(dm, d, C, n, m, zq, zk, v, ic, fc, ir, fr, bi, bf_):
    L = zq.shape[0]
    q = _silu(zq)
    k = _silu(zk) * (dm ** -0.5)
    i_c, f_c = ic + bi, jax.nn.log_sigmoid(fc + bf_)
    i_r, f_r = ir + bi, jax.nn.log_sigmoid(fr + bf_)
    r = lax.broadcasted_iota(jnp.int32, (L, L), 0)
    c = lax.broadcasted_iota(jnp.int32, (L, L), 1)
    sgn = jnp.where(d == 0, r - c, c - r)
    mask = sgn >= 0
    b_c = jnp.sum(jnp.where(mask, f_r, 0.0), axis=-1, keepdims=True)
    b_r = jnp.sum(jnp.where(sgn <= 0, f_c, 0.0), axis=0, keepdims=True)
    log_inter = b_c + m
    logD = jnp.where(mask, b_c - b_r + i_r, -jnp.inf)
    m_t = jnp.maximum(log_inter, jnp.max(logD, axis=-1, keepdims=True))
    Dm = jnp.exp(logD - m_t)
    w_inter = jnp.exp(log_inter - m_t)
    scores = bdot_nt(q, k) * Dm
    num = bdot(scores, v) + w_inter * bdot_nt(q, C)
    den = jnp.sum(scores, axis=-1, keepdims=True) + w_inter * jnp.sum(q * n, axis=-1, keepdims=True)
    h = num / jnp.maximum(jnp.abs(den), jnp.exp(-m_t))
    bL = jnp.sum(f_c, axis=0, keepdims=True)
    log_w = bL - b_c + i_c
    m_new = jnp.maximum(bL + m, jnp.max(log_w, axis=0, keepdims=True))
    decay = jnp.exp(bL + m - m_new)
    w = jnp.exp(log_w - m_new)
    C_new = decay * C + bdot_tn(w * v, k)
    n_new = decay * n + jnp.sum(w * k, axis=0, keepdims=True)
    return C_new, n_new, m_new, h


def _mlstm_in_specs(L, dm, hm, hb, nc, step_of):
    ng = hm // hb

    def chunk(d, j):
        s = step_of(j)
        return s + d * (nc - 1 - 2 * s)
    return [
        pl.BlockSpec((L, hb * dm), lambda d, g, j: (chunk(d, j), g)),
        pl.BlockSpec((L, hb * dm), lambda d, g, j: (chunk(d, j), ng + g)),
        pl.BlockSpec((L, hb * dm), lambda d, g, j: (chunk(d, j), 2 * ng + g)),
        pl.BlockSpec((None, hb, L, 1), lambda d, g, j: (d, g, chunk(d, j), 0)),
        pl.BlockSpec((None, hb, L, 1), lambda d, g, j: (d, g, chunk(d, j), 0)),
        pl.BlockSpec((None, hb, 1, L), lambda d, g, j: (d, g, 0, chunk(d, j))),
        pl.BlockSpec((None, hb, 1, L), lambda d, g, j: (d, g, 0, chunk(d, j))),
        pl.BlockSpec((None, hb, 1, 1), lambda d, g, j: (d, g, 0, 0)),
        pl.BlockSpec((None, hb, 1, 1), lambda d, g, j: (d, g, 0, 0)),
    ], chunk


def mlstm_fwd(z, proj, gates, hm, dm, name, hb=None):
    S = z.shape[0]
    L = CHUNK
    nc = S // L
    hb = hm if hb is None else hb
    in_specs, chunk = _mlstm_in_specs(L, dm, hm, hb, nc, lambda j: j)

    def body(zq, zk, v, ic, fc, ir, fr, bi, bf_, h_ref, cs_ref, ns_ref, ms_ref, C_sc, n_sc, m_sc):
        d = pl.program_id(0)

        @pl.when(pl.program_id(2) == 0)
        def _():
            C_sc[...] = jnp.zeros_like(C_sc)
            n_sc[...] = jnp.zeros_like(n_sc)
            m_sc[...] = jnp.full_like(m_sc, M_INIT)

        for hh in range(hb):
            cols = slice(hh * dm, (hh + 1) * dm)
            C, n, m = C_sc[hh], n_sc[hh], m_sc[hh]
            cs_ref[hh], ns_ref[hh], ms_ref[hh] = C, n, m
            C2, n2, m2, h = _mlstm_step(dm, d, C, n, m, zq[:, cols], zk[:, cols], v[:, cols], ic[hh], fc[hh],
                                        ir[hh], fr[hh], bi[hh], bf_[hh])
            C_sc[hh], n_sc[hh], m_sc[hh] = C2, n2, m2
            h_ref[:, cols] = h

    return pl.pallas_call(
        body, name=name, grid=(2, hm // hb, nc), in_specs=in_specs,
        out_specs=[pl.BlockSpec((None, L, hb * dm), lambda d, g, j: (d, chunk(d, j), g)),
                   pl.BlockSpec((None, hb, None, dm, dm), lambda d, g, j: (d, g, j, 0, 0)),
                   pl.BlockSpec((None, hb, None, 1, dm), lambda d, g, j: (d, g, j, 0, 0)),
                   pl.BlockSpec((None, hb, None, 1, 1), lambda d, g, j: (d, g, j, 0, 0))],
        out_shape=[jax.ShapeDtypeStruct((2, S, hm * dm), f32), jax.ShapeDtypeStruct((2, hm, nc, dm, dm), f32),
                   jax.ShapeDtypeStruct((2, hm, nc, 1, dm), f32), jax.ShapeDtypeStruct((2, hm, nc, 1, 1), f32)],
        scratch_shapes=[pltpu.VMEM((hb, dm, dm), f32), pltpu.VMEM((hb, 1, dm), f32), pltpu.VMEM((hb, 1, 1), f32)],
        compiler_params=_cp(("arbitrary", "arbitrary", "arbitrary")))(z, z, proj, *gates)


def mlstm_bwd(z, proj, gates, states, dh, hm, dm, name, hb=None):
    S = z.shape[0]
    L = CHUNK
    nc = S // L
    hb = hm if hb is None else hb
    in_specs, chunk = _mlstm_in_specs(L, dm, hm, hb, nc, lambda j: nc - 1 - j)
    st = lambda j: nc - 1 - j
    in_specs = in_specs + [
        pl.BlockSpec((None, hb, None, dm, dm), lambda d, g, j: (d, g, st(j), 0, 0)),
        pl.BlockSpec((None, hb, None, 1, dm), lambda d, g, j: (d, g, st(j), 0, 0)),
        pl.BlockSpec((None, hb, None, 1, 1), lambda d, g, j: (d, g, st(j), 0, 0)),
        pl.BlockSpec((L, hb * dm), lambda d, g, j: (chunk(d, j), g)),
    ]

    def body(zq, zk, v, ic, fc, ir, fr, bi, bf_, cs, ns, ms, dh_ref,
             dzq, dzk, dv, dic, dfc, dir_, dfr, dbi, dbf, dC_sc, dn_sc, dm_sc):
        d = pl.program_id(0)

        @pl.when(pl.program_id(2) == 0)
        def _():
            dC_sc[...] = jnp.zeros_like(dC_sc)
            dn_sc[...] = jnp.zeros_like(dn_sc)
            dm_sc[...] = jnp.zeros_like(dm_sc)
            dbi[...] = jnp.zeros_like(dbi)
            dbf[...] = jnp.zeros_like(dbf)

        for hh in range(hb):
            cols = slice(hh * dm, (hh + 1) * dm)
            prim = (cs[hh], ns[hh], ms[hh], zq[:, cols], zk[:, cols], v[:, cols], ic[hh], fc[hh], ir[hh], fr[hh],
                    bi[hh], bf_[hh])
            _, pull = jax.vjp(functools.partial(_mlstm_step, dm, d), *prim)
            g = pull((dC_sc[hh], dn_sc[hh], dm_sc[hh], dh_ref[:, cols]))
            dC_sc[hh], dn_sc[hh], dm_sc[hh] = g[0], g[1], g[2]
            dzq[:, cols], dzk[:, cols], dv[:, cols] = g[3], g[4], g[5]
            dic[hh], dfc[hh], dir_[hh], dfr[hh] = g[6], g[7], g[8], g[9]
            dbi[hh] += g[10]
            dbf[hh] += g[11]

    tile = pl.BlockSpec((None, L, hb * dm), lambda d, g, j: (d, chunk(d, j), g))
    col = pl.BlockSpec((None, hb, L, 1), lambda d, g, j: (d, g, chunk(d, j), 0))
    row = pl.BlockSpec((None, hb, 1, L), lambda d, g, j: (d, g, 0, chunk(d, j)))
    one = pl.BlockSpec((None, hb, 1, 1), lambda d, g, j: (d, g, 0, 0))
    big = jax.ShapeDtypeStruct((2, S, hm * dm), f32)
    cols_ = jax.ShapeDtypeStruct((2, hm, S, 1), f32)
    rows_ = jax.ShapeDtypeStruct((2, hm, 1, S), f32)
    ones_ = jax.ShapeDtypeStruct((2, hm, 1, 1), f32)
    return pl.pallas_call(
        body, name=name, grid=(2, hm // hb, nc), in_specs=in_specs,
        out_specs=[tile, tile, tile, col, col, row, row, one, one],
        out_shape=[big, big, big, cols_, cols_, rows_, rows_, ones_, ones_],
        scratch_shapes=[pltpu.VMEM((hb, dm, dm), f32), pltpu.VMEM((hb, 1, dm), f32), pltpu.VMEM((hb, 1, 1), f32)],
        compiler_params=_cp(("arbitrary", "arbitrary", "arbitrary")))(z, z, proj, *gates, *states, dh)


def _blocks_to_cols(g):
    return g.transpose(1, 0, 2).reshape(g.shape[1], N_DEV * g.shape[2])


def _cols_to_blocks(a):
    return a.reshape(a.shape[0], N_DEV, a.shape[1] // N_DEV).transpose(1, 0, 2)


def _pad_cols(a, n):
    return jnp.pad(a, ((0, 0), (0, n - a.shape[1])))


def _relu2(u):
    r = jnp.maximum(u, 0.0)
    return r * r


def kernel(x, c, positions, w_ada, b_ada, norm_mix_g, w_in, b_gates, conv_w, conv_b, q_lora_g, w_uq, kv_lora_g, w_ukv, q_norm_g, k_norm_g, mlstm_norm_g, w_out, norm_mlp_g, w_ff1, w_ff2, loss_target, m_w_ada, m_b_ada, m_norm_mix_g, m_w_in, m_b_gates, m_conv_w, m_conv_b, m_q_lora_g, m_w_uq, m_kv_lora_g, m_w_ukv, m_q_norm_g, m_k_norm_g, m_mlstm_norm_g, m_w_out, m_norm_mlp_g, m_w_ff1, m_w_ff2, v_w_ada, v_b_ada, v_norm_mix_g, v_w_in, v_b_gates, v_conv_w, v_conv_b, v_q_lora_g, v_w_uq, v_kv_lora_g, v_w_ukv, v_q_norm_g, v_k_norm_g, v_mlstm_norm_g, v_w_out, v_norm_mlp_g, v_w_ff1, v_w_ff2):
    S, D = x.shape[1], x.shape[2]
    QL, KVL = w_uq.shape[1], w_ukv.shape[1]
    H = w_uq.shape[2] * N_DEV // QK_DIM
    HM = mlstm_norm_g.shape[1]
    DM = mlstm_norm_g.shape[2] * N_DEV
    MW = HM * DM
    D_IN = w_in.shape[2] * N_DEV
    NADA = w_ada.shape[2]
    assert D_IN == QL + KVL + ROPE + 4 * MW + N_GATES and DM % LANE == 0 and S % CHUNK == 0
    assert (4 * MW) % QL == 0 and (4 * MW + QL) % KVL == 0 and KVL % LANE == 0
    idx = 4 * lax.axis_index("x") + 2 * lax.axis_index("y") + lax.axis_index("c")
    x2, tgt = x[0], loss_target[0]

    g_in, g_uq, g_ukv, g_conv, g_mn, c_all = all_gather(
        [w_in[0].astype(bf16), w_uq[0].astype(bf16), w_ukv[0].astype(bf16), conv_w[0], mlstm_norm_g[0], c],
        "gather_weights")
    c_all = c_all.reshape(N_DEV, D)
    xi, yi, ci = lax.axis_index("x"), lax.axis_index("y"), lax.axis_index("c")
    slots = jnp.stack([4 * (1 - xi) + 2 * yi + ci, 4 * xi + 2 * (1 - yi) + ci, 4 * (1 - xi) + 2 * (1 - yi) + ci]).astype(jnp.int32)

    wi = _blocks_to_cols(g_in)
    o_cq, o_ckv, o_kpe, o_m, o_g = 0, QL, QL + KVL, QL + KVL + ROPE, QL + KVL + ROPE + 4 * MW
    w_in_p = jnp.concatenate([wi[:, o_m:o_g], wi[:, o_cq:o_kpe], _pad_cols(wi[:, o_kpe:o_m], LANE),
                              _pad_cols(wi[:, o_g:], LANE)], axis=1)
    NP = w_in_p.shape[1]
    cb_cq, cb_ckv, cb_kpe, cb_g = 4 * MW // QL, (4 * MW + QL) // KVL, (4 * MW + QL + KVL) // LANE, NP // LANE - 1
    w_uq_p = jnp.pad(_blocks_to_cols(g_uq).reshape(QL, H, QK_DIM), ((0, 0), (0, 0), (0, QK_PAD - QK_DIM))).reshape(QL, H * QK_PAD)
    w_ukv_p = _blocks_to_cols(g_ukv).reshape(KVL, H, 2, NOPE).transpose(0, 2, 1, 3).reshape(KVL, 2 * H * NOPE)
    conv_w_f = jnp.pad(_blocks_to_cols(g_conv), ((0, 8 - CONV_W), (0, 0)))
    mn_g = _blocks_to_cols(g_mn).reshape(1, MW)
    gqn = _pad_cols(q_norm_g, QK_PAD)
    gkn = _pad_cols(k_norm_g, QK_PAD)
    fr_np = np.zeros((1, LANE), np.float32)
    fr_np[0, :HALF] = fr_np[0, HALF:ROPE] = ROPE_THETA ** (-np.arange(HALF, dtype=np.float32) / HALF)
    freqs = jnp.asarray(fr_np)
    pos = positions.astype(f32).reshape(S, 1)

    b_blk = lax.dynamic_slice(b_ada, (0, idx * NADA), (1, NADA))
    mod_part = ada_fwd(c_all, w_ada[0], b_blk, "ada_fwd")
    (mod_all,) = all_gather([mod_part], "gather_mod")
    mod = lax.dynamic_index_in_dim(mod_all, idx, axis=1, keepdims=False).reshape(1, N_DEV * NADA)
    shift1, scale1, gate1, shift2, scale2, gate2 = [mod[:, k * D:(k + 1) * D] for k in range(6)]

    (h,) = rowwise(f_norm_mod, [Row(x2)], [norm_mix_g, shift1, scale1], [(D, bf16)], n_rows=S, tile=256, name="norm_mix")
    proj = mm(h, w_in_p, name="proj_in", out_dtype=f32)
    r_cq, r_ckv, r_kpe = Row(proj, QL, cb_cq), Row(proj, KVL, cb_ckv), Row(proj, LANE, cb_kpe)
    f_prep = make_f_mla_prep(H, QK_DIM ** -0.5 * math.log2(math.e))
    prep_params = [q_lora_g, kv_lora_g, gqn, gkn, w_uq_p, w_ukv_p, freqs]
    Q, K, V = rowwise(f_prep, [r_cq, r_ckv, r_kpe, Row(pos, diff=False)], prep_params,
                      [(H * QK_PAD, bf16), (H * QK_PAD, bf16), (H * V_DIM, bf16)], n_rows=S, tile=256, name="mla_prep")
    attn, lse, (g_out, g_ff1, g_ff2) = flash_fwd(
        Q, K, V, H, "flash_fwd", side=[w_out[0].astype(bf16), w_ff1[0].astype(bf16), w_ff2[0].astype(bf16)])
    w_out_f = g_out.reshape(N_DEV * g_out.shape[1], D)
    w_ff2_f = g_ff2.reshape(N_DEV * g_ff2.shape[1], D)

    conv_bias = conv_b
    z = conv_fwd(proj, 2 * MW, conv_w_f, conv_bias, "conv_fwd")
    graw = proj[:, cb_g * LANE:cb_g * LANE + N_GATES].reshape(S, 4, HM)
    gcol = graw.transpose(1, 2, 0).reshape(2, 2, HM, S)
    bg = b_gates.reshape(2, 2, HM)
    gates = (gcol[:, 0].reshape(2, HM, S, 1), gcol[:, 1].reshape(2, HM, S, 1),
             gcol[:, 0].reshape(2, HM, 1, S), gcol[:, 1].reshape(2, HM, 1, S),
             bg[:, 0].reshape(2, HM, 1, 1), bg[:, 1].reshape(2, HM, 1, 1))
    hdir, cs, ns, ms = mlstm_fwd(z, proj, gates, HM, DM, "mlstm_fwd")
    f_post = make_f_mlstm_post(HM, DM)
    post_rows = [Row(hdir, MW, 0, lead=0), Row(hdir, MW, 0, lead=1), Row(proj, MW, 3)]
    (ml_out,) = rowwise(f_post, post_rows, [mn_g], [(MW, bf16)], n_rows=S, tile=256, name="mlstm_post")

    cat = jnp.concatenate([attn, ml_out], axis=1)
    mixed = mm(cat, w_out_f, name="proj_out", out_dtype=f32)
    mlp_params = [gate1, norm_mlp_g, shift2, scale2]
    x1, h2 = rowwise(f_resid_norm_mod, [Row(x2), Row(mixed)], mlp_params, [(D, f32), (D, bf16)],
                     n_rows=S, tile=256, name="resid_norm_mlp")
    u = mm(h2, g_ff1, name="ff1", out_dtype=bf16)
    y = mm(u, w_ff2_f, name="ff2", a_fn=_relu2, out_dtype=f32)
    loss_l, d_out, d_y, d_gate2 = loss_head(x1, y, tgt, gate2, "loss_head")
    loss = lax.psum(loss_l[0, 0], AXES)

    dw_ff2 = mm(u, d_y, name="dw_ff2", ta=True, a_fn=_relu2, out_dtype=bf16)
    d_u = mm(d_y, w_ff2_f, name="d_u", tb=True, epi=lambda acc, uu: acc * (2.0 * jnp.maximum(uu.astype(f32), 0.0)),
             extras=(u,), out_dtype=bf16)
    dw_ff1 = mm(h2, d_u, name="dw_ff1", ta=True, out_dtype=bf16, out_blocks=True)
    d_h2 = mm(d_u, g_ff1, name="d_h2", tb=True, out_dtype=f32, tm=512, tn=2048)
    (d_x1, d_mixed), (d_gate1, d_g_mlp, d_shift2, d_scale2) = rowwise_vjp(
        f_resid_norm_mod, [Row(x2), Row(mixed)], mlp_params, [Row(d_out), Row(d_h2)],
        n_rows=S, tile=256, name="resid_norm_mlp_bwd", row_grad_dtypes=[f32, bf16])
    dw_out = mm(cat, d_mixed, name="dw_out", ta=True, out_dtype=bf16)
    d_cat = mm(d_mixed, w_out_f, name="d_cat", tb=True, out_dtype=f32, tm=512, tn=2048)

    post_rows_b = [post_rows[0], Row(hdir, MW, 0, lead=1, diff=False), post_rows[2]]
    (dh, d_om), (d_mn_g,) = rowwise_vjp(
        f_post, post_rows_b, [mn_g], [Row(d_cat, MW, H * V_DIM // MW)], n_rows=S, tile=256, name="mlstm_post_bwd")
    dzq, dzk, dvm, dic, dfc, dir_, dfr, dbi, dbf = mlstm_bwd(z, proj, gates, (cs, ns, ms), dh, HM, DM, "mlstm_bwd")
    (dz,) = rowwise(f_add_pairs, [Row(dzq, MW, 0, lead=0), Row(dzq, MW, 0, lead=1), Row(dzk, MW, 0, lead=0),
                                  Row(dzk, MW, 0, lead=1)], [], [(2 * MW, f32)], n_rows=S, tile=256, name="dz_sum")
    (d_vm,) = rowwise(f_add, [Row(dvm, MW, 0, lead=0), Row(dvm, MW, 0, lead=1)], [], [(MW, bf16)], n_rows=S, tile=256,
                      name="dv_sum")
    d_qk, d_conv_w, d_conv_b = conv_bwd(dz, proj, 2 * MW, conv_w_f, "conv_bwd")
    dg = jnp.stack([dic.reshape(2, HM, S) + dir_.reshape(2, HM, S), dfc.reshape(2, HM, S) + dfr.reshape(2, HM, S)], axis=1)
    d_gates = dg.reshape(4 * HM, S).T
    d_b_gates = jnp.stack([dbi.reshape(2, HM), dbf.reshape(2, HM)], axis=1).reshape(1, N_GATES)

    mlp_g = [dw_out.reshape(N_DEV, -1, D), dw_ff1, dw_ff2.reshape(N_DEV, -1, D)]
    mlp_tags = ["w_out", "w_ff1", "w_ff2"]
    mlp_sib = pair_exchange(mlp_g, "grad_pair_exchange_mlp")
    mlp_part = [chip_partials(g, r, slots, "grad_chip_partials_" + t) for g, r, t in zip(mlp_g, mlp_sib, mlp_tags)]
    dq, dk, dv, mlp_chips = flash_bwd(Q, K, V, attn, lse.reshape(H, 1, S), d_cat, 0, H, "flash_bwd", side=mlp_part)
    (d_cq, d_ckv, d_kpe), (d_gq, d_gkv, d_gqn, d_gkn, dw_uq_p, dw_ukv_p) = rowwise_vjp(
        f_prep, [r_cq, r_ckv, r_kpe, Row(pos, diff=False)], prep_params, [Row(dq), Row(dk), Row(dv)],
        n_rows=S, tile=256, name="mla_prep_bwd", row_grad_dtypes=[bf16, bf16, bf16],
        param_diff=[True, True, True, True, True, True, False])

    d_proj = jnp.concatenate([d_qk.astype(bf16), d_vm, d_om.astype(bf16), d_cq, d_ckv, d_kpe,
                              _pad_cols(d_gates.astype(bf16), LANE)], axis=1)
    dw_in_p = mm(h, d_proj, name="dw_in", ta=True, out_dtype=bf16)

    dwi = jnp.concatenate([dw_in_p[:, 4 * MW:4 * MW + QL + KVL + ROPE], dw_in_p[:, :4 * MW],
                           dw_in_p[:, cb_g * LANE:cb_g * LANE + N_GATES]], axis=1)
    dw_uq = dw_uq_p.reshape(QL, H, QK_PAD)[:, :, :QK_DIM].reshape(QL, H * QK_DIM)
    dw_ukv = dw_ukv_p.reshape(KVL, 2, H, NOPE).transpose(0, 2, 1, 3).reshape(KVL, 2 * H * NOPE)
    tiny = [(w_uq, m_w_uq, v_w_uq, _cols_to_blocks(dw_uq)),
            (w_ukv, m_w_ukv, v_w_ukv, _cols_to_blocks(dw_ukv)),
            (conv_w, m_conv_w, v_conv_w, _cols_to_blocks(d_conv_w[:CONV_W])),
            (mlstm_norm_g, m_mlstm_norm_g, v_mlstm_norm_g, _cols_to_blocks(d_mn_g.reshape(HM, DM)))]
    tsizes = [int(np.prod(b[0].shape)) for b in tiny]
    T = sum(tsizes)
    PC = 512
    PR = -(-T // (PC * 64)) * 64
    gpack = jnp.concatenate([b[3].astype(bf16).reshape(N_DEV, -1) for b in tiny], axis=1)
    gpack = jnp.pad(gpack, ((0, 0), (0, PR * PC - T))).reshape(N_DEV, PR, PC)
    wpack = lambda k: jnp.pad(jnp.concatenate([b[k].reshape(1, -1) for b in tiny], axis=1),
                              ((0, 0), (0, PR * PC - T))).reshape(PR, PC)
    late_g = [_cols_to_blocks(dwi), gpack]
    late_sib = pair_exchange(late_g, "grad_pair_exchange")
    late_part = [chip_partials(g, r, slots, "grad_chip_partials_" + t) for g, r, t in zip(late_g, late_sib, ["w_in", "tiny"])]
    d_h, late_chips = mm(d_proj, w_in_p, name="d_h", tb=True, out_dtype=f32, tm=512, tn=2048, side=late_part)
    (grad_x,), (d_g_mix, d_shift1, d_scale1) = rowwise_vjp(
        f_norm_mod_thru, [Row(x2)], [norm_mix_g, shift1, scale1], [Row(d_h), Row(d_x1)],
        n_rows=S, tile=256, name="norm_mix_bwd")

    dmod = jnp.concatenate([d_shift1, d_scale1, d_gate1, d_shift2, d_scale2, d_gate2], axis=1)
    small = [(norm_mix_g, m_norm_mix_g, v_norm_mix_g, d_g_mix), (b_gates, m_b_gates, v_b_gates, d_b_gates),
             (conv_b, m_conv_b, v_conv_b, d_conv_b), (q_lora_g, m_q_lora_g, v_q_lora_g, d_gq),
             (kv_lora_g, m_kv_lora_g, v_kv_lora_g, d_gkv), (q_norm_g, m_q_norm_g, v_q_norm_g, d_gqn[:, :QK_DIM]),
             (k_norm_g, m_k_norm_g, v_k_norm_g, d_gkn[:, :QK_DIM]), (norm_mlp_g, m_norm_mlp_g, v_norm_mlp_g, d_g_mlp),
             (b_ada, m_b_ada, v_b_ada, dmod)]
    sizes = [s[0].shape[1] for s in small]
    P = sum(sizes)
    PP = -(-P // LANE) * LANE
    pack = lambda k: _pad_cols(jnp.concatenate([s[k] for s in small], axis=1), PP)
    (sg_all,) = all_gather([pack(3)], "gather_small_grads")
    s_out = adamw([sg_all[k] for k in range(N_DEV)], pack(0), pack(1), pack(2), "adamw_small")
    offs = np.concatenate([[0], np.cumsum(sizes)])
    small_out = [[o[:, offs[k]:offs[k + 1]] for o in s_out] for k in range(len(small))]

    dmod_all = sg_all[:, 0, offs[-2]:offs[-1]]
    dmod_blk = lax.dynamic_slice(dmod_all, (0, idx * NADA), (N_DEV, NADA))
    g_w_ada = ada_wgrad(c_all, dmod_blk, "ada_wgrad")
    ada_out = adamw([g_w_ada], w_ada[0], m_w_ada[0], v_w_ada[0], "adamw_ada")

    large = [(w_in[0], m_w_in[0], v_w_in[0], late_g[0]),
             (w_out[0], m_w_out[0], v_w_out[0], mlp_g[0]),
             (w_ff1[0], m_w_ff1[0], v_w_ff1[0], mlp_g[1]),
             (w_ff2[0], m_w_ff2[0], v_w_ff2[0], mlp_g[2]),
             (wpack(0), wpack(1), wpack(2), gpack)]
    tags = ["w_in", "w_out", "w_ff1", "w_ff2", "tiny"]
    from_sibling = [late_sib[0]] + list(mlp_sib) + [late_sib[1]]
    from_chips = [late_chips[0]] + list(mlp_chips) + [late_chips[1]]
    l_out = []
    for (w_, m_, v_, g), r, fc, t in zip(large, from_sibling, from_chips, tags):
        mine = lax.dynamic_index_in_dim(g, idx, axis=0, keepdims=False)
        sib = lax.dynamic_index_in_dim(r, 2 * xi + yi, axis=0, keepdims=False)
        l_out.append(adamw([mine, sib, fc[0], fc[1], fc[2]], w_, m_, v_, "adamw_" + t))
    toffs = np.concatenate([[0], np.cumsum(tsizes)])
    tiny_out = [[o.reshape(-1)[toffs[k]:toffs[k + 1]].reshape(tiny[k][0].shape) for o in l_out[4]] for k in range(len(tiny))]
    big_out = [[o[None] for o in l_out[0]], tiny_out[0], tiny_out[1], [o[None] for o in l_out[1]],
               [o[None] for o in l_out[2]], [o[None] for o in l_out[3]], tiny_out[2], tiny_out[3]]

    names = ["w_ada", "b_ada", "norm_mix_g", "w_in", "b_gates", "conv_w", "conv_b", "q_lora_g", "w_uq", "kv_lora_g",
             "w_ukv", "q_norm_g", "k_norm_g", "mlstm_norm_g", "w_out", "norm_mlp_g", "w_ff1", "w_ff2"]
    res = {"w_ada": [o[None] for o in ada_out]}
    for k, nm in enumerate(["norm_mix_g", "b_gates", "conv_b", "q_lora_g", "kv_lora_g", "q_norm_g", "k_norm_g",
                            "norm_mlp_g", "b_ada"]):
        res[nm] = small_out[k]
    for k, nm in enumerate(["w_in", "w_uq", "w_ukv", "w_out", "w_ff1", "w_ff2", "conv_w", "mlstm_norm_g"]):
        res[nm] = big_out[k]
    outs = [loss, grad_x[None]]
    for part in range(4):
        outs += [res[nm][part] for nm in names]
    return tuple(outs)
```

```python
import functools
import math

import numpy as np
import jax
import jax.numpy as jnp
from jax import lax
from jax.experimental import pallas as pl
from jax.experimental.pallas import tpu as pltpu

f32 = jnp.float32
bf16 = jnp.bfloat16

N_DEV = 8
AXES = ("x", "y", "c")
MESH = pl.DeviceIdType.MESH

NOPE = 128
ROPE = 64
HALF = ROPE // 2
QK_DIM = NOPE + ROPE
QK_PAD = 256
V_DIM = 128
ROPE_THETA = 10000.0
CHUNK = 128
CONV_W = 5
N_GATES = 16
EPS = 1e-6
M_INIT = -1e30

ADAM_LR, ADAM_B1, ADAM_B2, ADAM_EPS, ADAM_WD, ADAM_STEP = 0.001, 0.9, 0.999, 1e-08, 0.01, 10

LANE = 128
VMEM_LIMIT = 56 * 1024 * 1024


def _cp(sem=None, vmem=VMEM_LIMIT):
    return pltpu.CompilerParams(dimension_semantics=sem, vmem_limit_bytes=vmem)


def _pick(n, target):
    best = None
    t = LANE
    while t <= min(n, target):
        if n % t == 0:
            best = t
        t += LANE
    return best if best is not None else n


def _pick_rows(n, target):
    t = min(n, target)
    while n % t:
        t -= 8
    return t


def _make_dots(cast, precision):
    def dg(a, b, ca, cb):
        if cast is not None:
            a = a.astype(cast)
            b = b.astype(cast)
        return lax.dot_general(a, b, (((ca,), (cb,)), ((), ())), precision=precision, preferred_element_type=f32)

    @jax.custom_vjp
    def nn(a, b):
        return dg(a, b, 1, 0)

    def nn_f(a, b):
        return dg(a, b, 1, 0), (a, b)

    def nn_b(res, g):
        a, b = res
        return dg(g, b, 1, 1).astype(a.dtype), dg(a, g, 0, 0).astype(b.dtype)

    nn.defvjp(nn_f, nn_b)

    @jax.custom_vjp
    def nt(a, b):
        return dg(a, b, 1, 1)

    def nt_f(a, b):
        return dg(a, b, 1, 1), (a, b)

    def nt_b(res, g):
        a, b = res
        return dg(g, b, 1, 0).astype(a.dtype), dg(g, a, 0, 0).astype(b.dtype)

    nt.defvjp(nt_f, nt_b)

    @jax.custom_vjp
    def tn(a, b):
        return dg(a, b, 0, 0)

    def tn_f(a, b):
        return dg(a, b, 0, 0), (a, b)

    def tn_b(res, g):
        a, b = res
        return dg(b, g, 1, 1).astype(a.dtype), dg(a, g, 1, 0).astype(b.dtype)

    tn.defvjp(tn_f, tn_b)
    return nn, nt, tn


bdot, bdot_nt, bdot_tn = _make_dots(bf16, None)
hdot, hdot_nt, hdot_tn = _make_dots(None, lax.Precision.HIGHEST)


def _silu(x):
    return x * jax.nn.sigmoid(x)


def _rms(x, n):
    return x * lax.rsqrt(jnp.sum(x * x, axis=-1, keepdims=True) * (1.0 / n) + EPS)


def _place():
    return lax.axis_index("x"), lax.axis_index("y"), lax.axis_index("c")


def _gather_phases(ins, outs, send_sems, recv_sems, local_sems):
    n = len(ins)
    x, y, c = _place()
    me, sibling = (x, y, c), (x, y, 1 - c)
    chips = [(1 - x, y), (x, 1 - y), (1 - x, 1 - y)]

    def slot(o, p):
        return outs[o].at[4 * p[0] + 2 * p[1] + p[2]]

    def copy(o, k, block, to, src=None):
        dst = slot(o, block)
        return pltpu.make_async_remote_copy(
            src_ref=dst if src is None else src, dst_ref=dst,
            send_sem=send_sems.at[o, k], recv_sem=recv_sems.at[o, k],
            device_id=to, device_id_type=MESH)

    def local(o):
        return pltpu.make_async_copy(ins[o], slot(o, me), local_sems.at[o])

    def first(o):
        return [copy(o, 0, me, sibling, src=ins[o])] + [copy(o, 1 + j, me, (*chip, c), src=ins[o])
                                                        for j, chip in enumerate(chips)]

    def start():
        for o in range(n):
            local(o).start()
        for o in range(n):
            for cp in first(o):
                cp.start()

    def mid():
        for o in range(n):
            for j, chip in enumerate(chips):
                copy(o, 1 + j, (*chip, c), me).wait_recv()
                copy(o, 4 + j, (*chip, c), sibling).start()

    def finish():
        for o in range(n):
            copy(o, 0, sibling, me).wait_recv()
            for j, chip in enumerate(chips):
                copy(o, 4 + j, (*chip, 1 - c), me).wait_recv()
        for o in range(n):
            for cp in first(o):
                cp.wait_send()
            for j, chip in enumerate(chips):
                copy(o, 4 + j, (*chip, c), sibling).wait_send()
        for o in range(n):
            local(o).wait()

    return start, mid, finish


def _gather_scratch(n):
    return [pltpu.SemaphoreType.DMA((n, 7)), pltpu.SemaphoreType.DMA((n, 7)), pltpu.SemaphoreType.DMA((n,))]


def all_gather(ops, name):
    n = len(ops)

    def body(*refs):
        start, mid, finish = _gather_phases(refs[:n], refs[n:2 * n], *refs[2 * n:])
        start()
        mid()
        finish()

    anyspec = pl.BlockSpec(memory_space=pl.ANY)
    return pl.pallas_call(
        body, name=name,
        out_shape=[jax.ShapeDtypeStruct((N_DEV,) + o.shape, o.dtype) for o in ops],
        in_specs=[anyspec] * n, out_specs=[anyspec] * n,
        scratch_shapes=_gather_scratch(n),
    )(*ops)


def pair_exchange(gs, name):
    n = len(gs)

    def body(*refs):
        g_refs, out_refs = refs[:n], refs[n:2 * n]
        send_sems, recv_sems = refs[2 * n:]
        x, y, c = _place()
        sibling = (x, y, 1 - c)
        cps = []
        for o in range(n):
            for q in range(4):
                cp = pltpu.make_async_remote_copy(
                    src_ref=g_refs[o].at[2 * q + (1 - c)], dst_ref=out_refs[o].at[q],
                    send_sem=send_sems.at[o, q], recv_sem=recv_sems.at[o, q],
                    device_id=sibling, device_id_type=MESH)
                cp.start()
                cps.append(cp)
        for cp in cps:
            cp.wait_recv()
        for cp in cps:
            cp.wait_send()

    anyspec = pl.BlockSpec(memory_space=pl.ANY)
    return pl.pallas_call(
        body, name=name, out_shape=[jax.ShapeDtypeStruct((4,) + g.shape[1:], g.dtype) for g in gs],
        in_specs=[anyspec] * n, out_specs=[anyspec] * n,
        scratch_shapes=[pltpu.SemaphoreType.DMA((n, 4)), pltpu.SemaphoreType.DMA((n, 4))],
    )(*gs)


def _chip_exchange_phases(p_refs, out_refs, send_sems, recv_sems):
    n = len(p_refs)
    x, y, c = _place()
    chips = [(1 - x, y), (x, 1 - y), (1 - x, 1 - y)]

    def copies():
        return [pltpu.make_async_remote_copy(
            src_ref=p_refs[o].at[j], dst_ref=out_refs[o].at[j],
            send_sem=send_sems.at[o, j], recv_sem=recv_sems.at[o, j],
            device_id=(*chip, c), device_id_type=MESH) for o in range(n) for j, chip in enumerate(chips)]

    def start():
        for cp in copies():
            cp.start()

    def finish():
        for cp in copies():
            cp.wait_recv()
        for cp in copies():
            cp.wait_send()

    return start, finish


def _chip_exchange_scratch(n):
    return [pltpu.SemaphoreType.DMA((n, 3)), pltpu.SemaphoreType.DMA((n, 3))]


def chip_partials(g, recv, slots, name):
    _, R, C = g.shape
    tr = _pick_rows(R, 512)

    def body(s_ref, a_ref, b_ref, o_ref):
        o_ref[...] = (a_ref[...].astype(f32) + b_ref[...].astype(f32)).astype(o_ref.dtype)

    grid_spec = pltpu.PrefetchScalarGridSpec(
        num_scalar_prefetch=1, grid=(3, R // tr),
        in_specs=[pl.BlockSpec((None, tr, C), lambda j, i, s: (s[j], i, 0)),
                  pl.BlockSpec((None, tr, C), lambda j, i, s: (s[j] // 2, i, 0))],
        out_specs=pl.BlockSpec((None, tr, C), lambda j, i, s: (j, i, 0)))
    return pl.pallas_call(body, name=name, grid_spec=grid_spec,
                          out_shape=jax.ShapeDtypeStruct((3, R, C), g.dtype),
                          compiler_params=_cp(("arbitrary", "arbitrary")))(slots, g, recv)


def adamw(parts, w, m, v, name, rows=256):
    R, C = w.shape
    tr = _pick_rows(R, rows)
    npart = len(parts)
    c1 = 1.0 - ADAM_B1 ** ADAM_STEP
    c2 = 1.0 - ADAM_B2 ** ADAM_STEP

    def body(*refs):
        p_refs = refs[:npart]
        w_ref, m_ref, v_ref, g_out, d_out, m_out, v_out = refs[npart:]
        g = p_refs[0][...].astype(f32)
        for p in p_refs[1:]:
            g = g + p[...].astype(f32)
        mn = ADAM_B1 * m_ref[...] + (1.0 - ADAM_B1) * g
        vn = ADAM_B2 * v_ref[...] + (1.0 - ADAM_B2) * (g * g)
        m_hat = mn / c1
        v_hat = vn / c2
        g_out[...] = g
        d_out[...] = -ADAM_LR * (m_hat / (jnp.sqrt(v_hat) + ADAM_EPS) + ADAM_WD * w_ref[...])
        m_out[...] = mn
        v_out[...] = vn

    spec = pl.BlockSpec((tr, C), lambda i: (i, 0))
    return pl.pallas_call(
        body, name=name, grid=(R // tr,),
        in_specs=[spec] * (npart + 3), out_specs=[spec] * 4,
        out_shape=[jax.ShapeDtypeStruct((R, C), f32)] * 4,
        compiler_params=_cp(("arbitrary",)))(*parts, w, m, v)


def mm(a, b, *, name, ta=False, tb=False, a_fn=None, epi=None, extras=(), out_dtype=f32, out_blocks=False, side=(),
       tm=1024, tn=1024, tk=2048):
    K, M = a.shape if ta else a.shape[::-1]
    b3 = b.ndim == 3
    if b3:
        assert not tb
        N, K2 = N_DEV * b.shape[2], b.shape[1]
    else:
        N, K2 = b.shape if tb else b.shape[::-1]
    assert K == K2, (a.shape, b.shape, ta, tb)
    n_split = N // N_DEV if (out_blocks or b3) else N
    tm, tn, tk = _pick(M, tm), _pick(n_split, tn), _pick(K, tk)
    nb = n_split // tn
    nk = K // tk
    ne = len(extras)
    assert not (out_blocks and ne)
    dims = (((0 if ta else 1,), (1 if tb else 0,)), ((), ()))

    ns = len(side)
    n_steps = (M // tm) * (N // tn) * nk
    assert ns == 0 or n_steps >= 2

    def body(a_ref, b_ref, *rest):
        e_refs, o_ref, acc = rest[:ne], rest[ne + ns], rest[ne + 2 * ns + 1]
        k = pl.program_id(2)
        if ns:
            step = (pl.program_id(0) * (N // tn) + pl.program_id(1)) * nk + k
            x_start, x_finish = _chip_exchange_phases(rest[ne:ne + ns], rest[ne + ns + 1:ne + 2 * ns + 1],
                                                      *rest[ne + 2 * ns + 2:])
            pl.when(step == 0)(x_start)

        @pl.when(k == 0)
        def _():
            acc[...] = jnp.zeros_like(acc)

        av = a_ref[...]
        if a_fn is not None:
            av = a_fn(av.astype(f32))
        acc[...] += lax.dot_general(av.astype(bf16), b_ref[...].astype(bf16), dims, preferred_element_type=f32)

        @pl.when(k == nk - 1)
        def _():
            r = acc[...]
            if epi is not None:
                r = epi(r, *[e[...] for e in e_refs])
            o_ref[...] = r.astype(o_ref.dtype)

        if ns:
            pl.when(step == n_steps - 1)(x_finish)

    a_spec = pl.BlockSpec((tk, tm), lambda i, j, k: (k, i)) if ta else pl.BlockSpec((tm, tk), lambda i, j, k: (i, k))
    if b3:
        b_spec = pl.BlockSpec((None, tk, tn), lambda i, j, k: (j // nb, k, j % nb))
    else:
        b_spec = pl.BlockSpec((tn, tk), lambda i, j, k: (j, k)) if tb else pl.BlockSpec((tk, tn), lambda i, j, k: (k, j))
    if out_blocks:
        o_spec = pl.BlockSpec((None, tm, tn), lambda i, j, k: (j // nb, i, j % nb))
        o_shape = jax.ShapeDtypeStruct((N_DEV, M, N // N_DEV), out_dtype)
    else:
        o_spec = pl.BlockSpec((tm, tn), lambda i, j, k: (i, j))
        o_shape = jax.ShapeDtypeStruct((M, N), out_dtype)
    if not ns:
        return pl.pallas_call(
            body, name=name, grid=(M // tm, N // tn, nk),
            in_specs=[a_spec, b_spec] + [o_spec] * ne, out_specs=o_spec,
            out_shape=o_shape,
            scratch_shapes=[pltpu.VMEM((tm, tn), f32)],
            compiler_params=_cp(("parallel", "parallel", "arbitrary")))(a, b, *extras)
    anyspec = pl.BlockSpec(memory_space=pl.ANY)
    res = pl.pallas_call(
        body, name=name, grid=(M // tm, N // tn, nk),
        in_specs=[a_spec, b_spec] + [o_spec] * ne + [anyspec] * ns, out_specs=[o_spec] + [anyspec] * ns,
        out_shape=[o_shape] + [jax.ShapeDtypeStruct(p.shape, p.dtype) for p in side],
        scratch_shapes=[pltpu.VMEM((tm, tn), f32)] + _chip_exchange_scratch(ns),
        compiler_params=_cp(("arbitrary", "arbitrary", "arbitrary")))(a, b, *extras, *side)
    return res[0], list(res[1:])


class Row:
    def __init__(self, arr, width=None, col=0, lead=None, diff=True):
        self.arr, self.col, self.lead, self.diff = arr, col, lead, diff
        self.width = arr.shape[-1] if width is None else width

    def spec(self, t):
        col, lead = self.col, self.lead
        if lead is None:
            return pl.BlockSpec((t, self.width), lambda i: (i, col))
        return pl.BlockSpec((None, t, self.width), lambda i: (lead, i, col))


def _whole(p):
    return pl.BlockSpec(p.shape, lambda i: (0,) * p.ndim)


def rowwise(fn, rows, params, outs, *, n_rows, tile, name):
    t = _pick_rows(n_rows, tile)
    nr, npar, no = len(rows), len(params), len(outs)

    def body(*refs):
        r_refs, p_refs, o_refs = refs[:nr], refs[nr:nr + npar], refs[nr + npar:]
        res = fn(*[r[...].astype(f32) for r in r_refs], *[p[...] for p in p_refs])
        for o_ref, val in zip(o_refs, res):
            o_ref[...] = val.astype(o_ref.dtype)

    return pl.pallas_call(
        body, name=name, grid=(n_rows // t,),
        in_specs=[r.spec(t) for r in rows] + [_whole(p) for p in params],
        out_specs=[pl.BlockSpec((t, w), lambda i: (i, 0)) for w, _ in outs],
        out_shape=[jax.ShapeDtypeStruct((n_rows, w), dt) for w, dt in outs],
        compiler_params=_cp(("arbitrary",)))(*[r.arr for r in rows], *params)


def rowwise_vjp(fn, rows, params, cts, *, n_rows, tile, name, row_grad_dtypes=None, param_diff=None):
    t = _pick_rows(n_rows, tile)
    nr, npar, nc = len(rows), len(params), len(cts)
    param_diff = [True] * npar if param_diff is None else param_diff
    d_rows = [k for k, r in enumerate(rows) if r.diff]
    d_pars = [k for k in range(npar) if param_diff[k]]
    row_grad_dtypes = [f32] * len(d_rows) if row_grad_dtypes is None else row_grad_dtypes

    def body(*refs):
        r_refs, p_refs = refs[:nr], refs[nr:nr + npar]
        c_refs = refs[nr + npar:nr + npar + nc]
        dr_refs = refs[nr + npar + nc:nr + npar + nc + len(d_rows)]
        dp_refs = refs[nr + npar + nc + len(d_rows):]
        rv = [r[...].astype(f32) for r in r_refs]
        pv = [p[...] for p in p_refs]

        def g(*dvals):
            full_r, full_p = list(rv), list(pv)
            for k, val in zip(d_rows, dvals[:len(d_rows)]):
                full_r[k] = val
            for k, val in zip(d_pars, dvals[len(d_rows):]):
                full_p[k] = val
            return tuple(fn(*full_r, *full_p))

        prim = [rv[k] for k in d_rows] + [pv[k].astype(f32) for k in d_pars]
        _, pull = jax.vjp(g, *prim)
        grads = pull(tuple(c[...].astype(f32) for c in c_refs))
        for ref, val in zip(dr_refs, grads[:len(d_rows)]):
            ref[...] = val.astype(ref.dtype)

        @pl.when(pl.program_id(0) == 0)
        def _():
            for ref in dp_refs:
                ref[...] = jnp.zeros_like(ref)

        for ref, val in zip(dp_refs, grads[len(d_rows):]):
            ref[...] += val

    out_specs = [pl.BlockSpec((t, rows[k].width), lambda i: (i, 0)) for k in d_rows]
    out_specs += [_whole(params[k]) for k in d_pars]
    out_shape = [jax.ShapeDtypeStruct((n_rows, rows[k].width), dt) for k, dt in zip(d_rows, row_grad_dtypes)]
    out_shape += [jax.ShapeDtypeStruct(params[k].shape, f32) for k in d_pars]
    res = pl.pallas_call(
        body, name=name, grid=(n_rows // t,),
        in_specs=[r.spec(t) for r in rows] + [_whole(p) for p in params] + [c.spec(t) for c in cts],
        out_specs=out_specs, out_shape=out_shape,
        compiler_params=_cp(("arbitrary",)))(*[r.arr for r in rows], *params, *[c.arr for c in cts])
    return res[:len(d_rows)], res[len(d_rows):]


def f_norm_mod(x, g, shift, scale):
    return (_rms(x, x.shape[-1]) * g * (1.0 + scale) + shift,)


def f_norm_mod_thru(x, g, shift, scale):
    return f_norm_mod(x, g, shift, scale) + (x,)


def f_resid_norm_mod(x, mixed, gate1, g2, shift2, scale2):
    x1 = x + gate1 * mixed
    return (x1,) + f_norm_mod(x1, g2, shift2, scale2)


def _rotate_half_raw(u):
    lane = lax.broadcasted_iota(jnp.int32, u.shape, 1)
    up = pltpu.roll(u, LANE - HALF, 1)
    down = pltpu.roll(u, HALF, 1)
    return jnp.where(lane < HALF, -up, jnp.where(lane < ROPE, down, 0.0))


@jax.custom_vjp
def _rotate_half(u):
    return _rotate_half_raw(u)


_rotate_half.defvjp(lambda u: (_rotate_half_raw(u), None), lambda _, g: (-_rotate_half_raw(g),))


def make_f_mla_prep(n_heads, q_scale):
    def fn(cq, ckv, kpe, pos, gq, gkv, gqn, gkn, w_uq, w_ukv, freqs):
        ang = pos * freqs
        cos, sin = jnp.cos(ang), jnp.sin(ang)

        def rope(u):
            return u * cos + _rotate_half(u) * sin

        qraw = bdot(_rms(cq, cq.shape[-1]) * gq, w_uq)
        kv = bdot(_rms(ckv, ckv.shape[-1]) * gkv, w_ukv)
        kpe_ss = jnp.sum(kpe * kpe, axis=-1, keepdims=True)
        qs, ks = [], []
        for h in range(n_heads):
            qh = _rms(qraw[:, h * QK_PAD:(h + 1) * QK_PAD], QK_DIM) * gqn
            qs += [qh[:, :NOPE], rope(qh[:, NOPE:])]
            kn = kv[:, h * NOPE:(h + 1) * NOPE]
            r = lax.rsqrt((jnp.sum(kn * kn, axis=-1, keepdims=True) + kpe_ss) * (1.0 / QK_DIM) + EPS)
            ks += [kn * r * gkn[:, :NOPE], rope(kpe * r * gkn[:, NOPE:])]
        return jnp.concatenate(qs, axis=-1) * q_scale, jnp.concatenate(ks, axis=-1), kv[:, n_heads * NOPE:]
    return fn


def make_f_mlstm_post(n_heads, dm):
    def fn(hf, hb, o, g):
        hm = hf + hb
        outs = []
        for h in range(n_heads):
            sl = slice(h * dm, (h + 1) * dm)
            outs.append(jax.nn.sigmoid(o[:, sl]) * (_rms(hm[:, sl], dm) * g[:, sl]))
        return (jnp.concatenate(outs, axis=-1),)
    return fn


def f_add_pairs(a0, a1, b0, b1):
    return (jnp.concatenate([a0 + a1, b0 + b1], axis=-1),)


def f_add(a, b):
    return (a + b,)


def loss_head(x1, y, target, gate2, name, tile=256):
    S, D = x1.shape
    t = _pick_rows(S, tile)

    def body(x1_ref, y_ref, t_ref, g_ref, loss_ref, dout_ref, dy_ref, dgate_ref):
        @pl.when(pl.program_id(0) == 0)
        def _():
            loss_ref[...] = jnp.zeros_like(loss_ref)
            dgate_ref[...] = jnp.zeros_like(dgate_ref)

        yv, gv = y_ref[...], g_ref[...]
        e = x1_ref[...] + gv * yv - t_ref[...]
        loss_ref[...] += 0.5 * jnp.sum(jnp.sum(e * e, axis=-1, keepdims=True) * (1.0 / D), axis=0, keepdims=True)
        d_out = e * (1.0 / D)
        dout_ref[...] = d_out
        dy_ref[...] = (d_out * gv).astype(dy_ref.dtype)
        dgate_ref[...] += jnp.sum(d_out * yv, axis=0, keepdims=True)

    row = pl.BlockSpec((t, D), lambda i: (i, 0))
    return pl.pallas_call(
        body, name=name, grid=(S // t,),
        in_specs=[row, row, row, pl.BlockSpec((1, D), lambda i: (0, 0))],
        out_specs=[pl.BlockSpec((1, 1), lambda i: (0, 0)), row, row, pl.BlockSpec((1, D), lambda i: (0, 0))],
        out_shape=[jax.ShapeDtypeStruct((1, 1), f32), jax.ShapeDtypeStruct((S, D), f32),
                   jax.ShapeDtypeStruct((S, D), bf16), jax.ShapeDtypeStruct((1, D), f32)],
        compiler_params=_cp(("arbitrary",)))(x1, y, target, gate2)


def ada_fwd(c_all, w_blk, b_blk, name):
    B, D = c_all.shape
    N = w_blk.shape[1]
    tn = _pick(N, 512)

    def body(c_ref, w_ref, b_ref, o_ref):
        o_ref[...] = bdot(_silu(c_ref[...]), w_ref[...]) + b_ref[...]

    return pl.pallas_call(
        body, name=name, grid=(N // tn,),
        in_specs=[pl.BlockSpec((B, D), lambda j: (0, 0)), pl.BlockSpec((D, tn), lambda j: (0, j)),
                  pl.BlockSpec((1, tn), lambda j: (0, j))],
        out_specs=pl.BlockSpec((B, tn), lambda j: (0, j)),
        out_shape=jax.ShapeDtypeStruct((B, N), f32), compiler_params=_cp(("arbitrary",)))(c_all, w_blk, b_blk)


def ada_wgrad(c_all, dmod_blk, name):
    B, D = c_all.shape
    N = dmod_blk.shape[1]
    tn = _pick(N, 512)

    def body(c_ref, d_ref, o_ref):
        o_ref[...] = hdot_tn(_silu(c_ref[...]), d_ref[...])

    return pl.pallas_call(
        body, name=name, grid=(N // tn,),
        in_specs=[pl.BlockSpec((B, D), lambda j: (0, 0)), pl.BlockSpec((B, tn), lambda j: (0, j))],
        out_specs=pl.BlockSpec((D, tn), lambda j: (0, j)),
        out_shape=jax.ShapeDtypeStruct((D, N), f32), compiler_params=_cp(("arbitrary",)))(c_all, dmod_blk)


def _nt(a, b):
    return lax.dot_general(a, b, (((1,), (1,)), ((), ())), preferred_element_type=f32)


def _tn(a, b):
    return lax.dot_general(a, b, (((0,), (0,)), ((), ())), preferred_element_type=f32)


def flash_fwd(q, k, v, n_heads, name, side=(), tq=512, tk=8192, sub=1024):
    S = q.shape[0]
    tq, tk = _pick(S, tq), _pick(S, tk)
    sub = _pick(tk, sub)
    nk, nsub = S // tk, tk // sub
    ns = len(side)
    n_steps = n_heads * (S // tq) * nk
    assert ns == 0 or n_steps >= 3

    def body(*refs):
        q_ref, k_ref, v_ref = refs[:3]
        o_ref, lse_ref = refs[3 + ns:5 + ns]
        m_sc, l_sc, acc_sc = refs[5 + 2 * ns:8 + 2 * ns]
        j = pl.program_id(2)
        step = (pl.program_id(0) * (S // tq) + pl.program_id(1)) * nk + j
        if ns:
            g_start, g_mid, g_finish = _gather_phases(refs[3:3 + ns], refs[5 + ns:5 + 2 * ns], *refs[8 + 2 * ns:])
            pl.when(step == 0)(g_start)
            pl.when(step == n_steps // 2)(g_mid)

        @pl.when(j == 0)
        def _():
            m_sc[...] = jnp.full_like(m_sc, -jnp.inf)
            l_sc[...] = jnp.zeros_like(l_sc)
            acc_sc[...] = jnp.zeros_like(acc_sc)

        qv = q_ref[...]
        m = m_sc[...]
        ss = [_nt(qv, k_ref[b * sub:(b + 1) * sub, :]) for b in range(nsub)]
        mx = ss[0]
        for s in ss[1:]:
            mx = jnp.maximum(mx, s)
        m_new = jnp.maximum(m, jnp.max(mx, axis=-1, keepdims=True))
        alpha = jnp.exp2(m - m_new)
        psum, pv = None, None
        for b in range(nsub):
            p = jnp.exp2(ss[b] - m_new)
            d = jnp.dot(p.astype(bf16), v_ref[b * sub:(b + 1) * sub, :], preferred_element_type=f32)
            psum = p if psum is None else psum + p
            pv = d if pv is None else pv + d
        m, l, acc = m_new, alpha * l_sc[...] + jnp.sum(psum, axis=-1, keepdims=True), alpha * acc_sc[...] + pv
        m_sc[...], l_sc[...], acc_sc[...] = m, l, acc

        @pl.when(j == nk - 1)
        def _():
            o_ref[...] = (acc / l).astype(o_ref.dtype)
            lse_ref[...] = m + jnp.log2(l)

        if ns:
            pl.when(step == n_steps - 1)(g_finish)

    anyspec = pl.BlockSpec(memory_space=pl.ANY)
    res = pl.pallas_call(
        body, name=name, grid=(n_heads, S // tq, nk),
        in_specs=[pl.BlockSpec((tq, QK_PAD), lambda h, i, j: (i, h)),
                  pl.BlockSpec((tk, QK_PAD), lambda h, i, j: (j, h)),
                  pl.BlockSpec((tk, V_DIM), lambda h, i, j: (j, h))] + [anyspec] * ns,
        out_specs=[pl.BlockSpec((tq, V_DIM), lambda h, i, j: (i, h)),
                   pl.BlockSpec((None, tq, 1), lambda h, i, j: (h, i, 0))] + [anyspec] * ns,
        out_shape=[jax.ShapeDtypeStruct((S, n_heads * V_DIM), bf16), jax.ShapeDtypeStruct((n_heads, S, 1), f32)]
        + [jax.ShapeDtypeStruct((N_DEV,) + a.shape, a.dtype) for a in side],
        scratch_shapes=[pltpu.VMEM((tq, 1), f32), pltpu.VMEM((tq, 1), f32), pltpu.VMEM((tq, V_DIM), f32)]
        + (_gather_scratch(ns) if ns else []),
        compiler_params=_cp(("arbitrary", "arbitrary", "arbitrary")))(q, k, v, *side)
    return res[0], res[1], list(res[2:])


def flash_bwd(q, k, v, o, lse_row, do, do_col0, n_heads, name, side=(), tq=1024, tk=8192, sub=512):
    S = q.shape[0]
    tq, tk = _pick(S, tq), _pick(S, tk)
    sub = _pick(tk, sub)
    nsub = tk // sub
    ln2 = math.log(2.0)
    ns = len(side)
    n_steps = n_heads * (S // tq) * (S // tk)
    assert ns == 0 or n_steps >= 2

    def body(*refs):
        q_ref, k_ref, v_ref, o_ref, lse_ref, do_ref = refs[:6]
        dq_ref, dk_ref, dv_ref = refs[6 + ns:9 + ns]
        i, j = pl.program_id(1), pl.program_id(2)
        step = (pl.program_id(0) * (S // tq) + i) * (S // tk) + j
        if ns:
            x_start, x_finish = _chip_exchange_phases(refs[6:6 + ns], refs[9 + ns:9 + 2 * ns], *refs[9 + 2 * ns:])
            pl.when(step == 0)(x_start)

        @pl.when(j == 0)
        def _():
            dq_ref[...] = jnp.zeros_like(dq_ref)

        @pl.when((i == 0) & (j == 0))
        def _():
            dk_ref[...] = jnp.zeros_like(dk_ref)
            dv_ref[...] = jnp.zeros_like(dv_ref)

        qv = q_ref[...]
        dof = do_ref[...].astype(f32)
        do_b = dof.astype(bf16)
        do_s = (dof * ln2).astype(bf16)
        delta = hdot_nt(jnp.ones((8, V_DIM), f32), dof * ln2 * o_ref[...].astype(f32))[0:1, :]
        lse = lse_ref[...]
        dq = jnp.zeros((tq, QK_PAD), f32)
        for b in range(nsub):
            kb = k_ref[b * sub:(b + 1) * sub, :]
            rows = pl.ds(pl.multiple_of(j * tk + b * sub, sub), sub)
            pt = jnp.exp2(_nt(kb, qv) - lse)
            dpt = _nt(v_ref[b * sub:(b + 1) * sub, :], do_s)
            dst = (pt * (dpt - delta)).astype(bf16)
            dv_ref[rows, :] += jnp.dot(pt.astype(bf16), do_b, preferred_element_type=f32)
            dk_ref[rows, :] += jnp.dot(dst, qv, preferred_element_type=f32)
            dq = dq + _tn(dst, kb)
        dq_ref[...] += dq
        if ns:
            pl.when(step == n_steps - 1)(x_finish)

    anyspec = pl.BlockSpec(memory_space=pl.ANY)
    res = pl.pallas_call(
        body, name=name, grid=(n_heads, S // tq, S // tk),
        in_specs=[pl.BlockSpec((tq, QK_PAD), lambda h, i, j: (i, h)),
                  pl.BlockSpec((tk, QK_PAD), lambda h, i, j: (j, h)),
                  pl.BlockSpec((tk, V_DIM), lambda h, i, j: (j, h)),
                  pl.BlockSpec((tq, V_DIM), lambda h, i, j: (i, h)),
                  pl.BlockSpec((None, 1, tq), lambda h, i, j: (h, 0, i)),
                  pl.BlockSpec((tq, V_DIM), lambda h, i, j: (i, do_col0 + h))] + [anyspec] * ns,
        out_specs=[pl.BlockSpec((tq, QK_PAD), lambda h, i, j: (i, h)),
                   pl.BlockSpec((S, QK_PAD), lambda h, i, j: (0, h)),
                   pl.BlockSpec((S, V_DIM), lambda h, i, j: (0, h))] + [anyspec] * ns,
        out_shape=[jax.ShapeDtypeStruct((S, n_heads * QK_PAD), f32), jax.ShapeDtypeStruct((S, n_heads * QK_PAD), f32),
                   jax.ShapeDtypeStruct((S, n_heads * V_DIM), f32)] + [jax.ShapeDtypeStruct(a.shape, a.dtype) for a in side],
        scratch_shapes=_chip_exchange_scratch(ns) if ns else [],
        compiler_params=_cp(("arbitrary", "arbitrary", "arbitrary")))(q, k, v, o, lse_row, do, *side)
    return res[0], res[1], res[2], list(res[3:])


def _shifted(prev, cur, nxt, k, first, last):
    if k == 0:
        return cur
    t = cur.shape[0]
    r = lax.broadcasted_iota(jnp.int32, (HALO,) + cur.shape[1:], 0)
    if k < 0:
        body = pltpu.roll(cur, -k, 0)
        edge = jnp.where(first, 0.0, pltpu.roll(prev, -k, 0))
        return jnp.concatenate([jnp.where(r < -k, edge, body[:HALO]), body[HALO:]], axis=0)
    body = pltpu.roll(cur, t - k, 0)
    edge = jnp.where(last, 0.0, pltpu.roll(nxt, HALO - k, 0))
    return jnp.concatenate([body[:t - HALO], jnp.where(r >= HALO - k, edge, body[t - HALO:])], axis=0)


HALO = 8


def _halo_specs(t, width, n_tiles, col=0):
    per = t // HALO
    return [pl.BlockSpec((HALO, width), lambda i: (jnp.maximum(i * per - 1, 0), col)),
            pl.BlockSpec((t, width), lambda i: (i, col)),
            pl.BlockSpec((HALO, width), lambda i: (jnp.minimum((i + 1) * per, n_tiles * per - 1), col))]


def conv_fwd(proj, width, w, b, name, tile=256):
    S = proj.shape[0]
    t = _pick_rows(S, tile)
    n_tiles = S // t

    def body(p_ref, c_ref, n_ref, w_ref, b_ref, z_ref):
        i = pl.program_id(0)
        first, last = i == 0, i == n_tiles - 1
        prev, cur, nxt = p_ref[...], c_ref[...], n_ref[...]
        z = b_ref[...] + jnp.zeros_like(cur)
        for j in range(CONV_W):
            z = z + w_ref[j:j + 1, :] * _shifted(prev, cur, nxt, j - CONV_W // 2, first, last)
        z_ref[...] = z

    return pl.pallas_call(
        body, name=name, grid=(n_tiles,),
        in_specs=_halo_specs(t, width, n_tiles) + [_whole(w), _whole(b)],
        out_specs=pl.BlockSpec((t, width), lambda i: (i, 0)),
        out_shape=jax.ShapeDtypeStruct((S, width), f32),
        compiler_params=_cp(("arbitrary",)))(proj, proj, proj, w, b)


def conv_bwd(dz, proj, width, w, name, tile=256):
    S = proj.shape[0]
    t = _pick_rows(S, tile)
    n_tiles = S // t

    def body(dp_ref, dc_ref, dn_ref, up_ref, uc_ref, un_ref, w_ref, du_ref, dw_ref, db_ref):
        i = pl.program_id(0)
        first, last = i == 0, i == n_tiles - 1

        @pl.when(first)
        def _():
            dw_ref[...] = jnp.zeros_like(dw_ref)
            db_ref[...] = jnp.zeros_like(db_ref)

        dprev, dcur, dnxt = dp_ref[...], dc_ref[...], dn_ref[...]
        uprev, ucur, unxt = up_ref[...], uc_ref[...], un_ref[...]
        du = jnp.zeros_like(dcur)
        for j in range(CONV_W):
            k = j - CONV_W // 2
            du = du + w_ref[j:j + 1, :] * _shifted(dprev, dcur, dnxt, -k, first, last)
            dw_ref[j:j + 1, :] += jnp.sum(dcur * _shifted(uprev, ucur, unxt, k, first, last), axis=0, keepdims=True)
        du_ref[...] = du
        db_ref[...] += jnp.sum(dcur, axis=0, keepdims=True)

    return pl.pallas_call(
        body, name=name, grid=(n_tiles,),
        in_specs=_halo_specs(t, width, n_tiles) + _halo_specs(t, width, n_tiles) + [_whole(w)],
        out_specs=[pl.BlockSpec((t, width), lambda i: (i, 0)), pl.BlockSpec((8, width), lambda i: (0, 0)),
                   pl.BlockSpec((1, width), lambda i: (0, 0))],
        out_shape=[jax.ShapeDtypeStruct((S, width), f32), jax.ShapeDtypeStruct((8, width), f32),
                   jax.ShapeDtypeStruct((1, width), f32)],
        compiler_params=_cp(("arbitrary",)))(dz, dz, dz, proj, proj, proj, w)


def _mlstm_step(dm, d, C, n, m, zq, zk, v, ic, fc, ir, fr, bi, bf_):
    L = zq.shape[0]
    q = _silu(zq)
    k = _silu(zk) * (dm ** -0.5)
    i_c, f_c = ic + bi, jax.nn.log_sigmoid(fc + bf_)
    i_r, f_r = ir + bi, jax.nn.log_sigmoid(fr + bf_)
    r = lax.broadcasted_iota(jnp.int32, (L, L), 0)
    c = lax.broadcasted_iota(jnp.int32, (L, L), 1)
    sgn = jnp.where(d == 0, r - c, c - r)
    mask = sgn >= 0
    b_c = jnp.sum(jnp.where(mask, f_r, 0.0), axis=-1, keepdims=True)
    b_r = jnp.sum(jnp.where(sgn <= 0, f_c, 0.0), axis=0, keepdims=True)
    log_inter = b_c + m
    logD = jnp.where(mask, b_c - b_r + i_r, -jnp.inf)
    m_t = jnp.maximum(log_inter, jnp.max(logD, axis=-1, keepdims=True))
    Dm = jnp.exp(logD - m_t)
    w_inter = jnp.exp(log_inter - m_t)
    scores = bdot_nt(q, k) * Dm
    num = bdot(scores, v) + w_inter * bdot_nt(q, C)
    den = jnp.sum(scores, axis=-1, keepdims=True) + w_inter * jnp.sum(q * n, axis=-1, keepdims=True)
    h = num / jnp.maximum(jnp.abs(den), jnp.exp(-m_t))
    bL = jnp.sum(f_c, axis=0, keepdims=True)
    log_w = bL - b_c + i_c
    m_new = jnp.maximum(bL + m, jnp.max(log_w, axis=0, keepdims=True))
    decay = jnp.exp(bL + m - m_new)
    w = jnp.exp(log_w - m_new)
    C_new = decay * C + bdot_tn(w * v, k)
    n_new = decay * n + jnp.sum(w * k, axis=0, keepdims=True)
    return C_new, n_new, m_new, h


def _mlstm_in_specs(L, dm, hm, hb, nc, step_of):
    ng = hm // hb

    def chunk(d, j):
        s = step_of(j)
        return s + d * (nc - 1 - 2 * s)
    return [
        pl.BlockSpec((L, hb * dm), lambda d, g, j: (chunk(d, j), g)),
        pl.BlockSpec((L, hb * dm), lambda d, g, j: (chunk(d, j), ng + g)),
        pl.BlockSpec((L, hb * dm), lambda d, g, j: (chunk(d, j), 2 * ng + g)),
        pl.BlockSpec((None, hb, L, 1), lambda d, g, j: (d, g, chunk(d, j), 0)),
        pl.BlockSpec((None, hb, L, 1), lambda d, g, j: (d, g, chunk(d, j), 0)),
        pl.BlockSpec((None, hb, 1, L), lambda d, g, j: (d, g, 0, chunk(d, j))),
        pl.BlockSpec((None, hb, 1, L), lambda d, g, j: (d, g, 0, chunk(d, j))),
        pl.BlockSpec((None, hb, 1, 1), lambda d, g, j: (d, g, 0, 0)),
        pl.BlockSpec((None, hb, 1, 1), lambda d, g, j: (d, g, 0, 0)),
    ], chunk


def mlstm_fwd(z, proj, gates, hm, dm, name, hb=None):
    S = z.shape[0]
    L = CHUNK
    nc = S // L
    hb = hm if hb is None else hb
    in_specs, chunk = _mlstm_in_specs(L, dm, hm, hb, nc, lambda j: j)

    def body(zq, zk, v, ic, fc, ir, fr, bi, bf_, h_ref, cs_ref, ns_ref, ms_ref, C_sc, n_sc, m_sc):
        d = pl.program_id(0)

        @pl.when(pl.program_id(2) == 0)
        def _():
            C_sc[...] = jnp.zeros_like(C_sc)
            n_sc[...] = jnp.zeros_like(n_sc)
            m_sc[...] = jnp.full_like(m_sc, M_INIT)

        for hh in range(hb):
            cols = slice(hh * dm, (hh + 1) * dm)
            C, n, m = C_sc[hh], n_sc[hh], m_sc[hh]
            cs_ref[hh], ns_ref[hh], ms_ref[hh] = C, n, m
            C2, n2, m2, h = _mlstm_step(dm, d, C, n, m, zq[:, cols], zk[:, cols], v[:, cols], ic[hh], fc[hh],
                                        ir[hh], fr[hh], bi[hh], bf_[hh])
            C_sc[hh], n_sc[hh], m_sc[hh] = C2, n2, m2
            h_ref[:, cols] = h

    return pl.pallas_call(
        body, name=name, grid=(2, hm // hb, nc), in_specs=in_specs,
        out_specs=[pl.BlockSpec((None, L, hb * dm), lambda d, g, j: (d, chunk(d, j), g)),
                   pl.BlockSpec((None, hb, None, dm, dm), lambda d, g, j: (d, g, j, 0, 0)),
                   pl.BlockSpec((None, hb, None, 1, dm), lambda d, g, j: (d, g, j, 0, 0)),
                   pl.BlockSpec((None, hb, None, 1, 1), lambda d, g, j: (d, g, j, 0, 0))],
        out_shape=[jax.ShapeDtypeStruct((2, S, hm * dm), f32), jax.ShapeDtypeStruct((2, hm, nc, dm, dm), f32),
                   jax.ShapeDtypeStruct((2, hm, nc, 1, dm), f32), jax.ShapeDtypeStruct((2, hm, nc, 1, 1), f32)],
        scratch_shapes=[pltpu.VMEM((hb, dm, dm), f32), pltpu.VMEM((hb, 1, dm), f32), pltpu.VMEM((hb, 1, 1), f32)],
        compiler_params=_cp(("arbitrary", "arbitrary", "arbitrary")))(z, z, proj, *gates)


def mlstm_bwd(z, proj, gates, states, dh, hm, dm, name, hb=None):
    S = z.shape[0]
    L = CHUNK
    nc = S // L
    hb = hm if hb is None else hb
    in_specs, chunk = _mlstm_in_specs(L, dm, hm, hb, nc, lambda j: nc - 1 - j)
    st = lambda j: nc - 1 - j
    in_specs = in_specs + [
        pl.BlockSpec((None, hb, None, dm, dm), lambda d, g, j: (d, g, st(j), 0, 0)),
        pl.BlockSpec((None, hb, None, 1, dm), lambda d, g, j: (d, g, st(j), 0, 0)),
        pl.BlockSpec((None, hb, None, 1, 1), lambda d, g, j: (d, g, st(j), 0, 0)),
        pl.BlockSpec((L, hb * dm), lambda d, g, j: (chunk(d, j), g)),
    ]

    def body(zq, zk, v, ic, fc, ir, fr, bi, bf_, cs, ns, ms, dh_ref,
             dzq, dzk, dv, dic, dfc, dir_, dfr, dbi, dbf, dC_sc, dn_sc, dm_sc):
        d = pl.program_id(0)

        @pl.when(pl.program_id(2) == 0)
        def _():
            dC_sc[...] = jnp.zeros_like(dC_sc)
            dn_sc[...] = jnp.zeros_like(dn_sc)
            dm_sc[...] = jnp.zeros_like(dm_sc)
            dbi[...] = jnp.zeros_like(dbi)
            dbf[...] = jnp.zeros_like(dbf)

        for hh in range(hb):
            cols = slice(hh * dm, (hh + 1) * dm)
            prim = (cs[hh], ns[hh], ms[hh], zq[:, cols], zk[:, cols], v[:, cols], ic[hh], fc[hh], ir[hh], fr[hh],
                    bi[hh], bf_[hh])
            _, pull = jax.vjp(functools.partial(_mlstm_step, dm, d), *prim)
            g = pull((dC_sc[hh], dn_sc[hh], dm_sc[hh], dh_ref[:, cols]))
            dC_sc[hh], dn_sc[hh], dm_sc[hh] = g[0], g[1], g[2]
            dzq[:, cols], dzk[:, cols], dv[:, cols] = g[3], g[4], g[5]
            dic[hh], dfc[hh], dir_[hh], dfr[hh] = g[6], g[7], g[8], g[9]
            dbi[hh] += g[10]
            dbf[hh] += g[11]

    tile = pl.BlockSpec((None, L, hb * dm), lambda d, g, j: (d, chunk(d, j), g))
    col = pl.BlockSpec((None, hb, L, 1), lambda d, g, j: (d, g, chunk(d, j), 0))
    row = pl.BlockSpec((None, hb, 1, L), lambda d, g, j: (d, g, 0, chunk(d, j)))
    one = pl.BlockSpec((None, hb, 1, 1), lambda d, g, j: (d, g, 0, 0))
    big = jax.ShapeDtypeStruct((2, S, hm * dm), f32)
    cols_ = jax.ShapeDtypeStruct((2, hm, S, 1), f32)
    rows_ = jax.ShapeDtypeStruct((2, hm, 1, S), f32)
    ones_ = jax.ShapeDtypeStruct((2, hm, 1, 1), f32)
    return pl.pallas_call(
        body, name=name, grid=(2, hm // hb, nc), in_specs=in_specs,
        out_specs=[tile, tile, tile, col, col, row, row, one, one],
        out_shape=[big, big, big, cols_, cols_, rows_, rows_, ones_, ones_],
        scratch_shapes=[pltpu.VMEM((hb, dm, dm), f32), pltpu.VMEM((hb, 1, dm), f32), pltpu.VMEM((hb, 1, 1), f32)],
        compiler_params=_cp(("arbitrary", "arbitrary", "arbitrary")))(z, z, proj, *gates, *states, dh)


def _blocks_to_cols(g):
    return g.transpose(1, 0, 2).reshape(g.shape[1], N_DEV * g.shape[2])


def _cols_to_blocks(a):
    return a.reshape(a.shape[0], N_DEV, a.shape[1] // N_DEV).transpose(1, 0, 2)


def _pad_cols(a, n):
    return jnp.pad(a, ((0, 0), (0, n - a.shape[1])))


def _relu2(u):
    r = jnp.maximum(u, 0.0)
    return r * r


def kernel(x, c, positions, w_ada, b_ada, norm_mix_g, w_in, b_gates, conv_w, conv_b, q_lora_g, w_uq, kv_lora_g, w_ukv, q_norm_g, k_norm_g, mlstm_norm_g, w_out, norm_mlp_g, w_ff1, w_ff2, loss_target, m_w_ada, m_b_ada, m_norm_mix_g, m_w_in, m_b_gates, m_conv_w, m_conv_b, m_q_lora_g, m_w_uq, m_kv_lora_g, m_w_ukv, m_q_norm_g, m_k_norm_g, m_mlstm_norm_g, m_w_out, m_norm_mlp_g, m_w_ff1, m_w_ff2, v_w_ada, v_b_ada, v_norm_mix_g, v_w_in, v_b_gates, v_conv_w, v_conv_b, v_q_lora_g, v_w_uq, v_kv_lora_g, v_w_ukv, v_q_norm_g, v_k_norm_g, v_mlstm_norm_g, v_w_out, v_norm_mlp_g, v_w_ff1, v_w_ff2):
    S, D = x.shape[1], x.shape[2]
    QL, KVL = w_uq.shape[1], w_ukv.shape[1]
    H = w_uq.shape[2] * N_DEV // QK_DIM
    HM = mlstm_norm_g.shape[1]
    DM = mlstm_norm_g.shape[2] * N_DEV
    MW = HM * DM
    D_IN = w_in.shape[2] * N_DEV
    NADA = w_ada.shape[2]
    assert D_IN == QL + KVL + ROPE + 4 * MW + N_GATES and DM % LANE == 0 and S % CHUNK == 0
    assert (4 * MW) % QL == 0 and (4 * MW + QL) % KVL == 0 and KVL % LANE == 0
    idx = 4 * lax.axis_index("x") + 2 * lax.axis_index("y") + lax.axis_index("c")
    x2, tgt = x[0], loss_target[0]

    g_in, g_uq, g_ukv, g_conv, g_mn, c_all = all_gather(
        [w_in[0].astype(bf16), w_uq[0].astype(bf16), w_ukv[0].astype(bf16), conv_w[0], mlstm_norm_g[0], c],
        "gather_weights")
    c_all = c_all.reshape(N_DEV, D)
    xi, yi, ci = lax.axis_index("x"), lax.axis_index("y"), lax.axis_index("c")
    slots = jnp.stack([4 * (1 - xi) + 2 * yi + ci, 4 * xi + 2 * (1 - yi) + ci, 4 * (1 - xi) + 2 * (1 - yi) + ci]).astype(jnp.int32)

    wi = _blocks_to_cols(g_in)
    o_cq, o_ckv, o_kpe, o_m, o_g = 0, QL, QL + KVL, QL + KVL + ROPE, QL + KVL + ROPE + 4 * MW
    w_in_p = jnp.concatenate([wi[:, o_m:o_g], wi[:, o_cq:o_kpe], _pad_cols(wi[:, o_kpe:o_m], LANE),
                              _pad_cols(wi[:, o_g:], LANE)], axis=1)
    NP = w_in_p.shape[1]
    cb_cq, cb_ckv, cb_kpe, cb_g = 4 * MW // QL, (4 * MW + QL) // KVL, (4 * MW + QL + KVL) // LANE, NP // LANE - 1
    w_uq_p = jnp.pad(_blocks_to_cols(g_uq).reshape(QL, H, QK_DIM), ((0, 0), (0, 0), (0, QK_PAD - QK_DIM))).reshape(QL, H * QK_PAD)
    w_ukv_p = _blocks_to_cols(g_ukv).reshape(KVL, H, 2, NOPE).transpose(0, 2, 1, 3).reshape(KVL, 2 * H * NOPE)
    conv_w_f = jnp.pad(_blocks_to_cols(g_conv), ((0, 8 - CONV_W), (0, 0)))
    mn_g = _blocks_to_cols(g_mn).reshape(1, MW)
    gqn = _pad_cols(q_norm_g, QK_PAD)
    gkn = _pad_cols(k_norm_g, QK_PAD)
    fr_np = np.zeros((1, LANE), np.float32)
    fr_np[0, :HALF] = fr_np[0, HALF:ROPE] = ROPE_THETA ** (-np.arange(HALF, dtype=np.float32) / HALF)
    freqs = jnp.asarray(fr_np)
    pos = positions.astype(f32).reshape(S, 1)

    b_blk = lax.dynamic_slice(b_ada, (0, idx * NADA), (1, NADA))
    mod_part = ada_fwd(c_all, w_ada[0], b_blk, "ada_fwd")
    (mod_all,) = all_gather([mod_part], "gather_mod")
    mod = lax.dynamic_index_in_dim(mod_all, idx, axis=1, keepdims=False).reshape(1, N_DEV * NADA)
    shift1, scale1, gate1, shift2, scale2, gate2 = [mod[:, k * D:(k + 1) * D] for k in range(6)]

    (h,) = rowwise(f_norm_mod, [Row(x2)], [norm_mix_g, shift1, scale1], [(D, bf16)], n_rows=S, tile=256, name="norm_mix")
    proj = mm(h, w_in_p, name="proj_in", out_dtype=f32)
    r_cq, r_ckv, r_kpe = Row(proj, QL, cb_cq), Row(proj, KVL, cb_ckv), Row(proj, LANE, cb_kpe)
    f_prep = make_f_mla_prep(H, QK_DIM ** -0.5 * math.log2(math.e))
    prep_params = [q_lora_g, kv_lora_g, gqn, gkn, w_uq_p, w_ukv_p, freqs]
    Q, K, V = rowwise(f_prep, [r_cq, r_ckv, r_kpe, Row(pos, diff=False)], prep_params,
                      [(H * QK_PAD, bf16), (H * QK_PAD, bf16), (H * V_DIM, bf16)], n_rows=S, tile=256, name="mla_prep")
    attn, lse, (g_out, g_ff1, g_ff2) = flash_fwd(
        Q, K, V, H, "flash_fwd", side=[w_out[0].astype(bf16), w_ff1[0].astype(bf16), w_ff2[0].astype(bf16)])
    w_out_f = g_out.reshape(N_DEV * g_out.shape[1], D)
    w_ff2_f = g_ff2.reshape(N_DEV * g_ff2.shape[1], D)

    conv_bias = conv_b
    z = conv_fwd(proj, 2 * MW, conv_w_f, conv_bias, "conv_fwd")
    graw = proj[:, cb_g * LANE:cb_g * LANE + N_GATES].reshape(S, 4, HM)
    gcol = graw.transpose(1, 2, 0).reshape(2, 2, HM, S)
    bg = b_gates.reshape(2, 2, HM)
    gates = (gcol[:, 0].reshape(2, HM, S, 1), gcol[:, 1].reshape(2, HM, S, 1),
             gcol[:, 0].reshape(2, HM, 1, S), gcol[:, 1].reshape(2, HM, 1, S),
             bg[:, 0].reshape(2, HM, 1, 1), bg[:, 1].reshape(2, HM, 1, 1))
    hdir, cs, ns, ms = mlstm_fwd(z, proj, gates, HM, DM, "mlstm_fwd")
    f_post = make_f_mlstm_post(HM, DM)
    post_rows = [Row(hdir, MW, 0, lead=0), Row(hdir, MW, 0, lead=1), Row(proj, MW, 3)]
    (ml_out,) = rowwise(f_post, post_rows, [mn_g], [(MW, bf16)], n_rows=S, tile=256, name="mlstm_post")

    cat = jnp.concatenate([attn, ml_out], axis=1)
    mixed = mm(cat, w_out_f, name="proj_out", out_dtype=f32)
    mlp_params = [gate1, norm_mlp_g, shift2, scale2]
    x1, h2 = rowwise(f_resid_norm_mod, [Row(x2), Row(mixed)], mlp_params, [(D, f32), (D, bf16)],
                     n_rows=S, tile=256, name="resid_norm_mlp")
    u = mm(h2, g_ff1, name="ff1", out_dtype=bf16)
    y = mm(u, w_ff2_f, name="ff2", a_fn=_relu2, out_dtype=f32)
    loss_l, d_out, d_y, d_gate2 = loss_head(x1, y, tgt, gate2, "loss_head")
    loss = lax.psum(loss_l[0, 0], AXES)

    dw_ff2 = mm(u, d_y, name="dw_ff2", ta=True, a_fn=_relu2, out_dtype=bf16)
    d_u = mm(d_y, w_ff2_f, name="d_u", tb=True, epi=lambda acc, uu: acc * (2.0 * jnp.maximum(uu.astype(f32), 0.0)),
             extras=(u,), out_dtype=bf16)
    dw_ff1 = mm(h2, d_u, name="dw_ff1", ta=True, out_dtype=bf16, out_blocks=True)
    w_ff1_t = g_ff1.transpose(0, 2, 1).reshape(-1, D)
    d_h2 = mm(d_u, w_ff1_t, name="d_h2", out_dtype=f32)
    (d_x1, d_mixed), (d_gate1, d_g_mlp, d_shift2, d_scale2) = rowwise_vjp(
        f_resid_norm_mod, [Row(x2), Row(mixed)], mlp_params, [Row(d_out), Row(d_h2)],
        n_rows=S, tile=256, name="resid_norm_mlp_bwd", row_grad_dtypes=[f32, bf16])
    dw_out = mm(cat, d_mixed, name="dw_out", ta=True, out_dtype=bf16)
    d_cat = mm(d_mixed, w_out_f, name="d_cat", tb=True, out_dtype=f32, tm=512, tn=2048)

    post_rows_b = [post_rows[0], Row(hdir, MW, 0, lead=1, diff=False), post_rows[2]]
    (dh, d_om), (d_mn_g,) = rowwise_vjp(
        f_post, post_rows_b, [mn_g], [Row(d_cat, MW, H * V_DIM // MW)], n_rows=S, tile=256, name="mlstm_post_bwd")
    dzq, dzk, dvm, dic, dfc, dir_, dfr, dbi, dbf = mlstm_bwd(z, proj, gates, (cs, ns, ms), dh, HM, DM, "mlstm_bwd")
    (dz,) = rowwise(f_add_pairs, [Row(dzq, MW, 0, lead=0), Row(dzq, MW, 0, lead=1), Row(dzk, MW, 0, lead=0),
                                  Row(dzk, MW, 0, lead=1)], [], [(2 * MW, f32)], n_rows=S, tile=256, name="dz_sum")
    (d_vm,) = rowwise(f_add, [Row(dvm, MW, 0, lead=0), Row(dvm, MW, 0, lead=1)], [], [(MW, bf16)], n_rows=S, tile=256,
                      name="dv_sum")
    d_qk, d_conv_w, d_conv_b = conv_bwd(dz, proj, 2 * MW, conv_w_f, "conv_bwd")
    dg = jnp.stack([dic.reshape(2, HM, S) + dir_.reshape(2, HM, S), dfc.reshape(2, HM, S) + dfr.reshape(2, HM, S)], axis=1)
    d_gates = dg.reshape(4 * HM, S).T
    d_b_gates = jnp.stack([dbi.reshape(2, HM), dbf.reshape(2, HM)], axis=1).reshape(1, N_GATES)

    mlp_g = [dw_out.reshape(N_DEV, -1, D), dw_ff1, dw_ff2.reshape(N_DEV, -1, D)]
    mlp_tags = ["w_out", "w_ff1", "w_ff2"]
    mlp_sib = pair_exchange(mlp_g, "grad_pair_exchange_mlp")
    mlp_part = [chip_partials(g, r, slots, "grad_chip_partials_" + t) for g, r, t in zip(mlp_g, mlp_sib, mlp_tags)]
    dq, dk, dv, mlp_chips = flash_bwd(Q, K, V, attn, lse.reshape(H, 1, S), d_cat, 0, H, "flash_bwd", side=mlp_part)
    (d_cq, d_ckv, d_kpe), (d_gq, d_gkv, d_gqn, d_gkn, dw_uq_p, dw_ukv_p) = rowwise_vjp(
        f_prep, [r_cq, r_ckv, r_kpe, Row(pos, diff=False)], prep_params, [Row(dq), Row(dk), Row(dv)],
        n_rows=S, tile=256, name="mla_prep_bwd", row_grad_dtypes=[bf16, bf16, bf16],
        param_diff=[True, True, True, True, True, True, False])

    d_proj = jnp.concatenate([d_qk.astype(bf16), d_vm, d_om.astype(bf16), d_cq, d_ckv, d_kpe,
                              _pad_cols(d_gates.astype(bf16), LANE)], axis=1)
    dw_in_p = mm(h, d_proj, name="dw_in", ta=True, out_dtype=bf16)

    dwi = jnp.concatenate([dw_in_p[:, 4 * MW:4 * MW + QL + KVL + ROPE], dw_in_p[:, :4 * MW],
                           dw_in_p[:, cb_g * LANE:cb_g * LANE + N_GATES]], axis=1)
    dw_uq = dw_uq_p.reshape(QL, H, QK_PAD)[:, :, :QK_DIM].reshape(QL, H * QK_DIM)
    dw_ukv = dw_ukv_p.reshape(KVL, 2, H, NOPE).transpose(0, 2, 1, 3).reshape(KVL, 2 * H * NOPE)
    tiny = [(w_uq, m_w_uq, v_w_uq, _cols_to_blocks(dw_uq)),
            (w_ukv, m_w_ukv, v_w_ukv, _cols_to_blocks(dw_ukv)),
            (conv_w, m_conv_w, v_conv_w, _cols_to_blocks(d_conv_w[:CONV_W])),
            (mlstm_norm_g, m_mlstm_norm_g, v_mlstm_norm_g, _cols_to_blocks(d_mn_g.reshape(HM, DM)))]
    tsizes = [int(np.prod(b[0].shape)) for b in tiny]
    T = sum(tsizes)
    PC = 512
    PR = -(-T // (PC * 64)) * 64
    gpack = jnp.concatenate([b[3].astype(bf16).reshape(N_DEV, -1) for b in tiny], axis=1)
    gpack = jnp.pad(gpack, ((0, 0), (0, PR * PC - T))).reshape(N_DEV, PR, PC)
    wpack = lambda k: jnp.pad(jnp.concatenate([b[k].reshape(1, -1) for b in tiny], axis=1),
                              ((0, 0), (0, PR * PC - T))).reshape(PR, PC)
    late_g = [_cols_to_blocks(dwi), gpack]
    late_sib = pair_exchange(late_g, "grad_pair_exchange")
    late_part = [chip_partials(g, r, slots, "grad_chip_partials_" + t) for g, r, t in zip(late_g, late_sib, ["w_in", "tiny"])]
    d_h, late_chips = mm(d_proj, w_in_p, name="d_h", tb=True, out_dtype=f32, tm=512, tn=2048, tk=2560, side=late_part)
    (grad_x,), (d_g_mix, d_shift1, d_scale1) = rowwise_vjp(
        f_norm_mod_thru, [Row(x2)], [norm_mix_g, shift1, scale1], [Row(d_h), Row(d_x1)],
        n_rows=S, tile=256, name="norm_mix_bwd")

    dmod = jnp.concatenate([d_shift1, d_scale1, d_gate1, d_shift2, d_scale2, d_gate2], axis=1)
    small = [(norm_mix_g, m_norm_mix_g, v_norm_mix_g, d_g_mix), (b_gates, m_b_gates, v_b_gates, d_b_gates),
             (conv_b, m_conv_b, v_conv_b, d_conv_b), (q_lora_g, m_q_lora_g, v_q_lora_g, d_gq),
             (kv_lora_g, m_kv_lora_g, v_kv_lora_g, d_gkv), (q_norm_g, m_q_norm_g, v_q_norm_g, d_gqn[:, :QK_DIM]),
             (k_norm_g, m_k_norm_g, v_k_norm_g, d_gkn[:, :QK_DIM]), (norm_mlp_g, m_norm_mlp_g, v_norm_mlp_g, d_g_mlp),
             (b_ada, m_b_ada, v_b_ada, dmod)]
    sizes = [s[0].shape[1] for s in small]
    P = sum(sizes)
    PP = -(-P // LANE) * LANE
    pack = lambda k: _pad_cols(jnp.concatenate([s[k] for s in small], axis=1), PP)
    (sg_all,) = all_gather([pack(3)], "gather_small_grads")
    s_out = adamw([sg_all[k] for k in range(N_DEV)], pack(0), pack(1), pack(2), "adamw_small")
    offs = np.concatenate([[0], np.cumsum(sizes)])
    small_out = [[o[:, offs[k]:offs[k + 1]] for o in s_out] for k in range(len(small))]

    dmod_all = sg_all[:, 0, offs[-2]:offs[-1]]
    dmod_blk = lax.dynamic_slice(dmod_all, (0, idx * NADA), (N_DEV, NADA))
    g_w_ada = ada_wgrad(c_all, dmod_blk, "ada_wgrad")
    ada_out = adamw([g_w_ada], w_ada[0], m_w_ada[0], v_w_ada[0], "adamw_ada")

    large = [(w_in[0], m_w_in[0], v_w_in[0], late_g[0]),
             (w_out[0], m_w_out[0], v_w_out[0], mlp_g[0]),
             (w_ff1[0], m_w_ff1[0], v_w_ff1[0], mlp_g[1]),
             (w_ff2[0], m_w_ff2[0], v_w_ff2[0], mlp_g[2]),
             (wpack(0), wpack(1), wpack(2), gpack)]
    tags = ["w_in", "w_out", "w_ff1", "w_ff2", "tiny"]
    from_sibling = [late_sib[0]] + list(mlp_sib) + [late_sib[1]]
    from_chips = [late_chips[0]] + list(mlp_chips) + [late_chips[1]]
    l_out = []
    for (w_, m_, v_, g), r, fc, t in zip(large, from_sibling, from_chips, tags):
        mine = lax.dynamic_index_in_dim(g, idx, axis=0, keepdims=False)
        sib = lax.dynamic_index_in_dim(r, 2 * xi + yi, axis=0, keepdims=False)
        l_out.append(adamw([mine, sib, fc[0], fc[1], fc[2]], w_, m_, v_, "adamw_" + t))
    toffs = np.concatenate([[0], np.cumsum(tsizes)])
    tiny_out = [[o.reshape(-1)[toffs[k]:toffs[k + 1]].reshape(tiny[k][0].shape) for o in l_out[4]] for k in range(len(tiny))]
    big_out = [[o[None] for o in l_out[0]], tiny_out[0], tiny_out[1], [o[None] for o in l_out[1]],
               [o[None] for o in l_out[2]], [o[None] for o in l_out[3]], tiny_out[2], tiny_out[3]]

    names = ["w_ada", "b_ada", "norm_mix_g", "w_in", "b_gates", "conv_w", "conv_b", "q_lora_g", "w_uq", "kv_lora_g",
             "w_ukv", "q_norm_g", "k_norm_g", "mlstm_norm_g", "w_out", "norm_mlp_g", "w_ff1", "w_ff2"]
    res = {"w_ada": [o[None] for o in ada_out]}
    for k, nm in enumerate(["norm_mix_g", "b_gates", "conv_b", "q_lora_g", "kv_lora_g", "q_norm_g", "k_norm_g",
                            "norm_mlp_g", "b_ada"]):
        res[nm] = small_out[k]
    for k, nm in enumerate(["w_in", "w_uq", "w_ukv", "w_out", "w_ff1", "w_ff2", "conv_w", "mlstm_norm_g"]):
        res[nm] = big_out[k]
    outs = [loss, grad_x[None]]
    for part in range(4):
        outs += [res[nm][part] for nm in names]
    return tuple(outs)
```

```python
import functools
import math

import numpy as np
import jax
import jax.numpy as jnp
from jax import lax
from jax.experimental import pallas as pl
from jax.experimental.pallas import tpu as pltpu

f32 = jnp.float32
bf16 = jnp.bfloat16

N_DEV = 8
AXES = ("x", "y", "c")
MESH = pl.DeviceIdType.MESH

NOPE = 128
ROPE = 64
HALF = ROPE // 2
QK_DIM = NOPE + ROPE
QK_PAD = 256
V_DIM = 128
ROPE_THETA = 10000.0
CHUNK = 128
CONV_W = 5
N_GATES = 16
EPS = 1e-6
M_INIT = -1e30

ADAM_LR, ADAM_B1, ADAM_B2, ADAM_EPS, ADAM_WD, ADAM_STEP = 0.001, 0.9, 0.999, 1e-08, 0.01, 10

LANE = 128
VMEM_LIMIT = 56 * 1024 * 1024


def _cp(sem=None, vmem=VMEM_LIMIT):
    return pltpu.CompilerParams(dimension_semantics=sem, vmem_limit_bytes=vmem)


def _pick(n, target):
    best = None
    t = LANE
    while t <= min(n, target):
        if n % t == 0:
            best = t
        t += LANE
    return best if best is not None else n


def _pick_rows(n, target):
    t = min(n, target)
    while n % t:
        t -= 8
    return t


def _make_dots(cast, precision):
    def dg(a, b, ca, cb):
        if cast is not None:
            a = a.astype(cast)
            b = b.astype(cast)
        return lax.dot_general(a, b, (((ca,), (cb,)), ((), ())), precision=precision, preferred_element_type=f32)

    @jax.custom_vjp
    def nn(a, b):
        return dg(a, b, 1, 0)

    def nn_f(a, b):
        return dg(a, b, 1, 0), (a, b)

    def nn_b(res, g):
        a, b = res
        return dg(g, b, 1, 1).astype(a.dtype), dg(a, g, 0, 0).astype(b.dtype)

    nn.defvjp(nn_f, nn_b)

    @jax.custom_vjp
    def nt(a, b):
        return dg(a, b, 1, 1)

    def nt_f(a, b):
        return dg(a, b, 1, 1), (a, b)

    def nt_b(res, g):
        a, b = res
        return dg(g, b, 1, 0).astype(a.dtype), dg(g, a, 0, 0).astype(b.dtype)

    nt.defvjp(nt_f, nt_b)

    @jax.custom_vjp
    def tn(a, b):
        return dg(a, b, 0, 0)

    def tn_f(a, b):
        return dg(a, b, 0, 0), (a, b)

    def tn_b(res, g):
        a, b = res
        return dg(b, g, 1, 1).astype(a.dtype), dg(a, g, 1, 0).astype(b.dtype)

    tn.defvjp(tn_f, tn_b)
    return nn, nt, tn


bdot, bdot_nt, bdot_tn = _make_dots(bf16, None)
hdot, hdot_nt, hdot_tn = _make_dots(None, lax.Precision.HIGHEST)


def _silu(x):
    return x * jax.nn.sigmoid(x)


def _rms(x, n):
    return x * lax.rsqrt(jnp.sum(x * x, axis=-1, keepdims=True) * (1.0 / n) + EPS)


def _place():
    return lax.axis_index("x"), lax.axis_index("y"), lax.axis_index("c")


def _gather_phases(ins, outs, send_sems, recv_sems, local_sems):
    n = len(ins)
    x, y, c = _place()
    me, sibling = (x, y, c), (x, y, 1 - c)
    chips = [(1 - x, y), (x, 1 - y), (1 - x, 1 - y)]

    def slot(o, p):
        return outs[o].at[4 * p[0] + 2 * p[1] + p[2]]

    def copy(o, k, block, to, src=None):
        dst = slot(o, block)
        return pltpu.make_async_remote_copy(
            src_ref=dst if src is None else src, dst_ref=dst,
            send_sem=send_sems.at[o, k], recv_sem=recv_sems.at[o, k],
            device_id=to, device_id_type=MESH)

    def local(o):
        return pltpu.make_async_copy(ins[o], slot(o, me), local_sems.at[o])

    def first(o):
        return [copy(o, 0, me, sibling, src=ins[o])] + [copy(o, 1 + j, me, (*chip, c), src=ins[o])
                                                        for j, chip in enumerate(chips)]

    def start():
        for o in range(n):
            local(o).start()
        for o in range(n):
            for cp in first(o):
                cp.start()

    def mid():
        for o in range(n):
            for j, chip in enumerate(chips):
                copy(o, 1 + j, (*chip, c), me).wait_recv()
                copy(o, 4 + j, (*chip, c), sibling).start()

    def finish():
        for o in range(n):
            copy(o, 0, sibling, me).wait_recv()
            for j, chip in enumerate(chips):
                copy(o, 4 + j, (*chip, 1 - c), me).wait_recv()
        for o in range(n):
            for cp in first(o):
                cp.wait_send()
            for j, chip in enumerate(chips):
                copy(o, 4 + j, (*chip, c), sibling).wait_send()
        for o in range(n):
            local(o).wait()

    return start, mid, finish


def _gather_scratch(n):
    return [pltpu.SemaphoreType.DMA((n, 7)), pltpu.SemaphoreType.DMA((n, 7)), pltpu.SemaphoreType.DMA((n,))]


def all_gather(ops, name):
    n = len(ops)

    def body(*refs):
        start, mid, finish = _gather_phases(refs[:n], refs[n:2 * n], *refs[2 * n:])
        start()
        mid()
        finish()

    anyspec = pl.BlockSpec(memory_space=pl.ANY)
    return pl.pallas_call(
        body, name=name,
        out_shape=[jax.ShapeDtypeStruct((N_DEV,) + o.shape, o.dtype) for o in ops],
        in_specs=[anyspec] * n, out_specs=[anyspec] * n,
        scratch_shapes=_gather_scratch(n),
    )(*ops)


def pair_exchange(gs, name):
    n = len(gs)

    def body(*refs):
        start, finish = _pair_exchange_phases(refs[:n], refs[n:2 * n], *refs[2 * n:])
        start()
        finish()

    anyspec = pl.BlockSpec(memory_space=pl.ANY)
    return pl.pallas_call(
        body, name=name, out_shape=[jax.ShapeDtypeStruct(_pair_exchange_shape(g), g.dtype) for g in gs],
        in_specs=[anyspec] * n, out_specs=[anyspec] * n,
        scratch_shapes=_pair_exchange_scratch(n),
    )(*gs)


def _pair_exchange_phases(g_refs, out_refs, send_sems, recv_sems):
    n = len(g_refs)
    x, y, c = _place()

    def copies():
        return [pltpu.make_async_remote_copy(
            src_ref=g_refs[o].at[2 * q + (1 - c)], dst_ref=out_refs[o].at[q],
            send_sem=send_sems.at[o, q], recv_sem=recv_sems.at[o, q],
            device_id=(x, y, 1 - c), device_id_type=MESH) for o in range(n) for q in range(4)]

    def start():
        for cp in copies():
            cp.start()

    def finish():
        for cp in copies():
            cp.wait_recv()
        for cp in copies():
            cp.wait_send()

    return start, finish


def _pair_exchange_scratch(n):
    return [pltpu.SemaphoreType.DMA((n, 4)), pltpu.SemaphoreType.DMA((n, 4))]


def _pair_exchange_shape(g):
    return (4,) + g.shape[1:]


def _chip_exchange_phases(p_refs, out_refs, send_sems, recv_sems):
    n = len(p_refs)
    x, y, c = _place()
    chips = [(1 - x, y), (x, 1 - y), (1 - x, 1 - y)]

    def copies():
        return [pltpu.make_async_remote_copy(
            src_ref=p_refs[o].at[j], dst_ref=out_refs[o].at[j],
            send_sem=send_sems.at[o, j], recv_sem=recv_sems.at[o, j],
            device_id=(*chip, c), device_id_type=MESH) for o in range(n) for j, chip in enumerate(chips)]

    def start():
        for cp in copies():
            cp.start()

    def finish():
        for cp in copies():
            cp.wait_recv()
        for cp in copies():
            cp.wait_send()

    return start, finish


def _chip_exchange_scratch(n):
    return [pltpu.SemaphoreType.DMA((n, 3)), pltpu.SemaphoreType.DMA((n, 3))]


def chip_partials(g, recv, slots, name):
    _, R, C = g.shape
    tr = _pick_rows(R, 512)

    def body(s_ref, a_ref, b_ref, o_ref):
        o_ref[...] = (a_ref[...].astype(f32) + b_ref[...].astype(f32)).astype(o_ref.dtype)

    grid_spec = pltpu.PrefetchScalarGridSpec(
        num_scalar_prefetch=1, grid=(3, R // tr),
        in_specs=[pl.BlockSpec((None, tr, C), lambda j, i, s: (s[j], i, 0)),
                  pl.BlockSpec((None, tr, C), lambda j, i, s: (s[j] // 2, i, 0))],
        out_specs=pl.BlockSpec((None, tr, C), lambda j, i, s: (j, i, 0)))
    return pl.pallas_call(body, name=name, grid_spec=grid_spec,
                          out_shape=jax.ShapeDtypeStruct((3, R, C), g.dtype),
                          compiler_params=_cp(("arbitrary", "arbitrary")))(slots, g, recv)


def adamw(parts, w, m, v, name, rows=256):
    R, C = w.shape
    tr = _pick_rows(R, rows)
    npart = len(parts)
    c1 = 1.0 - ADAM_B1 ** ADAM_STEP
    c2 = 1.0 - ADAM_B2 ** ADAM_STEP

    def body(*refs):
        p_refs = refs[:npart]
        w_ref, m_ref, v_ref, g_out, d_out, m_out, v_out = refs[npart:]
        g = p_refs[0][...].astype(f32)
        for p in p_refs[1:]:
            g = g + p[...].astype(f32)
        mn = ADAM_B1 * m_ref[...] + (1.0 - ADAM_B1) * g
        vn = ADAM_B2 * v_ref[...] + (1.0 - ADAM_B2) * (g * g)
        m_hat = mn / c1
        v_hat = vn / c2
        g_out[...] = g
        d_out[...] = -ADAM_LR * (m_hat / (jnp.sqrt(v_hat) + ADAM_EPS) + ADAM_WD * w_ref[...])
        m_out[...] = mn
        v_out[...] = vn

    spec = pl.BlockSpec((tr, C), lambda i: (i, 0))
    return pl.pallas_call(
        body, name=name, grid=(R // tr,),
        in_specs=[spec] * (npart + 3), out_specs=[spec] * 4,
        out_shape=[jax.ShapeDtypeStruct((R, C), f32)] * 4,
        compiler_params=_cp(("arbitrary",)))(*parts, w, m, v)


def mm(a, b, *, name, ta=False, tb=False, a_fn=None, epi=None, extras=(), out_dtype=f32, out_blocks=False, side=(),
       side_pair=False, tm=1024, tn=1024, tk=2048):
    K, M = a.shape if ta else a.shape[::-1]
    b3 = b.ndim == 3
    if b3:
        assert not tb
        N, K2 = N_DEV * b.shape[2], b.shape[1]
    else:
        N, K2 = b.shape if tb else b.shape[::-1]
    assert K == K2, (a.shape, b.shape, ta, tb)
    n_split = N // N_DEV if (out_blocks or b3) else N
    tm, tn, tk = _pick(M, tm), _pick(n_split, tn), _pick(K, tk)
    nb = n_split // tn
    nk = K // tk
    ne = len(extras)
    assert not (out_blocks and ne)
    dims = (((0 if ta else 1,), (1 if tb else 0,)), ((), ()))

    ns = len(side)
    n_steps = (M // tm) * (N // tn) * nk
    assert ns == 0 or n_steps >= 2
    if side_pair:
        side_phases, side_scratch, side_shape = _pair_exchange_phases, _pair_exchange_scratch, _pair_exchange_shape
    else:
        side_phases, side_scratch, side_shape = _chip_exchange_phases, _chip_exchange_scratch, lambda p: p.shape

    def body(a_ref, b_ref, *rest):
        e_refs, o_ref, acc = rest[:ne], rest[ne + ns], rest[ne + 2 * ns + 1]
        k = pl.program_id(2)
        if ns:
            step = (pl.program_id(0) * (N // tn) + pl.program_id(1)) * nk + k
            x_start, x_finish = side_phases(rest[ne:ne + ns], rest[ne + ns + 1:ne + 2 * ns + 1], *rest[ne + 2 * ns + 2:])
            pl.when(step == 0)(x_start)

        @pl.when(k == 0)
        def _():
            acc[...] = jnp.zeros_like(acc)

        av = a_ref[...]
        if a_fn is not None:
            av = a_fn(av.astype(f32))
        acc[...] += lax.dot_general(av.astype(bf16), b_ref[...].astype(bf16), dims, preferred_element_type=f32)

        @pl.when(k == nk - 1)
        def _():
            r = acc[...]
            if epi is not None:
                r = epi(r, *[e[...] for e in e_refs])
            o_ref[...] = r.astype(o_ref.dtype)

        if ns:
            pl.when(step == n_steps - 1)(x_finish)

    a_spec = pl.BlockSpec((tk, tm), lambda i, j, k: (k, i)) if ta else pl.BlockSpec((tm, tk), lambda i, j, k: (i, k))
    if b3:
        b_spec = pl.BlockSpec((None, tk, tn), lambda i, j, k: (j // nb, k, j % nb))
    else:
        b_spec = pl.BlockSpec((tn, tk), lambda i, j, k: (j, k)) if tb else pl.BlockSpec((tk, tn), lambda i, j, k: (k, j))
    if out_blocks:
        o_spec = pl.BlockSpec((None, tm, tn), lambda i, j, k: (j // nb, i, j % nb))
        o_shape = jax.ShapeDtypeStruct((N_DEV, M, N // N_DEV), out_dtype)
    else:
        o_spec = pl.BlockSpec((tm, tn), lambda i, j, k: (i, j))
        o_shape = jax.ShapeDtypeStruct((M, N), out_dtype)
    if not ns:
        return pl.pallas_call(
            body, name=name, grid=(M // tm, N // tn, nk),
            in_specs=[a_spec, b_spec] + [o_spec] * ne, out_specs=o_spec,
            out_shape=o_shape,
            scratch_shapes=[pltpu.VMEM((tm, tn), f32)],
            compiler_params=_cp(("parallel", "parallel", "arbitrary")))(a, b, *extras)
    anyspec = pl.BlockSpec(memory_space=pl.ANY)
    res = pl.pallas_call(
        body, name=name, grid=(M // tm, N // tn, nk),
        in_specs=[a_spec, b_spec] + [o_spec] * ne + [anyspec] * ns, out_specs=[o_spec] + [anyspec] * ns,
        out_shape=[o_shape] + [jax.ShapeDtypeStruct(side_shape(p), p.dtype) for p in side],
        scratch_shapes=[pltpu.VMEM((tm, tn), f32)] + side_scratch(ns),
        compiler_params=_cp(("arbitrary", "arbitrary", "arbitrary")))(a, b, *extras, *side)
    return res[0], list(res[1:])


class Row:
    def __init__(self, arr, width=None, col=0, lead=None, diff=True):
        self.arr, self.col, self.lead, self.diff = arr, col, lead, diff
        self.width = arr.shape[-1] if width is None else width

    def spec(self, t):
        col, lead = self.col, self.lead
        if lead is None:
            return pl.BlockSpec((t, self.width), lambda i: (i, col))
        return pl.BlockSpec((None, t, self.width), lambda i: (lead, i, col))


def _whole(p):
    return pl.BlockSpec(p.shape, lambda i: (0,) * p.ndim)


def rowwise(fn, rows, params, outs, *, n_rows, tile, name):
    t = _pick_rows(n_rows, tile)
    nr, npar, no = len(rows), len(params), len(outs)

    def body(*refs):
        r_refs, p_refs, o_refs = refs[:nr], refs[nr:nr + npar], refs[nr + npar:]
        res = fn(*[r[...].astype(f32) for r in r_refs], *[p[...] for p in p_refs])
        for o_ref, val in zip(o_refs, res):
            o_ref[...] = val.astype(o_ref.dtype)

    return pl.pallas_call(
        body, name=name, grid=(n_rows // t,),
        in_specs=[r.spec(t) for r in rows] + [_whole(p) for p in params],
        out_specs=[pl.BlockSpec((t, w), lambda i: (i, 0)) for w, _ in outs],
        out_shape=[jax.ShapeDtypeStruct((n_rows, w), dt) for w, dt in outs],
        compiler_params=_cp(("arbitrary",)))(*[r.arr for r in rows], *params)


def rowwise_vjp(fn, rows, params, cts, *, n_rows, tile, name, row_grad_dtypes=None, param_diff=None):
    t = _pick_rows(n_rows, tile)
    nr, npar, nc = len(rows), len(params), len(cts)
    param_diff = [True] * npar if param_diff is None else param_diff
    d_rows = [k for k, r in enumerate(rows) if r.diff]
    d_pars = [k for k in range(npar) if param_diff[k]]
    row_grad_dtypes = [f32] * len(d_rows) if row_grad_dtypes is None else row_grad_dtypes

    def body(*refs):
        r_refs, p_refs = refs[:nr], refs[nr:nr + npar]
        c_refs = refs[nr + npar:nr + npar + nc]
        dr_refs = refs[nr + npar + nc:nr + npar + nc + len(d_rows)]
        dp_refs = refs[nr + npar + nc + len(d_rows):]
        rv = [r[...].astype(f32) for r in r_refs]
        pv = [p[...] for p in p_refs]

        def g(*dvals):
            full_r, full_p = list(rv), list(pv)
            for k, val in zip(d_rows, dvals[:len(d_rows)]):
                full_r[k] = val
            for k, val in zip(d_pars, dvals[len(d_rows):]):
                full_p[k] = val
            return tuple(fn(*full_r, *full_p))

        prim = [rv[k] for k in d_rows] + [pv[k].astype(f32) for k in d_pars]
        _, pull = jax.vjp(g, *prim)
        grads = pull(tuple(c[...].astype(f32) for c in c_refs))
        for ref, val in zip(dr_refs, grads[:len(d_rows)]):
            ref[...] = val.astype(ref.dtype)

        @pl.when(pl.program_id(0) == 0)
        def _():
            for ref in dp_refs:
                ref[...] = jnp.zeros_like(ref)

        for ref, val in zip(dp_refs, grads[len(d_rows):]):
            ref[...] += val

    out_specs = [pl.BlockSpec((t, rows[k].width), lambda i: (i, 0)) for k in d_rows]
    out_specs += [_whole(params[k]) for k in d_pars]
    out_shape = [jax.ShapeDtypeStruct((n_rows, rows[k].width), dt) for k, dt in zip(d_rows, row_grad_dtypes)]
    out_shape += [jax.ShapeDtypeStruct(params[k].shape, f32) for k in d_pars]
    res = pl.pallas_call(
        body, name=name, grid=(n_rows // t,),
        in_specs=[r.spec(t) for r in rows] + [_whole(p) for p in params] + [c.spec(t) for c in cts],
        out_specs=out_specs, out_shape=out_shape,
        compiler_params=_cp(("arbitrary",)))(*[r.arr for r in rows], *params, *[c.arr for c in cts])
    return res[:len(d_rows)], res[len(d_rows):]


def f_norm_mod(x, g, shift, scale):
    return (_rms(x, x.shape[-1]) * g * (1.0 + scale) + shift,)


def f_norm_mod_thru(x, g, shift, scale):
    return f_norm_mod(x, g, shift, scale) + (x,)


def f_resid_norm_mod(x, mixed, gate1, g2, shift2, scale2):
    x1 = x + gate1 * mixed
    return (x1,) + f_norm_mod(x1, g2, shift2, scale2)


def _rope_rot():
    i = lax.broadcasted_iota(jnp.int32, (LANE, LANE), 0)
    j = lax.broadcasted_iota(jnp.int32, (LANE, LANE), 1)
    neg = jnp.where((i == j + HALF) & (j < HALF), -1.0, 0.0)
    pos = jnp.where((i == j - HALF) & (j >= HALF) & (j < ROPE), 1.0, 0.0)
    return (neg + pos).astype(f32)


def make_f_mla_prep(n_heads, q_scale):
    def fn(cq, ckv, kpe, pos, gq, gkv, gqn, gkn, w_uq, w_ukv, freqs):
        rot = _rope_rot()
        ang = pos * freqs
        cos, sin = jnp.cos(ang), jnp.sin(ang)

        def rope(u):
            return u * cos + hdot(u, rot) * sin

        qraw = bdot(_rms(cq, cq.shape[-1]) * gq, w_uq)
        kv = bdot(_rms(ckv, ckv.shape[-1]) * gkv, w_ukv)
        kpe_ss = jnp.sum(kpe * kpe, axis=-1, keepdims=True)
        qs, ks = [], []
        for h in range(n_heads):
            qh = _rms(qraw[:, h * QK_PAD:(h + 1) * QK_PAD], QK_DIM) * gqn
            qs += [qh[:, :NOPE], rope(qh[:, NOPE:])]
            kn = kv[:, h * NOPE:(h + 1) * NOPE]
            r = lax.rsqrt((jnp.sum(kn * kn, axis=-1, keepdims=True) + kpe_ss) * (1.0 / QK_DIM) + EPS)
            ks += [kn * r * gkn[:, :NOPE], rope(kpe * r * gkn[:, NOPE:])]
        return jnp.concatenate(qs, axis=-1) * q_scale, jnp.concatenate(ks, axis=-1), kv[:, n_heads * NOPE:]
    return fn


def make_f_mlstm_post(n_heads, dm):
    def fn(hf, hb, o, g):
        hm = hf + hb
        outs = []
        for h in range(n_heads):
            sl = slice(h * dm, (h + 1) * dm)
            outs.append(jax.nn.sigmoid(o[:, sl]) * (_rms(hm[:, sl], dm) * g[:, sl]))
        return (jnp.concatenate(outs, axis=-1),)
    return fn


def f_add(a, b):
    return (a + b,)


def loss_head(x1, y, target, gate2, name, tile=256):
    S, D = x1.shape
    t = _pick_rows(S, tile)

    def body(x1_ref, y_ref, t_ref, g_ref, loss_ref, dout_ref, dy_ref, dgate_ref):
        @pl.when(pl.program_id(0) == 0)
        def _():
            loss_ref[...] = jnp.zeros_like(loss_ref)
            dgate_ref[...] = jnp.zeros_like(dgate_ref)

        yv, gv = y_ref[...], g_ref[...]
        e = x1_ref[...] + gv * yv - t_ref[...]
        loss_ref[...] += 0.5 * jnp.sum(jnp.sum(e * e, axis=-1, keepdims=True) * (1.0 / D), axis=0, keepdims=True)
        d_out = e * (1.0 / D)
        dout_ref[...] = d_out
        dy_ref[...] = (d_out * gv).astype(dy_ref.dtype)
        dgate_ref[...] += jnp.sum(d_out * yv, axis=0, keepdims=True)

    row = pl.BlockSpec((t, D), lambda i: (i, 0))
    return pl.pallas_call(
        body, name=name, grid=(S // t,),
        in_specs=[row, row, row, pl.BlockSpec((1, D), lambda i: (0, 0))],
        out_specs=[pl.BlockSpec((1, 1), lambda i: (0, 0)), row, row, pl.BlockSpec((1, D), lambda i: (0, 0))],
        out_shape=[jax.ShapeDtypeStruct((1, 1), f32), jax.ShapeDtypeStruct((S, D), f32),
                   jax.ShapeDtypeStruct((S, D), bf16), jax.ShapeDtypeStruct((1, D), f32)],
        compiler_params=_cp(("arbitrary",)))(x1, y, target, gate2)


def ada_fwd(c_all, w_blk, b_blk, name):
    B, D = c_all.shape
    N = w_blk.shape[1]
    tn = _pick(N, 512)

    def body(c_ref, w_ref, b_ref, o_ref):
        o_ref[...] = bdot(_silu(c_ref[...]), w_ref[...]) + b_ref[...]

    return pl.pallas_call(
        body, name=name, grid=(N // tn,),
        in_specs=[pl.BlockSpec((B, D), lambda j: (0, 0)), pl.BlockSpec((D, tn), lambda j: (0, j)),
                  pl.BlockSpec((1, tn), lambda j: (0, j))],
        out_specs=pl.BlockSpec((B, tn), lambda j: (0, j)),
        out_shape=jax.ShapeDtypeStruct((B, N), f32), compiler_params=_cp(("arbitrary",)))(c_all, w_blk, b_blk)


def ada_wgrad(c_all, dmod_blk, name):
    B, D = c_all.shape
    N = dmod_blk.shape[1]
    tn = _pick(N, 512)

    def body(c_ref, d_ref, o_ref):
        o_ref[...] = hdot_tn(_silu(c_ref[...]), d_ref[...])

    return pl.pallas_call(
        body, name=name, grid=(N // tn,),
        in_specs=[pl.BlockSpec((B, D), lambda j: (0, 0)), pl.BlockSpec((B, tn), lambda j: (0, j))],
        out_specs=pl.BlockSpec((D, tn), lambda j: (0, j)),
        out_shape=jax.ShapeDtypeStruct((D, N), f32), compiler_params=_cp(("arbitrary",)))(c_all, dmod_blk)


def _nt(a, b):
    return lax.dot_general(a, b, (((1,), (1,)), ((), ())), preferred_element_type=f32)


def _tn(a, b):
    return lax.dot_general(a, b, (((0,), (0,)), ((), ())), preferred_element_type=f32)


def flash_fwd(q, k, v, n_heads, name, side=(), tq=512, tk=8192, sub=1024):
    S = q.shape[0]
    tq, tk = _pick(S, tq), _pick(S, tk)
    sub = _pick(tk, sub)
    nk, nsub = S // tk, tk // sub
    ns = len(side)
    n_steps = n_heads * (S // tq) * nk
    assert ns == 0 or n_steps >= 3

    def body(*refs):
        q_ref, k_ref, v_ref = refs[:3]
        o_ref, lse_ref = refs[3 + ns:5 + ns]
        m_sc, l_sc, acc_sc = refs[5 + 2 * ns:8 + 2 * ns]
        j = pl.program_id(2)
        step = (pl.program_id(0) * (S // tq) + pl.program_id(1)) * nk + j
        if ns:
            g_start, g_mid, g_finish = _gather_phases(refs[3:3 + ns], refs[5 + ns:5 + 2 * ns], *refs[8 + 2 * ns:])
            pl.when(step == 0)(g_start)
            pl.when(step == n_steps // 2)(g_mid)

        @pl.when(j == 0)
        def _():
            m_sc[...] = jnp.full_like(m_sc, -jnp.inf)
            l_sc[...] = jnp.zeros_like(l_sc)
            acc_sc[...] = jnp.zeros_like(acc_sc)

        qv = q_ref[...]
        m = m_sc[...]
        ss = [_nt(qv, k_ref[b * sub:(b + 1) * sub, :]) for b in range(nsub)]
        mx = ss[0]
        for s in ss[1:]:
            mx = jnp.maximum(mx, s)
        m_new = jnp.maximum(m, jnp.max(mx, axis=-1, keepdims=True))
        alpha = jnp.exp2(m - m_new)
        psum, pv = None, None
        for b in range(nsub):
            p = jnp.exp2(ss[b] - m_new)
            d = jnp.dot(p.astype(bf16), v_ref[b * sub:(b + 1) * sub, :], preferred_element_type=f32)
            psum = p if psum is None else psum + p
            pv = d if pv is None else pv + d
        m, l, acc = m_new, alpha * l_sc[...] + jnp.sum(psum, axis=-1, keepdims=True), alpha * acc_sc[...] + pv
        m_sc[...], l_sc[...], acc_sc[...] = m, l, acc

        @pl.when(j == nk - 1)
        def _():
            o_ref[...] = (acc / l).astype(o_ref.dtype)
            lse_ref[...] = m + jnp.log2(l)

        if ns:
            pl.when(step == n_steps - 1)(g_finish)

    anyspec = pl.BlockSpec(memory_space=pl.ANY)
    res = pl.pallas_call(
        body, name=name, grid=(n_heads, S // tq, nk),
        in_specs=[pl.BlockSpec((tq, QK_PAD), lambda h, i, j: (i, h)),
                  pl.BlockSpec((tk, QK_PAD), lambda h, i, j: (j, h)),
                  pl.BlockSpec((tk, V_DIM), lambda h, i, j: (j, h))] + [anyspec] * ns,
        out_specs=[pl.BlockSpec((tq, V_DIM), lambda h, i, j: (i, h)),
                   pl.BlockSpec((None, tq, 1), lambda h, i, j: (h, i, 0))] + [anyspec] * ns,
        out_shape=[jax.ShapeDtypeStruct((S, n_heads * V_DIM), bf16), jax.ShapeDtypeStruct((n_heads, S, 1), f32)]
        + [jax.ShapeDtypeStruct((N_DEV,) + a.shape, a.dtype) for a in side],
        scratch_shapes=[pltpu.VMEM((tq, 1), f32), pltpu.VMEM((tq, 1), f32), pltpu.VMEM((tq, V_DIM), f32)]
        + (_gather_scratch(ns) if ns else []),
        compiler_params=_cp(("arbitrary", "arbitrary", "arbitrary")))(q, k, v, *side)
    return res[0], res[1], list(res[2:])


def flash_bwd(q, k, v, o, lse_row, do, do_col0, n_heads, name, side=(), tq=1024, tk=8192, sub=512):
    S = q.shape[0]
    tq, tk = _pick(S, tq), _pick(S, tk)
    sub = _pick(tk, sub)
    nsub = tk // sub
    ln2 = math.log(2.0)
    ns = len(side)
    n_steps = n_heads * (S // tq) * (S // tk)
    assert ns == 0 or n_steps >= 2

    def body(*refs):
        q_ref, k_ref, v_ref, o_ref, lse_ref, do_ref = refs[:6]
        dq_ref, dk_ref, dv_ref = refs[6 + ns:9 + ns]
        i, j = pl.program_id(1), pl.program_id(2)
        step = (pl.program_id(0) * (S // tq) + i) * (S // tk) + j
        if ns:
            x_start, x_finish = _chip_exchange_phases(refs[6:6 + ns], refs[9 + ns:9 + 2 * ns], *refs[9 + 2 * ns:])
            pl.when(step == 0)(x_start)

        @pl.when(j == 0)
        def _():
            dq_ref[...] = jnp.zeros_like(dq_ref)

        @pl.when((i == 0) & (j == 0))
        def _():
            dk_ref[...] = jnp.zeros_like(dk_ref)
            dv_ref[...] = jnp.zeros_like(dv_ref)

        qv = q_ref[...]
        dof = do_ref[...].astype(f32)
        do_b = dof.astype(bf16)
        do_s = (dof * ln2).astype(bf16)
        delta = hdot_nt(jnp.ones((8, V_DIM), f32), dof * ln2 * o_ref[...].astype(f32))[0:1, :]
        lse = lse_ref[...]
        dq = jnp.zeros((tq, QK_PAD), f32)
        for b in range(nsub):
            kb = k_ref[b * sub:(b + 1) * sub, :]
            rows = pl.ds(pl.multiple_of(j * tk + b * sub, sub), sub)
            pt = jnp.exp2(_nt(kb, qv) - lse)
            dpt = _nt(v_ref[b * sub:(b + 1) * sub, :], do_s)
            dst = (pt * (dpt - delta)).astype(bf16)
            dv_ref[rows, :] += jnp.dot(pt.astype(bf16), do_b, preferred_element_type=f32)
            dk_ref[rows, :] += jnp.dot(dst, qv, preferred_element_type=f32)
            dq = dq + _tn(dst, kb)
        dq_ref[...] += dq
        if ns:
            pl.when(step == n_steps - 1)(x_finish)

    anyspec = pl.BlockSpec(memory_space=pl.ANY)
    res = pl.pallas_call(
        body, name=name, grid=(n_heads, S // tq, S // tk),
        in_specs=[pl.BlockSpec((tq, QK_PAD), lambda h, i, j: (i, h)),
                  pl.BlockSpec((tk, QK_PAD), lambda h, i, j: (j, h)),
                  pl.BlockSpec((tk, V_DIM), lambda h, i, j: (j, h)),
                  pl.BlockSpec((tq, V_DIM), lambda h, i, j: (i, h)),
                  pl.BlockSpec((None, 1, tq), lambda h, i, j: (h, 0, i)),
                  pl.BlockSpec((tq, V_DIM), lambda h, i, j: (i, do_col0 + h))] + [anyspec] * ns,
        out_specs=[pl.BlockSpec((tq, QK_PAD), lambda h, i, j: (i, h)),
                   pl.BlockSpec((S, QK_PAD), lambda h, i, j: (0, h)),
                   pl.BlockSpec((S, V_DIM), lambda h, i, j: (0, h))] + [anyspec] * ns,
        out_shape=[jax.ShapeDtypeStruct((S, n_heads * QK_PAD), f32), jax.ShapeDtypeStruct((S, n_heads * QK_PAD), f32),
                   jax.ShapeDtypeStruct((S, n_heads * V_DIM), f32)] + [jax.ShapeDtypeStruct(a.shape, a.dtype) for a in side],
        scratch_shapes=_chip_exchange_scratch(ns) if ns else [],
        compiler_params=_cp(("arbitrary", "arbitrary", "arbitrary")))(q, k, v, o, lse_row, do, *side)
    return res[0], res[1], res[2], list(res[3:])


def _shifted(prev, cur, nxt, k, first, last):
    if k == 0:
        return cur
    t = cur.shape[0]
    r = lax.broadcasted_iota(jnp.int32, (HALO,) + cur.shape[1:], 0)
    if k < 0:
        body = pltpu.roll(cur, -k, 0)
        edge = jnp.where(first, 0.0, pltpu.roll(prev, -k, 0))
        return jnp.concatenate([jnp.where(r < -k, edge, body[:HALO]), body[HALO:]], axis=0)
    body = pltpu.roll(cur, t - k, 0)
    edge = jnp.where(last, 0.0, pltpu.roll(nxt, HALO - k, 0))
    return jnp.concatenate([body[:t - HALO], jnp.where(r >= HALO - k, edge, body[t - HALO:])], axis=0)


HALO = 8


def _halo_specs(t, width, n_tiles, lead=None):
    per = t // HALO
    rows = [(HALO, lambda i: jnp.maximum(i * per - 1, 0)), (t, lambda i: i),
            (HALO, lambda i: jnp.minimum((i + 1) * per, n_tiles * per - 1))]
    if lead is None:
        return [pl.BlockSpec((r, width), lambda i, f=f: (f(i), 0)) for r, f in rows]
    return [pl.BlockSpec((None, r, width), lambda i, f=f: (lead, f(i), 0)) for r, f in rows]


def conv_fwd(proj, width, w, b, name, tile=256):
    S = proj.shape[0]
    t = _pick_rows(S, tile)
    n_tiles = S // t

    def body(p_ref, c_ref, n_ref, w_ref, b_ref, z_ref):
        i = pl.program_id(0)
        first, last = i == 0, i == n_tiles - 1
        prev, cur, nxt = p_ref[...], c_ref[...], n_ref[...]
        z = b_ref[...] + jnp.zeros_like(cur)
        for j in range(CONV_W):
            z = z + w_ref[j:j + 1, :] * _shifted(prev, cur, nxt, j - CONV_W // 2, first, last)
        z_ref[...] = z

    return pl.pallas_call(
        body, name=name, grid=(n_tiles,),
        in_specs=_halo_specs(t, width, n_tiles) + [_whole(w), _whole(b)],
        out_specs=pl.BlockSpec((t, width), lambda i: (i, 0)),
        out_shape=jax.ShapeDtypeStruct((S, width), f32),
        compiler_params=_cp(("arbitrary",)))(proj, proj, proj, w, b)


def conv_bwd(dzq, dzk, proj, width, w, name, tile=256):
    S = proj.shape[0]
    t = _pick_rows(S, tile)
    n_tiles = S // t
    half = width // 2

    def body(*refs):
        d_refs, (up_ref, uc_ref, un_ref, w_ref, du_ref, dw_ref, db_ref) = refs[:12], refs[12:]
        i = pl.program_id(0)
        first, last = i == 0, i == n_tiles - 1

        @pl.when(first)
        def _():
            dw_ref[...] = jnp.zeros_like(dw_ref)
            db_ref[...] = jnp.zeros_like(db_ref)

        dprev, dcur, dnxt = [jnp.concatenate([d_refs[p][...] + d_refs[3 + p][...], d_refs[6 + p][...] + d_refs[9 + p][...]],
                                             axis=1) for p in range(3)]
        uprev, ucur, unxt = up_ref[...], uc_ref[...], un_ref[...]
        du = jnp.zeros_like(dcur)
        for j in range(CONV_W):
            k = j - CONV_W // 2
            du = du + w_ref[j:j + 1, :] * _shifted(dprev, dcur, dnxt, -k, first, last)
            dw_ref[j:j + 1, :] += jnp.sum(dcur * _shifted(uprev, ucur, unxt, k, first, last), axis=0, keepdims=True)
        du_ref[...] = du
        db_ref[...] += jnp.sum(dcur, axis=0, keepdims=True)

    return pl.pallas_call(
        body, name=name, grid=(n_tiles,),
        in_specs=_halo_specs(t, half, n_tiles, 0) + _halo_specs(t, half, n_tiles, 1) + _halo_specs(t, half, n_tiles, 0)
        + _halo_specs(t, half, n_tiles, 1) + _halo_specs(t, width, n_tiles) + [_whole(w)],
        out_specs=[pl.BlockSpec((t, width), lambda i: (i, 0)), pl.BlockSpec((8, width), lambda i: (0, 0)),
                   pl.BlockSpec((1, width), lambda i: (0, 0))],
        out_shape=[jax.ShapeDtypeStruct((S, width), f32), jax.ShapeDtypeStruct((8, width), f32),
                   jax.ShapeDtypeStruct((1, width), f32)],
        compiler_params=_cp(("arbitrary",)))(*([dzq] * 6), *([dzk] * 6), proj, proj, proj, w)


def _mlstm_step(dm, d, C, n, m, zq, zk, v, ic, fc, ir, fr, bi, bf_):
    L = zq.shape[0]
    q = _silu(zq)
    k = _silu(zk) * (dm ** -0.5)
    i_c, f_c = ic + bi, jax.nn.log_sigmoid(fc + bf_)
    i_r, f_r = ir + bi, jax.nn.log_sigmoid(fr + bf_)
    r = lax.broadcasted_iota(jnp.int32, (L, L), 0)
    c = lax.broadcasted_iota(jnp.int32, (L, L), 1)
    sgn = jnp.where(d == 0, r - c, c - r)
    mask = sgn >= 0
    b_c = jnp.sum(jnp.where(mask, f_r, 0.0), axis=-1, keepdims=True)
    b_r = jnp.sum(jnp.where(sgn <= 0, f_c, 0.0), axis=0, keepdims=True)
    log_inter = b_c + m
    logD = jnp.where(mask, b_c - b_r + i_r, -jnp.inf)
    m_t = jnp.maximum(log_inter, jnp.max(logD, axis=-1, keepdims=True))
    Dm = jnp.exp(logD - m_t)
    w_inter = jnp.exp(log_inter - m_t)
    scores = bdot_nt(q, k) * Dm
    num = bdot(scores, v) + w_inter * bdot_nt(q, C)
    den = jnp.sum(scores, axis=-1, keepdims=True) + w_inter * jnp.sum(q * n, axis=-1, keepdims=True)
    h = num / jnp.maximum(jnp.abs(den), jnp.exp(-m_t))
    bL = jnp.sum(f_c, axis=0, keepdims=True)
    log_w = bL - b_c + i_c
    m_new = jnp.maximum(bL + m, jnp.max(log_w, axis=0, keepdims=True))
    decay = jnp.exp(bL + m - m_new)
    w = jnp.exp(log_w - m_new)
    C_new = decay * C + bdot_tn(w * v, k)
    n_new = decay * n + jnp.sum(w * k, axis=0, keepdims=True)
    return C_new, n_new, m_new, h


def _mlstm_in_specs(L, dm, hm, hb, nc, step_of):
    ng = hm // hb

    def chunk(d, j):
        s = step_of(j)
        return s + d * (nc - 1 - 2 * s)
    return [
        pl.BlockSpec((L, hb * dm), lambda d, g, j: (chunk(d, j), g)),
        pl.BlockSpec((L, hb * dm), lambda d, g, j: (chunk(d, j), ng + g)),
        pl.BlockSpec((L, hb * dm), lambda d, g, j: (chunk(d, j), 2 * ng + g)),
        pl.BlockSpec((None, hb, L, 1), lambda d, g, j: (d, g, chunk(d, j), 0)),
        pl.BlockSpec((None, hb, L, 1), lambda d, g, j: (d, g, chunk(d, j), 0)),
        pl.BlockSpec((None, hb, 1, L), lambda d, g, j: (d, g, 0, chunk(d, j))),
        pl.BlockSpec((None, hb, 1, L), lambda d, g, j: (d, g, 0, chunk(d, j))),
        pl.BlockSpec((None, hb, 1, 1), lambda d, g, j: (d, g, 0, 0)),
        pl.BlockSpec((None, hb, 1, 1), lambda d, g, j: (d, g, 0, 0)),
    ], chunk


def mlstm_fwd(z, proj, gates, hm, dm, name, hb=None):
    S = z.shape[0]
    L = CHUNK
    nc = S // L
    hb = hm if hb is None else hb
    in_specs, chunk = _mlstm_in_specs(L, dm, hm, hb, nc, lambda j: j)

    def body(zq, zk, v, ic, fc, ir, fr, bi, bf_, h_ref, cs_ref, ns_ref, ms_ref, C_sc, n_sc, m_sc):
        d = pl.program_id(0)

        @pl.when(pl.program_id(2) == 0)
        def _():
            C_sc[...] = jnp.zeros_like(C_sc)
            n_sc[...] = jnp.zeros_like(n_sc)
            m_sc[...] = jnp.full_like(m_sc, M_INIT)

        for hh in range(hb):
            cols = slice(hh * dm, (hh + 1) * dm)
            C, n, m = C_sc[hh], n_sc[hh], m_sc[hh]
            cs_ref[hh], ns_ref[hh], ms_ref[hh] = C, n, m
            C2, n2, m2, h = _mlstm_step(dm, d, C, n, m, zq[:, cols], zk[:, cols], v[:, cols], ic[hh], fc[hh],
                                        ir[hh], fr[hh], bi[hh], bf_[hh])
            C_sc[hh], n_sc[hh], m_sc[hh] = C2, n2, m2
            h_ref[:, cols] = h

    return pl.pallas_call(
        body, name=name, grid=(2, hm // hb, nc), in_specs=in_specs,
        out_specs=[pl.BlockSpec((None, L, hb * dm), lambda d, g, j: (d, chunk(d, j), g)),
                   pl.BlockSpec((None, hb, None, dm, dm), lambda d, g, j: (d, g, j, 0, 0)),
                   pl.BlockSpec((None, hb, None, 1, dm), lambda d, g, j: (d, g, j, 0, 0)),
                   pl.BlockSpec((None, hb, None, 1, 1), lambda d, g, j: (d, g, j, 0, 0))],
        out_shape=[jax.ShapeDtypeStruct((2, S, hm * dm), f32), jax.ShapeDtypeStruct((2, hm, nc, dm, dm), f32),
                   jax.ShapeDtypeStruct((2, hm, nc, 1, dm), f32), jax.ShapeDtypeStruct((2, hm, nc, 1, 1), f32)],
        scratch_shapes=[pltpu.VMEM((hb, dm, dm), f32), pltpu.VMEM((hb, 1, dm), f32), pltpu.VMEM((hb, 1, 1), f32)],
        compiler_params=_cp(("arbitrary", "arbitrary", "arbitrary")))(z, z, proj, *gates)


def mlstm_bwd(z, proj, gates, states, dh, hm, dm, name, hb=None):
    S = z.shape[0]
    L = CHUNK
    nc = S // L
    hb = hm if hb is None else hb
    in_specs, chunk = _mlstm_in_specs(L, dm, hm, hb, nc, lambda j: nc - 1 - j)
    st = lambda j: nc - 1 - j
    in_specs = in_specs + [
        pl.BlockSpec((None, hb, None, dm, dm), lambda d, g, j: (d, g, st(j), 0, 0)),
        pl.BlockSpec((None, hb, None, 1, dm), lambda d, g, j: (d, g, st(j), 0, 0)),
        pl.BlockSpec((None, hb, None, 1, 1), lambda d, g, j: (d, g, st(j), 0, 0)),
        pl.BlockSpec((L, hb * dm), lambda d, g, j: (chunk(d, j), g)),
    ]

    def body(zq, zk, v, ic, fc, ir, fr, bi, bf_, cs, ns, ms, dh_ref,
             dzq, dzk, dv, dic, dfc, dir_, dfr, dbi, dbf, dC_sc, dn_sc, dm_sc):
        d = pl.program_id(0)

        @pl.when(pl.program_id(2) == 0)
        def _():
            dC_sc[...] = jnp.zeros_like(dC_sc)
            dn_sc[...] = jnp.zeros_like(dn_sc)
            dm_sc[...] = jnp.zeros_like(dm_sc)
            dbi[...] = jnp.zeros_like(dbi)
            dbf[...] = jnp.zeros_like(dbf)

        for hh in range(hb):
            cols = slice(hh * dm, (hh + 1) * dm)
            prim = (cs[hh], ns[hh], ms[hh], zq[:, cols], zk[:, cols], v[:, cols], ic[hh], fc[hh], ir[hh], fr[hh],
                    bi[hh], bf_[hh])
            _, pull = jax.vjp(functools.partial(_mlstm_step, dm, d), *prim)
            g = pull((dC_sc[hh], dn_sc[hh], dm_sc[hh], dh_ref[:, cols]))
            dC_sc[hh], dn_sc[hh], dm_sc[hh] = g[0], g[1], g[2]
            dzq[:, cols], dzk[:, cols], dv[:, cols] = g[3], g[4], g[5]
            dic[hh], dfc[hh], dir_[hh], dfr[hh] = g[6], g[7], g[8], g[9]
            dbi[hh] += g[10]
            dbf[hh] += g[11]

    tile = pl.BlockSpec((None, L, hb * dm), lambda d, g, j: (d, chunk(d, j), g))
    col = pl.BlockSpec((None, hb, L, 1), lambda d, g, j: (d, g, chunk(d, j), 0))
    row = pl.BlockSpec((None, hb, 1, L), lambda d, g, j: (d, g, 0, chunk(d, j)))
    one = pl.BlockSpec((None, hb, 1, 1), lambda d, g, j: (d, g, 0, 0))
    big = jax.ShapeDtypeStruct((2, S, hm * dm), f32)
    cols_ = jax.ShapeDtypeStruct((2, hm, S, 1), f32)
    rows_ = jax.ShapeDtypeStruct((2, hm, 1, S), f32)
    ones_ = jax.ShapeDtypeStruct((2, hm, 1, 1), f32)
    return pl.pallas_call(
        body, name=name, grid=(2, hm // hb, nc), in_specs=in_specs,
        out_specs=[tile, tile, tile, col, col, row, row, one, one],
        out_shape=[big, big, big, cols_, cols_, rows_, rows_, ones_, ones_],
        scratch_shapes=[pltpu.VMEM((hb, dm, dm), f32), pltpu.VMEM((hb, 1, dm), f32), pltpu.VMEM((hb, 1, 1), f32)],
        compiler_params=_cp(("arbitrary", "arbitrary", "arbitrary")))(z, z, proj, *gates, *states, dh)


def _blocks_to_cols(g):
    return g.transpose(1, 0, 2).reshape(g.shape[1], N_DEV * g.shape[2])


def _cols_to_blocks(a):
    return a.reshape(a.shape[0], N_DEV, a.shape[1] // N_DEV).transpose(1, 0, 2)


def _pad_cols(a, n):
    return jnp.pad(a, ((0, 0), (0, n - a.shape[1])))


def _relu2(u):
    r = jnp.maximum(u, 0.0)
    return r * r


def kernel(x, c, positions, w_ada, b_ada, norm_mix_g, w_in, b_gates, conv_w, conv_b, q_lora_g, w_uq, kv_lora_g, w_ukv, q_norm_g, k_norm_g, mlstm_norm_g, w_out, norm_mlp_g, w_ff1, w_ff2, loss_target, m_w_ada, m_b_ada, m_norm_mix_g, m_w_in, m_b_gates, m_conv_w, m_conv_b, m_q_lora_g, m_w_uq, m_kv_lora_g, m_w_ukv, m_q_norm_g, m_k_norm_g, m_mlstm_norm_g, m_w_out, m_norm_mlp_g, m_w_ff1, m_w_ff2, v_w_ada, v_b_ada, v_norm_mix_g, v_w_in, v_b_gates, v_conv_w, v_conv_b, v_q_lora_g, v_w_uq, v_kv_lora_g, v_w_ukv, v_q_norm_g, v_k_norm_g, v_mlstm_norm_g, v_w_out, v_norm_mlp_g, v_w_ff1, v_w_ff2):
    S, D = x.shape[1], x.shape[2]
    QL, KVL = w_uq.shape[1], w_ukv.shape[1]
    H = w_uq.shape[2] * N_DEV // QK_DIM
    HM = mlstm_norm_g.shape[1]
    DM = mlstm_norm_g.shape[2] * N_DEV
    MW = HM * DM
    D_IN = w_in.shape[2] * N_DEV
    NADA = w_ada.shape[2]
    assert D_IN == QL + KVL + ROPE + 4 * MW + N_GATES and DM % LANE == 0 and S % CHUNK == 0
    assert (4 * MW) % QL == 0 and (4 * MW + QL) % KVL == 0 and KVL % LANE == 0
    idx = 4 * lax.axis_index("x") + 2 * lax.axis_index("y") + lax.axis_index("c")
    x2, tgt = x[0], loss_target[0]

    g_in, g_uq, g_ukv, g_conv, g_mn, c_all = all_gather(
        [w_in[0].astype(bf16), w_uq[0].astype(bf16), w_ukv[0].astype(bf16), conv_w[0], mlstm_norm_g[0], c],
        "gather_weights")
    c_all = c_all.reshape(N_DEV, D)
    xi, yi, ci = lax.axis_index("x"), lax.axis_index("y"), lax.axis_index("c")
    slots = jnp.stack([4 * (1 - xi) + 2 * yi + ci, 4 * xi + 2 * (1 - yi) + ci, 4 * (1 - xi) + 2 * (1 - yi) + ci]).astype(jnp.int32)

    wi = _blocks_to_cols(g_in)
    o_cq, o_ckv, o_kpe, o_m, o_g = 0, QL, QL + KVL, QL + KVL + ROPE, QL + KVL + ROPE + 4 * MW
    w_in_p = jnp.concatenate([wi[:, o_m:o_g], wi[:, o_cq:o_kpe], _pad_cols(wi[:, o_kpe:o_m], LANE),
                              _pad_cols(wi[:, o_g:], LANE)], axis=1)
    NP = w_in_p.shape[1]
    cb_cq, cb_ckv, cb_kpe, cb_g = 4 * MW // QL, (4 * MW + QL) // KVL, (4 * MW + QL + KVL) // LANE, NP // LANE - 1
    w_uq_p = jnp.pad(_blocks_to_cols(g_uq).reshape(QL, H, QK_DIM), ((0, 0), (0, 0), (0, QK_PAD - QK_DIM))).reshape(QL, H * QK_PAD)
    w_ukv_p = _blocks_to_cols(g_ukv).reshape(KVL, H, 2, NOPE).transpose(0, 2, 1, 3).reshape(KVL, 2 * H * NOPE)
    conv_w_f = jnp.pad(_blocks_to_cols(g_conv), ((0, 8 - CONV_W), (0, 0)))
    mn_g = _blocks_to_cols(g_mn).reshape(1, MW)
    gqn = _pad_cols(q_norm_g, QK_PAD)
    gkn = _pad_cols(k_norm_g, QK_PAD)
    fr_np = np.zeros((1, LANE), np.float32)
    fr_np[0, :HALF] = fr_np[0, HALF:ROPE] = ROPE_THETA ** (-np.arange(HALF, dtype=np.float32) / HALF)
    freqs = jnp.asarray(fr_np)
    pos = positions.astype(f32).reshape(S, 1)

    b_blk = lax.dynamic_slice(b_ada, (0, idx * NADA), (1, NADA))
    mod_part = ada_fwd(c_all, w_ada[0], b_blk, "ada_fwd")
    (mod_all,) = all_gather([mod_part], "gather_mod")
    mod = lax.dynamic_index_in_dim(mod_all, idx, axis=1, keepdims=False).reshape(1, N_DEV * NADA)
    shift1, scale1, gate1, shift2, scale2, gate2 = [mod[:, k * D:(k + 1) * D] for k in range(6)]

    (h,) = rowwise(f_norm_mod, [Row(x2)], [norm_mix_g, shift1, scale1], [(D, bf16)], n_rows=S, tile=256, name="norm_mix")
    proj = mm(h, w_in_p, name="proj_in", out_dtype=f32)
    r_cq, r_ckv, r_kpe = Row(proj, QL, cb_cq), Row(proj, KVL, cb_ckv), Row(proj, LANE, cb_kpe)
    f_prep = make_f_mla_prep(H, QK_DIM ** -0.5 * math.log2(math.e))
    prep_params = [q_lora_g, kv_lora_g, gqn, gkn, w_uq_p, w_ukv_p, freqs]
    Q, K, V = rowwise(f_prep, [r_cq, r_ckv, r_kpe, Row(pos, diff=False)], prep_params,
                      [(H * QK_PAD, bf16), (H * QK_PAD, bf16), (H * V_DIM, bf16)], n_rows=S, tile=256, name="mla_prep")
    attn, lse, (g_out, g_ff1, g_ff2) = flash_fwd(
        Q, K, V, H, "flash_fwd", side=[w_out[0].astype(bf16), w_ff1[0].astype(bf16), w_ff2[0].astype(bf16)])
    w_out_f = g_out.reshape(N_DEV * g_out.shape[1], D)
    w_ff2_f = g_ff2.reshape(N_DEV * g_ff2.shape[1], D)

    conv_bias = conv_b
    z = conv_fwd(proj, 2 * MW, conv_w_f, conv_bias, "conv_fwd")
    graw = proj[:, cb_g * LANE:cb_g * LANE + N_GATES].reshape(S, 4, HM)
    gcol = graw.transpose(1, 2, 0).reshape(2, 2, HM, S)
    bg = b_gates.reshape(2, 2, HM)
    gates = (gcol[:, 0].reshape(2, HM, S, 1), gcol[:, 1].reshape(2, HM, S, 1),
             gcol[:, 0].reshape(2, HM, 1, S), gcol[:, 1].reshape(2, HM, 1, S),
             bg[:, 0].reshape(2, HM, 1, 1), bg[:, 1].reshape(2, HM, 1, 1))
    hdir, cs, ns, ms = mlstm_fwd(z, proj, gates, HM, DM, "mlstm_fwd")
    f_post = make_f_mlstm_post(HM, DM)
    post_rows = [Row(hdir, MW, 0, lead=0), Row(hdir, MW, 0, lead=1), Row(proj, MW, 3)]
    (ml_out,) = rowwise(f_post, post_rows, [mn_g], [(MW, bf16)], n_rows=S, tile=256, name="mlstm_post")

    cat = jnp.concatenate([attn, ml_out], axis=1)
    mixed = mm(cat, w_out_f, name="proj_out", out_dtype=f32)
    mlp_params = [gate1, norm_mlp_g, shift2, scale2]
    x1, h2 = rowwise(f_resid_norm_mod, [Row(x2), Row(mixed)], mlp_params, [(D, f32), (D, bf16)],
                     n_rows=S, tile=256, name="resid_norm_mlp")
    u = mm(h2, g_ff1, name="ff1", out_dtype=bf16)
    y = mm(u, w_ff2_f, name="ff2", a_fn=_relu2, out_dtype=f32)
    loss_l, d_out, d_y, d_gate2 = loss_head(x1, y, tgt, gate2, "loss_head")
    loss = lax.psum(loss_l[0, 0], AXES)

    dw_ff2 = mm(u, d_y, name="dw_ff2", ta=True, a_fn=_relu2, out_dtype=bf16)
    d_u = mm(d_y, w_ff2_f, name="d_u", tb=True, epi=lambda acc, uu: acc * (2.0 * jnp.maximum(uu.astype(f32), 0.0)),
             extras=(u,), out_dtype=bf16)
    dw_ff1 = mm(h2, d_u, name="dw_ff1", ta=True, out_dtype=bf16, out_blocks=True)
    w_ff1_t = g_ff1.transpose(0, 2, 1).reshape(-1, D)
    d_h2 = mm(d_u, w_ff1_t, name="d_h2", out_dtype=f32)
    (d_x1, d_mixed), (d_gate1, d_g_mlp, d_shift2, d_scale2) = rowwise_vjp(
        f_resid_norm_mod, [Row(x2), Row(mixed)], mlp_params, [Row(d_out), Row(d_h2)],
        n_rows=S, tile=256, name="resid_norm_mlp_bwd", row_grad_dtypes=[f32, bf16])
    dw_out = mm(cat, d_mixed, name="dw_out", ta=True, out_dtype=bf16)
    mlp_g = [dw_out.reshape(N_DEV, -1, D), dw_ff1, dw_ff2.reshape(N_DEV, -1, D)]
    d_cat, mlp_sib = mm(d_mixed, w_out_f, name="d_cat", tb=True, out_dtype=f32, tm=512, tn=2048,
                        side=mlp_g, side_pair=True)

    post_rows_b = [post_rows[0], Row(hdir, MW, 0, lead=1, diff=False), post_rows[2]]
    (dh, d_om), (d_mn_g,) = rowwise_vjp(
        f_post, post_rows_b, [mn_g], [Row(d_cat, MW, H * V_DIM // MW)], n_rows=S, tile=256, name="mlstm_post_bwd")
    dzq, dzk, dvm, dic, dfc, dir_, dfr, dbi, dbf = mlstm_bwd(z, proj, gates, (cs, ns, ms), dh, HM, DM, "mlstm_bwd")
    (d_vm,) = rowwise(f_add, [Row(dvm, MW, 0, lead=0), Row(dvm, MW, 0, lead=1)], [], [(MW, bf16)], n_rows=S, tile=256,
                      name="dv_sum")
    d_qk, d_conv_w, d_conv_b = conv_bwd(dzq, dzk, proj, 2 * MW, conv_w_f, "conv_bwd")
    dg = jnp.stack([dic.reshape(2, HM, S) + dir_.reshape(2, HM, S), dfc.reshape(2, HM, S) + dfr.reshape(2, HM, S)], axis=1)
    d_gates = dg.reshape(4 * HM, S).T
    d_b_gates = jnp.stack([dbi.reshape(2, HM), dbf.reshape(2, HM)], axis=1).reshape(1, N_GATES)

    mlp_tags = ["w_out", "w_ff1", "w_ff2"]
    mlp_part = [chip_partials(g, r, slots, "grad_chip_partials_" + t) for g, r, t in zip(mlp_g, mlp_sib, mlp_tags)]
    dq, dk, dv, mlp_chips = flash_bwd(Q, K, V, attn, lse.reshape(H, 1, S), d_cat, 0, H, "flash_bwd", side=mlp_part)
    (d_cq, d_ckv, d_kpe), (d_gq, d_gkv, d_gqn, d_gkn, dw_uq_p, dw_ukv_p) = rowwise_vjp(
        f_prep, [r_cq, r_ckv, r_kpe, Row(pos, diff=False)], prep_params, [Row(dq), Row(dk), Row(dv)],
        n_rows=S, tile=256, name="mla_prep_bwd", row_grad_dtypes=[bf16, bf16, bf16],
        param_diff=[True, True, True, True, True, True, False])

    d_proj = jnp.concatenate([d_qk.astype(bf16), d_vm, d_om.astype(bf16), d_cq, d_ckv, d_kpe,
                              _pad_cols(d_gates.astype(bf16), LANE)], axis=1)
    dw_in_p = mm(h, d_proj, name="dw_in", ta=True, out_dtype=bf16)

    dwi = jnp.concatenate([dw_in_p[:, 4 * MW:4 * MW + QL + KVL + ROPE], dw_in_p[:, :4 * MW],
                           dw_in_p[:, cb_g * LANE:cb_g * LANE + N_GATES]], axis=1)
    dw_uq = dw_uq_p.reshape(QL, H, QK_PAD)[:, :, :QK_DIM].reshape(QL, H * QK_DIM)
    dw_ukv = dw_ukv_p.reshape(KVL, 2, H, NOPE).transpose(0, 2, 1, 3).reshape(KVL, 2 * H * NOPE)
    tiny = [(w_uq, m_w_uq, v_w_uq, _cols_to_blocks(dw_uq)),
            (w_ukv, m_w_ukv, v_w_ukv, _cols_to_blocks(dw_ukv)),
            (conv_w, m_conv_w, v_conv_w, _cols_to_blocks(d_conv_w[:CONV_W])),
            (mlstm_norm_g, m_mlstm_norm_g, v_mlstm_norm_g, _cols_to_blocks(d_mn_g.reshape(HM, DM)))]
    tsizes = [int(np.prod(b[0].shape)) for b in tiny]
    T = sum(tsizes)
    PC = 512
    PR = -(-T // (PC * 64)) * 64
    gpack = jnp.concatenate([b[3].astype(bf16).reshape(N_DEV, -1) for b in tiny], axis=1)
    gpack = jnp.pad(gpack, ((0, 0), (0, PR * PC - T))).reshape(N_DEV, PR, PC)
    wpack = lambda k: jnp.pad(jnp.concatenate([b[k].reshape(1, -1) for b in tiny], axis=1),
                              ((0, 0), (0, PR * PC - T))).reshape(PR, PC)
    late_g = [_cols_to_blocks(dwi), gpack]
    late_sib = pair_exchange(late_g, "grad_pair_exchange")
    late_part = [chip_partials(g, r, slots, "grad_chip_partials_" + t) for g, r, t in zip(late_g, late_sib, ["w_in", "tiny"])]
    d_h, late_chips = mm(d_proj, w_in_p, name="d_h", tb=True, out_dtype=f32, tm=512, tn=2048, tk=2560, side=late_part)
    (grad_x,), (d_g_mix, d_shift1, d_scale1) = rowwise_vjp(
        f_norm_mod_thru, [Row(x2)], [norm_mix_g, shift1, scale1], [Row(d_h), Row(d_x1)],
        n_rows=S, tile=256, name="norm_mix_bwd")

    dmod = jnp.concatenate([d_shift1, d_scale1, d_gate1, d_shift2, d_scale2, d_gate2], axis=1)
    small = [(norm_mix_g, m_norm_mix_g, v_norm_mix_g, d_g_mix), (b_gates, m_b_gates, v_b_gates, d_b_gates),
             (conv_b, m_conv_b, v_conv_b, d_conv_b), (q_lora_g, m_q_lora_g, v_q_lora_g, d_gq),
             (kv_lora_g, m_kv_lora_g, v_kv_lora_g, d_gkv), (q_norm_g, m_q_norm_g, v_q_norm_g, d_gqn[:, :QK_DIM]),
             (k_norm_g, m_k_norm_g, v_k_norm_g, d_gkn[:, :QK_DIM]), (norm_mlp_g, m_norm_mlp_g, v_norm_mlp_g, d_g_mlp),
             (b_ada, m_b_ada, v_b_ada, dmod)]
    sizes = [s[0].shape[1] for s in small]
    P = sum(sizes)
    PP = -(-P // LANE) * LANE
    pack = lambda k: _pad_cols(jnp.concatenate([s[k] for s in small], axis=1), PP)
    (sg_all,) = all_gather([pack(3)], "gather_small_grads")
    s_out = adamw([sg_all[k] for k in range(N_DEV)], pack(0), pack(1), pack(2), "adamw_small")
    offs = np.concatenate([[0], np.cumsum(sizes)])
    small_out = [[o[:, offs[k]:offs[k + 1]] for o in s_out] for k in range(len(small))]

    dmod_all = sg_all[:, 0, offs[-2]:offs[-1]]
    dmod_blk = lax.dynamic_slice(dmod_all, (0, idx * NADA), (N_DEV, NADA))
    g_w_ada = ada_wgrad(c_all, dmod_blk, "ada_wgrad")
    ada_out = adamw([g_w_ada], w_ada[0], m_w_ada[0], v_w_ada[0], "adamw_ada")

    large = [(w_in[0], m_w_in[0], v_w_in[0], late_g[0]),
             (w_out[0], m_w_out[0], v_w_out[0], mlp_g[0]),
             (w_ff1[0], m_w_ff1[0], v_w_ff1[0], mlp_g[1]),
             (w_ff2[0], m_w_ff2[0], v_w_ff2[0], mlp_g[2]),
             (wpack(0), wpack(1), wpack(2), gpack)]
    tags = ["w_in", "w_out", "w_ff1", "w_ff2", "tiny"]
    from_sibling = [late_sib[0]] + list(mlp_sib) + [late_sib[1]]
    from_chips = [late_chips[0]] + list(mlp_chips) + [late_chips[1]]
    l_out = []
    for (w_, m_, v_, g), r, fc, t in zip(large, from_sibling, from_chips, tags):
        mine = lax.dynamic_index_in_dim(g, idx, axis=0, keepdims=False)
        sib = lax.dynamic_index_in_dim(r, 2 * xi + yi, axis=0, keepdims=False)
        l_out.append(adamw([mine, sib, fc[0], fc[1], fc[2]], w_, m_, v_, "adamw_" + t))
    toffs = np.concatenate([[0], np.cumsum(tsizes)])
    tiny_out = [[o.reshape(-1)[toffs[k]:toffs[k + 1]].reshape(tiny[k][0].shape) for o in l_out[4]] for k in range(len(tiny))]
    big_out = [[o[None] for o in l_out[0]], tiny_out[0], tiny_out[1], [o[None] for o in l_out[1]],
               [o[None] for o in l_out[2]], [o[None] for o in l_out[3]], tiny_out[2], tiny_out[3]]

    names = ["w_ada", "b_ada", "norm_mix_g", "w_in", "b_gates", "conv_w", "conv_b", "q_lora_g", "w_uq", "kv_lora_g",
             "w_ukv", "q_norm_g", "k_norm_g", "mlstm_norm_g", "w_out", "norm_mlp_g", "w_ff1", "w_ff2"]
    res = {"w_ada": [o[None] for o in ada_out]}
    for k, nm in enumerate(["norm_mix_g", "b_gates", "conv_b", "q_lora_g", "kv_lora_g", "q_norm_g", "k_norm_g",
                            "norm_mlp_g", "b_ada"]):
        res[nm] = small_out[k]
    for k, nm in enumerate(["w_in", "w_uq", "w_ukv", "w_out", "w_ff1", "w_ff2", "conv_w", "mlstm_norm_g"]):
        res[nm] = big_out[k]
    outs = [loss, grad_x[None]]
    for part in range(4):
        outs += [res[nm][part] for nm in names]
    return tuple(outs)
```

```python
import functools
import math

import numpy as np
import jax
import jax.numpy as jnp
from jax import lax
from jax.experimental import pallas as pl
from jax.experimental.pallas import tpu as pltpu

f32 = jnp.float32
bf16 = jnp.bfloat16

N_DEV = 8
AXES = ("x", "y", "c")
MESH = pl.DeviceIdType.MESH

NOPE = 128
ROPE = 64
HALF = ROPE // 2
QK_DIM = NOPE + ROPE
QK_PAD = 256
V_DIM = 128
ROPE_THETA = 10000.0
CHUNK = 128
CONV_W = 5
N_GATES = 16
EPS = 1e-6
M_INIT = -1e30

ADAM_LR, ADAM_B1, ADAM_B2, ADAM_EPS, ADAM_WD, ADAM_STEP = 0.001, 0.9, 0.999, 1e-08, 0.01, 10

LANE = 128
VMEM_LIMIT = 56 * 1024 * 1024


def _cp(sem=None, vmem=VMEM_LIMIT):
    return pltpu.CompilerParams(dimension_semantics=sem, vmem_limit_bytes=vmem)


def _pick(n, target):
    best = None
    t = LANE
    while t <= min(n, target):
        if n % t == 0:
            best = t
        t += LANE
    return best if best is not None else n


def _pick_rows(n, target):
    t = min(n, target)
    while n % t:
        t -= 8
    return t


def _make_dots(cast, precision):
    def dg(a, b, ca, cb):
        if cast is not None:
            a = a.astype(cast)
            b = b.astype(cast)
        return lax.dot_general(a, b, (((ca,), (cb,)), ((), ())), precision=precision, preferred_element_type=f32)

    @jax.custom_vjp
    def nn(a, b):
        return dg(a, b, 1, 0)

    def nn_f(a, b):
        return dg(a, b, 1, 0), (a, b)

    def nn_b(res, g):
        a, b = res
        return dg(g, b, 1, 1).astype(a.dtype), dg(a, g, 0, 0).astype(b.dtype)

    nn.defvjp(nn_f, nn_b)

    @jax.custom_vjp
    def nt(a, b):
        return dg(a, b, 1, 1)

    def nt_f(a, b):
        return dg(a, b, 1, 1), (a, b)

    def nt_b(res, g):
        a, b = res
        return dg(g, b, 1, 0).astype(a.dtype), dg(g, a, 0, 0).astype(b.dtype)

    nt.defvjp(nt_f, nt_b)

    @jax.custom_vjp
    def tn(a, b):
        return dg(a, b, 0, 0)

    def tn_f(a, b):
        return dg(a, b, 0, 0), (a, b)

    def tn_b(res, g):
        a, b = res
        return dg(b, g, 1, 1).astype(a.dtype), dg(a, g, 1, 0).astype(b.dtype)

    tn.defvjp(tn_f, tn_b)
    return nn, nt, tn


bdot, bdot_nt, bdot_tn = _make_dots(bf16, None)
hdot, hdot_nt, hdot_tn = _make_dots(None, lax.Precision.HIGHEST)


def _silu(x):
    return x * jax.nn.sigmoid(x)


def _rms(x, n):
    return x * lax.rsqrt(jnp.sum(x * x, axis=-1, keepdims=True) * (1.0 / n) + EPS)


def _place():
    return lax.axis_index("x"), lax.axis_index("y"), lax.axis_index("c")


def _gather_phases(ins, outs, send_sems, recv_sems, local_sems):
    n = len(ins)
    x, y, c = _place()
    me, sibling = (x, y, c), (x, y, 1 - c)
    chips = [(1 - x, y), (x, 1 - y), (1 - x, 1 - y)]

    def slot(o, p):
        return outs[o].at[4 * p[0] + 2 * p[1] + p[2]]

    def copy(o, k, block, to, src=None):
        dst = slot(o, block)
        return pltpu.make_async_remote_copy(
            src_ref=dst if src is None else src, dst_ref=dst,
            send_sem=send_sems.at[o, k], recv_sem=recv_sems.at[o, k],
            device_id=to, device_id_type=MESH)

    def local(o):
        return pltpu.make_async_copy(ins[o], slot(o, me), local_sems.at[o])

    def first(o):
        return [copy(o, 0, me, sibling, src=ins[o])] + [copy(o, 1 + j, me, (*chip, c), src=ins[o])
                                                        for j, chip in enumerate(chips)]

    def start():
        for o in range(n):
            local(o).start()
        for o in range(n):
            for cp in first(o):
                cp.start()

    def mid():
        for o in range(n):
            for j, chip in enumerate(chips):
                copy(o, 1 + j, (*chip, c), me).wait_recv()
                copy(o, 4 + j, (*chip, c), sibling).start()

    def finish():
        for o in range(n):
            copy(o, 0, sibling, me).wait_recv()
            for j, chip in enumerate(chips):
                copy(o, 4 + j, (*chip, 1 - c), me).wait_recv()
        for o in range(n):
            for cp in first(o):
                cp.wait_send()
            for j, chip in enumerate(chips):
                copy(o, 4 + j, (*chip, c), sibling).wait_send()
        for o in range(n):
            local(o).wait()

    return start, mid, finish


def _gather_scratch(n):
    return [pltpu.SemaphoreType.DMA((n, 7)), pltpu.SemaphoreType.DMA((n, 7)), pltpu.SemaphoreType.DMA((n,))]


def all_gather(ops, name):
    n = len(ops)

    def body(*refs):
        start, mid, finish = _gather_phases(refs[:n], refs[n:2 * n], *refs[2 * n:])
        start()
        mid()
        finish()

    anyspec = pl.BlockSpec(memory_space=pl.ANY)
    return pl.pallas_call(
        body, name=name,
        out_shape=[jax.ShapeDtypeStruct((N_DEV,) + o.shape, o.dtype) for o in ops],
        in_specs=[anyspec] * n, out_specs=[anyspec] * n,
        scratch_shapes=_gather_scratch(n),
    )(*ops)


def pair_exchange(gs, name):
    n = len(gs)

    def body(*refs):
        start, finish = _pair_exchange_phases(refs[:n], refs[n:2 * n], *refs[2 * n:])
        start()
        finish()

    anyspec = pl.BlockSpec(memory_space=pl.ANY)
    return pl.pallas_call(
        body, name=name, out_shape=[jax.ShapeDtypeStruct(_pair_exchange_shape(g), g.dtype) for g in gs],
        in_specs=[anyspec] * n, out_specs=[anyspec] * n,
        scratch_shapes=_pair_exchange_scratch(n),
    )(*gs)


def _pair_exchange_phases(g_refs, out_refs, send_sems, recv_sems):
    n = len(g_refs)
    x, y, c = _place()

    def copies():
        return [pltpu.make_async_remote_copy(
            src_ref=g_refs[o].at[2 * q + (1 - c)], dst_ref=out_refs[o].at[q],
            send_sem=send_sems.at[o, q], recv_sem=recv_sems.at[o, q],
            device_id=(x, y, 1 - c), device_id_type=MESH) for o in range(n) for q in range(4)]

    def start():
        for cp in copies():
            cp.start()

    def finish():
        for cp in copies():
            cp.wait_recv()
        for cp in copies():
            cp.wait_send()

    return start, finish


def _pair_exchange_scratch(n):
    return [pltpu.SemaphoreType.DMA((n, 4)), pltpu.SemaphoreType.DMA((n, 4))]


def _pair_exchange_shape(g):
    return (4,) + g.shape[1:]


def _chip_exchange_phases(p_refs, out_refs, send_sems, recv_sems):
    n = len(p_refs)
    x, y, c = _place()
    chips = [(1 - x, y), (x, 1 - y), (1 - x, 1 - y)]

    def copies():
        return [pltpu.make_async_remote_copy(
            src_ref=p_refs[o].at[j], dst_ref=out_refs[o].at[j],
            send_sem=send_sems.at[o, j], recv_sem=recv_sems.at[o, j],
            device_id=(*chip, c), device_id_type=MESH) for o in range(n) for j, chip in enumerate(chips)]

    def start():
        for cp in copies():
            cp.start()

    def finish():
        for cp in copies():
            cp.wait_recv()
        for cp in copies():
            cp.wait_send()

    return start, finish


def _chip_exchange_scratch(n):
    return [pltpu.SemaphoreType.DMA((n, 3)), pltpu.SemaphoreType.DMA((n, 3))]


def chip_partials(g, recv, slots, name):
    _, R, C = g.shape
    tr = _pick_rows(R, 512)

    def body(s_ref, a_ref, b_ref, o_ref):
        o_ref[...] = (a_ref[...].astype(f32) + b_ref[...].astype(f32)).astype(o_ref.dtype)

    grid_spec = pltpu.PrefetchScalarGridSpec(
        num_scalar_prefetch=1, grid=(3, R // tr),
        in_specs=[pl.BlockSpec((None, tr, C), lambda j, i, s: (s[j], i, 0)),
                  pl.BlockSpec((None, tr, C), lambda j, i, s: (s[j] // 2, i, 0))],
        out_specs=pl.BlockSpec((None, tr, C), lambda j, i, s: (j, i, 0)))
    return pl.pallas_call(body, name=name, grid_spec=grid_spec,
                          out_shape=jax.ShapeDtypeStruct((3, R, C), g.dtype),
                          compiler_params=_cp(("arbitrary", "arbitrary")))(slots, g, recv)


def adamw(parts, w, m, v, name, rows=256):
    R, C = w.shape
    tr = _pick_rows(R, rows)
    npart = len(parts)
    c1 = 1.0 - ADAM_B1 ** ADAM_STEP
    c2 = 1.0 - ADAM_B2 ** ADAM_STEP

    def body(*refs):
        p_refs = refs[:npart]
        w_ref, m_ref, v_ref, g_out, d_out, m_out, v_out = refs[npart:]
        g = p_refs[0][...].astype(f32)
        for p in p_refs[1:]:
            g = g + p[...].astype(f32)
        mn = ADAM_B1 * m_ref[...] + (1.0 - ADAM_B1) * g
        vn = ADAM_B2 * v_ref[...] + (1.0 - ADAM_B2) * (g * g)
        m_hat = mn / c1
        v_hat = vn / c2
        g_out[...] = g
        d_out[...] = -ADAM_LR * (m_hat / (jnp.sqrt(v_hat) + ADAM_EPS) + ADAM_WD * w_ref[...])
        m_out[...] = mn
        v_out[...] = vn

    spec = pl.BlockSpec((tr, C), lambda i: (i, 0))
    return pl.pallas_call(
        body, name=name, grid=(R // tr,),
        in_specs=[spec] * (npart + 3), out_specs=[spec] * 4,
        out_shape=[jax.ShapeDtypeStruct((R, C), f32)] * 4,
        compiler_params=_cp(("arbitrary",)))(*parts, w, m, v)


def mm(a, b, *, name, ta=False, tb=False, a_fn=None, epi=None, extras=(), out_dtype=f32, out_blocks=False, side=(),
       side_pair=False, tm=1024, tn=1024, tk=2048):
    K, M = a.shape if ta else a.shape[::-1]
    b3 = b.ndim == 3
    if b3:
        assert not tb
        N, K2 = N_DEV * b.shape[2], b.shape[1]
    else:
        N, K2 = b.shape if tb else b.shape[::-1]
    assert K == K2, (a.shape, b.shape, ta, tb)
    n_split = N // N_DEV if (out_blocks or b3) else N
    tm, tn, tk = _pick(M, tm), _pick(n_split, tn), _pick(K, tk)
    nb = n_split // tn
    nk = K // tk
    ne = len(extras)
    assert not (out_blocks and ne)
    dims = (((0 if ta else 1,), (1 if tb else 0,)), ((), ()))

    ns = len(side)
    n_steps = (M // tm) * (N // tn) * nk
    assert ns == 0 or n_steps >= 2
    if side_pair:
        side_phases, side_scratch, side_shape = _pair_exchange_phases, _pair_exchange_scratch, _pair_exchange_shape
    else:
        side_phases, side_scratch, side_shape = _chip_exchange_phases, _chip_exchange_scratch, lambda p: p.shape

    def body(a_ref, b_ref, *rest):
        e_refs, o_ref, acc = rest[:ne], rest[ne + ns], rest[ne + 2 * ns + 1]
        k = pl.program_id(2)
        if ns:
            step = (pl.program_id(0) * (N // tn) + pl.program_id(1)) * nk + k
            x_start, x_finish = side_phases(rest[ne:ne + ns], rest[ne + ns + 1:ne + 2 * ns + 1], *rest[ne + 2 * ns + 2:])
            pl.when(step == 0)(x_start)

        @pl.when(k == 0)
        def _():
            acc[...] = jnp.zeros_like(acc)

        av = a_ref[...]
        if a_fn is not None:
            av = a_fn(av.astype(f32))
        acc[...] += lax.dot_general(av.astype(bf16), b_ref[...].astype(bf16), dims, preferred_element_type=f32)

        @pl.when(k == nk - 1)
        def _():
            r = acc[...]
            if epi is not None:
                r = epi(r, *[e[...] for e in e_refs])
            o_ref[...] = r.astype(o_ref.dtype)

        if ns:
            pl.when(step == n_steps - 1)(x_finish)

    a_spec = pl.BlockSpec((tk, tm), lambda i, j, k: (k, i)) if ta else pl.BlockSpec((tm, tk), lambda i, j, k: (i, k))
    if b3:
        b_spec = pl.BlockSpec((None, tk, tn), lambda i, j, k: (j // nb, k, j % nb))
    else:
        b_spec = pl.BlockSpec((tn, tk), lambda i, j, k: (j, k)) if tb else pl.BlockSpec((tk, tn), lambda i, j, k: (k, j))
    if out_blocks:
        o_spec = pl.BlockSpec((None, tm, tn), lambda i, j, k: (j // nb, i, j % nb))
        o_shape = jax.ShapeDtypeStruct((N_DEV, M, N // N_DEV), out_dtype)
    else:
        o_spec = pl.BlockSpec((tm, tn), lambda i, j, k: (i, j))
        o_shape = jax.ShapeDtypeStruct((M, N), out_dtype)
    if not ns:
        return pl.pallas_call(
            body, name=name, grid=(M // tm, N // tn, nk),
            in_specs=[a_spec, b_spec] + [o_spec] * ne, out_specs=o_spec,
            out_shape=o_shape,
            scratch_shapes=[pltpu.VMEM((tm, tn), f32)],
            compiler_params=_cp(("parallel", "parallel", "arbitrary")))(a, b, *extras)
    anyspec = pl.BlockSpec(memory_space=pl.ANY)
    res = pl.pallas_call(
        body, name=name, grid=(M // tm, N // tn, nk),
        in_specs=[a_spec, b_spec] + [o_spec] * ne + [anyspec] * ns, out_specs=[o_spec] + [anyspec] * ns,
        out_shape=[o_shape] + [jax.ShapeDtypeStruct(side_shape(p), p.dtype) for p in side],
        scratch_shapes=[pltpu.VMEM((tm, tn), f32)] + side_scratch(ns),
        compiler_params=_cp(("arbitrary", "arbitrary", "arbitrary")))(a, b, *extras, *side)
    return res[0], list(res[1:])


class Row:
    def __init__(self, arr, width=None, col=0, lead=None, diff=True):
        self.arr, self.col, self.lead, self.diff = arr, col, lead, diff
        self.width = arr.shape[-1] if width is None else width

    def spec(self, t):
        col, lead = self.col, self.lead
        if lead is None:
            return pl.BlockSpec((t, self.width), lambda i: (i, col))
        return pl.BlockSpec((None, t, self.width), lambda i: (lead, i, col))


def _whole(p):
    return pl.BlockSpec(p.shape, lambda i: (0,) * p.ndim)


def rowwise(fn, rows, params, outs, *, n_rows, tile, name, side=()):
    t = _pick_rows(n_rows, tile)
    nr, npar, no, ns = len(rows), len(params), len(outs), len(side)
    n_steps = n_rows // t
    assert ns == 0 or n_steps >= 3

    def body(*refs):
        r_refs, p_refs = refs[:nr], refs[nr:nr + npar]
        o_refs = refs[nr + npar + ns:nr + npar + ns + no]
        if ns:
            s_in = refs[nr + npar:nr + npar + ns]
            s_out = refs[nr + npar + ns + no:nr + npar + 2 * ns + no]
            g_start, g_mid, g_finish = _gather_phases(s_in, s_out, *refs[nr + npar + 2 * ns + no:])
            pl.when(pl.program_id(0) == 0)(g_start)
            pl.when(pl.program_id(0) == n_steps // 2)(g_mid)
        res = fn(*[r[...].astype(f32) for r in r_refs], *[p[...] for p in p_refs])
        for o_ref, val in zip(o_refs, res):
            o_ref[...] = val.astype(o_ref.dtype)
        if ns:
            pl.when(pl.program_id(0) == n_steps - 1)(g_finish)

    anyspec = pl.BlockSpec(memory_space=pl.ANY)
    res = pl.pallas_call(
        body, name=name, grid=(n_steps,),
        in_specs=[r.spec(t) for r in rows] + [_whole(p) for p in params] + [anyspec] * ns,
        out_specs=[pl.BlockSpec((t, w), lambda i: (i, 0)) for w, _ in outs] + [anyspec] * ns,
        out_shape=[jax.ShapeDtypeStruct((n_rows, w), dt) for w, dt in outs]
        + [jax.ShapeDtypeStruct((N_DEV,) + a.shape, a.dtype) for a in side],
        scratch_shapes=_gather_scratch(ns) if ns else [],
        compiler_params=_cp(("arbitrary",)))(*[r.arr for r in rows], *params, *side)
    return res


def rowwise_vjp(fn, rows, params, cts, *, n_rows, tile, name, row_grad_dtypes=None, param_diff=None):
    t = _pick_rows(n_rows, tile)
    nr, npar, nc = len(rows), len(params), len(cts)
    param_diff = [True] * npar if param_diff is None else param_diff
    d_rows = [k for k, r in enumerate(rows) if r.diff]
    d_pars = [k for k in range(npar) if param_diff[k]]
    row_grad_dtypes = [f32] * len(d_rows) if row_grad_dtypes is None else row_grad_dtypes

    def body(*refs):
        r_refs, p_refs = refs[:nr], refs[nr:nr + npar]
        c_refs = refs[nr + npar:nr + npar + nc]
        dr_refs = refs[nr + npar + nc:nr + npar + nc + len(d_rows)]
        dp_refs = refs[nr + npar + nc + len(d_rows):]
        rv = [r[...].astype(f32) for r in r_refs]
        pv = [p[...] for p in p_refs]

        def g(*dvals):
            full_r, full_p = list(rv), list(pv)
            for k, val in zip(d_rows, dvals[:len(d_rows)]):
                full_r[k] = val
            for k, val in zip(d_pars, dvals[len(d_rows):]):
                full_p[k] = val
            return tuple(fn(*full_r, *full_p))

        prim = [rv[k] for k in d_rows] + [pv[k].astype(f32) for k in d_pars]
        _, pull = jax.vjp(g, *prim)
        grads = pull(tuple(c[...].astype(f32) for c in c_refs))
        for ref, val in zip(dr_refs, grads[:len(d_rows)]):
            ref[...] = val.astype(ref.dtype)

        @pl.when(pl.program_id(0) == 0)
        def _():
            for ref in dp_refs:
                ref[...] = jnp.zeros_like(ref)

        for ref, val in zip(dp_refs, grads[len(d_rows):]):
            ref[...] += val

    out_specs = [pl.BlockSpec((t, rows[k].width), lambda i: (i, 0)) for k in d_rows]
    out_specs += [_whole(params[k]) for k in d_pars]
    out_shape = [jax.ShapeDtypeStruct((n_rows, rows[k].width), dt) for k, dt in zip(d_rows, row_grad_dtypes)]
    out_shape += [jax.ShapeDtypeStruct(params[k].shape, f32) for k in d_pars]
    res = pl.pallas_call(
        body, name=name, grid=(n_rows // t,),
        in_specs=[r.spec(t) for r in rows] + [_whole(p) for p in params] + [c.spec(t) for c in cts],
        out_specs=out_specs, out_shape=out_shape,
        compiler_params=_cp(("arbitrary",)))(*[r.arr for r in rows], *params, *[c.arr for c in cts])
    return res[:len(d_rows)], res[len(d_rows):]


def f_norm_mod(x, g, shift, scale):
    return (_rms(x, x.shape[-1]) * g * (1.0 + scale) + shift,)


def f_norm_mod_thru(x, g, shift, scale):
    return f_norm_mod(x, g, shift, scale) + (x,)


def f_resid_norm_mod(x, mixed, gate1, g2, shift2, scale2):
    x1 = x + gate1 * mixed
    return (x1,) + f_norm_mod(x1, g2, shift2, scale2)


def _rope_rot():
    i = lax.broadcasted_iota(jnp.int32, (LANE, LANE), 0)
    j = lax.broadcasted_iota(jnp.int32, (LANE, LANE), 1)
    neg = jnp.where((i == j + HALF) & (j < HALF), -1.0, 0.0)
    pos = jnp.where((i == j - HALF) & (j >= HALF) & (j < ROPE), 1.0, 0.0)
    return (neg + pos).astype(f32)


def make_f_mla_prep(n_heads, q_scale):
    def fn(cq, ckv, kpe, pos, gq, gkv, gqn, gkn, w_uq, w_ukv, freqs):
        rot = _rope_rot()
        ang = pos * freqs
        cos, sin = jnp.cos(ang), jnp.sin(ang)

        def rope(u):
            return u * cos + hdot(u, rot) * sin

        qraw = bdot(_rms(cq, cq.shape[-1]) * gq, w_uq)
        kv = bdot(_rms(ckv, ckv.shape[-1]) * gkv, w_ukv)
        kpe_ss = jnp.sum(kpe * kpe, axis=-1, keepdims=True)
        qs, ks = [], []
        for h in range(n_heads):
            qh = _rms(qraw[:, h * QK_PAD:(h + 1) * QK_PAD], QK_DIM) * gqn
            qs += [qh[:, :NOPE], rope(qh[:, NOPE:])]
            kn = kv[:, h * NOPE:(h + 1) * NOPE]
            r = lax.rsqrt((jnp.sum(kn * kn, axis=-1, keepdims=True) + kpe_ss) * (1.0 / QK_DIM) + EPS)
            ks += [kn * r * gkn[:, :NOPE], rope(kpe * r * gkn[:, NOPE:])]
        return jnp.concatenate(qs, axis=-1) * q_scale, jnp.concatenate(ks, axis=-1), kv[:, n_heads * NOPE:]
    return fn


def make_f_mlstm_post(n_heads, dm):
    def fn(hf, hb, o, g):
        hm = hf + hb
        outs = []
        for h in range(n_heads):
            sl = slice(h * dm, (h + 1) * dm)
            outs.append(jax.nn.sigmoid(o[:, sl]) * (_rms(hm[:, sl], dm) * g[:, sl]))
        return (jnp.concatenate(outs, axis=-1),)
    return fn


def f_add(a, b):
    return (a + b,)


def loss_head(x1, y, target, gate2, name, tile=256):
    S, D = x1.shape
    t = _pick_rows(S, tile)

    def body(x1_ref, y_ref, t_ref, g_ref, loss_ref, dout_ref, dy_ref, dgate_ref):
        @pl.when(pl.program_id(0) == 0)
        def _():
            loss_ref[...] = jnp.zeros_like(loss_ref)
            dgate_ref[...] = jnp.zeros_like(dgate_ref)

        yv, gv = y_ref[...], g_ref[...]
        e = x1_ref[...] + gv * yv - t_ref[...]
        loss_ref[...] += 0.5 * jnp.sum(jnp.sum(e * e, axis=-1, keepdims=True) * (1.0 / D), axis=0, keepdims=True)
        d_out = e * (1.0 / D)
        dout_ref[...] = d_out
        dy_ref[...] = (d_out * gv).astype(dy_ref.dtype)
        dgate_ref[...] += jnp.sum(d_out * yv, axis=0, keepdims=True)

    row = pl.BlockSpec((t, D), lambda i: (i, 0))
    return pl.pallas_call(
        body, name=name, grid=(S // t,),
        in_specs=[row, row, row, pl.BlockSpec((1, D), lambda i: (0, 0))],
        out_specs=[pl.BlockSpec((1, 1), lambda i: (0, 0)), row, row, pl.BlockSpec((1, D), lambda i: (0, 0))],
        out_shape=[jax.ShapeDtypeStruct((1, 1), f32), jax.ShapeDtypeStruct((S, D), f32),
                   jax.ShapeDtypeStruct((S, D), bf16), jax.ShapeDtypeStruct((1, D), f32)],
        compiler_params=_cp(("arbitrary",)))(x1, y, target, gate2)


def ada_fwd(c_all, w_blk, b_blk, name):
    B, D = c_all.shape
    N = w_blk.shape[1]
    tn = _pick(N, 512)

    def body(c_ref, w_ref, b_ref, o_ref):
        o_ref[...] = bdot(_silu(c_ref[...]), w_ref[...]) + b_ref[...]

    return pl.pallas_call(
        body, name=name, grid=(N // tn,),
        in_specs=[pl.BlockSpec((B, D), lambda j: (0, 0)), pl.BlockSpec((D, tn), lambda j: (0, j)),
                  pl.BlockSpec((1, tn), lambda j: (0, j))],
        out_specs=pl.BlockSpec((B, tn), lambda j: (0, j)),
        out_shape=jax.ShapeDtypeStruct((B, N), f32), compiler_params=_cp(("arbitrary",)))(c_all, w_blk, b_blk)


def ada_wgrad(c_all, dmod_blk, name):
    B, D = c_all.shape
    N = dmod_blk.shape[1]
    tn = _pick(N, 512)

    def body(c_ref, d_ref, o_ref):
        o_ref[...] = hdot_tn(_silu(c_ref[...]), d_ref[...])

    return pl.pallas_call(
        body, name=name, grid=(N // tn,),
        in_specs=[pl.BlockSpec((B, D), lambda j: (0, 0)), pl.BlockSpec((B, tn), lambda j: (0, j))],
        out_specs=pl.BlockSpec((D, tn), lambda j: (0, j)),
        out_shape=jax.ShapeDtypeStruct((D, N), f32), compiler_params=_cp(("arbitrary",)))(c_all, dmod_blk)


def _nt(a, b):
    return lax.dot_general(a, b, (((1,), (1,)), ((), ())), preferred_element_type=f32)


def _tn(a, b):
    return lax.dot_general(a, b, (((0,), (0,)), ((), ())), preferred_element_type=f32)


def flash_fwd(q, k, v, n_heads, name, side=(), tq=512, tk=8192, sub=1024):
    S = q.shape[0]
    tq, tk = _pick(S, tq), _pick(S, tk)
    sub = _pick(tk, sub)
    nk, nsub = S // tk, tk // sub
    ns = len(side)
    n_steps = n_heads * (S // tq) * nk
    assert ns == 0 or n_steps >= 3

    def body(*refs):
        q_ref, k_ref, v_ref = refs[:3]
        o_ref, lse_ref = refs[3 + ns:5 + ns]
        m_sc, l_sc, acc_sc = refs[5 + 2 * ns:8 + 2 * ns]
        j = pl.program_id(2)
        step = (pl.program_id(0) * (S // tq) + pl.program_id(1)) * nk + j
        if ns:
            g_start, g_mid, g_finish = _gather_phases(refs[3:3 + ns], refs[5 + ns:5 + 2 * ns], *refs[8 + 2 * ns:])
            pl.when(step == 0)(g_start)
            pl.when(step == n_steps // 2)(g_mid)

        @pl.when(j == 0)
        def _():
            m_sc[...] = jnp.full_like(m_sc, -jnp.inf)
            l_sc[...] = jnp.zeros_like(l_sc)
            acc_sc[...] = jnp.zeros_like(acc_sc)

        qv = q_ref[...]
        m = m_sc[...]
        ss = [_nt(qv, k_ref[b * sub:(b + 1) * sub, :]) for b in range(nsub)]
        mx = ss[0]
        for s in ss[1:]:
            mx = jnp.maximum(mx, s)
        m_new = jnp.maximum(m, jnp.max(mx, axis=-1, keepdims=True))
        alpha = jnp.exp2(m - m_new)
        psum, pv = None, None
        for b in range(nsub):
            p = jnp.exp2(ss[b] - m_new)
            d = jnp.dot(p.astype(bf16), v_ref[b * sub:(b + 1) * sub, :], preferred_element_type=f32)
            psum = p if psum is None else psum + p
            pv = d if pv is None else pv + d
        m, l, acc = m_new, alpha * l_sc[...] + jnp.sum(psum, axis=-1, keepdims=True), alpha * acc_sc[...] + pv
        m_sc[...], l_sc[...], acc_sc[...] = m, l, acc

        @pl.when(j == nk - 1)
        def _():
            o_ref[...] = (acc / l).astype(o_ref.dtype)
            lse_ref[...] = m + jnp.log2(l)

        if ns:
            pl.when(step == n_steps - 1)(g_finish)

    anyspec = pl.BlockSpec(memory_space=pl.ANY)
    res = pl.pallas_call(
        body, name=name, grid=(n_heads, S // tq, nk),
        in_specs=[pl.BlockSpec((tq, QK_PAD), lambda h, i, j: (i, h)),
                  pl.BlockSpec((tk, QK_PAD), lambda h, i, j: (j, h)),
                  pl.BlockSpec((tk, V_DIM), lambda h, i, j: (j, h))] + [anyspec] * ns,
        out_specs=[pl.BlockSpec((tq, V_DIM), lambda h, i, j: (i, h)),
                   pl.BlockSpec((None, tq, 1), lambda h, i, j: (h, i, 0))] + [anyspec] * ns,
        out_shape=[jax.ShapeDtypeStruct((S, n_heads * V_DIM), bf16), jax.ShapeDtypeStruct((n_heads, S, 1), f32)]
        + [jax.ShapeDtypeStruct((N_DEV,) + a.shape, a.dtype) for a in side],
        scratch_shapes=[pltpu.VMEM((tq, 1), f32), pltpu.VMEM((tq, 1), f32), pltpu.VMEM((tq, V_DIM), f32)]
        + (_gather_scratch(ns) if ns else []),
        compiler_params=_cp(("arbitrary", "arbitrary", "arbitrary")))(q, k, v, *side)
    return res[0], res[1], list(res[2:])


def flash_bwd(q, k, v, o, lse_row, do, do_col0, n_heads, name, side=(), tq=1024, tk=8192, sub=512):
    S = q.shape[0]
    tq, tk = _pick(S, tq), _pick(S, tk)
    sub = _pick(tk, sub)
    nsub = tk // sub
    ln2 = math.log(2.0)
    ns = len(side)
    n_steps = n_heads * (S // tq) * (S // tk)
    assert ns == 0 or n_steps >= 2

    def body(*refs):
        q_ref, k_ref, v_ref, o_ref, lse_ref, do_ref = refs[:6]
        dq_ref, dk_ref, dv_ref = refs[6 + ns:9 + ns]
        i, j = pl.program_id(1), pl.program_id(2)
        step = (pl.program_id(0) * (S // tq) + i) * (S // tk) + j
        if ns:
            x_start, x_finish = _chip_exchange_phases(refs[6:6 + ns], refs[9 + ns:9 + 2 * ns], *refs[9 + 2 * ns:])
            pl.when(step == 0)(x_start)

        @pl.when(j == 0)
        def _():
            dq_ref[...] = jnp.zeros_like(dq_ref)

        @pl.when((i == 0) & (j == 0))
        def _():
            dk_ref[...] = jnp.zeros_like(dk_ref)
            dv_ref[...] = jnp.zeros_like(dv_ref)

        qv = q_ref[...]
        dof = do_ref[...].astype(f32)
        do_b = dof.astype(bf16)
        do_s = (dof * ln2).astype(bf16)
        delta = hdot_nt(jnp.ones((8, V_DIM), f32), dof * ln2 * o_ref[...].astype(f32))[0:1, :]
        lse = lse_ref[...]
        dq = jnp.zeros((tq, QK_PAD), f32)
        for b in range(nsub):
            kb = k_ref[b * sub:(b + 1) * sub, :]
            rows = pl.ds(pl.multiple_of(j * tk + b * sub, sub), sub)
            pt = jnp.exp2(_nt(kb, qv) - lse)
            dpt = _nt(v_ref[b * sub:(b + 1) * sub, :], do_s)
            dst = (pt * (dpt - delta)).astype(bf16)
            dv_ref[rows, :] += jnp.dot(pt.astype(bf16), do_b, preferred_element_type=f32)
            dk_ref[rows, :] += jnp.dot(dst, qv, preferred_element_type=f32)
            dq = dq + _tn(dst, kb)
        dq_ref[...] += dq
        if ns:
            pl.when(step == n_steps - 1)(x_finish)

    anyspec = pl.BlockSpec(memory_space=pl.ANY)
    res = pl.pallas_call(
        body, name=name, grid=(n_heads, S // tq, S // tk),
        in_specs=[pl.BlockSpec((tq, QK_PAD), lambda h, i, j: (i, h)),
                  pl.BlockSpec((tk, QK_PAD), lambda h, i, j: (j, h)),
                  pl.BlockSpec((tk, V_DIM), lambda h, i, j: (j, h)),
                  pl.BlockSpec((tq, V_DIM), lambda h, i, j: (i, h)),
                  pl.BlockSpec((None, 1, tq), lambda h, i, j: (h, 0, i)),
                  pl.BlockSpec((tq, V_DIM), lambda h, i, j: (i, do_col0 + h))] + [anyspec] * ns,
        out_specs=[pl.BlockSpec((tq, QK_PAD), lambda h, i, j: (i, h)),
                   pl.BlockSpec((S, QK_PAD), lambda h, i, j: (0, h)),
                   pl.BlockSpec((S, V_DIM), lambda h, i, j: (0, h))] + [anyspec] * ns,
        out_shape=[jax.ShapeDtypeStruct((S, n_heads * QK_PAD), f32), jax.ShapeDtypeStruct((S, n_heads * QK_PAD), f32),
                   jax.ShapeDtypeStruct((S, n_heads * V_DIM), f32)] + [jax.ShapeDtypeStruct(a.shape, a.dtype) for a in side],
        scratch_shapes=_chip_exchange_scratch(ns) if ns else [],
        compiler_params=_cp(("arbitrary", "arbitrary", "arbitrary")))(q, k, v, o, lse_row, do, *side)
    return res[0], res[1], res[2], list(res[3:])


def _shifted(prev, cur, nxt, k, first, last):
    if k == 0:
        return cur
    t = cur.shape[0]
    r = lax.broadcasted_iota(jnp.int32, (HALO,) + cur.shape[1:], 0)
    if k < 0:
        body = pltpu.roll(cur, -k, 0)
        edge = jnp.where(first, 0.0, pltpu.roll(prev, -k, 0))
        return jnp.concatenate([jnp.where(r < -k, edge, body[:HALO]), body[HALO:]], axis=0)
    body = pltpu.roll(cur, t - k, 0)
    edge = jnp.where(last, 0.0, pltpu.roll(nxt, HALO - k, 0))
    return jnp.concatenate([body[:t - HALO], jnp.where(r >= HALO - k, edge, body[t - HALO:])], axis=0)


HALO = 8


def _halo_specs(t, width, n_tiles, lead=None):
    per = t // HALO
    rows = [(HALO, lambda i: jnp.maximum(i * per - 1, 0)), (t, lambda i: i),
            (HALO, lambda i: jnp.minimum((i + 1) * per, n_tiles * per - 1))]
    if lead is None:
        return [pl.BlockSpec((r, width), lambda i, f=f: (f(i), 0)) for r, f in rows]
    return [pl.BlockSpec((None, r, width), lambda i, f=f: (lead, f(i), 0)) for r, f in rows]


def conv_fwd(proj, width, w, b, name, tile=256):
    S = proj.shape[0]
    t = _pick_rows(S, tile)
    n_tiles = S // t

    def body(p_ref, c_ref, n_ref, w_ref, b_ref, z_ref):
        i = pl.program_id(0)
        first, last = i == 0, i == n_tiles - 1
        prev, cur, nxt = p_ref[...], c_ref[...], n_ref[...]
        z = b_ref[...] + jnp.zeros_like(cur)
        for j in range(CONV_W):
            z = z + w_ref[j:j + 1, :] * _shifted(prev, cur, nxt, j - CONV_W // 2, first, last)
        z_ref[...] = z

    return pl.pallas_call(
        body, name=name, grid=(n_tiles,),
        in_specs=_halo_specs(t, width, n_tiles) + [_whole(w), _whole(b)],
        out_specs=pl.BlockSpec((t, width), lambda i: (i, 0)),
        out_shape=jax.ShapeDtypeStruct((S, width), f32),
        compiler_params=_cp(("arbitrary",)))(proj, proj, proj, w, b)


def conv_bwd(dzq, dzk, proj, width, w, name, tile=256):
    S = proj.shape[0]
    t = _pick_rows(S, tile)
    n_tiles = S // t
    half = width // 2

    def body(*refs):
        d_refs, (up_ref, uc_ref, un_ref, w_ref, du_ref, dw_ref, db_ref) = refs[:12], refs[12:]
        i = pl.program_id(0)
        first, last = i == 0, i == n_tiles - 1

        @pl.when(first)
        def _():
            dw_ref[...] = jnp.zeros_like(dw_ref)
            db_ref[...] = jnp.zeros_like(db_ref)

        dprev, dcur, dnxt = [jnp.concatenate([d_refs[p][...] + d_refs[3 + p][...], d_refs[6 + p][...] + d_refs[9 + p][...]],
                                             axis=1) for p in range(3)]
        uprev, ucur, unxt = up_ref[...], uc_ref[...], un_ref[...]
        du = jnp.zeros_like(dcur)
        for j in range(CONV_W):
            k = j - CONV_W // 2
            du = du + w_ref[j:j + 1, :] * _shifted(dprev, dcur, dnxt, -k, first, last)
            dw_ref[j:j + 1, :] += jnp.sum(dcur * _shifted(uprev, ucur, unxt, k, first, last), axis=0, keepdims=True)
        du_ref[...] = du
        db_ref[...] += jnp.sum(dcur, axis=0, keepdims=True)

    return pl.pallas_call(
        body, name=name, grid=(n_tiles,),
        in_specs=_halo_specs(t, half, n_tiles, 0) + _halo_specs(t, half, n_tiles, 1) + _halo_specs(t, half, n_tiles, 0)
        + _halo_specs(t, half, n_tiles, 1) + _halo_specs(t, width, n_tiles) + [_whole(w)],
        out_specs=[pl.BlockSpec((t, width), lambda i: (i, 0)), pl.BlockSpec((8, width), lambda i: (0, 0)),
                   pl.BlockSpec((1, width), lambda i: (0, 0))],
        out_shape=[jax.ShapeDtypeStruct((S, width), f32), jax.ShapeDtypeStruct((8, width), f32),
                   jax.ShapeDtypeStruct((1, width), f32)],
        compiler_params=_cp(("arbitrary",)))(*([dzq] * 6), *([dzk] * 6), proj, proj, proj, w)


def _mlstm_step(dm, d, C, n, m, zq, zk, v, ic, fc, ir, fr, bi, bf_):
    L = zq.shape[0]
    q = _silu(zq)
    k = _silu(zk) * (dm ** -0.5)
    i_c, f_c = ic + bi, jax.nn.log_sigmoid(fc + bf_)
    i_r, f_r = ir + bi, jax.nn.log_sigmoid(fr + bf_)
    r = lax.broadcasted_iota(jnp.int32, (L, L), 0)
    c = lax.broadcasted_iota(jnp.int32, (L, L), 1)
    sgn = jnp.where(d == 0, r - c, c - r)
    mask = sgn >= 0
    b_c = jnp.sum(jnp.where(mask, f_r, 0.0), axis=-1, keepdims=True)
    b_r = jnp.sum(jnp.where(sgn <= 0, f_c, 0.0), axis=0, keepdims=True)
    log_inter = b_c + m
    logD = jnp.where(mask, b_c - b_r + i_r, -jnp.inf)
    m_t = jnp.maximum(log_inter, jnp.max(logD, axis=-1, keepdims=True))
    Dm = jnp.exp(logD - m_t)
    w_inter = jnp.exp(log_inter - m_t)
    scores = bdot_nt(q, k) * Dm
    num = bdot(scores, v) + w_inter * bdot_nt(q, C)
    den = jnp.sum(scores, axis=-1, keepdims=True) + w_inter * jnp.sum(q * n, axis=-1, keepdims=True)
    h = num / jnp.maximum(jnp.abs(den), jnp.exp(-m_t))
    bL = jnp.sum(f_c, axis=0, keepdims=True)
    log_w = bL - b_c + i_c
    m_new = jnp.maximum(bL + m, jnp.max(log_w, axis=0, keepdims=True))
    decay = jnp.exp(bL + m - m_new)
    w = jnp.exp(log_w - m_new)
    C_new = decay * C + bdot_tn(w * v, k)
    n_new = decay * n + jnp.sum(w * k, axis=0, keepdims=True)
    return C_new, n_new, m_new, h


def _mlstm_in_specs(L, dm, hm, hb, nc, step_of):
    ng = hm // hb

    def chunk(d, j):
        s = step_of(j)
        return s + d * (nc - 1 - 2 * s)
    return [
        pl.BlockSpec((L, hb * dm), lambda d, g, j: (chunk(d, j), g)),
        pl.BlockSpec((L, hb * dm), lambda d, g, j: (chunk(d, j), ng + g)),
        pl.BlockSpec((L, hb * dm), lambda d, g, j: (chunk(d, j), 2 * ng + g)),
        pl.BlockSpec((None, hb, L, 1), lambda d, g, j: (d, g, chunk(d, j), 0)),
        pl.BlockSpec((None, hb, L, 1), lambda d, g, j: (d, g, chunk(d, j), 0)),
        pl.BlockSpec((None, hb, 1, L), lambda d, g, j: (d, g, 0, chunk(d, j))),
        pl.BlockSpec((None, hb, 1, L), lambda d, g, j: (d, g, 0, chunk(d, j))),
        pl.BlockSpec((None, hb, 1, 1), lambda d, g, j: (d, g, 0, 0)),
        pl.BlockSpec((None, hb, 1, 1), lambda d, g, j: (d, g, 0, 0)),
    ], chunk


def mlstm_fwd(z, proj, gates, hm, dm, name, hb=None):
    S = z.shape[0]
    L = CHUNK
    nc = S // L
    hb = hm if hb is None else hb
    in_specs, chunk = _mlstm_in_specs(L, dm, hm, hb, nc, lambda j: j)

    def body(zq, zk, v, ic, fc, ir, fr, bi, bf_, h_ref, cs_ref, ns_ref, ms_ref, C_sc, n_sc, m_sc):
        d = pl.program_id(0)

        @pl.when(pl.program_id(2) == 0)
        def _():
            C_sc[...] = jnp.zeros_like(C_sc)
            n_sc[...] = jnp.zeros_like(n_sc)
            m_sc[...] = jnp.full_like(m_sc, M_INIT)

        for hh in range(hb):
            cols = slice(hh * dm, (hh + 1) * dm)
            C, n, m = C_sc[hh], n_sc[hh], m_sc[hh]
            cs_ref[hh], ns_ref[hh], ms_ref[hh] = C, n, m
            C2, n2, m2, h = _mlstm_step(dm, d, C, n, m, zq[:, cols], zk[:, cols], v[:, cols], ic[hh], fc[hh],
                                        ir[hh], fr[hh], bi[hh], bf_[hh])
            C_sc[hh], n_sc[hh], m_sc[hh] = C2, n2, m2
            h_ref[:, cols] = h

    return pl.pallas_call(
        body, name=name, grid=(2, hm // hb, nc), in_specs=in_specs,
        out_specs=[pl.BlockSpec((None, L, hb * dm), lambda d, g, j: (d, chunk(d, j), g)),
                   pl.BlockSpec((None, hb, None, dm, dm), lambda d, g, j: (d, g, j, 0, 0)),
                   pl.BlockSpec((None, hb, None, 1, dm), lambda d, g, j: (d, g, j, 0, 0)),
                   pl.BlockSpec((None, hb, None, 1, 1), lambda d, g, j: (d, g, j, 0, 0))],
        out_shape=[jax.ShapeDtypeStruct((2, S, hm * dm), f32), jax.ShapeDtypeStruct((2, hm, nc, dm, dm), f32),
                   jax.ShapeDtypeStruct((2, hm, nc, 1, dm), f32), jax.ShapeDtypeStruct((2, hm, nc, 1, 1), f32)],
        scratch_shapes=[pltpu.VMEM((hb, dm, dm), f32), pltpu.VMEM((hb, 1, dm), f32), pltpu.VMEM((hb, 1, 1), f32)],
        compiler_params=_cp(("arbitrary", "arbitrary", "arbitrary")))(z, z, proj, *gates)


def mlstm_bwd(z, proj, gates, states, dh, hm, dm, name, hb=None):
    S = z.shape[0]
    L = CHUNK
    nc = S // L
    hb = hm if hb is None else hb
    in_specs, chunk = _mlstm_in_specs(L, dm, hm, hb, nc, lambda j: nc - 1 - j)
    st = lambda j: nc - 1 - j
    in_specs = in_specs + [
        pl.BlockSpec((None, hb, None, dm, dm), lambda d, g, j: (d, g, st(j), 0, 0)),
        pl.BlockSpec((None, hb, None, 1, dm), lambda d, g, j: (d, g, st(j), 0, 0)),
        pl.BlockSpec((None, hb, None, 1, 1), lambda d, g, j: (d, g, st(j), 0, 0)),
        pl.BlockSpec((L, hb * dm), lambda d, g, j: (chunk(d, j), g)),
    ]

    def body(zq, zk, v, ic, fc, ir, fr, bi, bf_, cs, ns, ms, dh_ref,
             dzq, dzk, dv, dic, dfc, dir_, dfr, dbi, dbf, dC_sc, dn_sc, dm_sc):
        d = pl.program_id(0)

        @pl.when(pl.program_id(2) == 0)
        def _():
            dC_sc[...] = jnp.zeros_like(dC_sc)
            dn_sc[...] = jnp.zeros_like(dn_sc)
            dm_sc[...] = jnp.zeros_like(dm_sc)
            dbi[...] = jnp.zeros_like(dbi)
            dbf[...] = jnp.zeros_like(dbf)

        for hh in range(hb):
            cols = slice(hh * dm, (hh + 1) * dm)
            prim = (cs[hh], ns[hh], ms[hh], zq[:, cols], zk[:, cols], v[:, cols], ic[hh], fc[hh], ir[hh], fr[hh],
                    bi[hh], bf_[hh])
            _, pull = jax.vjp(functools.partial(_mlstm_step, dm, d), *prim)
            g = pull((dC_sc[hh], dn_sc[hh], dm_sc[hh], dh_ref[:, cols]))
            dC_sc[hh], dn_sc[hh], dm_sc[hh] = g[0], g[1], g[2]
            dzq[:, cols], dzk[:, cols], dv[:, cols] = g[3], g[4], g[5]
            dic[hh], dfc[hh], dir_[hh], dfr[hh] = g[6], g[7], g[8], g[9]
            dbi[hh] += g[10]
            dbf[hh] += g[11]

    tile = pl.BlockSpec((None, L, hb * dm), lambda d, g, j: (d, chunk(d, j), g))
    col = pl.BlockSpec((None, hb, L, 1), lambda d, g, j: (d, g, chunk(d, j), 0))
    row = pl.BlockSpec((None, hb, 1, L), lambda d, g, j: (d, g, 0, chunk(d, j)))
    one = pl.BlockSpec((None, hb, 1, 1), lambda d, g, j: (d, g, 0, 0))
    big = jax.ShapeDtypeStruct((2, S, hm * dm), f32)
    cols_ = jax.ShapeDtypeStruct((2, hm, S, 1), f32)
    rows_ = jax.ShapeDtypeStruct((2, hm, 1, S), f32)
    ones_ = jax.ShapeDtypeStruct((2, hm, 1, 1), f32)
    return pl.pallas_call(
        body, name=name, grid=(2, hm // hb, nc), in_specs=in_specs,
        out_specs=[tile, tile, tile, col, col, row, row, one, one],
        out_shape=[big, big, big, cols_, cols_, rows_, rows_, ones_, ones_],
        scratch_shapes=[pltpu.VMEM((hb, dm, dm), f32), pltpu.VMEM((hb, 1, dm), f32), pltpu.VMEM((hb, 1, 1), f32)],
        compiler_params=_cp(("arbitrary", "arbitrary", "arbitrary")))(z, z, proj, *gates, *states, dh)


def _blocks_to_cols(g):
    return g.transpose(1, 0, 2).reshape(g.shape[1], N_DEV * g.shape[2])


def _cols_to_blocks(a):
    return a.reshape(a.shape[0], N_DEV, a.shape[1] // N_DEV).transpose(1, 0, 2)


def _pad_cols(a, n):
    return jnp.pad(a, ((0, 0), (0, n - a.shape[1])))


def _relu2(u):
    r = jnp.maximum(u, 0.0)
    return r * r


def kernel(x, c, positions, w_ada, b_ada, norm_mix_g, w_in, b_gates, conv_w, conv_b, q_lora_g, w_uq, kv_lora_g, w_ukv, q_norm_g, k_norm_g, mlstm_norm_g, w_out, norm_mlp_g, w_ff1, w_ff2, loss_target, m_w_ada, m_b_ada, m_norm_mix_g, m_w_in, m_b_gates, m_conv_w, m_conv_b, m_q_lora_g, m_w_uq, m_kv_lora_g, m_w_ukv, m_q_norm_g, m_k_norm_g, m_mlstm_norm_g, m_w_out, m_norm_mlp_g, m_w_ff1, m_w_ff2, v_w_ada, v_b_ada, v_norm_mix_g, v_w_in, v_b_gates, v_conv_w, v_conv_b, v_q_lora_g, v_w_uq, v_kv_lora_g, v_w_ukv, v_q_norm_g, v_k_norm_g, v_mlstm_norm_g, v_w_out, v_norm_mlp_g, v_w_ff1, v_w_ff2):
    S, D = x.shape[1], x.shape[2]
    QL, KVL = w_uq.shape[1], w_ukv.shape[1]
    H = w_uq.shape[2] * N_DEV // QK_DIM
    HM = mlstm_norm_g.shape[1]
    DM = mlstm_norm_g.shape[2] * N_DEV
    MW = HM * DM
    D_IN = w_in.shape[2] * N_DEV
    NADA = w_ada.shape[2]
    assert D_IN == QL + KVL + ROPE + 4 * MW + N_GATES and DM % LANE == 0 and S % CHUNK == 0
    assert (4 * MW) % QL == 0 and (4 * MW + QL) % KVL == 0 and KVL % LANE == 0
    idx = 4 * lax.axis_index("x") + 2 * lax.axis_index("y") + lax.axis_index("c")
    x2, tgt = x[0], loss_target[0]

    (c_all,) = all_gather([c], "gather_cond")
    c_all = c_all.reshape(N_DEV, D)
    xi, yi, ci = lax.axis_index("x"), lax.axis_index("y"), lax.axis_index("c")
    slots = jnp.stack([4 * (1 - xi) + 2 * yi + ci, 4 * xi + 2 * (1 - yi) + ci, 4 * (1 - xi) + 2 * (1 - yi) + ci]).astype(jnp.int32)
    gqn = _pad_cols(q_norm_g, QK_PAD)
    gkn = _pad_cols(k_norm_g, QK_PAD)
    fr_np = np.zeros((1, LANE), np.float32)
    fr_np[0, :HALF] = fr_np[0, HALF:ROPE] = ROPE_THETA ** (-np.arange(HALF, dtype=np.float32) / HALF)
    freqs = jnp.asarray(fr_np)
    pos = positions.astype(f32).reshape(S, 1)

    b_blk = lax.dynamic_slice(b_ada, (0, idx * NADA), (1, NADA))
    mod_part = ada_fwd(c_all, w_ada[0], b_blk, "ada_fwd")
    (mod_all,) = all_gather([mod_part], "gather_mod")
    mod = lax.dynamic_index_in_dim(mod_all, idx, axis=1, keepdims=False).reshape(1, N_DEV * NADA)
    shift1, scale1, gate1, shift2, scale2, gate2 = [mod[:, k * D:(k + 1) * D] for k in range(6)]

    h, g_in, g_uq, g_ukv, g_conv, g_mn = rowwise(
        f_norm_mod, [Row(x2)], [norm_mix_g, shift1, scale1], [(D, bf16)], n_rows=S, tile=256, name="norm_mix",
        side=[w_in[0].astype(bf16), w_uq[0].astype(bf16), w_ukv[0].astype(bf16), conv_w[0], mlstm_norm_g[0]])
    wi = _blocks_to_cols(g_in)
    o_cq, o_ckv, o_kpe, o_m, o_g = 0, QL, QL + KVL, QL + KVL + ROPE, QL + KVL + ROPE + 4 * MW
    w_in_p = jnp.concatenate([wi[:, o_m:o_g], wi[:, o_cq:o_kpe], _pad_cols(wi[:, o_kpe:o_m], LANE),
                              _pad_cols(wi[:, o_g:], LANE)], axis=1)
    NP = w_in_p.shape[1]
    cb_cq, cb_ckv, cb_kpe, cb_g = 4 * MW // QL, (4 * MW + QL) // KVL, (4 * MW + QL + KVL) // LANE, NP // LANE - 1
    w_uq_p = jnp.pad(_blocks_to_cols(g_uq).reshape(QL, H, QK_DIM), ((0, 0), (0, 0), (0, QK_PAD - QK_DIM))).reshape(QL, H * QK_PAD)
    w_ukv_p = _blocks_to_cols(g_ukv).reshape(KVL, H, 2, NOPE).transpose(0, 2, 1, 3).reshape(KVL, 2 * H * NOPE)
    conv_w_f = jnp.pad(_blocks_to_cols(g_conv), ((0, 8 - CONV_W), (0, 0)))
    mn_g = _blocks_to_cols(g_mn).reshape(1, MW)
    proj = mm(h, w_in_p, name="proj_in", out_dtype=f32)
    r_cq, r_ckv, r_kpe = Row(proj, QL, cb_cq), Row(proj, KVL, cb_ckv), Row(proj, LANE, cb_kpe)
    f_prep = make_f_mla_prep(H, QK_DIM ** -0.5 * math.log2(math.e))
    prep_params = [q_lora_g, kv_lora_g, gqn, gkn, w_uq_p, w_ukv_p, freqs]
    Q, K, V = rowwise(f_prep, [r_cq, r_ckv, r_kpe, Row(pos, diff=False)], prep_params,
                      [(H * QK_PAD, bf16), (H * QK_PAD, bf16), (H * V_DIM, bf16)], n_rows=S, tile=256, name="mla_prep")
    attn, lse, (g_out, g_ff1, g_ff2) = flash_fwd(
        Q, K, V, H, "flash_fwd", side=[w_out[0].astype(bf16), w_ff1[0].astype(bf16), w_ff2[0].astype(bf16)])
    w_out_f = g_out.reshape(N_DEV * g_out.shape[1], D)
    w_ff2_f = g_ff2.reshape(N_DEV * g_ff2.shape[1], D)

    conv_bias = conv_b
    z = conv_fwd(proj, 2 * MW, conv_w_f, conv_bias, "conv_fwd")
    graw = proj[:, cb_g * LANE:cb_g * LANE + N_GATES].reshape(S, 4, HM)
    gcol = graw.transpose(1, 2, 0).reshape(2, 2, HM, S)
    bg = b_gates.reshape(2, 2, HM)
    gates = (gcol[:, 0].reshape(2, HM, S, 1), gcol[:, 1].reshape(2, HM, S, 1),
             gcol[:, 0].reshape(2, HM, 1, S), gcol[:, 1].reshape(2, HM, 1, S),
             bg[:, 0].reshape(2, HM, 1, 1), bg[:, 1].reshape(2, HM, 1, 1))
    hdir, cs, ns, ms = mlstm_fwd(z, proj, gates, HM, DM, "mlstm_fwd")
    f_post = make_f_mlstm_post(HM, DM)
    post_rows = [Row(hdir, MW, 0, lead=0), Row(hdir, MW, 0, lead=1), Row(proj, MW, 3)]
    (ml_out,) = rowwise(f_post, post_rows, [mn_g], [(MW, bf16)], n_rows=S, tile=256, name="mlstm_post")

    cat = jnp.concatenate([attn, ml_out], axis=1)
    mixed = mm(cat, w_out_f, name="proj_out", out_dtype=f32)
    mlp_params = [gate1, norm_mlp_g, shift2, scale2]
    x1, h2 = rowwise(f_resid_norm_mod, [Row(x2), Row(mixed)], mlp_params, [(D, f32), (D, bf16)],
                     n_rows=S, tile=256, name="resid_norm_mlp")
    u = mm(h2, g_ff1, name="ff1", out_dtype=bf16)
    y = mm(u, w_ff2_f, name="ff2", a_fn=_relu2, out_dtype=f32)
    loss_l, d_out, d_y, d_gate2 = loss_head(x1, y, tgt, gate2, "loss_head")
    loss = lax.psum(loss_l[0, 0], AXES)

    dw_ff2 = mm(u, d_y, name="dw_ff2", ta=True, a_fn=_relu2, out_dtype=bf16)
    d_u = mm(d_y, w_ff2_f, name="d_u", tb=True, epi=lambda acc, uu: acc * (2.0 * jnp.maximum(uu.astype(f32), 0.0)),
             extras=(u,), out_dtype=bf16)
    dw_ff1 = mm(h2, d_u, name="dw_ff1", ta=True, out_dtype=bf16, out_blocks=True)
    w_ff1_t = g_ff1.transpose(0, 2, 1).reshape(-1, D)
    d_h2 = mm(d_u, w_ff1_t, name="d_h2", out_dtype=f32)
    (d_x1, d_mixed), (d_gate1, d_g_mlp, d_shift2, d_scale2) = rowwise_vjp(
        f_resid_norm_mod, [Row(x2), Row(mixed)], mlp_params, [Row(d_out), Row(d_h2)],
        n_rows=S, tile=256, name="resid_norm_mlp_bwd", row_grad_dtypes=[f32, bf16])
    dw_out = mm(cat, d_mixed, name="dw_out", ta=True, out_dtype=bf16)
    mlp_g = [dw_out.reshape(N_DEV, -1, D), dw_ff1, dw_ff2.reshape(N_DEV, -1, D)]
    d_cat, mlp_sib = mm(d_mixed, w_out_f, name="d_cat", tb=True, out_dtype=f32, tm=512, tn=2048,
                        side=mlp_g, side_pair=True)

    post_rows_b = [post_rows[0], Row(hdir, MW, 0, lead=1, diff=False), post_rows[2]]
    (dh, d_om), (d_mn_g,) = rowwise_vjp(
        f_post, post_rows_b, [mn_g], [Row(d_cat, MW, H * V_DIM // MW)], n_rows=S, tile=256, name="mlstm_post_bwd")
    dzq, dzk, dvm, dic, dfc, dir_, dfr, dbi, dbf = mlstm_bwd(z, proj, gates, (cs, ns, ms), dh, HM, DM, "mlstm_bwd")
    (d_vm,) = rowwise(f_add, [Row(dvm, MW, 0, lead=0), Row(dvm, MW, 0, lead=1)], [], [(MW, bf16)], n_rows=S, tile=256,
                      name="dv_sum")
    d_qk, d_conv_w, d_conv_b = conv_bwd(dzq, dzk, proj, 2 * MW, conv_w_f, "conv_bwd")
    dg = jnp.stack([dic.reshape(2, HM, S) + dir_.reshape(2, HM, S), dfc.reshape(2, HM, S) + dfr.reshape(2, HM, S)], axis=1)
    d_gates = dg.reshape(4 * HM, S).T
    d_b_gates = jnp.stack([dbi.reshape(2, HM), dbf.reshape(2, HM)], axis=1).reshape(1, N_GATES)

    mlp_tags = ["w_out", "w_ff1", "w_ff2"]
    mlp_part = [chip_partials(g, r, slots, "grad_chip_partials_" + t) for g, r, t in zip(mlp_g, mlp_sib, mlp_tags)]
    dq, dk, dv, mlp_chips = flash_bwd(Q, K, V, attn, lse.reshape(H, 1, S), d_cat, 0, H, "flash_bwd", side=mlp_part)
    (d_cq, d_ckv, d_kpe), (d_gq, d_gkv, d_gqn, d_gkn, dw_uq_p, dw_ukv_p) = rowwise_vjp(
        f_prep, [r_cq, r_ckv, r_kpe, Row(pos, diff=False)], prep_params, [Row(dq), Row(dk), Row(dv)],
        n_rows=S, tile=256, name="mla_prep_bwd", row_grad_dtypes=[bf16, bf16, bf16],
        param_diff=[True, True, True, True, True, True, False])

    d_proj = jnp.concatenate([d_qk.astype(bf16), d_vm, d_om.astype(bf16), d_cq, d_ckv, d_kpe,
                              _pad_cols(d_gates.astype(bf16), LANE)], axis=1)
    dw_in_p = mm(h, d_proj, name="dw_in", ta=True, out_dtype=bf16)

    dwi = jnp.concatenate([dw_in_p[:, 4 * MW:4 * MW + QL + KVL + ROPE], dw_in_p[:, :4 * MW],
                           dw_in_p[:, cb_g * LANE:cb_g * LANE + N_GATES]], axis=1)
    dw_uq = dw_uq_p.reshape(QL, H, QK_PAD)[:, :, :QK_DIM].reshape(QL, H * QK_DIM)
    dw_ukv = dw_ukv_p.reshape(KVL, 2, H, NOPE).transpose(0, 2, 1, 3).reshape(KVL, 2 * H * NOPE)
    tiny = [(w_uq, m_w_uq, v_w_uq, _cols_to_blocks(dw_uq)),
            (w_ukv, m_w_ukv, v_w_ukv, _cols_to_blocks(dw_ukv)),
            (conv_w, m_conv_w, v_conv_w, _cols_to_blocks(d_conv_w[:CONV_W])),
            (mlstm_norm_g, m_mlstm_norm_g, v_mlstm_norm_g, _cols_to_blocks(d_mn_g.reshape(HM, DM)))]
    tsizes = [int(np.prod(b[0].shape)) for b in tiny]
    T = sum(tsizes)
    PC = 512
    PR = -(-T // (PC * 64)) * 64
    gpack = jnp.concatenate([b[3].astype(bf16).reshape(N_DEV, -1) for b in tiny], axis=1)
    gpack = jnp.pad(gpack, ((0, 0), (0, PR * PC - T))).reshape(N_DEV, PR, PC)
    wpack = lambda k: jnp.pad(jnp.concatenate([b[k].reshape(1, -1) for b in tiny], axis=1),
                              ((0, 0), (0, PR * PC - T))).reshape(PR, PC)
    late_g = [_cols_to_blocks(dwi), gpack]
    late_sib = pair_exchange(late_g, "grad_pair_exchange")
    late_part = [chip_partials(g, r, slots, "grad_chip_partials_" + t) for g, r, t in zip(late_g, late_sib, ["w_in", "tiny"])]
    d_h, late_chips = mm(d_proj, w_in_p, name="d_h", tb=True, out_dtype=f32, tm=512, tn=2048, tk=2560, side=late_part)
    (grad_x,), (d_g_mix, d_shift1, d_scale1) = rowwise_vjp(
        f_norm_mod_thru, [Row(x2)], [norm_mix_g, shift1, scale1], [Row(d_h), Row(d_x1)],
        n_rows=S, tile=256, name="norm_mix_bwd")

    dmod = jnp.concatenate([d_shift1, d_scale1, d_gate1, d_shift2, d_scale2, d_gate2], axis=1)
    small = [(norm_mix_g, m_norm_mix_g, v_norm_mix_g, d_g_mix), (b_gates, m_b_gates, v_b_gates, d_b_gates),
             (conv_b, m_conv_b, v_conv_b, d_conv_b), (q_lora_g, m_q_lora_g, v_q_lora_g, d_gq),
             (kv_lora_g, m_kv_lora_g, v_kv_lora_g, d_gkv), (q_norm_g, m_q_norm_g, v_q_norm_g, d_gqn[:, :QK_DIM]),
             (k_norm_g, m_k_norm_g, v_k_norm_g, d_gkn[:, :QK_DIM]), (norm_mlp_g, m_norm_mlp_g, v_norm_mlp_g, d_g_mlp),
             (b_ada, m_b_ada, v_b_ada, dmod)]
    sizes = [s[0].shape[1] for s in small]
    P = sum(sizes)
    PP = -(-P // LANE) * LANE
    pack = lambda k: _pad_cols(jnp.concatenate([s[k] for s in small], axis=1), PP)
    (sg_all,) = all_gather([pack(3)], "gather_small_grads")
    s_out = adamw([sg_all[k] for k in range(N_DEV)], pack(0), pack(1), pack(2), "adamw_small")
    offs = np.concatenate([[0], np.cumsum(sizes)])
    small_out = [[o[:, offs[k]:offs[k + 1]] for o in s_out] for k in range(len(small))]

    dmod_all = sg_all[:, 0, offs[-2]:offs[-1]]
    dmod_blk = lax.dynamic_slice(dmod_all, (0, idx * NADA), (N_DEV, NADA))
    g_w_ada = ada_wgrad(c_all, dmod_blk, "ada_wgrad")
    ada_out = adamw([g_w_ada], w_ada[0], m_w_ada[0], v_w_ada[0], "adamw_ada")

    large = [(w_in[0], m_w_in[0], v_w_in[0], late_g[0]),
             (w_out[0], m_w_out[0], v_w_out[0], mlp_g[0]),
             (w_ff1[0], m_w_ff1[0], v_w_ff1[0], mlp_g[1]),
             (w_ff2[0], m_w_ff2[0], v_w_ff2[0], mlp_g[2]),
             (wpack(0), wpack(1), wpack(2), gpack)]
    tags = ["w_in", "w_out", "w_ff1", "w_ff2", "tiny"]
    from_sibling = [late_sib[0]] + list(mlp_sib) + [late_sib[1]]
    from_chips = [late_chips[0]] + list(mlp_chips) + [late_chips[1]]
    l_out = []
    for (w_, m_, v_, g), r, fc, t in zip(large, from_sibling, from_chips, tags):
        mine = lax.dynamic_index_in_dim(g, idx, axis=0, keepdims=False)
        sib = lax.dynamic_index_in_dim(r, 2 * xi + yi, axis=0, keepdims=False)
        l_out.append(adamw([mine, sib, fc[0], fc[1], fc[2]], w_, m_, v_, "adamw_" + t))
    toffs = np.concatenate([[0], np.cumsum(tsizes)])
    tiny_out = [[o.reshape(-1)[toffs[k]:toffs[k + 1]].reshape(tiny[k][0].shape) for o in l_out[4]] for k in range(len(tiny))]
    big_out = [[o[None] for o in l_out[0]], tiny_out[0], tiny_out[1], [o[None] for o in l_out[1]],
               [o[None] for o in l_out[2]], [o[None] for o in l_out[3]], tiny_out[2], tiny_out[3]]

    names = ["w_ada", "b_ada", "norm_mix_g", "w_in", "b_gates", "conv_w", "conv_b", "q_lora_g", "w_uq", "kv_lora_g",
             "w_ukv", "q_norm_g", "k_norm_g", "mlstm_norm_g", "w_out", "norm_mlp_g", "w_ff1", "w_ff2"]
    res = {"w_ada": [o[None] for o in ada_out]}
    for k, nm in enumerate(["norm_mix_g", "b_gates", "conv_b", "q_lora_g", "kv_lora_g", "q_norm_g", "k_norm_g",
                            "norm_mlp_g", "b_ada"]):
        res[nm] = small_out[k]
    for k, nm in enumerate(["w_in", "w_uq", "w_ukv", "w_out", "w_ff1", "w_ff2", "conv_w", "mlstm_norm_g"]):
        res[nm] = big_out[k]
    outs = [loss, grad_x[None]]
    for part in range(4):
        outs += [res[nm][part] for nm in names]
    return tuple(outs)
```

```python
import functools
import math

import numpy as np
import jax
import jax.numpy as jnp
from jax import lax
from jax.experimental import pallas as pl
from jax.experimental.pallas import tpu as pltpu

f32 = jnp.float32
bf16 = jnp.bfloat16

N_DEV = 8
AXES = ("x", "y", "c")
MESH = pl.DeviceIdType.MESH

NOPE = 128
ROPE = 64
HALF = ROPE // 2
QK_DIM = NOPE + ROPE
QK_PAD = 256
V_DIM = 128
ROPE_THETA = 10000.0
CHUNK = 128
CONV_W = 5
N_GATES = 16
EPS = 1e-6
M_INIT = -1e30

ADAM_LR, ADAM_B1, ADAM_B2, ADAM_EPS, ADAM_WD, ADAM_STEP = 0.001, 0.9, 0.999, 1e-08, 0.01, 10

LANE = 128
VMEM_LIMIT = 56 * 1024 * 1024


def _cp(sem=None, vmem=VMEM_LIMIT):
    return pltpu.CompilerParams(dimension_semantics=sem, vmem_limit_bytes=vmem)


def _pick(n, target):
    best = None
    t = LANE
    while t <= min(n, target):
        if n % t == 0:
            best = t
        t += LANE
    return best if best is not None else n


def _pick_rows(n, target):
    t = min(n, target)
    while n % t:
        t -= 8
    return t


def _make_dots(cast, precision):
    def dg(a, b, ca, cb):
        if cast is not None:
            a = a.astype(cast)
            b = b.astype(cast)
        return lax.dot_general(a, b, (((ca,), (cb,)), ((), ())), precision=precision, preferred_element_type=f32)

    @jax.custom_vjp
    def nn(a, b):
        return dg(a, b, 1, 0)

    def nn_f(a, b):
        return dg(a, b, 1, 0), (a, b)

    def nn_b(res, g):
        a, b = res
        return dg(g, b, 1, 1).astype(a.dtype), dg(a, g, 0, 0).astype(b.dtype)

    nn.defvjp(nn_f, nn_b)

    @jax.custom_vjp
    def nt(a, b):
        return dg(a, b, 1, 1)

    def nt_f(a, b):
        return dg(a, b, 1, 1), (a, b)

    def nt_b(res, g):
        a, b = res
        return dg(g, b, 1, 0).astype(a.dtype), dg(g, a, 0, 0).astype(b.dtype)

    nt.defvjp(nt_f, nt_b)

    @jax.custom_vjp
    def tn(a, b):
        return dg(a, b, 0, 0)

    def tn_f(a, b):
        return dg(a, b, 0, 0), (a, b)

    def tn_b(res, g):
        a, b = res
        return dg(b, g, 1, 1).astype(a.dtype), dg(a, g, 1, 0).astype(b.dtype)

    tn.defvjp(tn_f, tn_b)
    return nn, nt, tn


bdot, bdot_nt, bdot_tn = _make_dots(bf16, None)
hdot, hdot_nt, hdot_tn = _make_dots(None, lax.Precision.HIGHEST)


def _silu(x):
    return x * jax.nn.sigmoid(x)


def _rms(x, n):
    return x * lax.rsqrt(jnp.sum(x * x, axis=-1, keepdims=True) * (1.0 / n) + EPS)


def _place():
    return lax.axis_index("x"), lax.axis_index("y"), lax.axis_index("c")


def _gather_phases(ins, outs, send_sems, recv_sems, local_sems):
    n = len(ins)
    x, y, c = _place()
    me, sibling = (x, y, c), (x, y, 1 - c)
    chips = [(1 - x, y), (x, 1 - y), (1 - x, 1 - y)]

    def slot(o, p):
        return outs[o].at[4 * p[0] + 2 * p[1] + p[2]]

    def copy(o, k, block, to, src=None):
        dst = slot(o, block)
        return pltpu.make_async_remote_copy(
            src_ref=dst if src is None else src, dst_ref=dst,
            send_sem=send_sems.at[o, k], recv_sem=recv_sems.at[o, k],
            device_id=to, device_id_type=MESH)

    def local(o):
        return pltpu.make_async_copy(ins[o], slot(o, me), local_sems.at[o])

    def first(o):
        return [copy(o, 0, me, sibling, src=ins[o])] + [copy(o, 1 + j, me, (*chip, c), src=ins[o])
                                                        for j, chip in enumerate(chips)]

    def start():
        for o in range(n):
            local(o).start()
        for o in range(n):
            for cp in first(o):
                cp.start()

    def mid():
        for o in range(n):
            for j, chip in enumerate(chips):
                copy(o, 1 + j, (*chip, c), me).wait_recv()
                copy(o, 4 + j, (*chip, c), sibling).start()

    def finish():
        for o in range(n):
            copy(o, 0, sibling, me).wait_recv()
            for j, chip in enumerate(chips):
                copy(o, 4 + j, (*chip, 1 - c), me).wait_recv()
        for o in range(n):
            for cp in first(o):
                cp.wait_send()
            for j, chip in enumerate(chips):
                copy(o, 4 + j, (*chip, c), sibling).wait_send()
        for o in range(n):
            local(o).wait()

    return start, mid, finish


def _gather_scratch(n):
    return [pltpu.SemaphoreType.DMA((n, 7)), pltpu.SemaphoreType.DMA((n, 7)), pltpu.SemaphoreType.DMA((n,))]


def all_gather(ops, name):
    n = len(ops)

    def body(*refs):
        start, mid, finish = _gather_phases(refs[:n], refs[n:2 * n], *refs[2 * n:])
        start()
        mid()
        finish()

    anyspec = pl.BlockSpec(memory_space=pl.ANY)
    return pl.pallas_call(
        body, name=name,
        out_shape=[jax.ShapeDtypeStruct((N_DEV,) + o.shape, o.dtype) for o in ops],
        in_specs=[anyspec] * n, out_specs=[anyspec] * n,
        scratch_shapes=_gather_scratch(n),
    )(*ops)


def pair_exchange(gs, name):
    n = len(gs)

    def body(*refs):
        start, finish = _pair_exchange_phases(refs[:n], refs[n:2 * n], *refs[2 * n:])
        start()
        finish()

    anyspec = pl.BlockSpec(memory_space=pl.ANY)
    return pl.pallas_call(
        body, name=name, out_shape=[jax.ShapeDtypeStruct(_pair_exchange_shape(g), g.dtype) for g in gs],
        in_specs=[anyspec] * n, out_specs=[anyspec] * n,
        scratch_shapes=_pair_exchange_scratch(n),
    )(*gs)


def _pair_exchange_phases(g_refs, out_refs, send_sems, recv_sems):
    n = len(g_refs)
    x, y, c = _place()

    def copies():
        return [pltpu.make_async_remote_copy(
            src_ref=g_refs[o].at[2 * q + (1 - c)], dst_ref=out_refs[o].at[q],
            send_sem=send_sems.at[o, q], recv_sem=recv_sems.at[o, q],
            device_id=(x, y, 1 - c), device_id_type=MESH) for o in range(n) for q in range(4)]

    def start():
        for cp in copies():
            cp.start()

    def finish():
        for cp in copies():
            cp.wait_recv()
        for cp in copies():
            cp.wait_send()

    return start, finish


def _pair_exchange_scratch(n):
    return [pltpu.SemaphoreType.DMA((n, 4)), pltpu.SemaphoreType.DMA((n, 4))]


def _pair_exchange_shape(g):
    return (4,) + g.shape[1:]


def _chip_exchange_phases(p_refs, out_refs, send_sems, recv_sems):
    n = len(p_refs)
    x, y, c = _place()
    chips = [(1 - x, y), (x, 1 - y), (1 - x, 1 - y)]

    def copies():
        return [pltpu.make_async_remote_copy(
            src_ref=p_refs[o].at[j], dst_ref=out_refs[o].at[j],
            send_sem=send_sems.at[o, j], recv_sem=recv_sems.at[o, j],
            device_id=(*chip, c), device_id_type=MESH) for o in range(n) for j, chip in enumerate(chips)]

    def start():
        for cp in copies():
            cp.start()

    def finish():
        for cp in copies():
            cp.wait_recv()
        for cp in copies():
            cp.wait_send()

    return start, finish


def _chip_exchange_scratch(n):
    return [pltpu.SemaphoreType.DMA((n, 3)), pltpu.SemaphoreType.DMA((n, 3))]


def chip_partials(g, recv, slots, name):
    _, R, C = g.shape
    tr = _pick_rows(R, 512)

    def body(s_ref, a_ref, b_ref, o_ref):
        o_ref[...] = (a_ref[...].astype(f32) + b_ref[...].astype(f32)).astype(o_ref.dtype)

    grid_spec = pltpu.PrefetchScalarGridSpec(
        num_scalar_prefetch=1, grid=(3, R // tr),
        in_specs=[pl.BlockSpec((None, tr, C), lambda j, i, s: (s[j], i, 0)),
                  pl.BlockSpec((None, tr, C), lambda j, i, s: (s[j] // 2, i, 0))],
        out_specs=pl.BlockSpec((None, tr, C), lambda j, i, s: (j, i, 0)))
    return pl.pallas_call(body, name=name, grid_spec=grid_spec,
                          out_shape=jax.ShapeDtypeStruct((3, R, C), g.dtype),
                          compiler_params=_cp(("arbitrary", "arbitrary")))(slots, g, recv)


def adamw(parts, w, m, v, name, rows=256):
    R, C = w.shape
    tr = _pick_rows(R, rows)
    npart = len(parts)
    c1 = 1.0 - ADAM_B1 ** ADAM_STEP
    c2 = 1.0 - ADAM_B2 ** ADAM_STEP

    def body(*refs):
        p_refs = refs[:npart]
        w_ref, m_ref, v_ref, g_out, d_out, m_out, v_out = refs[npart:]
        g = p_refs[0][...].astype(f32)
        for p in p_refs[1:]:
            g = g + p[...].astype(f32)
        mn = ADAM_B1 * m_ref[...] + (1.0 - ADAM_B1) * g
        vn = ADAM_B2 * v_ref[...] + (1.0 - ADAM_B2) * (g * g)
        m_hat = mn / c1
        v_hat = vn / c2
        g_out[...] = g
        d_out[...] = -ADAM_LR * (m_hat / (jnp.sqrt(v_hat) + ADAM_EPS) + ADAM_WD * w_ref[...])
        m_out[...] = mn
        v_out[...] = vn

    spec = pl.BlockSpec((tr, C), lambda i: (i, 0))
    return pl.pallas_call(
        body, name=name, grid=(R // tr,),
        in_specs=[spec] * (npart + 3), out_specs=[spec] * 4,
        out_shape=[jax.ShapeDtypeStruct((R, C), f32)] * 4,
        compiler_params=_cp(("arbitrary",)))(*parts, w, m, v)


def mm(a, b, *, name, ta=False, tb=False, a_fn=None, epi=None, extras=(), out_dtype=f32, out_blocks=False, side=(),
       side_pair=False, tm=1024, tn=1024, tk=2048):
    K, M = a.shape if ta else a.shape[::-1]
    b3 = b.ndim == 3
    if b3:
        assert not tb
        N, K2 = N_DEV * b.shape[2], b.shape[1]
    else:
        N, K2 = b.shape if tb else b.shape[::-1]
    assert K == K2, (a.shape, b.shape, ta, tb)
    n_split = N // N_DEV if (out_blocks or b3) else N
    tm, tn, tk = _pick(M, tm), _pick(n_split, tn), _pick(K, tk)
    nb = n_split // tn
    nk = K // tk
    ne = len(extras)
    assert not (out_blocks and ne)
    dims = (((0 if ta else 1,), (1 if tb else 0,)), ((), ()))

    ns = len(side)
    n_steps = (M // tm) * (N // tn) * nk
    assert ns == 0 or n_steps >= 2
    if side_pair:
        side_phases, side_scratch, side_shape = _pair_exchange_phases, _pair_exchange_scratch, _pair_exchange_shape
    else:
        side_phases, side_scratch, side_shape = _chip_exchange_phases, _chip_exchange_scratch, lambda p: p.shape

    def body(a_ref, b_ref, *rest):
        e_refs, o_ref, acc = rest[:ne], rest[ne + ns], rest[ne + 2 * ns + 1]
        k = pl.program_id(2)
        if ns:
            step = (pl.program_id(0) * (N // tn) + pl.program_id(1)) * nk + k
            x_start, x_finish = side_phases(rest[ne:ne + ns], rest[ne + ns + 1:ne + 2 * ns + 1], *rest[ne + 2 * ns + 2:])
            pl.when(step == 0)(x_start)

        @pl.when(k == 0)
        def _():
            acc[...] = jnp.zeros_like(acc)

        av = a_ref[...]
        if a_fn is not None:
            av = a_fn(av.astype(f32))
        acc[...] += lax.dot_general(av.astype(bf16), b_ref[...].astype(bf16), dims, preferred_element_type=f32)

        @pl.when(k == nk - 1)
        def _():
            r = acc[...]
            if epi is not None:
                r = epi(r, *[e[...] for e in e_refs])
            o_ref[...] = r.astype(o_ref.dtype)

        if ns:
            pl.when(step == n_steps - 1)(x_finish)

    a_spec = pl.BlockSpec((tk, tm), lambda i, j, k: (k, i)) if ta else pl.BlockSpec((tm, tk), lambda i, j, k: (i, k))
    if b3:
        b_spec = pl.BlockSpec((None, tk, tn), lambda i, j, k: (j // nb, k, j % nb))
    else:
        b_spec = pl.BlockSpec((tn, tk), lambda i, j, k: (j, k)) if tb else pl.BlockSpec((tk, tn), lambda i, j, k: (k, j))
    if out_blocks:
        o_spec = pl.BlockSpec((None, tm, tn), lambda i, j, k: (j // nb, i, j % nb))
        o_shape = jax.ShapeDtypeStruct((N_DEV, M, N // N_DEV), out_dtype)
    else:
        o_spec = pl.BlockSpec((tm, tn), lambda i, j, k: (i, j))
        o_shape = jax.ShapeDtypeStruct((M, N), out_dtype)
    if not ns:
        return pl.pallas_call(
            body, name=name, grid=(M // tm, N // tn, nk),
            in_specs=[a_spec, b_spec] + [o_spec] * ne, out_specs=o_spec,
            out_shape=o_shape,
            scratch_shapes=[pltpu.VMEM((tm, tn), f32)],
            compiler_params=_cp(("parallel", "parallel", "arbitrary")))(a, b, *extras)
    anyspec = pl.BlockSpec(memory_space=pl.ANY)
    res = pl.pallas_call(
        body, name=name, grid=(M // tm, N // tn, nk),
        in_specs=[a_spec, b_spec] + [o_spec] * ne + [anyspec] * ns, out_specs=[o_spec] + [anyspec] * ns,
        out_shape=[o_shape] + [jax.ShapeDtypeStruct(side_shape(p), p.dtype) for p in side],
        scratch_shapes=[pltpu.VMEM((tm, tn), f32)] + side_scratch(ns),
        compiler_params=_cp(("arbitrary", "arbitrary", "arbitrary")))(a, b, *extras, *side)
    return res[0], list(res[1:])


class Row:
    def __init__(self, arr, width=None, col=0, lead=None, diff=True):
        self.arr, self.col, self.lead, self.diff = arr, col, lead, diff
        self.width = arr.shape[-1] if width is None else width

    def spec(self, t):
        col, lead = self.col, self.lead
        if lead is None:
            return pl.BlockSpec((t, self.width), lambda i: (i, col))
        return pl.BlockSpec((None, t, self.width), lambda i: (lead, i, col))


def _whole(p):
    return pl.BlockSpec(p.shape, lambda i: (0,) * p.ndim)


def rowwise(fn, rows, params, outs, *, n_rows, tile, name, side=()):
    t = _pick_rows(n_rows, tile)
    nr, npar, no, ns = len(rows), len(params), len(outs), len(side)
    n_steps = n_rows // t
    assert ns == 0 or n_steps >= 3

    def body(*refs):
        r_refs, p_refs = refs[:nr], refs[nr:nr + npar]
        o_refs = refs[nr + npar + ns:nr + npar + ns + no]
        if ns:
            s_in = refs[nr + npar:nr + npar + ns]
            s_out = refs[nr + npar + ns + no:nr + npar + 2 * ns + no]
            g_start, g_mid, g_finish = _gather_phases(s_in, s_out, *refs[nr + npar + 2 * ns + no:])
            pl.when(pl.program_id(0) == 0)(g_start)
            pl.when(pl.program_id(0) == n_steps // 2)(g_mid)
        res = fn(*[r[...].astype(f32) for r in r_refs], *[p[...] for p in p_refs])
        for o_ref, val in zip(o_refs, res):
            o_ref[...] = val.astype(o_ref.dtype)
        if ns:
            pl.when(pl.program_id(0) == n_steps - 1)(g_finish)

    anyspec = pl.BlockSpec(memory_space=pl.ANY)
    res = pl.pallas_call(
        body, name=name, grid=(n_steps,),
        in_specs=[r.spec(t) for r in rows] + [_whole(p) for p in params] + [anyspec] * ns,
        out_specs=[pl.BlockSpec((t, w), lambda i: (i, 0)) for w, _ in outs] + [anyspec] * ns,
        out_shape=[jax.ShapeDtypeStruct((n_rows, w), dt) for w, dt in outs]
        + [jax.ShapeDtypeStruct((N_DEV,) + a.shape, a.dtype) for a in side],
        scratch_shapes=_gather_scratch(ns) if ns else [],
        compiler_params=_cp(("arbitrary",)))(*[r.arr for r in rows], *params, *side)
    return res


def rowwise_vjp(fn, rows, params, cts, *, n_rows, tile, name, row_grad_dtypes=None, param_diff=None):
    t = _pick_rows(n_rows, tile)
    nr, npar, nc = len(rows), len(params), len(cts)
    param_diff = [True] * npar if param_diff is None else param_diff
    d_rows = [k for k, r in enumerate(rows) if r.diff]
    d_pars = [k for k in range(npar) if param_diff[k]]
    row_grad_dtypes = [f32] * len(d_rows) if row_grad_dtypes is None else row_grad_dtypes

    def body(*refs):
        r_refs, p_refs = refs[:nr], refs[nr:nr + npar]
        c_refs = refs[nr + npar:nr + npar + nc]
        dr_refs = refs[nr + npar + nc:nr + npar + nc + len(d_rows)]
        dp_refs = refs[nr + npar + nc + len(d_rows):]
        rv = [r[...].astype(f32) for r in r_refs]
        pv = [p[...] for p in p_refs]

        def g(*dvals):
            full_r, full_p = list(rv), list(pv)
            for k, val in zip(d_rows, dvals[:len(d_rows)]):
                full_r[k] = val
            for k, val in zip(d_pars, dvals[len(d_rows):]):
                full_p[k] = val
            return tuple(fn(*full_r, *full_p))

        prim = [rv[k] for k in d_rows] + [pv[k].astype(f32) for k in d_pars]
        _, pull = jax.vjp(g, *prim)
        grads = pull(tuple(c[...].astype(f32) for c in c_refs))
        for ref, val in zip(dr_refs, grads[:len(d_rows)]):
            ref[...] = val.astype(ref.dtype)

        @pl.when(pl.program_id(0) == 0)
        def _():
            for ref in dp_refs:
                ref[...] = jnp.zeros_like(ref)

        for ref, val in zip(dp_refs, grads[len(d_rows):]):
            ref[...] += val

    out_specs = [pl.BlockSpec((t, rows[k].width), lambda i: (i, 0)) for k in d_rows]
    out_specs += [_whole(params[k]) for k in d_pars]
    out_shape = [jax.ShapeDtypeStruct((n_rows, rows[k].width), dt) for k, dt in zip(d_rows, row_grad_dtypes)]
    out_shape += [jax.ShapeDtypeStruct(params[k].shape, f32) for k in d_pars]
    res = pl.pallas_call(
        body, name=name, grid=(n_rows // t,),
        in_specs=[r.spec(t) for r in rows] + [_whole(p) for p in params] + [c.spec(t) for c in cts],
        out_specs=out_specs, out_shape=out_shape,
        compiler_params=_cp(("arbitrary",)))(*[r.arr for r in rows], *params, *[c.arr for c in cts])
    return res[:len(d_rows)], res[len(d_rows):]


def f_norm_mod(x, g, shift, scale):
    return (_rms(x, x.shape[-1]) * g * (1.0 + scale) + shift,)


def f_norm_mod_thru(x, g, shift, scale):
    return f_norm_mod(x, g, shift, scale) + (x,)


def f_resid_norm_mod(x, mixed, gate1, g2, shift2, scale2):
    x1 = x + gate1 * mixed
    return (x1,) + f_norm_mod(x1, g2, shift2, scale2)


def _rope_rot():
    i = lax.broadcasted_iota(jnp.int32, (LANE, LANE), 0)
    j = lax.broadcasted_iota(jnp.int32, (LANE, LANE), 1)
    neg = jnp.where((i == j + HALF) & (j < HALF), -1.0, 0.0)
    pos = jnp.where((i == j - HALF) & (j >= HALF) & (j < ROPE), 1.0, 0.0)
    return (neg + pos).astype(f32)


def make_f_mla_prep(n_heads, q_scale):
    def fn(cq, ckv, kpe, pos, gq, gkv, gqn, gkn, w_uq, w_ukv, freqs):
        rot = _rope_rot()
        ang = pos * freqs
        cos, sin = jnp.cos(ang), jnp.sin(ang)

        def rope(u):
            return u * cos + hdot(u, rot) * sin

        qraw = bdot(_rms(cq, cq.shape[-1]) * gq, w_uq)
        kv = bdot(_rms(ckv, ckv.shape[-1]) * gkv, w_ukv)
        kpe_ss = jnp.sum(kpe * kpe, axis=-1, keepdims=True)
        qs, ks = [], []
        for h in range(n_heads):
            qh = _rms(qraw[:, h * QK_PAD:(h + 1) * QK_PAD], QK_DIM) * gqn
            qs += [qh[:, :NOPE], rope(qh[:, NOPE:])]
            kn = kv[:, h * NOPE:(h + 1) * NOPE]
            r = lax.rsqrt((jnp.sum(kn * kn, axis=-1, keepdims=True) + kpe_ss) * (1.0 / QK_DIM) + EPS)
            ks += [kn * r * gkn[:, :NOPE], rope(kpe * r * gkn[:, NOPE:])]
        return jnp.concatenate(qs, axis=-1) * q_scale, jnp.concatenate(ks, axis=-1), kv[:, n_heads * NOPE:]
    return fn


def make_f_mlstm_post(n_heads, dm):
    def fn(hf, hb, o, g):
        hm = hf + hb
        outs = []
        for h in range(n_heads):
            sl = slice(h * dm, (h + 1) * dm)
            outs.append(jax.nn.sigmoid(o[:, sl]) * (_rms(hm[:, sl], dm) * g[:, sl]))
        return (jnp.concatenate(outs, axis=-1),)
    return fn


def loss_head(x1, y, target, gate2, name, tile=256):
    S, D = x1.shape
    t = _pick_rows(S, tile)

    def body(x1_ref, y_ref, t_ref, g_ref, loss_ref, dout_ref, dy_ref, dgate_ref):
        @pl.when(pl.program_id(0) == 0)
        def _():
            loss_ref[...] = jnp.zeros_like(loss_ref)
            dgate_ref[...] = jnp.zeros_like(dgate_ref)

        yv, gv = y_ref[...], g_ref[...]
        e = x1_ref[...] + gv * yv - t_ref[...]
        loss_ref[...] += 0.5 * jnp.sum(jnp.sum(e * e, axis=-1, keepdims=True) * (1.0 / D), axis=0, keepdims=True)
        d_out = e * (1.0 / D)
        dout_ref[...] = d_out
        dy_ref[...] = (d_out * gv).astype(dy_ref.dtype)
        dgate_ref[...] += jnp.sum(d_out * yv, axis=0, keepdims=True)

    row = pl.BlockSpec((t, D), lambda i: (i, 0))
    return pl.pallas_call(
        body, name=name, grid=(S // t,),
        in_specs=[row, row, row, pl.BlockSpec((1, D), lambda i: (0, 0))],
        out_specs=[pl.BlockSpec((1, 1), lambda i: (0, 0)), row, row, pl.BlockSpec((1, D), lambda i: (0, 0))],
        out_shape=[jax.ShapeDtypeStruct((1, 1), f32), jax.ShapeDtypeStruct((S, D), f32),
                   jax.ShapeDtypeStruct((S, D), bf16), jax.ShapeDtypeStruct((1, D), f32)],
        compiler_params=_cp(("arbitrary",)))(x1, y, target, gate2)


def ada_fwd(c_all, w_blk, b_blk, name):
    B, D = c_all.shape
    N = w_blk.shape[1]
    tn = _pick(N, 512)

    def body(c_ref, w_ref, b_ref, o_ref):
        o_ref[...] = bdot(_silu(c_ref[...]), w_ref[...]) + b_ref[...]

    return pl.pallas_call(
        body, name=name, grid=(N // tn,),
        in_specs=[pl.BlockSpec((B, D), lambda j: (0, 0)), pl.BlockSpec((D, tn), lambda j: (0, j)),
                  pl.BlockSpec((1, tn), lambda j: (0, j))],
        out_specs=pl.BlockSpec((B, tn), lambda j: (0, j)),
        out_shape=jax.ShapeDtypeStruct((B, N), f32), compiler_params=_cp(("arbitrary",)))(c_all, w_blk, b_blk)


def ada_wgrad(c_all, dmod_blk, name):
    B, D = c_all.shape
    N = dmod_blk.shape[1]
    tn = _pick(N, 512)

    def body(c_ref, d_ref, o_ref):
        o_ref[...] = hdot_tn(_silu(c_ref[...]), d_ref[...])

    return pl.pallas_call(
        body, name=name, grid=(N // tn,),
        in_specs=[pl.BlockSpec((B, D), lambda j: (0, 0)), pl.BlockSpec((B, tn), lambda j: (0, j))],
        out_specs=pl.BlockSpec((D, tn), lambda j: (0, j)),
        out_shape=jax.ShapeDtypeStruct((D, N), f32), compiler_params=_cp(("arbitrary",)))(c_all, dmod_blk)


def _nt(a, b):
    return lax.dot_general(a, b, (((1,), (1,)), ((), ())), preferred_element_type=f32)


def _tn(a, b):
    return lax.dot_general(a, b, (((0,), (0,)), ((), ())), preferred_element_type=f32)


def flash_fwd(q, k, v, n_heads, name, side=(), tq=512, tk=8192, sub=1024):
    S = q.shape[0]
    tq, tk = _pick(S, tq), _pick(S, tk)
    sub = _pick(tk, sub)
    nk, nsub = S // tk, tk // sub
    ns = len(side)
    n_steps = n_heads * (S // tq) * nk
    assert ns == 0 or n_steps >= 3

    def body(*refs):
        q_ref, k_ref, v_ref = refs[:3]
        o_ref, lse_ref = refs[3 + ns:5 + ns]
        m_sc, l_sc, acc_sc = refs[5 + 2 * ns:8 + 2 * ns]
        j = pl.program_id(2)
        step = (pl.program_id(0) * (S // tq) + pl.program_id(1)) * nk + j
        if ns:
            g_start, g_mid, g_finish = _gather_phases(refs[3:3 + ns], refs[5 + ns:5 + 2 * ns], *refs[8 + 2 * ns:])
            pl.when(step == 0)(g_start)
            pl.when(step == n_steps // 2)(g_mid)

        @pl.when(j == 0)
        def _():
            m_sc[...] = jnp.full_like(m_sc, -jnp.inf)
            l_sc[...] = jnp.zeros_like(l_sc)
            acc_sc[...] = jnp.zeros_like(acc_sc)

        qv = q_ref[...]
        m = m_sc[...]
        ss = [_nt(qv, k_ref[b * sub:(b + 1) * sub, :]) for b in range(nsub)]
        mx = ss[0]
        for s in ss[1:]:
            mx = jnp.maximum(mx, s)
        m_new = jnp.maximum(m, jnp.max(mx, axis=-1, keepdims=True))
        alpha = jnp.exp2(m - m_new)
        psum, pv = None, None
        for b in range(nsub):
            p = jnp.exp2(ss[b] - m_new)
            d = jnp.dot(p.astype(bf16), v_ref[b * sub:(b + 1) * sub, :], preferred_element_type=f32)
            psum = p if psum is None else psum + p
            pv = d if pv is None else pv + d
        m, l, acc = m_new, alpha * l_sc[...] + jnp.sum(psum, axis=-1, keepdims=True), alpha * acc_sc[...] + pv
        m_sc[...], l_sc[...], acc_sc[...] = m, l, acc

        @pl.when(j == nk - 1)
        def _():
            o_ref[...] = (acc / l).astype(o_ref.dtype)
            lse_ref[...] = m + jnp.log2(l)

        if ns:
            pl.when(step == n_steps - 1)(g_finish)

    anyspec = pl.BlockSpec(memory_space=pl.ANY)
    res = pl.pallas_call(
        body, name=name, grid=(n_heads, S // tq, nk),
        in_specs=[pl.BlockSpec((tq, QK_PAD), lambda h, i, j: (i, h)),
                  pl.BlockSpec((tk, QK_PAD), lambda h, i, j: (j, h)),
                  pl.BlockSpec((tk, V_DIM), lambda h, i, j: (j, h))] + [anyspec] * ns,
        out_specs=[pl.BlockSpec((tq, V_DIM), lambda h, i, j: (i, h)),
                   pl.BlockSpec((None, tq, 1), lambda h, i, j: (h, i, 0))] + [anyspec] * ns,
        out_shape=[jax.ShapeDtypeStruct((S, n_heads * V_DIM), bf16), jax.ShapeDtypeStruct((n_heads, S, 1), f32)]
        + [jax.ShapeDtypeStruct((N_DEV,) + a.shape, a.dtype) for a in side],
        scratch_shapes=[pltpu.VMEM((tq, 1), f32), pltpu.VMEM((tq, 1), f32), pltpu.VMEM((tq, V_DIM), f32)]
        + (_gather_scratch(ns) if ns else []),
        compiler_params=_cp(("arbitrary", "arbitrary", "arbitrary")))(q, k, v, *side)
    return res[0], res[1], list(res[2:])


def flash_bwd(q, k, v, o, lse_row, do, do_col0, n_heads, name, side=(), tq=1024, tk=8192, sub=512):
    S = q.shape[0]
    tq, tk = _pick(S, tq), _pick(S, tk)
    sub = _pick(tk, sub)
    nsub = tk // sub
    ln2 = math.log(2.0)
    ns = len(side)
    n_steps = n_heads * (S // tq) * (S // tk)
    assert ns == 0 or n_steps >= 2

    def body(*refs):
        q_ref, k_ref, v_ref, o_ref, lse_ref, do_ref = refs[:6]
        dq_ref, dk_ref, dv_ref = refs[6 + ns:9 + ns]
        i, j = pl.program_id(1), pl.program_id(2)
        step = (pl.program_id(0) * (S // tq) + i) * (S // tk) + j
        if ns:
            x_start, x_finish = _chip_exchange_phases(refs[6:6 + ns], refs[9 + ns:9 + 2 * ns], *refs[9 + 2 * ns:])
            pl.when(step == 0)(x_start)

        @pl.when(j == 0)
        def _():
            dq_ref[...] = jnp.zeros_like(dq_ref)

        @pl.when((i == 0) & (j == 0))
        def _():
            dk_ref[...] = jnp.zeros_like(dk_ref)
            dv_ref[...] = jnp.zeros_like(dv_ref)

        qv = q_ref[...]
        dof = do_ref[...].astype(f32)
        do_b = dof.astype(bf16)
        do_s = (dof * ln2).astype(bf16)
        delta = hdot_nt(jnp.ones((8, V_DIM), f32), dof * ln2 * o_ref[...].astype(f32))[0:1, :]
        lse = lse_ref[...]
        dq = jnp.zeros((tq, QK_PAD), f32)
        for b in range(nsub):
            kb = k_ref[b * sub:(b + 1) * sub, :]
            rows = pl.ds(pl.multiple_of(j * tk + b * sub, sub), sub)
            pt = jnp.exp2(_nt(kb, qv) - lse)
            dpt = _nt(v_ref[b * sub:(b + 1) * sub, :], do_s)
            dst = (pt * (dpt - delta)).astype(bf16)
            dv_ref[rows, :] += jnp.dot(pt.astype(bf16), do_b, preferred_element_type=f32)
            dk_ref[rows, :] += jnp.dot(dst, qv, preferred_element_type=f32)
            dq = dq + _tn(dst, kb)
        dq_ref[...] += dq
        if ns:
            pl.when(step == n_steps - 1)(x_finish)

    anyspec = pl.BlockSpec(memory_space=pl.ANY)
    res = pl.pallas_call(
        body, name=name, grid=(n_heads, S // tq, S // tk),
        in_specs=[pl.BlockSpec((tq, QK_PAD), lambda h, i, j: (i, h)),
                  pl.BlockSpec((tk, QK_PAD), lambda h, i, j: (j, h)),
                  pl.BlockSpec((tk, V_DIM), lambda h, i, j: (j, h)),
                  pl.BlockSpec((tq, V_DIM), lambda h, i, j: (i, h)),
                  pl.BlockSpec((None, 1, tq), lambda h, i, j: (h, 0, i)),
                  pl.BlockSpec((tq, V_DIM), lambda h, i, j: (i, do_col0 + h))] + [anyspec] * ns,
        out_specs=[pl.BlockSpec((tq, QK_PAD), lambda h, i, j: (i, h)),
                   pl.BlockSpec((S, QK_PAD), lambda h, i, j: (0, h)),
                   pl.BlockSpec((S, V_DIM), lambda h, i, j: (0, h))] + [anyspec] * ns,
        out_shape=[jax.ShapeDtypeStruct((S, n_heads * QK_PAD), f32), jax.ShapeDtypeStruct((S, n_heads * QK_PAD), f32),
                   jax.ShapeDtypeStruct((S, n_heads * V_DIM), f32)] + [jax.ShapeDtypeStruct(a.shape, a.dtype) for a in side],
        scratch_shapes=_chip_exchange_scratch(ns) if ns else [],
        compiler_params=_cp(("arbitrary", "arbitrary", "arbitrary")))(q, k, v, o, lse_row, do, *side)
    return res[0], res[1], res[2], list(res[3:])


def _shifted(prev, cur, nxt, k, first, last):
    if k == 0:
        return cur
    t = cur.shape[0]
    r = lax.broadcasted_iota(jnp.int32, (HALO,) + cur.shape[1:], 0)
    if k < 0:
        body = pltpu.roll(cur, -k, 0)
        edge = jnp.where(first, 0.0, pltpu.roll(prev, -k, 0))
        return jnp.concatenate([jnp.where(r < -k, edge, body[:HALO]), body[HALO:]], axis=0)
    body = pltpu.roll(cur, t - k, 0)
    edge = jnp.where(last, 0.0, pltpu.roll(nxt, HALO - k, 0))
    return jnp.concatenate([body[:t - HALO], jnp.where(r >= HALO - k, edge, body[t - HALO:])], axis=0)


HALO = 8


def _halo_specs(t, width, n_tiles, lead=None):
    per = t // HALO
    rows = [(HALO, lambda i: jnp.maximum(i * per - 1, 0)), (t, lambda i: i),
            (HALO, lambda i: jnp.minimum((i + 1) * per, n_tiles * per - 1))]
    if lead is None:
        return [pl.BlockSpec((r, width), lambda i, f=f: (f(i), 0)) for r, f in rows]
    return [pl.BlockSpec((None, r, width), lambda i, f=f: (lead, f(i), 0)) for r, f in rows]


def conv_fwd(proj, width, w, b, name, tile=256):
    S = proj.shape[0]
    t = _pick_rows(S, tile)
    n_tiles = S // t

    def body(p_ref, c_ref, n_ref, w_ref, b_ref, z_ref):
        i = pl.program_id(0)
        first, last = i == 0, i == n_tiles - 1
        prev, cur, nxt = p_ref[...], c_ref[...], n_ref[...]
        z = b_ref[...] + jnp.zeros_like(cur)
        for j in range(CONV_W):
            z = z + w_ref[j:j + 1, :] * _shifted(prev, cur, nxt, j - CONV_W // 2, first, last)
        z_ref[...] = z

    return pl.pallas_call(
        body, name=name, grid=(n_tiles,),
        in_specs=_halo_specs(t, width, n_tiles) + [_whole(w), _whole(b)],
        out_specs=pl.BlockSpec((t, width), lambda i: (i, 0)),
        out_shape=jax.ShapeDtypeStruct((S, width), f32),
        compiler_params=_cp(("arbitrary",)))(proj, proj, proj, w, b)


def conv_bwd(dzq, dzk, dvm, proj, width, w, name, tile=256):
    S = proj.shape[0]
    t = _pick_rows(S, tile)
    n_tiles = S // t
    half = width // 2

    def body(*refs):
        d_refs, (v0_ref, v1_ref, up_ref, uc_ref, un_ref, w_ref, du_ref, dw_ref, db_ref, dv_ref) = refs[:12], refs[12:]
        i = pl.program_id(0)
        first, last = i == 0, i == n_tiles - 1

        @pl.when(first)
        def _():
            dw_ref[...] = jnp.zeros_like(dw_ref)
            db_ref[...] = jnp.zeros_like(db_ref)

        dv_ref[...] = (v0_ref[...] + v1_ref[...]).astype(dv_ref.dtype)

        dprev, dcur, dnxt = [jnp.concatenate([d_refs[p][...] + d_refs[3 + p][...], d_refs[6 + p][...] + d_refs[9 + p][...]],
                                             axis=1) for p in range(3)]
        uprev, ucur, unxt = up_ref[...], uc_ref[...], un_ref[...]
        du = jnp.zeros_like(dcur)
        for j in range(CONV_W):
            k = j - CONV_W // 2
            du = du + w_ref[j:j + 1, :] * _shifted(dprev, dcur, dnxt, -k, first, last)
            dw_ref[j:j + 1, :] += jnp.sum(dcur * _shifted(uprev, ucur, unxt, k, first, last), axis=0, keepdims=True)
        du_ref[...] = du
        db_ref[...] += jnp.sum(dcur, axis=0, keepdims=True)

    return pl.pallas_call(
        body, name=name, grid=(n_tiles,),
        in_specs=_halo_specs(t, half, n_tiles, 0) + _halo_specs(t, half, n_tiles, 1) + _halo_specs(t, half, n_tiles, 0)
        + _halo_specs(t, half, n_tiles, 1)
        + [pl.BlockSpec((None, t, half), lambda i: (0, i, 0)), pl.BlockSpec((None, t, half), lambda i: (1, i, 0))]
        + _halo_specs(t, width, n_tiles) + [_whole(w)],
        out_specs=[pl.BlockSpec((t, width), lambda i: (i, 0)), pl.BlockSpec((8, width), lambda i: (0, 0)),
                   pl.BlockSpec((1, width), lambda i: (0, 0)), pl.BlockSpec((t, half), lambda i: (i, 0))],
        out_shape=[jax.ShapeDtypeStruct((S, width), f32), jax.ShapeDtypeStruct((8, width), f32),
                   jax.ShapeDtypeStruct((1, width), f32), jax.ShapeDtypeStruct((S, half), bf16)],
        compiler_params=_cp(("arbitrary",)))(*([dzq] * 6), *([dzk] * 6), dvm, dvm, proj, proj, proj, w)


def _mlstm_step(dm, d, C, n, m, zq, zk, v, ic, fc, ir, fr, bi, bf_):
    L = zq.shape[0]
    q = _silu(zq)
    k = _silu(zk) * (dm ** -0.5)
    i_c, f_c = ic + bi, jax.nn.log_sigmoid(fc + bf_)
    i_r, f_r = ir + bi, jax.nn.log_sigmoid(fr + bf_)
    r = lax.broadcasted_iota(jnp.int32, (L, L), 0)
    c = lax.broadcasted_iota(jnp.int32, (L, L), 1)
    sgn = jnp.where(d == 0, r - c, c - r)
    mask = sgn >= 0
    b_c = jnp.sum(jnp.where(mask, f_r, 0.0), axis=-1, keepdims=True)
    b_r = jnp.sum(jnp.where(sgn <= 0, f_c, 0.0), axis=0, keepdims=True)
    log_inter = b_c + m
    logD = jnp.where(mask, b_c - b_r + i_r, -jnp.inf)
    m_t = jnp.maximum(log_inter, jnp.max(logD, axis=-1, keepdims=True))
    Dm = jnp.exp(logD - m_t)
    w_inter = jnp.exp(log_inter - m_t)
    scores = bdot_nt(q, k) * Dm
    num = bdot(scores, v) + w_inter * bdot_nt(q, C)
    den = jnp.sum(scores, axis=-1, keepdims=True) + w_inter * jnp.sum(q * n, axis=-1, keepdims=True)
    h = num / jnp.maximum(jnp.abs(den), jnp.exp(-m_t))
    bL = jnp.sum(f_c, axis=0, keepdims=True)
    log_w = bL - b_c + i_c
    m_new = jnp.maximum(bL + m, jnp.max(log_w, axis=0, keepdims=True))
    decay = jnp.exp(bL + m - m_new)
    w = jnp.exp(log_w - m_new)
    C_new = decay * C + bdot_tn(w * v, k)
    n_new = decay * n + jnp.sum(w * k, axis=0, keepdims=True)
    return C_new, n_new, m_new, h


def _mlstm_in_specs(L, dm, hm, hb, nc, step_of):
    ng = hm // hb

    def chunk(d, j):
        s = step_of(j)
        return s + d * (nc - 1 - 2 * s)
    return [
        pl.BlockSpec((L, hb * dm), lambda d, g, j: (chunk(d, j), g)),
        pl.BlockSpec((L, hb * dm), lambda d, g, j: (chunk(d, j), ng + g)),
        pl.BlockSpec((L, hb * dm), lambda d, g, j: (chunk(d, j), 2 * ng + g)),
        pl.BlockSpec((None, hb, L, 1), lambda d, g, j: (d, g, chunk(d, j), 0)),
        pl.BlockSpec((None, hb, L, 1), lambda d, g, j: (d, g, chunk(d, j), 0)),
        pl.BlockSpec((None, hb, 1, L), lambda d, g, j: (d, g, 0, chunk(d, j))),
        pl.BlockSpec((None, hb, 1, L), lambda d, g, j: (d, g, 0, chunk(d, j))),
        pl.BlockSpec((None, hb, 1, 1), lambda d, g, j: (d, g, 0, 0)),
        pl.BlockSpec((None, hb, 1, 1), lambda d, g, j: (d, g, 0, 0)),
    ], chunk


def mlstm_fwd(z, proj, gates, hm, dm, name, hb=None):
    S = z.shape[0]
    L = CHUNK
    nc = S // L
    hb = hm if hb is None else hb
    in_specs, chunk = _mlstm_in_specs(L, dm, hm, hb, nc, lambda j: j)

    def body(zq, zk, v, ic, fc, ir, fr, bi, bf_, h_ref, cs_ref, ns_ref, ms_ref, C_sc, n_sc, m_sc):
        d = pl.program_id(0)

        @pl.when(pl.program_id(2) == 0)
        def _():
            C_sc[...] = jnp.zeros_like(C_sc)
            n_sc[...] = jnp.zeros_like(n_sc)
            m_sc[...] = jnp.full_like(m_sc, M_INIT)

        for hh in range(hb):
            cols = slice(hh * dm, (hh + 1) * dm)
            C, n, m = C_sc[hh], n_sc[hh], m_sc[hh]
            cs_ref[hh], ns_ref[hh], ms_ref[hh] = C, n, m
            C2, n2, m2, h = _mlstm_step(dm, d, C, n, m, zq[:, cols], zk[:, cols], v[:, cols], ic[hh], fc[hh],
                                        ir[hh], fr[hh], bi[hh], bf_[hh])
            C_sc[hh], n_sc[hh], m_sc[hh] = C2, n2, m2
            h_ref[:, cols] = h

    return pl.pallas_call(
        body, name=name, grid=(2, hm // hb, nc), in_specs=in_specs,
        out_specs=[pl.BlockSpec((None, L, hb * dm), lambda d, g, j: (d, chunk(d, j), g)),
                   pl.BlockSpec((None, hb, None, dm, dm), lambda d, g, j: (d, g, j, 0, 0)),
                   pl.BlockSpec((None, hb, None, 1, dm), lambda d, g, j: (d, g, j, 0, 0)),
                   pl.BlockSpec((None, hb, None, 1, 1), lambda d, g, j: (d, g, j, 0, 0))],
        out_shape=[jax.ShapeDtypeStruct((2, S, hm * dm), f32), jax.ShapeDtypeStruct((2, hm, nc, dm, dm), f32),
                   jax.ShapeDtypeStruct((2, hm, nc, 1, dm), f32), jax.ShapeDtypeStruct((2, hm, nc, 1, 1), f32)],
        scratch_shapes=[pltpu.VMEM((hb, dm, dm), f32), pltpu.VMEM((hb, 1, dm), f32), pltpu.VMEM((hb, 1, 1), f32)],
        compiler_params=_cp(("arbitrary", "arbitrary", "arbitrary")))(z, z, proj, *gates)


def mlstm_bwd(z, proj, gates, states, dh, hm, dm, name, hb=None):
    S = z.shape[0]
    L = CHUNK
    nc = S // L
    hb = hm if hb is None else hb
    in_specs, chunk = _mlstm_in_specs(L, dm, hm, hb, nc, lambda j: nc - 1 - j)
    st = lambda j: nc - 1 - j
    in_specs = in_specs + [
        pl.BlockSpec((None, hb, None, dm, dm), lambda d, g, j: (d, g, st(j), 0, 0)),
        pl.BlockSpec((None, hb, None, 1, dm), lambda d, g, j: (d, g, st(j), 0, 0)),
        pl.BlockSpec((None, hb, None, 1, 1), lambda d, g, j: (d, g, st(j), 0, 0)),
        pl.BlockSpec((L, hb * dm), lambda d, g, j: (chunk(d, j), g)),
    ]

    def body(zq, zk, v, ic, fc, ir, fr, bi, bf_, cs, ns, ms, dh_ref,
             dzq, dzk, dv, dic, dfc, dir_, dfr, dbi, dbf, dC_sc, dn_sc, dm_sc):
        d = pl.program_id(0)

        @pl.when(pl.program_id(2) == 0)
        def _():
            dC_sc[...] = jnp.zeros_like(dC_sc)
            dn_sc[...] = jnp.zeros_like(dn_sc)
            dm_sc[...] = jnp.zeros_like(dm_sc)
            dbi[...] = jnp.zeros_like(dbi)
            dbf[...] = jnp.zeros_like(dbf)

        for hh in range(hb):
            cols = slice(hh * dm, (hh + 1) * dm)
            prim = (cs[hh], ns[hh], ms[hh], zq[:, cols], zk[:, cols], v[:, cols], ic[hh], fc[hh], ir[hh], fr[hh],
                    bi[hh], bf_[hh])
            _, pull = jax.vjp(functools.partial(_mlstm_step, dm, d), *prim)
            g = pull((dC_sc[hh], dn_sc[hh], dm_sc[hh], dh_ref[:, cols]))
            dC_sc[hh], dn_sc[hh], dm_sc[hh] = g[0], g[1], g[2]
            dzq[:, cols], dzk[:, cols], dv[:, cols] = g[3], g[4], g[5]
            dic[hh], dfc[hh], dir_[hh], dfr[hh] = g[6], g[7], g[8], g[9]
            dbi[hh] += g[10]
            dbf[hh] += g[11]

    tile = pl.BlockSpec((None, L, hb * dm), lambda d, g, j: (d, chunk(d, j), g))
    col = pl.BlockSpec((None, hb, L, 1), lambda d, g, j: (d, g, chunk(d, j), 0))
    row = pl.BlockSpec((None, hb, 1, L), lambda d, g, j: (d, g, 0, chunk(d, j)))
    one = pl.BlockSpec((None, hb, 1, 1), lambda d, g, j: (d, g, 0, 0))
    big = jax.ShapeDtypeStruct((2, S, hm * dm), f32)
    cols_ = jax.ShapeDtypeStruct((2, hm, S, 1), f32)
    rows_ = jax.ShapeDtypeStruct((2, hm, 1, S), f32)
    ones_ = jax.ShapeDtypeStruct((2, hm, 1, 1), f32)
    return pl.pallas_call(
        body, name=name, grid=(2, hm // hb, nc), in_specs=in_specs,
        out_specs=[tile, tile, tile, col, col, row, row, one, one],
        out_shape=[big, big, big, cols_, cols_, rows_, rows_, ones_, ones_],
        scratch_shapes=[pltpu.VMEM((hb, dm, dm), f32), pltpu.VMEM((hb, 1, dm), f32), pltpu.VMEM((hb, 1, 1), f32)],
        compiler_params=_cp(("arbitrary", "arbitrary", "arbitrary")))(z, z, proj, *gates, *states, dh)


def _blocks_to_cols(g):
    return g.transpose(1, 0, 2).reshape(g.shape[1], N_DEV * g.shape[2])


def _cols_to_blocks(a):
    return a.reshape(a.shape[0], N_DEV, a.shape[1] // N_DEV).transpose(1, 0, 2)


def _pad_cols(a, n):
    return jnp.pad(a, ((0, 0), (0, n - a.shape[1])))


def _relu2(u):
    r = jnp.maximum(u, 0.0)
    return r * r


def kernel(x, c, positions, w_ada, b_ada, norm_mix_g, w_in, b_gates, conv_w, conv_b, q_lora_g, w_uq, kv_lora_g, w_ukv, q_norm_g, k_norm_g, mlstm_norm_g, w_out, norm_mlp_g, w_ff1, w_ff2, loss_target, m_w_ada, m_b_ada, m_norm_mix_g, m_w_in, m_b_gates, m_conv_w, m_conv_b, m_q_lora_g, m_w_uq, m_kv_lora_g, m_w_ukv, m_q_norm_g, m_k_norm_g, m_mlstm_norm_g, m_w_out, m_norm_mlp_g, m_w_ff1, m_w_ff2, v_w_ada, v_b_ada, v_norm_mix_g, v_w_in, v_b_gates, v_conv_w, v_conv_b, v_q_lora_g, v_w_uq, v_kv_lora_g, v_w_ukv, v_q_norm_g, v_k_norm_g, v_mlstm_norm_g, v_w_out, v_norm_mlp_g, v_w_ff1, v_w_ff2):
    S, D = x.shape[1], x.shape[2]
    QL, KVL = w_uq.shape[1], w_ukv.shape[1]
    H = w_uq.shape[2] * N_DEV // QK_DIM
    HM = mlstm_norm_g.shape[1]
    DM = mlstm_norm_g.shape[2] * N_DEV
    MW = HM * DM
    D_IN = w_in.shape[2] * N_DEV
    NADA = w_ada.shape[2]
    assert D_IN == QL + KVL + ROPE + 4 * MW + N_GATES and DM % LANE == 0 and S % CHUNK == 0
    assert (4 * MW) % QL == 0 and (4 * MW + QL) % KVL == 0 and KVL % LANE == 0
    idx = 4 * lax.axis_index("x") + 2 * lax.axis_index("y") + lax.axis_index("c")
    x2, tgt = x[0], loss_target[0]

    (c_all,) = all_gather([c], "gather_cond")
    c_all = c_all.reshape(N_DEV, D)
    xi, yi, ci = lax.axis_index("x"), lax.axis_index("y"), lax.axis_index("c")
    slots = jnp.stack([4 * (1 - xi) + 2 * yi + ci, 4 * xi + 2 * (1 - yi) + ci, 4 * (1 - xi) + 2 * (1 - yi) + ci]).astype(jnp.int32)
    gqn = _pad_cols(q_norm_g, QK_PAD)
    gkn = _pad_cols(k_norm_g, QK_PAD)
    fr_np = np.zeros((1, LANE), np.float32)
    fr_np[0, :HALF] = fr_np[0, HALF:ROPE] = ROPE_THETA ** (-np.arange(HALF, dtype=np.float32) / HALF)
    freqs = jnp.asarray(fr_np)
    pos = positions.astype(f32).reshape(S, 1)

    b_blk = lax.dynamic_slice(b_ada, (0, idx * NADA), (1, NADA))
    mod_part = ada_fwd(c_all, w_ada[0], b_blk, "ada_fwd")
    (mod_all,) = all_gather([mod_part], "gather_mod")
    mod = lax.dynamic_index_in_dim(mod_all, idx, axis=1, keepdims=False).reshape(1, N_DEV * NADA)
    shift1, scale1, gate1, shift2, scale2, gate2 = [mod[:, k * D:(k + 1) * D] for k in range(6)]

    h, g_in, g_uq, g_ukv, g_conv, g_mn = rowwise(
        f_norm_mod, [Row(x2)], [norm_mix_g, shift1, scale1], [(D, bf16)], n_rows=S, tile=256, name="norm_mix",
        side=[w_in[0].astype(bf16), w_uq[0].astype(bf16), w_ukv[0].astype(bf16), conv_w[0], mlstm_norm_g[0]])
    wi = _blocks_to_cols(g_in)
    o_cq, o_ckv, o_kpe, o_m, o_g = 0, QL, QL + KVL, QL + KVL + ROPE, QL + KVL + ROPE + 4 * MW
    w_in_p = jnp.concatenate([wi[:, o_m:o_g], wi[:, o_cq:o_kpe], _pad_cols(wi[:, o_kpe:o_m], LANE),
                              _pad_cols(wi[:, o_g:], LANE)], axis=1)
    NP = w_in_p.shape[1]
    cb_cq, cb_ckv, cb_kpe, cb_g = 4 * MW // QL, (4 * MW + QL) // KVL, (4 * MW + QL + KVL) // LANE, NP // LANE - 1
    w_uq_p = jnp.pad(_blocks_to_cols(g_uq).reshape(QL, H, QK_DIM), ((0, 0), (0, 0), (0, QK_PAD - QK_DIM))).reshape(QL, H * QK_PAD)
    w_ukv_p = _blocks_to_cols(g_ukv).reshape(KVL, H, 2, NOPE).transpose(0, 2, 1, 3).reshape(KVL, 2 * H * NOPE)
    conv_w_f = jnp.pad(_blocks_to_cols(g_conv), ((0, 8 - CONV_W), (0, 0)))
    mn_g = _blocks_to_cols(g_mn).reshape(1, MW)
    proj = mm(h, w_in_p, name="proj_in", out_dtype=f32)
    r_cq, r_ckv, r_kpe = Row(proj, QL, cb_cq), Row(proj, KVL, cb_ckv), Row(proj, LANE, cb_kpe)
    f_prep = make_f_mla_prep(H, QK_DIM ** -0.5 * math.log2(math.e))
    prep_params = [q_lora_g, kv_lora_g, gqn, gkn, w_uq_p, w_ukv_p, freqs]
    Q, K, V = rowwise(f_prep, [r_cq, r_ckv, r_kpe, Row(pos, diff=False)], prep_params,
                      [(H * QK_PAD, bf16), (H * QK_PAD, bf16), (H * V_DIM, bf16)], n_rows=S, tile=256, name="mla_prep")
    attn, lse, (g_out, g_ff1, g_ff2) = flash_fwd(
        Q, K, V, H, "flash_fwd", side=[w_out[0].astype(bf16), w_ff1[0].astype(bf16), w_ff2[0].astype(bf16)])
    w_out_f = g_out.reshape(N_DEV * g_out.shape[1], D)
    w_ff2_f = g_ff2.reshape(N_DEV * g_ff2.shape[1], D)

    conv_bias = conv_b
    z = conv_fwd(proj, 2 * MW, conv_w_f, conv_bias, "conv_fwd")
    graw = proj[:, cb_g * LANE:cb_g * LANE + N_GATES].reshape(S, 4, HM)
    gcol = graw.transpose(1, 2, 0).reshape(2, 2, HM, S)
    bg = b_gates.reshape(2, 2, HM)
    gates = (gcol[:, 0].reshape(2, HM, S, 1), gcol[:, 1].reshape(2, HM, S, 1),
             gcol[:, 0].reshape(2, HM, 1, S), gcol[:, 1].reshape(2, HM, 1, S),
             bg[:, 0].reshape(2, HM, 1, 1), bg[:, 1].reshape(2, HM, 1, 1))
    hdir, cs, ns, ms = mlstm_fwd(z, proj, gates, HM, DM, "mlstm_fwd")
    f_post = make_f_mlstm_post(HM, DM)
    post_rows = [Row(hdir, MW, 0, lead=0), Row(hdir, MW, 0, lead=1), Row(proj, MW, 3)]
    (ml_out,) = rowwise(f_post, post_rows, [mn_g], [(MW, bf16)], n_rows=S, tile=256, name="mlstm_post")

    cat = jnp.concatenate([attn, ml_out], axis=1)
    mixed = mm(cat, w_out_f, name="proj_out", out_dtype=f32)
    mlp_params = [gate1, norm_mlp_g, shift2, scale2]
    x1, h2 = rowwise(f_resid_norm_mod, [Row(x2), Row(mixed)], mlp_params, [(D, f32), (D, bf16)],
                     n_rows=S, tile=256, name="resid_norm_mlp")
    u = mm(h2, g_ff1, name="ff1", out_dtype=bf16)
    y = mm(u, w_ff2_f, name="ff2", a_fn=_relu2, out_dtype=f32)
    loss_l, d_out, d_y, d_gate2 = loss_head(x1, y, tgt, gate2, "loss_head")
    loss = lax.psum(loss_l[0, 0], AXES)

    dw_ff2 = mm(u, d_y, name="dw_ff2", ta=True, a_fn=_relu2, out_dtype=bf16)
    d_u = mm(d_y, w_ff2_f, name="d_u", tb=True, epi=lambda acc, uu: acc * (2.0 * jnp.maximum(uu.astype(f32), 0.0)),
             extras=(u,), out_dtype=bf16)
    dw_ff1 = mm(h2, d_u, name="dw_ff1", ta=True, out_dtype=bf16, out_blocks=True)
    w_ff1_t = g_ff1.transpose(0, 2, 1).reshape(-1, D)
    d_h2 = mm(d_u, w_ff1_t, name="d_h2", out_dtype=f32)
    (d_x1, d_mixed), (d_gate1, d_g_mlp, d_shift2, d_scale2) = rowwise_vjp(
        f_resid_norm_mod, [Row(x2), Row(mixed)], mlp_params, [Row(d_out), Row(d_h2)],
        n_rows=S, tile=256, name="resid_norm_mlp_bwd", row_grad_dtypes=[f32, bf16])
    dw_out = mm(cat, d_mixed, name="dw_out", ta=True, out_dtype=bf16)
    mlp_g = [dw_out.reshape(N_DEV, -1, D), dw_ff1, dw_ff2.reshape(N_DEV, -1, D)]
    d_cat, mlp_sib = mm(d_mixed, w_out_f, name="d_cat", tb=True, out_dtype=f32, tm=512, tn=2048,
                        side=mlp_g, side_pair=True)

    post_rows_b = [post_rows[0], Row(hdir, MW, 0, lead=1, diff=False), post_rows[2]]
    (dh, d_om), (d_mn_g,) = rowwise_vjp(
        f_post, post_rows_b, [mn_g], [Row(d_cat, MW, H * V_DIM // MW)], n_rows=S, tile=256, name="mlstm_post_bwd")
    dzq, dzk, dvm, dic, dfc, dir_, dfr, dbi, dbf = mlstm_bwd(z, proj, gates, (cs, ns, ms), dh, HM, DM, "mlstm_bwd")
    d_qk, d_conv_w, d_conv_b, d_vm = conv_bwd(dzq, dzk, dvm, proj, 2 * MW, conv_w_f, "conv_bwd")
    dg = jnp.stack([dic.reshape(2, HM, S) + dir_.reshape(2, HM, S), dfc.reshape(2, HM, S) + dfr.reshape(2, HM, S)], axis=1)
    d_gates = dg.reshape(4 * HM, S).T
    d_b_gates = jnp.stack([dbi.reshape(2, HM), dbf.reshape(2, HM)], axis=1).reshape(1, N_GATES)

    mlp_tags = ["w_out", "w_ff1", "w_ff2"]
    mlp_part = [chip_partials(g, r, slots, "grad_chip_partials_" + t) for g, r, t in zip(mlp_g, mlp_sib, mlp_tags)]
    dq, dk, dv, mlp_chips = flash_bwd(Q, K, V, attn, lse.reshape(H, 1, S), d_cat, 0, H, "flash_bwd", side=mlp_part)
    (d_cq, d_ckv, d_kpe), (d_gq, d_gkv, d_gqn, d_gkn, dw_uq_p, dw_ukv_p) = rowwise_vjp(
        f_prep, [r_cq, r_ckv, r_kpe, Row(pos, diff=False)], prep_params, [Row(dq), Row(dk), Row(dv)],
        n_rows=S, tile=256, name="mla_prep_bwd", row_grad_dtypes=[bf16, bf16, bf16],
        param_diff=[True, True, True, True, True, True, False])

    d_proj = jnp.concatenate([d_qk.astype(bf16), d_vm, d_om.astype(bf16), d_cq, d_ckv, d_kpe,
                              _pad_cols(d_gates.astype(bf16), LANE)], axis=1)
    dw_in_p = mm(h, d_proj, name="dw_in", ta=True, out_dtype=bf16)

    dwi = jnp.concatenate([dw_in_p[:, 4 * MW:4 * MW + QL + KVL + ROPE], dw_in_p[:, :4 * MW],
                           dw_in_p[:, cb_g * LANE:cb_g * LANE + N_GATES]], axis=1)
    dw_uq = dw_uq_p.reshape(QL, H, QK_PAD)[:, :, :QK_DIM].reshape(QL, H * QK_DIM)
    dw_ukv = dw_ukv_p.reshape(KVL, 2, H, NOPE).transpose(0, 2, 1, 3).reshape(KVL, 2 * H * NOPE)
    tiny = [(w_uq, m_w_uq, v_w_uq, _cols_to_blocks(dw_uq)),
            (w_ukv, m_w_ukv, v_w_ukv, _cols_to_blocks(dw_ukv)),
            (conv_w, m_conv_w, v_conv_w, _cols_to_blocks(d_conv_w[:CONV_W])),
            (mlstm_norm_g, m_mlstm_norm_g, v_mlstm_norm_g, _cols_to_blocks(d_mn_g.reshape(HM, DM)))]
    tsizes = [int(np.prod(b[0].shape)) for b in tiny]
    T = sum(tsizes)
    PC = 512
    PR = -(-T // (PC * 64)) * 64
    gpack = jnp.concatenate([b[3].astype(bf16).reshape(N_DEV, -1) for b in tiny], axis=1)
    gpack = jnp.pad(gpack, ((0, 0), (0, PR * PC - T))).reshape(N_DEV, PR, PC)
    wpack = lambda k: jnp.pad(jnp.concatenate([b[k].reshape(1, -1) for b in tiny], axis=1),
                              ((0, 0), (0, PR * PC - T))).reshape(PR, PC)
    late_g = [_cols_to_blocks(dwi), gpack]
    late_sib = pair_exchange(late_g, "grad_pair_exchange")
    late_part = [chip_partials(g, r, slots, "grad_chip_partials_" + t) for g, r, t in zip(late_g, late_sib, ["w_in", "tiny"])]
    d_h, late_chips = mm(d_proj, w_in_p, name="d_h", tb=True, out_dtype=f32, tm=512, tn=2048, tk=2560, side=late_part)
    (grad_x,), (d_g_mix, d_shift1, d_scale1) = rowwise_vjp(
        f_norm_mod_thru, [Row(x2)], [norm_mix_g, shift1, scale1], [Row(d_h), Row(d_x1)],
        n_rows=S, tile=256, name="norm_mix_bwd")

    dmod = jnp.concatenate([d_shift1, d_scale1, d_gate1, d_shift2, d_scale2, d_gate2], axis=1)
    small = [(norm_mix_g, m_norm_mix_g, v_norm_mix_g, d_g_mix), (b_gates, m_b_gates, v_b_gates, d_b_gates),
             (conv_b, m_conv_b, v_conv_b, d_conv_b), (q_lora_g, m_q_lora_g, v_q_lora_g, d_gq),
             (kv_lora_g, m_kv_lora_g, v_kv_lora_g, d_gkv), (q_norm_g, m_q_norm_g, v_q_norm_g, d_gqn[:, :QK_DIM]),
             (k_norm_g, m_k_norm_g, v_k_norm_g, d_gkn[:, :QK_DIM]), (norm_mlp_g, m_norm_mlp_g, v_norm_mlp_g, d_g_mlp),
             (b_ada, m_b_ada, v_b_ada, dmod)]
    sizes = [s[0].shape[1] for s in small]
    P = sum(sizes)
    PP = -(-P // LANE) * LANE
    pack = lambda k: _pad_cols(jnp.concatenate([s[k] for s in small], axis=1), PP)
    (sg_all,) = all_gather([pack(3)], "gather_small_grads")
    s_out = adamw([sg_all[k] for k in range(N_DEV)], pack(0), pack(1), pack(2), "adamw_small")
    offs = np.concatenate([[0], np.cumsum(sizes)])
    small_out = [[o[:, offs[k]:offs[k + 1]] for o in s_out] for k in range(len(small))]

    dmod_all = sg_all[:, 0, offs[-2]:offs[-1]]
    dmod_blk = lax.dynamic_slice(dmod_all, (0, idx * NADA), (N_DEV, NADA))
    g_w_ada = ada_wgrad(c_all, dmod_blk, "ada_wgrad")
    ada_out = adamw([g_w_ada], w_ada[0], m_w_ada[0], v_w_ada[0], "adamw_ada")

    large = [(w_in[0], m_w_in[0], v_w_in[0], late_g[0]),
             (w_out[0], m_w_out[0], v_w_out[0], mlp_g[0]),
             (w_ff1[0], m_w_ff1[0], v_w_ff1[0], mlp_g[1]),
             (w_ff2[0], m_w_ff2[0], v_w_ff2[0], mlp_g[2]),
             (wpack(0), wpack(1), wpack(2), gpack)]
    tags = ["w_in", "w_out", "w_ff1", "w_ff2", "tiny"]
    from_sibling = [late_sib[0]] + list(mlp_sib) + [late_sib[1]]
    from_chips = [late_chips[0]] + list(mlp_chips) + [late_chips[1]]
    l_out = []
    for (w_, m_, v_, g), r, fc, t in zip(large, from_sibling, from_chips, tags):
        mine = lax.dynamic_index_in_dim(g, idx, axis=0, keepdims=False)
        sib = lax.dynamic_index_in_dim(r, 2 * xi + yi, axis=0, keepdims=False)
        l_out.append(adamw([mine, sib, fc[0], fc[1], fc[2]], w_, m_, v_, "adamw_" + t))
    toffs = np.concatenate([[0], np.cumsum(tsizes)])
    tiny_out = [[o.reshape(-1)[toffs[k]:toffs[k + 1]].reshape(tiny[k][0].shape) for o in l_out[4]] for k in range(len(tiny))]
    big_out = [[o[None] for o in l_out[0]], tiny_out[0], tiny_out[1], [o[None] for o in l_out[1]],
               [o[None] for o in l_out[2]], [o[None] for o in l_out[3]], tiny_out[2], tiny_out[3]]

    names = ["w_ada", "b_ada", "norm_mix_g", "w_in", "b_gates", "conv_w", "conv_b", "q_lora_g", "w_uq", "kv_lora_g",
             "w_ukv", "q_norm_g", "k_norm_g", "mlstm_norm_g", "w_out", "norm_mlp_g", "w_ff1", "w_ff2"]
    res = {"w_ada": [o[None] for o in ada_out]}
    for k, nm in enumerate(["norm_mix_g", "b_gates", "conv_b", "q_lora_g", "kv_lora_g", "q_norm_g", "k_norm_g",
                            "norm_mlp_g", "b_ada"]):
        res[nm] = small_out[k]
    for k, nm in enumerate(["w_in", "w_uq", "w_ukv", "w_out", "w_ff1", "w_ff2", "conv_w", "mlstm_norm_g"]):
        res[nm] = big_out[k]
    outs = [loss, grad_x[None]]
    for part in range(4):
        outs += [res[nm][part] for nm in names]
    return tuple(outs)
```

```python
import functools
import math

import numpy as np
import jax
import jax.numpy as jnp
from jax import lax
from jax.experimental import pallas as pl
from jax.experimental.pallas import tpu as pltpu

f32 = jnp.float32
bf16 = jnp.bfloat16

N_DEV = 8
AXES = ("x", "y", "c")
MESH = pl.DeviceIdType.MESH

NOPE = 128
ROPE = 64
HALF = ROPE // 2
QK_DIM = NOPE + ROPE
QK_PAD = 256
V_DIM = 128
ROPE_THETA = 10000.0
CHUNK = 128
CONV_W = 5
N_GATES = 16
EPS = 1e-6
M_INIT = -1e30

ADAM_LR, ADAM_B1, ADAM_B2, ADAM_EPS, ADAM_WD, ADAM_STEP = 0.001, 0.9, 0.999, 1e-08, 0.01, 10

LANE = 128
VMEM_LIMIT = 56 * 1024 * 1024


def _cp(sem=None, vmem=VMEM_LIMIT):
    return pltpu.CompilerParams(dimension_semantics=sem, vmem_limit_bytes=vmem)


def _pick(n, target):
    best = None
    t = LANE
    while t <= min(n, target):
        if n % t == 0:
            best = t
        t += LANE
    return best if best is not None else n


def _pick_rows(n, target):
    t = min(n, target)
    while n % t:
        t -= 8
    return t


def _make_dots(cast, precision):
    def dg(a, b, ca, cb):
        if cast is not None:
            a = a.astype(cast)
            b = b.astype(cast)
        return lax.dot_general(a, b, (((ca,), (cb,)), ((), ())), precision=precision, preferred_element_type=f32)

    @jax.custom_vjp
    def nn(a, b):
        return dg(a, b, 1, 0)

    def nn_f(a, b):
        return dg(a, b, 1, 0), (a, b)

    def nn_b(res, g):
        a, b = res
        return dg(g, b, 1, 1).astype(a.dtype), dg(a, g, 0, 0).astype(b.dtype)

    nn.defvjp(nn_f, nn_b)

    @jax.custom_vjp
    def nt(a, b):
        return dg(a, b, 1, 1)

    def nt_f(a, b):
        return dg(a, b, 1, 1), (a, b)

    def nt_b(res, g):
        a, b = res
        return dg(g, b, 1, 0).astype(a.dtype), dg(g, a, 0, 0).astype(b.dtype)

    nt.defvjp(nt_f, nt_b)

    @jax.custom_vjp
    def tn(a, b):
        return dg(a, b, 0, 0)

    def tn_f(a, b):
        return dg(a, b, 0, 0), (a, b)

    def tn_b(res, g):
        a, b = res
        return dg(b, g, 1, 1).astype(a.dtype), dg(a, g, 1, 0).astype(b.dtype)

    tn.defvjp(tn_f, tn_b)
    return nn, nt, tn


bdot, bdot_nt, bdot_tn = _make_dots(bf16, None)
hdot, hdot_nt, hdot_tn = _make_dots(None, lax.Precision.HIGHEST)


def _silu(x):
    return x * jax.nn.sigmoid(x)


def _rms(x, n):
    return x * lax.rsqrt(jnp.sum(x * x, axis=-1, keepdims=True) * (1.0 / n) + EPS)


def _place():
    return lax.axis_index("x"), lax.axis_index("y"), lax.axis_index("c")


def _gather_phases(ins, outs, send_sems, recv_sems, local_sems):
    n = len(ins)
    x, y, c = _place()
    me, sibling = (x, y, c), (x, y, 1 - c)
    chips = [(1 - x, y), (x, 1 - y), (1 - x, 1 - y)]

    def slot(o, p):
        return outs[o].at[4 * p[0] + 2 * p[1] + p[2]]

    def copy(o, k, block, to, src=None):
        dst = slot(o, block)
        return pltpu.make_async_remote_copy(
            src_ref=dst if src is None else src, dst_ref=dst,
            send_sem=send_sems.at[o, k], recv_sem=recv_sems.at[o, k],
            device_id=to, device_id_type=MESH)

    def local(o):
        return pltpu.make_async_copy(ins[o], slot(o, me), local_sems.at[o])

    def first(o):
        return [copy(o, 0, me, sibling, src=ins[o])] + [copy(o, 1 + j, me, (*chip, c), src=ins[o])
                                                        for j, chip in enumerate(chips)]

    def start():
        for o in range(n):
            local(o).start()
        for o in range(n):
            for cp in first(o):
                cp.start()

    def mid():
        for o in range(n):
            for j, chip in enumerate(chips):
                copy(o, 1 + j, (*chip, c), me).wait_recv()
                copy(o, 4 + j, (*chip, c), sibling).start()

    def finish():
        for o in range(n):
            copy(o, 0, sibling, me).wait_recv()
            for j, chip in enumerate(chips):
                copy(o, 4 + j, (*chip, 1 - c), me).wait_recv()
        for o in range(n):
            for cp in first(o):
                cp.wait_send()
            for j, chip in enumerate(chips):
                copy(o, 4 + j, (*chip, c), sibling).wait_send()
        for o in range(n):
            local(o).wait()

    return start, mid, finish


def _gather_scratch(n):
    return [pltpu.SemaphoreType.DMA((n, 7)), pltpu.SemaphoreType.DMA((n, 7)), pltpu.SemaphoreType.DMA((n,))]


def all_gather(ops, name):
    n = len(ops)

    def body(*refs):
        start, mid, finish = _gather_phases(refs[:n], refs[n:2 * n], *refs[2 * n:])
        start()
        mid()
        finish()

    anyspec = pl.BlockSpec(memory_space=pl.ANY)
    return pl.pallas_call(
        body, name=name,
        out_shape=[jax.ShapeDtypeStruct((N_DEV,) + o.shape, o.dtype) for o in ops],
        in_specs=[anyspec] * n, out_specs=[anyspec] * n,
        scratch_shapes=_gather_scratch(n),
    )(*ops)


def pair_exchange(gs, name):
    n = len(gs)

    def body(*refs):
        start, finish = _pair_exchange_phases(refs[:n], refs[n:2 * n], *refs[2 * n:])
        start()
        finish()

    anyspec = pl.BlockSpec(memory_space=pl.ANY)
    return pl.pallas_call(
        body, name=name, out_shape=[jax.ShapeDtypeStruct(_pair_exchange_shape(g), g.dtype) for g in gs],
        in_specs=[anyspec] * n, out_specs=[anyspec] * n,
        scratch_shapes=_pair_exchange_scratch(n),
    )(*gs)


def _pair_exchange_phases(g_refs, out_refs, send_sems, recv_sems):
    n = len(g_refs)
    x, y, c = _place()

    def copies():
        return [pltpu.make_async_remote_copy(
            src_ref=g_refs[o].at[2 * q + (1 - c)], dst_ref=out_refs[o].at[q],
            send_sem=send_sems.at[o, q], recv_sem=recv_sems.at[o, q],
            device_id=(x, y, 1 - c), device_id_type=MESH) for o in range(n) for q in range(4)]

    def start():
        for cp in copies():
            cp.start()

    def finish():
        for cp in copies():
            cp.wait_recv()
        for cp in copies():
            cp.wait_send()

    return start, finish


def _pair_exchange_scratch(n):
    return [pltpu.SemaphoreType.DMA((n, 4)), pltpu.SemaphoreType.DMA((n, 4))]


def _pair_exchange_shape(g):
    return (4,) + g.shape[1:]


def _chip_exchange_phases(p_refs, out_refs, send_sems, recv_sems):
    n = len(p_refs)
    x, y, c = _place()
    chips = [(1 - x, y), (x, 1 - y), (1 - x, 1 - y)]

    def copies():
        return [pltpu.make_async_remote_copy(
            src_ref=p_refs[o].at[j], dst_ref=out_refs[o].at[j],
            send_sem=send_sems.at[o, j], recv_sem=recv_sems.at[o, j],
            device_id=(*chip, c), device_id_type=MESH) for o in range(n) for j, chip in enumerate(chips)]

    def start():
        for cp in copies():
            cp.start()

    def finish():
        for cp in copies():
            cp.wait_recv()
        for cp in copies():
            cp.wait_send()

    return start, finish


def _chip_exchange_scratch(n):
    return [pltpu.SemaphoreType.DMA((n, 3)), pltpu.SemaphoreType.DMA((n, 3))]


def chip_partials(g, recv, slots, name):
    _, R, C = g.shape
    tr = _pick_rows(R, 512)

    def body(s_ref, a_ref, b_ref, o_ref):
        o_ref[...] = (a_ref[...].astype(f32) + b_ref[...].astype(f32)).astype(o_ref.dtype)

    grid_spec = pltpu.PrefetchScalarGridSpec(
        num_scalar_prefetch=1, grid=(3, R // tr),
        in_specs=[pl.BlockSpec((None, tr, C), lambda j, i, s: (s[j], i, 0)),
                  pl.BlockSpec((None, tr, C), lambda j, i, s: (s[j] // 2, i, 0))],
        out_specs=pl.BlockSpec((None, tr, C), lambda j, i, s: (j, i, 0)))
    return pl.pallas_call(body, name=name, grid_spec=grid_spec,
                          out_shape=jax.ShapeDtypeStruct((3, R, C), g.dtype),
                          compiler_params=_cp(("arbitrary", "arbitrary")))(slots, g, recv)


def adamw(parts, w, m, v, name, rows=256):
    R, C = w.shape
    tr = _pick_rows(R, rows)
    npart = len(parts)
    c1 = 1.0 - ADAM_B1 ** ADAM_STEP
    c2 = 1.0 - ADAM_B2 ** ADAM_STEP

    def body(*refs):
        p_refs = refs[:npart]
        w_ref, m_ref, v_ref, g_out, d_out, m_out, v_out = refs[npart:]
        g = p_refs[0][...].astype(f32)
        for p in p_refs[1:]:
            g = g + p[...].astype(f32)
        mn = ADAM_B1 * m_ref[...] + (1.0 - ADAM_B1) * g
        vn = ADAM_B2 * v_ref[...] + (1.0 - ADAM_B2) * (g * g)
        m_hat = mn / c1
        v_hat = vn / c2
        g_out[...] = g
        d_out[...] = -ADAM_LR * (m_hat / (jnp.sqrt(v_hat) + ADAM_EPS) + ADAM_WD * w_ref[...])
        m_out[...] = mn
        v_out[...] = vn

    spec = pl.BlockSpec((tr, C), lambda i: (i, 0))
    return pl.pallas_call(
        body, name=name, grid=(R // tr,),
        in_specs=[spec] * (npart + 3), out_specs=[spec] * 4,
        out_shape=[jax.ShapeDtypeStruct((R, C), f32)] * 4,
        compiler_params=_cp(("arbitrary",)))(*parts, w, m, v)


def mm(a, b, *, name, ta=False, tb=False, a_fn=None, epi=None, extras=(), out_dtype=f32, out_blocks=False, side=(),
       side_pair=False, tm=1024, tn=1024, tk=2048):
    K, M = a.shape if ta else a.shape[::-1]
    b3 = b.ndim == 3
    if b3:
        assert not tb
        N, K2 = N_DEV * b.shape[2], b.shape[1]
    else:
        N, K2 = b.shape if tb else b.shape[::-1]
    assert K == K2, (a.shape, b.shape, ta, tb)
    n_split = N // N_DEV if (out_blocks or b3) else N
    tm, tn, tk = _pick(M, tm), _pick(n_split, tn), _pick(K, tk)
    nb = n_split // tn
    nk = K // tk
    ne = len(extras)
    assert not (out_blocks and ne)
    dims = (((0 if ta else 1,), (1 if tb else 0,)), ((), ()))

    ns = len(side)
    n_steps = (M // tm) * (N // tn) * nk
    assert ns == 0 or n_steps >= 2
    if side_pair:
        side_phases, side_scratch, side_shape = _pair_exchange_phases, _pair_exchange_scratch, _pair_exchange_shape
    else:
        side_phases, side_scratch, side_shape = _chip_exchange_phases, _chip_exchange_scratch, lambda p: p.shape

    def body(a_ref, b_ref, *rest):
        e_refs, o_ref, acc = rest[:ne], rest[ne + ns], rest[ne + 2 * ns + 1]
        k = pl.program_id(2)
        if ns:
            step = (pl.program_id(0) * (N // tn) + pl.program_id(1)) * nk + k
            x_start, x_finish = side_phases(rest[ne:ne + ns], rest[ne + ns + 1:ne + 2 * ns + 1], *rest[ne + 2 * ns + 2:])
            pl.when(step == 0)(x_start)

        @pl.when(k == 0)
        def _():
            acc[...] = jnp.zeros_like(acc)

        av = a_ref[...]
        if a_fn is not None:
            av = a_fn(av.astype(f32))
        acc[...] += lax.dot_general(av.astype(bf16), b_ref[...].astype(bf16), dims, preferred_element_type=f32)

        @pl.when(k == nk - 1)
        def _():
            r = acc[...]
            if epi is not None:
                r = epi(r, *[e[...] for e in e_refs])
            o_ref[...] = r.astype(o_ref.dtype)

        if ns:
            pl.when(step == n_steps - 1)(x_finish)

    a_spec = pl.BlockSpec((tk, tm), lambda i, j, k: (k, i)) if ta else pl.BlockSpec((tm, tk), lambda i, j, k: (i, k))
    if b3:
        b_spec = pl.BlockSpec((None, tk, tn), lambda i, j, k: (j // nb, k, j % nb))
    else:
        b_spec = pl.BlockSpec((tn, tk), lambda i, j, k: (j, k)) if tb else pl.BlockSpec((tk, tn), lambda i, j, k: (k, j))
    if out_blocks:
        o_spec = pl.BlockSpec((None, tm, tn), lambda i, j, k: (j // nb, i, j % nb))
        o_shape = jax.ShapeDtypeStruct((N_DEV, M, N // N_DEV), out_dtype)
    else:
        o_spec = pl.BlockSpec((tm, tn), lambda i, j, k: (i, j))
        o_shape = jax.ShapeDtypeStruct((M, N), out_dtype)
    if not ns:
        return pl.pallas_call(
            body, name=name, grid=(M // tm, N // tn, nk),
            in_specs=[a_spec, b_spec] + [o_spec] * ne, out_specs=o_spec,
            out_shape=o_shape,
            scratch_shapes=[pltpu.VMEM((tm, tn), f32)],
            compiler_params=_cp(("parallel", "parallel", "arbitrary")))(a, b, *extras)
    anyspec = pl.BlockSpec(memory_space=pl.ANY)
    res = pl.pallas_call(
        body, name=name, grid=(M // tm, N // tn, nk),
        in_specs=[a_spec, b_spec] + [o_spec] * ne + [anyspec] * ns, out_specs=[o_spec] + [anyspec] * ns,
        out_shape=[o_shape] + [jax.ShapeDtypeStruct(side_shape(p), p.dtype) for p in side],
        scratch_shapes=[pltpu.VMEM((tm, tn), f32)] + side_scratch(ns),
        compiler_params=_cp(("arbitrary", "arbitrary", "arbitrary")))(a, b, *extras, *side)
    return res[0], list(res[1:])


class Row:
    def __init__(self, arr, width=None, col=0, lead=None, diff=True):
        self.arr, self.col, self.lead, self.diff = arr, col, lead, diff
        self.width = arr.shape[-1] if width is None else width

    def spec(self, t):
        col, lead = self.col, self.lead
        if lead is None:
            return pl.BlockSpec((t, self.width), lambda i: (i, col))
        return pl.BlockSpec((None, t, self.width), lambda i: (lead, i, col))


def _whole(p):
    return pl.BlockSpec(p.shape, lambda i: (0,) * p.ndim)


def rowwise(fn, rows, params, outs, *, n_rows, tile, name, side=()):
    t = _pick_rows(n_rows, tile)
    nr, npar, no, ns = len(rows), len(params), len(outs), len(side)
    n_steps = n_rows // t
    assert ns == 0 or n_steps >= 3

    def body(*refs):
        r_refs, p_refs = refs[:nr], refs[nr:nr + npar]
        o_refs = refs[nr + npar + ns:nr + npar + ns + no]
        if ns:
            s_in = refs[nr + npar:nr + npar + ns]
            s_out = refs[nr + npar + ns + no:nr + npar + 2 * ns + no]
            g_start, g_mid, g_finish = _gather_phases(s_in, s_out, *refs[nr + npar + 2 * ns + no:])
            pl.when(pl.program_id(0) == 0)(g_start)
            pl.when(pl.program_id(0) == n_steps // 2)(g_mid)
        res = fn(*[r[...].astype(f32) for r in r_refs], *[p[...] for p in p_refs])
        for o_ref, val in zip(o_refs, res):
            o_ref[...] = val.astype(o_ref.dtype)
        if ns:
            pl.when(pl.program_id(0) == n_steps - 1)(g_finish)

    anyspec = pl.BlockSpec(memory_space=pl.ANY)
    res = pl.pallas_call(
        body, name=name, grid=(n_steps,),
        in_specs=[r.spec(t) for r in rows] + [_whole(p) for p in params] + [anyspec] * ns,
        out_specs=[pl.BlockSpec((t, w), lambda i: (i, 0)) for w, _ in outs] + [anyspec] * ns,
        out_shape=[jax.ShapeDtypeStruct((n_rows, w), dt) for w, dt in outs]
        + [jax.ShapeDtypeStruct((N_DEV,) + a.shape, a.dtype) for a in side],
        scratch_shapes=_gather_scratch(ns) if ns else [],
        compiler_params=_cp(("arbitrary",)))(*[r.arr for r in rows], *params, *side)
    return res


def rowwise_vjp(fn, rows, params, cts, *, n_rows, tile, name, row_grad_dtypes=None, param_diff=None):
    t = _pick_rows(n_rows, tile)
    nr, npar, nc = len(rows), len(params), len(cts)
    param_diff = [True] * npar if param_diff is None else param_diff
    d_rows = [k for k, r in enumerate(rows) if r.diff]
    d_pars = [k for k in range(npar) if param_diff[k]]
    row_grad_dtypes = [f32] * len(d_rows) if row_grad_dtypes is None else row_grad_dtypes

    def body(*refs):
        r_refs, p_refs = refs[:nr], refs[nr:nr + npar]
        c_refs = refs[nr + npar:nr + npar + nc]
        dr_refs = refs[nr + npar + nc:nr + npar + nc + len(d_rows)]
        dp_refs = refs[nr + npar + nc + len(d_rows):]
        rv = [r[...].astype(f32) for r in r_refs]
        pv = [p[...] for p in p_refs]

        def g(*dvals):
            full_r, full_p = list(rv), list(pv)
            for k, val in zip(d_rows, dvals[:len(d_rows)]):
                full_r[k] = val
            for k, val in zip(d_pars, dvals[len(d_rows):]):
                full_p[k] = val
            return tuple(fn(*full_r, *full_p))

        prim = [rv[k] for k in d_rows] + [pv[k].astype(f32) for k in d_pars]
        _, pull = jax.vjp(g, *prim)
        grads = pull(tuple(c[...].astype(f32) for c in c_refs))
        for ref, val in zip(dr_refs, grads[:len(d_rows)]):
            ref[...] = val.astype(ref.dtype)

        @pl.when(pl.program_id(0) == 0)
        def _():
            for ref in dp_refs:
                ref[...] = jnp.zeros_like(ref)

        for ref, val in zip(dp_refs, grads[len(d_rows):]):
            ref[...] += val

    out_specs = [pl.BlockSpec((t, rows[k].width), lambda i: (i, 0)) for k in d_rows]
    out_specs += [_whole(params[k]) for k in d_pars]
    out_shape = [jax.ShapeDtypeStruct((n_rows, rows[k].width), dt) for k, dt in zip(d_rows, row_grad_dtypes)]
    out_shape += [jax.ShapeDtypeStruct(params[k].shape, f32) for k in d_pars]
    res = pl.pallas_call(
        body, name=name, grid=(n_rows // t,),
        in_specs=[r.spec(t) for r in rows] + [_whole(p) for p in params] + [c.spec(t) for c in cts],
        out_specs=out_specs, out_shape=out_shape,
        compiler_params=_cp(("arbitrary",)))(*[r.arr for r in rows], *params, *[c.arr for c in cts])
    return res[:len(d_rows)], res[len(d_rows):]


def f_norm_mod(x, g, shift, scale):
    return (_rms(x, x.shape[-1]) * g * (1.0 + scale) + shift,)


def f_norm_mod_thru(x, g, shift, scale):
    return f_norm_mod(x, g, shift, scale) + (x,)


def f_resid_norm_mod(x, mixed, gate1, g2, shift2, scale2):
    x1 = x + gate1 * mixed
    return (x1,) + f_norm_mod(x1, g2, shift2, scale2)


def _rope_rot():
    i = lax.broadcasted_iota(jnp.int32, (LANE, LANE), 0)
    j = lax.broadcasted_iota(jnp.int32, (LANE, LANE), 1)
    neg = jnp.where((i == j + HALF) & (j < HALF), -1.0, 0.0)
    pos = jnp.where((i == j - HALF) & (j >= HALF) & (j < ROPE), 1.0, 0.0)
    return (neg + pos).astype(f32)


def make_f_mla_prep(n_heads, q_scale):
    def fn(cq, ckv, kpe, pos, gq, gkv, gqn, gkn, w_uq, w_ukv, freqs):
        rot = _rope_rot()
        ang = pos * freqs
        cos, sin = jnp.cos(ang), jnp.sin(ang)

        def rope(u):
            return u * cos + hdot(u, rot) * sin

        qraw = bdot(_rms(cq, cq.shape[-1]) * gq, w_uq)
        kv = bdot(_rms(ckv, ckv.shape[-1]) * gkv, w_ukv)
        kpe_ss = jnp.sum(kpe * kpe, axis=-1, keepdims=True)
        qs, ks = [], []
        for h in range(n_heads):
            qh = _rms(qraw[:, h * QK_PAD:(h + 1) * QK_PAD], QK_DIM) * gqn
            qs += [qh[:, :NOPE], rope(qh[:, NOPE:])]
            kn = kv[:, h * NOPE:(h + 1) * NOPE]
            r = lax.rsqrt((jnp.sum(kn * kn, axis=-1, keepdims=True) + kpe_ss) * (1.0 / QK_DIM) + EPS)
            ks += [kn * r * gkn[:, :NOPE], rope(kpe * r * gkn[:, NOPE:])]
        return jnp.concatenate(qs, axis=-1) * q_scale, jnp.concatenate(ks, axis=-1), kv[:, n_heads * NOPE:]
    return fn


def make_f_mlstm_post(n_heads, dm):
    def fn(hf, hb, o, g):
        hm = hf + hb
        outs = []
        for h in range(n_heads):
            sl = slice(h * dm, (h + 1) * dm)
            outs.append(jax.nn.sigmoid(o[:, sl]) * (_rms(hm[:, sl], dm) * g[:, sl]))
        return (jnp.concatenate(outs, axis=-1),)
    return fn


def loss_head(x1, y, target, gate2, name, tile=256):
    S, D = x1.shape
    t = _pick_rows(S, tile)

    def body(x1_ref, y_ref, t_ref, g_ref, loss_ref, dout_ref, dy_ref, dgate_ref):
        @pl.when(pl.program_id(0) == 0)
        def _():
            loss_ref[...] = jnp.zeros_like(loss_ref)
            dgate_ref[...] = jnp.zeros_like(dgate_ref)

        yv, gv = y_ref[...], g_ref[...]
        e = x1_ref[...] + gv * yv - t_ref[...]
        loss_ref[...] += 0.5 * jnp.sum(jnp.sum(e * e, axis=-1, keepdims=True) * (1.0 / D), axis=0, keepdims=True)
        d_out = e * (1.0 / D)
        dout_ref[...] = d_out
        dy_ref[...] = (d_out * gv).astype(dy_ref.dtype)
        dgate_ref[...] += jnp.sum(d_out * yv, axis=0, keepdims=True)

    row = pl.BlockSpec((t, D), lambda i: (i, 0))
    return pl.pallas_call(
        body, name=name, grid=(S // t,),
        in_specs=[row, row, row, pl.BlockSpec((1, D), lambda i: (0, 0))],
        out_specs=[pl.BlockSpec((1, 1), lambda i: (0, 0)), row, row, pl.BlockSpec((1, D), lambda i: (0, 0))],
        out_shape=[jax.ShapeDtypeStruct((1, 1), f32), jax.ShapeDtypeStruct((S, D), f32),
                   jax.ShapeDtypeStruct((S, D), bf16), jax.ShapeDtypeStruct((1, D), f32)],
        compiler_params=_cp(("arbitrary",)))(x1, y, target, gate2)


def ada_fwd(c_all, w_blk, b_blk, name):
    B, D = c_all.shape
    N = w_blk.shape[1]
    tn = _pick(N, 512)

    def body(c_ref, w_ref, b_ref, o_ref):
        o_ref[...] = bdot(_silu(c_ref[...]), w_ref[...]) + b_ref[...]

    return pl.pallas_call(
        body, name=name, grid=(N // tn,),
        in_specs=[pl.BlockSpec((B, D), lambda j: (0, 0)), pl.BlockSpec((D, tn), lambda j: (0, j)),
                  pl.BlockSpec((1, tn), lambda j: (0, j))],
        out_specs=pl.BlockSpec((B, tn), lambda j: (0, j)),
        out_shape=jax.ShapeDtypeStruct((B, N), f32), compiler_params=_cp(("arbitrary",)))(c_all, w_blk, b_blk)


def ada_wgrad(c_all, dmod_blk, name):
    B, D = c_all.shape
    N = dmod_blk.shape[1]
    tn = _pick(N, 512)

    def body(c_ref, d_ref, o_ref):
        o_ref[...] = hdot_tn(_silu(c_ref[...]), d_ref[...])

    return pl.pallas_call(
        body, name=name, grid=(N // tn,),
        in_specs=[pl.BlockSpec((B, D), lambda j: (0, 0)), pl.BlockSpec((B, tn), lambda j: (0, j))],
        out_specs=pl.BlockSpec((D, tn), lambda j: (0, j)),
        out_shape=jax.ShapeDtypeStruct((D, N), f32), compiler_params=_cp(("arbitrary",)))(c_all, dmod_blk)


def _nt(a, b):
    return lax.dot_general(a, b, (((1,), (1,)), ((), ())), preferred_element_type=f32)


def _tn(a, b):
    return lax.dot_general(a, b, (((0,), (0,)), ((), ())), preferred_element_type=f32)


def flash_fwd(q, k, v, n_heads, name, side=(), tq=512, tk=8192, sub=1024):
    S = q.shape[0]
    tq, tk = _pick(S, tq), _pick(S, tk)
    sub = _pick(tk, sub)
    nk, nsub = S // tk, tk // sub
    ns = len(side)
    n_steps = n_heads * (S // tq) * nk
    assert ns == 0 or n_steps >= 3

    def body(*refs):
        q_ref, k_ref, v_ref = refs[:3]
        o_ref, lse_ref = refs[3 + ns:5 + ns]
        m_sc, l_sc, acc_sc = refs[5 + 2 * ns:8 + 2 * ns]
        j = pl.program_id(2)
        step = (pl.program_id(0) * (S // tq) + pl.program_id(1)) * nk + j
        if ns:
            g_start, g_mid, g_finish = _gather_phases(refs[3:3 + ns], refs[5 + ns:5 + 2 * ns], *refs[8 + 2 * ns:])
            pl.when(step == 0)(g_start)
            pl.when(step == n_steps // 2)(g_mid)

        if nk > 1:
            @pl.when(j == 0)
            def _():
                m_sc[...] = jnp.full_like(m_sc, -jnp.inf)
                l_sc[...] = jnp.zeros_like(l_sc)
                acc_sc[...] = jnp.zeros_like(acc_sc)

        qv = q_ref[...]
        ss =[_nt(qv, k_ref[b * sub:(b + 1) * sub, :]) for b in range(nsub)]
        mx = ss[0]
        for s in ss[1:]:
            mx = jnp.maximum(mx, s)
        m_new = jnp.max(mx, axis=-1, keepdims=True)
        if nk > 1:
            m_new = jnp.maximum(m_sc[...], m_new)
        psum, pv = None, None
        for b in range(nsub):
            p = jnp.exp2(ss[b] - m_new)
            d = jnp.dot(p.astype(bf16), v_ref[b * sub:(b + 1) * sub, :], preferred_element_type=f32)
            psum = p if psum is None else psum + p
            pv = d if pv is None else pv + d
        row_sum = jnp.sum(psum, axis=-1, keepdims=True)
        if nk == 1:
            o_ref[...] = (pv / row_sum).astype(o_ref.dtype)
            lse_ref[...] = m_new + jnp.log2(row_sum)
        else:
            alpha = jnp.exp2(m_sc[...] - m_new)
            m, l, acc = m_new, alpha * l_sc[...] + row_sum, alpha * acc_sc[...] + pv
            m_sc[...], l_sc[...], acc_sc[...] = m, l, acc

            @pl.when(j == nk - 1)
            def _():
                o_ref[...] = (acc / l).astype(o_ref.dtype)
                lse_ref[...] = m + jnp.log2(l)

        if ns:
            pl.when(step == n_steps - 1)(g_finish)

    anyspec = pl.BlockSpec(memory_space=pl.ANY)
    res = pl.pallas_call(
        body, name=name, grid=(n_heads, S // tq, nk),
        in_specs=[pl.BlockSpec((tq, QK_PAD), lambda h, i, j: (i, h)),
                  pl.BlockSpec((tk, QK_PAD), lambda h, i, j: (j, h)),
                  pl.BlockSpec((tk, V_DIM), lambda h, i, j: (j, h))] + [anyspec] * ns,
        out_specs=[pl.BlockSpec((tq, V_DIM), lambda h, i, j: (i, h)),
                   pl.BlockSpec((None, tq, 1), lambda h, i, j: (h, i, 0))] + [anyspec] * ns,
        out_shape=[jax.ShapeDtypeStruct((S, n_heads * V_DIM), bf16), jax.ShapeDtypeStruct((n_heads, S, 1), f32)]
        + [jax.ShapeDtypeStruct((N_DEV,) + a.shape, a.dtype) for a in side],
        scratch_shapes=[pltpu.VMEM((tq, 1), f32), pltpu.VMEM((tq, 1), f32), pltpu.VMEM((tq, V_DIM), f32)]
        + (_gather_scratch(ns) if ns else []),
        compiler_params=_cp(("arbitrary", "arbitrary", "arbitrary")))(q, k, v, *side)
    return res[0], res[1], list(res[2:])


def flash_bwd(q, k, v, o, lse_row, do, do_col0, n_heads, name, side=(), tq=1024, tk=8192, sub=512):
    S = q.shape[0]
    tq, tk = _pick(S, tq), _pick(S, tk)
    sub = _pick(tk, sub)
    nsub = tk // sub
    ln2 = math.log(2.0)
    ns = len(side)
    n_steps = n_heads * (S // tq) * (S // tk)
    assert ns == 0 or n_steps >= 2

    def body(*refs):
        q_ref, k_ref, v_ref, o_ref, lse_ref, do_ref = refs[:6]
        dq_ref, dk_ref, dv_ref = refs[6 + ns:9 + ns]
        i, j = pl.program_id(1), pl.program_id(2)
        step = (pl.program_id(0) * (S // tq) + i) * (S // tk) + j
        if ns:
            x_start, x_finish = _chip_exchange_phases(refs[6:6 + ns], refs[9 + ns:9 + 2 * ns], *refs[9 + 2 * ns:])
            pl.when(step == 0)(x_start)

        @pl.when(j == 0)
        def _():
            dq_ref[...] = jnp.zeros_like(dq_ref)

        @pl.when((i == 0) & (j == 0))
        def _():
            dk_ref[...] = jnp.zeros_like(dk_ref)
            dv_ref[...] = jnp.zeros_like(dv_ref)

        qv = q_ref[...]
        dof = do_ref[...].astype(f32)
        do_b = dof.astype(bf16)
        do_s = (dof * ln2).astype(bf16)
        delta = hdot_nt(jnp.ones((8, V_DIM), f32), dof * ln2 * o_ref[...].astype(f32))[0:1, :]
        lse = lse_ref[...]
        dq = jnp.zeros((tq, QK_PAD), f32)
        for b in range(nsub):
            kb = k_ref[b * sub:(b + 1) * sub, :]
            rows = pl.ds(pl.multiple_of(j * tk + b * sub, sub), sub)
            pt = jnp.exp2(_nt(kb, qv) - lse)
            dpt = _nt(v_ref[b * sub:(b + 1) * sub, :], do_s)
            dst = (pt * (dpt - delta)).astype(bf16)
            dv_ref[rows, :] += jnp.dot(pt.astype(bf16), do_b, preferred_element_type=f32)
            dk_ref[rows, :] += jnp.dot(dst, qv, preferred_element_type=f32)
            dq = dq + _tn(dst, kb)
        dq_ref[...] += dq
        if ns:
            pl.when(step == n_steps - 1)(x_finish)

    anyspec = pl.BlockSpec(memory_space=pl.ANY)
    res = pl.pallas_call(
        body, name=name, grid=(n_heads, S // tq, S // tk),
        in_specs=[pl.BlockSpec((tq, QK_PAD), lambda h, i, j: (i, h)),
                  pl.BlockSpec((tk, QK_PAD), lambda h, i, j: (j, h)),
                  pl.BlockSpec((tk, V_DIM), lambda h, i, j: (j, h)),
                  pl.BlockSpec((tq, V_DIM), lambda h, i, j: (i, h)),
                  pl.BlockSpec((None, 1, tq), lambda h, i, j: (h, 0, i)),
                  pl.BlockSpec((tq, V_DIM), lambda h, i, j: (i, do_col0 + h))] + [anyspec] * ns,
        out_specs=[pl.BlockSpec((tq, QK_PAD), lambda h, i, j: (i, h)),
                   pl.BlockSpec((S, QK_PAD), lambda h, i, j: (0, h)),
                   pl.BlockSpec((S, V_DIM), lambda h, i, j: (0, h))] + [anyspec] * ns,
        out_shape=[jax.ShapeDtypeStruct((S, n_heads * QK_PAD), f32), jax.ShapeDtypeStruct((S, n_heads * QK_PAD), f32),
                   jax.ShapeDtypeStruct((S, n_heads * V_DIM), f32)] + [jax.ShapeDtypeStruct(a.shape, a.dtype) for a in side],
        scratch_shapes=_chip_exchange_scratch(ns) if ns else [],
        compiler_params=_cp(("arbitrary", "arbitrary", "arbitrary")))(q, k, v, o, lse_row, do, *side)
    return res[0], res[1], res[2], list(res[3:])


def _shifted(prev, cur, nxt, k, first, last):
    if k == 0:
        return cur
    t = cur.shape[0]
    r = lax.broadcasted_iota(jnp.int32, (HALO,) + cur.shape[1:], 0)
    if k < 0:
        body = pltpu.roll(cur, -k, 0)
        edge = jnp.where(first, 0.0, pltpu.roll(prev, -k, 0))
        return jnp.concatenate([jnp.where(r < -k, edge, body[:HALO]), body[HALO:]], axis=0)
    body = pltpu.roll(cur, t - k, 0)
    edge = jnp.where(last, 0.0, pltpu.roll(nxt, HALO - k, 0))
    return jnp.concatenate([body[:t - HALO], jnp.where(r >= HALO - k, edge, body[t - HALO:])], axis=0)


HALO = 8


def _halo_specs(t, width, n_tiles, lead=None):
    per = t // HALO
    rows = [(HALO, lambda i: jnp.maximum(i * per - 1, 0)), (t, lambda i: i),
            (HALO, lambda i: jnp.minimum((i + 1) * per, n_tiles * per - 1))]
    if lead is None:
        return [pl.BlockSpec((r, width), lambda i, f=f: (f(i), 0)) for r, f in rows]
    return [pl.BlockSpec((None, r, width), lambda i, f=f: (lead, f(i), 0)) for r, f in rows]


def conv_fwd(proj, width, w, b, name, tile=256):
    S = proj.shape[0]
    t = _pick_rows(S, tile)
    n_tiles = S // t

    def body(p_ref, c_ref, n_ref, w_ref, b_ref, z_ref):
        i = pl.program_id(0)
        first, last = i == 0, i == n_tiles - 1
        prev, cur, nxt = p_ref[...], c_ref[...], n_ref[...]
        z = b_ref[...] + jnp.zeros_like(cur)
        for j in range(CONV_W):
            z = z + w_ref[j:j + 1, :] * _shifted(prev, cur, nxt, j - CONV_W // 2, first, last)
        z_ref[...] = z

    return pl.pallas_call(
        body, name=name, grid=(n_tiles,),
        in_specs=_halo_specs(t, width, n_tiles) + [_whole(w), _whole(b)],
        out_specs=pl.BlockSpec((t, width), lambda i: (i, 0)),
        out_shape=jax.ShapeDtypeStruct((S, width), f32),
        compiler_params=_cp(("arbitrary",)))(proj, proj, proj, w, b)


def conv_bwd(dzq, dzk, dvm, proj, width, w, name, tile=256):
    S = proj.shape[0]
    t = _pick_rows(S, tile)
    n_tiles = S // t
    half = width // 2

    def body(*refs):
        d_refs, (v0_ref, v1_ref, up_ref, uc_ref, un_ref, w_ref, du_ref, dw_ref, db_ref, dv_ref) = refs[:12], refs[12:]
        i = pl.program_id(0)
        first, last = i == 0, i == n_tiles - 1

        @pl.when(first)
        def _():
            dw_ref[...] = jnp.zeros_like(dw_ref)
            db_ref[...] = jnp.zeros_like(db_ref)

        dv_ref[...] = (v0_ref[...] + v1_ref[...]).astype(dv_ref.dtype)

        dprev, dcur, dnxt = [jnp.concatenate([d_refs[p][...] + d_refs[3 + p][...], d_refs[6 + p][...] + d_refs[9 + p][...]],
                                             axis=1) for p in range(3)]
        uprev, ucur, unxt = up_ref[...], uc_ref[...], un_ref[...]
        du = jnp.zeros_like(dcur)
        for j in range(CONV_W):
            k = j - CONV_W // 2
            du = du + w_ref[j:j + 1, :] * _shifted(dprev, dcur, dnxt, -k, first, last)
            dw_ref[j:j + 1, :] += jnp.sum(dcur * _shifted(uprev, ucur, unxt, k, first, last), axis=0, keepdims=True)
        du_ref[...] = du
        db_ref[...] += jnp.sum(dcur, axis=0, keepdims=True)

    return pl.pallas_call(
        body, name=name, grid=(n_tiles,),
        in_specs=_halo_specs(t, half, n_tiles, 0) + _halo_specs(t, half, n_tiles, 1) + _halo_specs(t, half, n_tiles, 0)
        + _halo_specs(t, half, n_tiles, 1)
        + [pl.BlockSpec((None, t, half), lambda i: (0, i, 0)), pl.BlockSpec((None, t, half), lambda i: (1, i, 0))]
        + _halo_specs(t, width, n_tiles) + [_whole(w)],
        out_specs=[pl.BlockSpec((t, width), lambda i: (i, 0)), pl.BlockSpec((8, width), lambda i: (0, 0)),
                   pl.BlockSpec((1, width), lambda i: (0, 0)), pl.BlockSpec((t, half), lambda i: (i, 0))],
        out_shape=[jax.ShapeDtypeStruct((S, width), f32), jax.ShapeDtypeStruct((8, width), f32),
                   jax.ShapeDtypeStruct((1, width), f32), jax.ShapeDtypeStruct((S, half), bf16)],
        compiler_params=_cp(("arbitrary",)))(*([dzq] * 6), *([dzk] * 6), dvm, dvm, proj, proj, proj, w)


def _mlstm_step(dm, d, C, n, m, zq, zk, v, ic, fc, ir, fr, bi, bf_):
    L = zq.shape[0]
    q = _silu(zq)
    k = _silu(zk) * (dm ** -0.5)
    i_c, f_c = ic + bi, jax.nn.log_sigmoid(fc + bf_)
    i_r, f_r = ir + bi, jax.nn.log_sigmoid(fr + bf_)
    r = lax.broadcasted_iota(jnp.int32, (L, L), 0)
    c = lax.broadcasted_iota(jnp.int32, (L, L), 1)
    sgn = jnp.where(d == 0, r - c, c - r)
    mask = sgn >= 0
    b_c = jnp.sum(jnp.where(mask, f_r, 0.0), axis=-1, keepdims=True)
    b_r = jnp.sum(jnp.where(sgn <= 0, f_c, 0.0), axis=0, keepdims=True)
    log_inter = b_c + m
    logD = jnp.where(mask, b_c - b_r + i_r, -jnp.inf)
    m_t = jnp.maximum(log_inter, jnp.max(logD, axis=-1, keepdims=True))
    Dm = jnp.exp(logD - m_t)
    w_inter = jnp.exp(log_inter - m_t)
    scores = bdot_nt(q, k) * Dm
    num = bdot(scores, v) + w_inter * bdot_nt(q, C)
    den = jnp.sum(scores, axis=-1, keepdims=True) + w_inter * jnp.sum(q * n, axis=-1, keepdims=True)
    h = num / jnp.maximum(jnp.abs(den), jnp.exp(-m_t))
    bL = jnp.sum(f_c, axis=0, keepdims=True)
    log_w = bL - b_c + i_c
    m_new = jnp.maximum(bL + m, jnp.max(log_w, axis=0, keepdims=True))
    decay = jnp.exp(bL + m - m_new)
    w = jnp.exp(log_w - m_new)
    C_new = decay * C + bdot_tn(w * v, k)
    n_new = decay * n + jnp.sum(w * k, axis=0, keepdims=True)
    return C_new, n_new, m_new, h


def _mlstm_in_specs(L, dm, hm, hb, nc, step_of):
    ng = hm // hb

    def chunk(d, j):
        s = step_of(j)
        return s + d * (nc - 1 - 2 * s)
    return [
        pl.BlockSpec((L, hb * dm), lambda d, g, j: (chunk(d, j), g)),
        pl.BlockSpec((L, hb * dm), lambda d, g, j: (chunk(d, j), ng + g)),
        pl.BlockSpec((L, hb * dm), lambda d, g, j: (chunk(d, j), 2 * ng + g)),
        pl.BlockSpec((None, hb, L, 1), lambda d, g, j: (d, g, chunk(d, j), 0)),
        pl.BlockSpec((None, hb, L, 1), lambda d, g, j: (d, g, chunk(d, j), 0)),
        pl.BlockSpec((None, hb, 1, L), lambda d, g, j: (d, g, 0, chunk(d, j))),
        pl.BlockSpec((None, hb, 1, L), lambda d, g, j: (d, g, 0, chunk(d, j))),
        pl.BlockSpec((None, hb, 1, 1), lambda d, g, j: (d, g, 0, 0)),
        pl.BlockSpec((None, hb, 1, 1), lambda d, g, j: (d, g, 0, 0)),
    ], chunk


def mlstm_fwd(z, proj, gates, hm, dm, name, hb=None):
    S = z.shape[0]
    L = CHUNK
    nc = S // L
    hb = hm if hb is None else hb
    in_specs, chunk = _mlstm_in_specs(L, dm, hm, hb, nc, lambda j: j)

    def body(zq, zk, v, ic, fc, ir, fr, bi, bf_, h_ref, cs_ref, ns_ref, ms_ref, C_sc, n_sc, m_sc):
        d = pl.program_id(0)

        @pl.when(pl.program_id(2) == 0)
        def _():
            C_sc[...] = jnp.zeros_like(C_sc)
            n_sc[...] = jnp.zeros_like(n_sc)
            m_sc[...] = jnp.full_like(m_sc, M_INIT)

        for hh in range(hb):
            cols = slice(hh * dm, (hh + 1) * dm)
            C, n, m = C_sc[hh], n_sc[hh], m_sc[hh]
            cs_ref[hh], ns_ref[hh], ms_ref[hh] = C, n, m
            C2, n2, m2, h = _mlstm_step(dm, d, C, n, m, zq[:, cols], zk[:, cols], v[:, cols], ic[hh], fc[hh],
                                        ir[hh], fr[hh], bi[hh], bf_[hh])
            C_sc[hh], n_sc[hh], m_sc[hh] = C2, n2, m2
            h_ref[:, cols] = h

    return pl.pallas_call(
        body, name=name, grid=(2, hm // hb, nc), in_specs=in_specs,
        out_specs=[pl.BlockSpec((None, L, hb * dm), lambda d, g, j: (d, chunk(d, j), g)),
                   pl.BlockSpec((None, hb, None, dm, dm), lambda d, g, j: (d, g, j, 0, 0)),
                   pl.BlockSpec((None, hb, None, 1, dm), lambda d, g, j: (d, g, j, 0, 0)),
                   pl.BlockSpec((None, hb, None, 1, 1), lambda d, g, j: (d, g, j, 0, 0))],
        out_shape=[jax.ShapeDtypeStruct((2, S, hm * dm), f32), jax.ShapeDtypeStruct((2, hm, nc, dm, dm), f32),
                   jax.ShapeDtypeStruct((2, hm, nc, 1, dm), f32), jax.ShapeDtypeStruct((2, hm, nc, 1, 1), f32)],
        scratch_shapes=[pltpu.VMEM((hb, dm, dm), f32), pltpu.VMEM((hb, 1, dm), f32), pltpu.VMEM((hb, 1, 1), f32)],
        compiler_params=_cp(("arbitrary", "arbitrary", "arbitrary")))(z, z, proj, *gates)


def mlstm_bwd(z, proj, gates, states, dh, hm, dm, name, hb=None):
    S = z.shape[0]
    L = CHUNK
    nc = S // L
    hb = hm if hb is None else hb
    in_specs, chunk = _mlstm_in_specs(L, dm, hm, hb, nc, lambda j: nc - 1 - j)
    st = lambda j: nc - 1 - j
    in_specs = in_specs + [
        pl.BlockSpec((None, hb, None, dm, dm), lambda d, g, j: (d, g, st(j), 0, 0)),
        pl.BlockSpec((None, hb, None, 1, dm), lambda d, g, j: (d, g, st(j), 0, 0)),
        pl.BlockSpec((None, hb, None, 1, 1), lambda d, g, j: (d, g, st(j), 0, 0)),
        pl.BlockSpec((L, hb * dm), lambda d, g, j: (chunk(d, j), g)),
    ]

    def body(zq, zk, v, ic, fc, ir, fr, bi, bf_, cs, ns, ms, dh_ref,
             dzq, dzk, dv, dic, dfc, dir_, dfr, dbi, dbf, dC_sc, dn_sc, dm_sc):
        d = pl.program_id(0)

        @pl.when(pl.program_id(2) == 0)
        def _():
            dC_sc[...] = jnp.zeros_like(dC_sc)
            dn_sc[...] = jnp.zeros_like(dn_sc)
            dm_sc[...] = jnp.zeros_like(dm_sc)
            dbi[...] = jnp.zeros_like(dbi)
            dbf[...] = jnp.zeros_like(dbf)

        for hh in range(hb):
            cols = slice(hh * dm, (hh + 1) * dm)
            prim = (cs[hh], ns[hh], ms[hh], zq[:, cols], zk[:, cols], v[:, cols], ic[hh], fc[hh], ir[hh], fr[hh],
                    bi[hh], bf_[hh])
            _, pull = jax.vjp(functools.partial(_mlstm_step, dm, d), *prim)
            g = pull((dC_sc[hh], dn_sc[hh], dm_sc[hh], dh_ref[:, cols]))
            dC_sc[hh], dn_sc[hh], dm_sc[hh] = g[0], g[1], g[2]
            dzq[:, cols], dzk[:, cols], dv[:, cols] = g[3], g[4], g[5]
            dic[hh], dfc[hh], dir_[hh], dfr[hh] = g[6], g[7], g[8], g[9]
            dbi[hh] += g[10]
            dbf[hh] += g[11]

    tile = pl.BlockSpec((None, L, hb * dm), lambda d, g, j: (d, chunk(d, j), g))
    col = pl.BlockSpec((None, hb, L, 1), lambda d, g, j: (d, g, chunk(d, j), 0))
    row = pl.BlockSpec((None, hb, 1, L), lambda d, g, j: (d, g, 0, chunk(d, j)))
    one = pl.BlockSpec((None, hb, 1, 1), lambda d, g, j: (d, g, 0, 0))
    big = jax.ShapeDtypeStruct((2, S, hm * dm), f32)
    cols_ = jax.ShapeDtypeStruct((2, hm, S, 1), f32)
    rows_ = jax.ShapeDtypeStruct((2, hm, 1, S), f32)
    ones_ = jax.ShapeDtypeStruct((2, hm, 1, 1), f32)
    return pl.pallas_call(
        body, name=name, grid=(2, hm // hb, nc), in_specs=in_specs,
        out_specs=[tile, tile, tile, col, col, row, row, one, one],
        out_shape=[big, big, big, cols_, cols_, rows_, rows_, ones_, ones_],
        scratch_shapes=[pltpu.VMEM((hb, dm, dm), f32), pltpu.VMEM((hb, 1, dm), f32), pltpu.VMEM((hb, 1, 1), f32)],
        compiler_params=_cp(("arbitrary", "arbitrary", "arbitrary")))(z, z, proj, *gates, *states, dh)


def _blocks_to_cols(g):
    return g.transpose(1, 0, 2).reshape(g.shape[1], N_DEV * g.shape[2])


def _cols_to_blocks(a):
    return a.reshape(a.shape[0], N_DEV, a.shape[1] // N_DEV).transpose(1, 0, 2)


def _pad_cols(a, n):
    return jnp.pad(a, ((0, 0), (0, n - a.shape[1])))


def _relu2(u):
    r = jnp.maximum(u, 0.0)
    return r * r


def kernel(x, c, positions, w_ada, b_ada, norm_mix_g, w_in, b_gates, conv_w, conv_b, q_lora_g, w_uq, kv_lora_g, w_ukv, q_norm_g, k_norm_g, mlstm_norm_g, w_out, norm_mlp_g, w_ff1, w_ff2, loss_target, m_w_ada, m_b_ada, m_norm_mix_g, m_w_in, m_b_gates, m_conv_w, m_conv_b, m_q_lora_g, m_w_uq, m_kv_lora_g, m_w_ukv, m_q_norm_g, m_k_norm_g, m_mlstm_norm_g, m_w_out, m_norm_mlp_g, m_w_ff1, m_w_ff2, v_w_ada, v_b_ada, v_norm_mix_g, v_w_in, v_b_gates, v_conv_w, v_conv_b, v_q_lora_g, v_w_uq, v_kv_lora_g, v_w_ukv, v_q_norm_g, v_k_norm_g, v_mlstm_norm_g, v_w_out, v_norm_mlp_g, v_w_ff1, v_w_ff2):
    S, D = x.shape[1], x.shape[2]
    QL, KVL = w_uq.shape[1], w_ukv.shape[1]
    H = w_uq.shape[2] * N_DEV // QK_DIM
    HM = mlstm_norm_g.shape[1]
    DM = mlstm_norm_g.shape[2] * N_DEV
    MW = HM * DM
    D_IN = w_in.shape[2] * N_DEV
    NADA = w_ada.shape[2]
    assert D_IN == QL + KVL + ROPE + 4 * MW + N_GATES and DM % LANE == 0 and S % CHUNK == 0
    assert (4 * MW) % QL == 0 and (4 * MW + QL) % KVL == 0 and KVL % LANE == 0
    idx = 4 * lax.axis_index("x") + 2 * lax.axis_index("y") + lax.axis_index("c")
    x2, tgt = x[0], loss_target[0]

    (c_all,) = all_gather([c], "gather_cond")
    c_all = c_all.reshape(N_DEV, D)
    xi, yi, ci = lax.axis_index("x"), lax.axis_index("y"), lax.axis_index("c")
    slots = jnp.stack([4 * (1 - xi) + 2 * yi + ci, 4 * xi + 2 * (1 - yi) + ci, 4 * (1 - xi) + 2 * (1 - yi) + ci]).astype(jnp.int32)
    gqn = _pad_cols(q_norm_g, QK_PAD)
    gkn = _pad_cols(k_norm_g, QK_PAD)
    fr_np = np.zeros((1, LANE), np.float32)
    fr_np[0, :HALF] = fr_np[0, HALF:ROPE] = ROPE_THETA ** (-np.arange(HALF, dtype=np.float32) / HALF)
    freqs = jnp.asarray(fr_np)
    pos = positions.astype(f32).reshape(S, 1)

    b_blk = lax.dynamic_slice(b_ada, (0, idx * NADA), (1, NADA))
    mod_part = ada_fwd(c_all, w_ada[0], b_blk, "ada_fwd")
    (mod_all,) = all_gather([mod_part], "gather_mod")
    mod = lax.dynamic_index_in_dim(mod_all, idx, axis=1, keepdims=False).reshape(1, N_DEV * NADA)
    shift1, scale1, gate1, shift2, scale2, gate2 = [mod[:, k * D:(k + 1) * D] for k in range(6)]

    h, g_in, g_uq, g_ukv, g_conv, g_mn = rowwise(
        f_norm_mod, [Row(x2)], [norm_mix_g, shift1, scale1], [(D, bf16)], n_rows=S, tile=256, name="norm_mix",
        side=[w_in[0].astype(bf16), w_uq[0].astype(bf16), w_ukv[0].astype(bf16), conv_w[0], mlstm_norm_g[0]])
    wi = _blocks_to_cols(g_in)
    o_cq, o_ckv, o_kpe, o_m, o_g = 0, QL, QL + KVL, QL + KVL + ROPE, QL + KVL + ROPE + 4 * MW
    w_in_p = jnp.concatenate([wi[:, o_m:o_g], wi[:, o_cq:o_kpe], _pad_cols(wi[:, o_kpe:o_m], LANE),
                              _pad_cols(wi[:, o_g:], LANE)], axis=1)
    NP = w_in_p.shape[1]
    cb_cq, cb_ckv, cb_kpe, cb_g = 4 * MW // QL, (4 * MW + QL) // KVL, (4 * MW + QL + KVL) // LANE, NP // LANE - 1
    w_uq_p = jnp.pad(_blocks_to_cols(g_uq).reshape(QL, H, QK_DIM), ((0, 0), (0, 0), (0, QK_PAD - QK_DIM))).reshape(QL, H * QK_PAD)
    w_ukv_p = _blocks_to_cols(g_ukv).reshape(KVL, H, 2, NOPE).transpose(0, 2, 1, 3).reshape(KVL, 2 * H * NOPE)
    conv_w_f = jnp.pad(_blocks_to_cols(g_conv), ((0, 8 - CONV_W), (0, 0)))
    mn_g = _blocks_to_cols(g_mn).reshape(1, MW)
    proj = mm(h, w_in_p, name="proj_in", out_dtype=f32)
    r_cq, r_ckv, r_kpe = Row(proj, QL, cb_cq), Row(proj, KVL, cb_ckv), Row(proj, LANE, cb_kpe)
    f_prep = make_f_mla_prep(H, QK_DIM ** -0.5 * math.log2(math.e))
    prep_params = [q_lora_g, kv_lora_g, gqn, gkn, w_uq_p, w_ukv_p, freqs]
    Q, K, V = rowwise(f_prep, [r_cq, r_ckv, r_kpe, Row(pos, diff=False)], prep_params,
                      [(H * QK_PAD, bf16), (H * QK_PAD, bf16), (H * V_DIM, bf16)], n_rows=S, tile=256, name="mla_prep")
    attn, lse, (g_out, g_ff1, g_ff2) = flash_fwd(
        Q, K, V, H, "flash_fwd", side=[w_out[0].astype(bf16), w_ff1[0].astype(bf16), w_ff2[0].astype(bf16)])
    w_out_f = g_out.reshape(N_DEV * g_out.shape[1], D)
    w_ff2_f = g_ff2.reshape(N_DEV * g_ff2.shape[1], D)

    conv_bias = conv_b
    z = conv_fwd(proj, 2 * MW, conv_w_f, conv_bias, "conv_fwd")
    graw = proj[:, cb_g * LANE:cb_g * LANE + N_GATES].reshape(S, 4, HM)
    gcol = graw.transpose(1, 2, 0).reshape(2, 2, HM, S)
    bg = b_gates.reshape(2, 2, HM)
    gates = (gcol[:, 0].reshape(2, HM, S, 1), gcol[:, 1].reshape(2, HM, S, 1),
             gcol[:, 0].reshape(2, HM, 1, S), gcol[:, 1].reshape(2, HM, 1, S),
             bg[:, 0].reshape(2, HM, 1, 1), bg[:, 1].reshape(2, HM, 1, 1))
    hdir, cs, ns, ms = mlstm_fwd(z, proj, gates, HM, DM, "mlstm_fwd")
    f_post = make_f_mlstm_post(HM, DM)
    post_rows = [Row(hdir, MW, 0, lead=0), Row(hdir, MW, 0, lead=1), Row(proj, MW, 3)]
    (ml_out,) = rowwise(f_post, post_rows, [mn_g], [(MW, bf16)], n_rows=S, tile=256, name="mlstm_post")

    cat = jnp.concatenate([attn, ml_out], axis=1)
    mixed = mm(cat, w_out_f, name="proj_out", out_dtype=f32)
    mlp_params = [gate1, norm_mlp_g, shift2, scale2]
    x1, h2 = rowwise(f_resid_norm_mod, [Row(x2), Row(mixed)], mlp_params, [(D, f32), (D, bf16)],
                     n_rows=S, tile=256, name="resid_norm_mlp")
    u = mm(h2, g_ff1, name="ff1", out_dtype=bf16)
    y = mm(u, w_ff2_f, name="ff2", a_fn=_relu2, out_dtype=f32)
    loss_l, d_out, d_y, d_gate2 = loss_head(x1, y, tgt, gate2, "loss_head")
    loss = lax.psum(loss_l[0, 0], AXES)

    dw_ff2 = mm(u, d_y, name="dw_ff2", ta=True, a_fn=_relu2, out_dtype=bf16)
    d_u = mm(d_y, w_ff2_f, name="d_u", tb=True, epi=lambda acc, uu: acc * (2.0 * jnp.maximum(uu.astype(f32), 0.0)),
             extras=(u,), out_dtype=bf16)
    dw_ff1 = mm(h2, d_u, name="dw_ff1", ta=True, out_dtype=bf16, out_blocks=True)
    w_ff1_t = g_ff1.transpose(0, 2, 1).reshape(-1, D)
    d_h2 = mm(d_u, w_ff1_t, name="d_h2", out_dtype=f32)
    (d_x1, d_mixed), (d_gate1, d_g_mlp, d_shift2, d_scale2) = rowwise_vjp(
        f_resid_norm_mod, [Row(x2), Row(mixed)], mlp_params, [Row(d_out), Row(d_h2)],
        n_rows=S, tile=256, name="resid_norm_mlp_bwd", row_grad_dtypes=[f32, bf16])
    dw_out = mm(cat, d_mixed, name="dw_out", ta=True, out_dtype=bf16)
    mlp_g = [dw_out.reshape(N_DEV, -1, D), dw_ff1, dw_ff2.reshape(N_DEV, -1, D)]
    d_cat, mlp_sib = mm(d_mixed, w_out_f, name="d_cat", tb=True, out_dtype=f32, tm=512, tn=2048,
                        side=mlp_g, side_pair=True)

    post_rows_b = [post_rows[0], Row(hdir, MW, 0, lead=1, diff=False), post_rows[2]]
    (dh, d_om), (d_mn_g,) = rowwise_vjp(
        f_post, post_rows_b, [mn_g], [Row(d_cat, MW, H * V_DIM // MW)], n_rows=S, tile=256, name="mlstm_post_bwd")
    dzq, dzk, dvm, dic, dfc, dir_, dfr, dbi, dbf = mlstm_bwd(z, proj, gates, (cs, ns, ms), dh, HM, DM, "mlstm_bwd")
    d_qk, d_conv_w, d_conv_b, d_vm = conv_bwd(dzq, dzk, dvm, proj, 2 * MW, conv_w_f, "conv_bwd")
    dg = jnp.stack([dic.reshape(2, HM, S) + dir_.reshape(2, HM, S), dfc.reshape(2, HM, S) + dfr.reshape(2, HM, S)], axis=1)
    d_gates = dg.reshape(4 * HM, S).T
    d_b_gates = jnp.stack([dbi.reshape(2, HM), dbf.reshape(2, HM)], axis=1).reshape(1, N_GATES)

    mlp_tags = ["w_out", "w_ff1", "w_ff2"]
    mlp_part = [chip_partials(g, r, slots, "grad_chip_partials_" + t) for g, r, t in zip(mlp_g, mlp_sib, mlp_tags)]
    dq, dk, dv, mlp_chips = flash_bwd(Q, K, V, attn, lse.reshape(H, 1, S), d_cat, 0, H, "flash_bwd", side=mlp_part)
    (d_cq, d_ckv, d_kpe), (d_gq, d_gkv, d_gqn, d_gkn, dw_uq_p, dw_ukv_p) = rowwise_vjp(
        f_prep, [r_cq, r_ckv, r_kpe, Row(pos, diff=False)], prep_params, [Row(dq), Row(dk), Row(dv)],
        n_rows=S, tile=256, name="mla_prep_bwd", row_grad_dtypes=[bf16, bf16, bf16],
        param_diff=[True, True, True, True, True, True, False])

    d_proj = jnp.concatenate([d_qk.astype(bf16), d_vm, d_om.astype(bf16), d_cq, d_ckv, d_kpe,
                              _pad_cols(d_gates.astype(bf16), LANE)], axis=1)
    dw_in_p = mm(h, d_proj, name="dw_in", ta=True, out_dtype=bf16)

    dwi = jnp.concatenate([dw_in_p[:, 4 * MW:4 * MW + QL + KVL + ROPE], dw_in_p[:, :4 * MW],
                           dw_in_p[:, cb_g * LANE:cb_g * LANE + N_GATES]], axis=1)
    dw_uq = dw_uq_p.reshape(QL, H, QK_PAD)[:, :, :QK_DIM].reshape(QL, H * QK_DIM)
    dw_ukv = dw_ukv_p.reshape(KVL, 2, H, NOPE).transpose(0, 2, 1, 3).reshape(KVL, 2 * H * NOPE)
    tiny = [(w_uq, m_w_uq, v_w_uq, _cols_to_blocks(dw_uq)),
            (w_ukv, m_w_ukv, v_w_ukv, _cols_to_blocks(dw_ukv)),
            (conv_w, m_conv_w, v_conv_w, _cols_to_blocks(d_conv_w[:CONV_W])),
            (mlstm_norm_g, m_mlstm_norm_g, v_mlstm_norm_g, _cols_to_blocks(d_mn_g.reshape(HM, DM)))]
    tsizes = [int(np.prod(b[0].shape)) for b in tiny]
    T = sum(tsizes)
    PC = 512
    PR = -(-T // (PC * 64)) * 64
    gpack = jnp.concatenate([b[3].astype(bf16).reshape(N_DEV, -1) for b in tiny], axis=1)
    gpack = jnp.pad(gpack, ((0, 0), (0, PR * PC - T))).reshape(N_DEV, PR, PC)
    wpack = lambda k: jnp.pad(jnp.concatenate([b[k].reshape(1, -1) for b in tiny], axis=1),
                              ((0, 0), (0, PR * PC - T))).reshape(PR, PC)
    late_g = [_cols_to_blocks(dwi), gpack]
    late_sib = pair_exchange(late_g, "grad_pair_exchange")
    late_part = [chip_partials(g, r, slots, "grad_chip_partials_" + t) for g, r, t in zip(late_g, late_sib, ["w_in", "tiny"])]
    d_h, late_chips = mm(d_proj, w_in_p, name="d_h", tb=True, out_dtype=f32, tm=512, tn=2048, tk=2560, side=late_part)
    (grad_x,), (d_g_mix, d_shift1, d_scale1) = rowwise_vjp(
        f_norm_mod_thru, [Row(x2)], [norm_mix_g, shift1, scale1], [Row(d_h), Row(d_x1)],
        n_rows=S, tile=256, name="norm_mix_bwd")

    dmod = jnp.concatenate([d_shift1, d_scale1, d_gate1, d_shift2, d_scale2, d_gate2], axis=1)
    small = [(norm_mix_g, m_norm_mix_g, v_norm_mix_g, d_g_mix), (b_gates, m_b_gates, v_b_gates, d_b_gates),
             (conv_b, m_conv_b, v_conv_b, d_conv_b), (q_lora_g, m_q_lora_g, v_q_lora_g, d_gq),
             (kv_lora_g, m_kv_lora_g, v_kv_lora_g, d_gkv), (q_norm_g, m_q_norm_g, v_q_norm_g, d_gqn[:, :QK_DIM]),
             (k_norm_g, m_k_norm_g, v_k_norm_g, d_gkn[:, :QK_DIM]), (norm_mlp_g, m_norm_mlp_g, v_norm_mlp_g, d_g_mlp),
             (b_ada, m_b_ada, v_b_ada, dmod)]
    sizes = [s[0].shape[1] for s in small]
    P = sum(sizes)
    PP = -(-P // LANE) * LANE
    pack = lambda k: _pad_cols(jnp.concatenate([s[k] for s in small], axis=1), PP)
    (sg_all,) = all_gather([pack(3)], "gather_small_grads")
    s_out = adamw([sg_all[k] for k in range(N_DEV)], pack(0), pack(1), pack(2), "adamw_small")
    offs = np.concatenate([[0], np.cumsum(sizes)])
    small_out = [[o[:, offs[k]:offs[k + 1]] for o in s_out] for k in range(len(small))]

    dmod_all = sg_all[:, 0, offs[-2]:offs[-1]]
    dmod_blk = lax.dynamic_slice(dmod_all, (0, idx * NADA), (N_DEV, NADA))
    g_w_ada = ada_wgrad(c_all, dmod_blk, "ada_wgrad")
    ada_out = adamw([g_w_ada], w_ada[0], m_w_ada[0], v_w_ada[0], "adamw_ada")

    large = [(w_in[0], m_w_in[0], v_w_in[0], late_g[0]),
             (w_out[0], m_w_out[0], v_w_out[0], mlp_g[0]),
             (w_ff1[0], m_w_ff1[0], v_w_ff1[0], mlp_g[1]),
             (w_ff2[0], m_w_ff2[0], v_w_ff2[0], mlp_g[2]),
             (wpack(0), wpack(1), wpack(2), gpack)]
    tags = ["w_in", "w_out", "w_ff1", "w_ff2", "tiny"]
    from_sibling = [late_sib[0]] + list(mlp_sib) + [late_sib[1]]
    from_chips = [late_chips[0]] + list(mlp_chips) + [late_chips[1]]
    l_out = []
    for (w_, m_, v_, g), r, fc, t in zip(large, from_sibling, from_chips, tags):
        mine = lax.dynamic_index_in_dim(g, idx, axis=0, keepdims=False)
        sib = lax.dynamic_index_in_dim(r, 2 * xi + yi, axis=0, keepdims=False)
        l_out.append(adamw([mine, sib, fc[0], fc[1], fc[2]], w_, m_, v_, "adamw_" + t))
    toffs = np.concatenate([[0], np.cumsum(tsizes)])
    tiny_out = [[o.reshape(-1)[toffs[k]:toffs[k + 1]].reshape(tiny[k][0].shape) for o in l_out[4]] for k in range(len(tiny))]
    big_out = [[o[None] for o in l_out[0]], tiny_out[0], tiny_out[1], [o[None] for o in l_out[1]],
               [o[None] for o in l_out[2]], [o[None] for o in l_out[3]], tiny_out[2], tiny_out[3]]

    names = ["w_ada", "b_ada", "norm_mix_g", "w_in", "b_gates", "conv_w", "conv_b", "q_lora_g", "w_uq", "kv_lora_g",
             "w_ukv", "q_norm_g", "k_norm_g", "mlstm_norm_g", "w_out", "norm_mlp_g", "w_ff1", "w_ff2"]
    res = {"w_ada": [o[None] for o in ada_out]}
    for k, nm in enumerate(["norm_mix_g", "b_gates", "conv_b", "q_lora_g", "kv_lora_g", "q_norm_g", "k_norm_g",
                            "norm_mlp_g", "b_ada"]):
        res[nm] = small_out[k]
    for k, nm in enumerate(["w_in", "w_uq", "w_ukv", "w_out", "w_ff1", "w_ff2", "conv_w", "mlstm_norm_g"]):
        res[nm] = big_out[k]
    outs = [loss, grad_x[None]]
    for part in range(4):
        outs += [res[nm][part] for nm in names]
    return tuple(outs)
```
